```python
import jax, jax.numpy as jnp
from jax import lax
import numpy as np


D_MODEL = 1024
BATCH = 2
SEQ = 8192
DEPTH = 4

GRID_W = 64
CTX_LEN = 256
EPS = 1e-6
CONV_A_DIM = 512
CONV_A_WIDTH = 31
CONV_B_DIM = 512
CONV_B_WIDTH = 3
EA_VAL = 0
EA_GATE = EA_VAL + CONV_A_DIM
EB_B = EA_GATE + CONV_A_DIM
EB_C = EB_B + CONV_B_DIM
EB_H = EB_C + CONV_B_DIM
EVEN_IN = EB_H + CONV_B_DIM
MLA_HEADS = 8
MLA_NOPE = 64
MLA_ROPE = 32
MLA_V = 64
Q_LORA = 384
KV_LORA = 256
MLA_SCALE = (MLA_NOPE + MLA_ROPE) ** -0.5
AXIS_DIM = MLA_ROPE // 2
AXIS_PAIRS = AXIS_DIM // 2
ROPE_BASE = 10000.0
Q_BLOCK = 128
POOL_WINDOWS = (2, 4, 8, 16)
POOL_GROUP = 128
POOL_DIM = POOL_GROUP * len(POOL_WINDOWS)
KV_OFF = Q_LORA
KR_OFF = KV_OFF + KV_LORA
POOL_OFF = KR_OFF + MLA_ROPE
ODD_IN = POOL_OFF + POOL_DIM
MIX_OUT = 1024
N_GROUPS = 4
EXPERTS_PER_GROUP = 4
N_EXPERTS = N_GROUPS * EXPERTS_PER_GROUP
TOP_K = 2
D_EXPERT = 512
N_EVEN = (DEPTH + 1) // 2
N_ODD = DEPTH // 2

kernel_name = 'hybrid_conv_mla_pool_hmoe_diffusion_prefix'


def rms_norm(x, g):
    xf = x.astype(jnp.float32)
    y = xf * lax.rsqrt(jnp.mean(xf * xf, axis=-1, keepdims=True) + EPS)
    return (y * g.astype(jnp.float32)).astype(x.dtype)


def layer_norm(x, g, b):
    xf = x.astype(jnp.float32)
    xc = xf - jnp.mean(xf, axis=-1, keepdims=True)
    y = xc * lax.rsqrt(jnp.mean(xc * xc, axis=-1, keepdims=True) + EPS)
    return (y * g.astype(jnp.float32) + b.astype(jnp.float32)).astype(x.dtype)


def modulate(h, shift, scale):
    return h * (1 + scale) + shift


def depthwise_conv(x, w):
    k = w.shape[0]
    return lax.conv_general_dilated(x, w[:, None, :], window_strides=(1,), padding=[((k - 1) // 2, k // 2)],
                                    dimension_numbers=('NWC', 'WIO', 'NWC'), feature_group_count=x.shape[-1])


def even_mixer(h, w_in, conv_a_w, conv_a_b, ln_a_g, ln_a_b, conv_b_w, w_out):
    u = h @ w_in
    a = u[..., EA_VAL:EA_GATE] * jax.nn.sigmoid(u[..., EA_GATE:EB_B])
    a = depthwise_conv(a, conv_a_w) + conv_a_b
    a = jax.nn.silu(layer_norm(a, ln_a_g, ln_a_b))
    gb, gc, hb = u[..., EB_B:EB_C], u[..., EB_C:EB_H], u[..., EB_H:EVEN_IN]
    b = gb * depthwise_conv(gc * hb, conv_b_w)
    return jnp.concatenate([a, b], axis=-1) @ w_out


def apply_axial_rope(x, cos, sin):
    xf = x.astype(jnp.float32)
    out = []
    for a in range(2):
        seg = xf[..., a * AXIS_DIM:(a + 1) * AXIS_DIM]
        x1, x2 = seg[..., :AXIS_PAIRS], seg[..., AXIS_PAIRS:]
        ca = cos[..., a * AXIS_PAIRS:(a + 1) * AXIS_PAIRS]
        sa = sin[..., a * AXIS_PAIRS:(a + 1) * AXIS_PAIRS]
        out += [x1 * ca - x2 * sa, x1 * sa + x2 * ca]
    return jnp.concatenate(out, axis=-1).astype(x.dtype)


def mla_q(q_c, q_norm_g, w_uq):
    b, l, _ = q_c.shape
    q = (rms_norm(q_c, q_norm_g) @ w_uq).reshape(b, l, MLA_HEADS, MLA_NOPE + MLA_ROPE)
    return q[..., :MLA_NOPE], q[..., MLA_NOPE:]


def mla_kv(kv_c, kv_norm_g, w_ukv):
    b, l, _ = kv_c.shape
    kv = (rms_norm(kv_c, kv_norm_g) @ w_ukv).reshape(b, l, MLA_HEADS, MLA_NOPE + MLA_V)
    return kv[..., :MLA_NOPE], kv[..., MLA_NOPE:]


def mla_attend(q_nope, q_rope, k_nope, k_rope, v):
    b, lq, h, _ = q_nope.shape
    nb = lq // Q_BLOCK

    def blocks(a):
        return jnp.moveaxis(a.reshape((b, nb, Q_BLOCK) + a.shape[2:]), 1, 0)

    def one(qs):
        qn, qr = qs
        s = jnp.einsum('bqhd,bkhd->bhqk', qn, k_nope) + jnp.einsum('bqhr,bkr->bhqk', qr, k_rope)
        p = jax.nn.softmax(s.astype(jnp.float32) * MLA_SCALE, axis=-1).astype(v.dtype)
        return jnp.einsum('bhqk,bkhd->bqhd', p, v)

    out = lax.map(one, (blocks(q_nope), blocks(q_rope)))
    return jnp.moveaxis(out, 0, 1).reshape(b, lq, h * MLA_V)


def centred_mean_minus_self(x, w):
    b, l, ch = x.shape
    cs = jnp.concatenate([jnp.zeros((b, 1, ch), jnp.float32), jnp.cumsum(x.astype(jnp.float32), axis=1)], axis=1)
    t = jnp.arange(l)
    lo = jnp.clip(t - w // 2, 0, l)
    hi = jnp.clip(t - w // 2 + w, 0, l)
    mean = (cs[:, hi] - cs[:, lo]) / (hi - lo).astype(jnp.float32)[None, :, None]
    return mean.astype(x.dtype) - x


def pool_mixer(u, w_pool, b_pool, s_pool):
    b, l, _ = u.shape
    parts = [centred_mean_minus_self(u[..., g * POOL_GROUP:(g + 1) * POOL_GROUP], w) for g, w in enumerate(POOL_WINDOWS)]
    p = jnp.stack(parts, axis=2)
    y = jnp.einsum('blgc,gcd->blgd', p, w_pool) + b_pool
    return y.reshape(b, l, POOL_DIM) * s_pool


def odd_mixer(h_lat, h_ctx, cos, sin, w_in, q_norm_g, w_uq, kv_norm_g, w_ukv, w_pool, b_pool, s_pool, w_out, ctx_out):
    u = h_lat @ w_in
    o = 0 if ctx_out else KV_OFF
    u_ctx = h_ctx @ (w_in if ctx_out else w_in[:, KV_OFF:POOL_OFF])
    kn_c, v_c = mla_kv(u_ctx[..., KV_OFF - o:KR_OFF - o], kv_norm_g, w_ukv)
    kr_c = u_ctx[..., KR_OFF - o:POOL_OFF - o]
    kn_l, v_l = mla_kv(u[..., KV_OFF:KR_OFF], kv_norm_g, w_ukv)
    kr_l = apply_axial_rope(u[..., KR_OFF:POOL_OFF], cos, sin)
    qn_l, qr_l = mla_q(u[..., :KV_OFF], q_norm_g, w_uq)
    qr_l = apply_axial_rope(qr_l, cos[:, None, :], sin[:, None, :])
    att = mla_attend(qn_l, qr_l, jnp.concatenate([kn_c, kn_l], axis=1), jnp.concatenate([kr_c, kr_l], axis=1),
                     jnp.concatenate([v_c, v_l], axis=1))
    pool = pool_mixer(u[..., POOL_OFF:], w_pool, b_pool, s_pool)
    y_lat = jnp.concatenate([att, pool], axis=-1) @ w_out
    y_ctx = None
    if ctx_out:
        qn_c, qr_c = mla_q(u_ctx[..., :KV_OFF], q_norm_g, w_uq)
        att_c = mla_attend(qn_c, qr_c, kn_c, kr_c, v_c)
        pool_c = pool_mixer(u_ctx[..., POOL_OFF:], w_pool, b_pool, s_pool)
        y_ctx = jnp.concatenate([att_c, pool_c], axis=-1) @ w_out
    return y_lat, y_ctx


def hier_moe(h, wg, bg, we, be, w1, w3, w2):
    shp = h.shape
    t = h.reshape(-1, shp[-1])
    g_logits = (t @ wg + bg).astype(jnp.float32)
    g_idx = jnp.argmax(g_logits, axis=-1)
    g_w = jnp.take_along_axis(jax.nn.softmax(g_logits, axis=-1), g_idx[:, None], axis=-1)
    e_logits = (t @ we + be).astype(jnp.float32).reshape(-1, N_GROUPS, EXPERTS_PER_GROUP)
    e_in = jnp.take_along_axis(e_logits, g_idx[:, None, None], axis=1)[:, 0]
    top_v, top_i = lax.top_k(e_in, TOP_K)
    top_w = jax.nn.softmax(top_v, axis=-1) * g_w
    eid = g_idx[:, None] * EXPERTS_PER_GROUP + top_i
    gates = jnp.einsum('nk,nke->ne', top_w, jax.nn.one_hot(eid, N_EXPERTS, dtype=jnp.float32)).astype(t.dtype)
    out = jnp.zeros_like(t)
    for e in range(N_EXPERTS):
        hid = jax.nn.silu(t @ w1[e]) * (t @ w3[e])
        out = out + gates[:, e:e + 1] * (hid @ w2[e])
    return out.reshape(shp)


def setup_inputs(seed: int = 0) -> dict:
    key = jax.random.key(seed)
    ks = iter(jax.random.split(key, 40))
    D = D_MODEL

    def nrm(shape, scale):
        return jax.random.normal(next(ks), shape, jnp.float32) * scale

    def gain(shape):
        return 1.0 + nrm(shape, 0.1)

    return {
        'x': nrm((BATCH, SEQ, D), 1.0),
        'c': nrm((BATCH, D), 1.0),
        'ctx': nrm((BATCH, CTX_LEN, D), 1.0),
        'c_ctx': nrm((D,), 1.0),
        'w_mod': nrm((DEPTH, D, 6 * D), 0.5 * D ** -0.5),
        'b_mod': nrm((DEPTH, 6 * D), 0.02),
        'norm_g': gain((DEPTH, 2, D)),
        'ev_w_in': nrm((N_EVEN, D, EVEN_IN), D ** -0.5),
        'ev_conv_a_w': nrm((N_EVEN, CONV_A_WIDTH, CONV_A_DIM), CONV_A_WIDTH ** -0.5),
        'ev_conv_a_b': nrm((N_EVEN, CONV_A_DIM), 0.02),
        'ev_ln_a_g': gain((N_EVEN, CONV_A_DIM)),
        'ev_ln_a_b': nrm((N_EVEN, CONV_A_DIM), 0.02),
        'ev_conv_b_w': nrm((N_EVEN, CONV_B_WIDTH, CONV_B_DIM), CONV_B_WIDTH ** -0.5),
        'ev_w_out': nrm((N_EVEN, MIX_OUT, D), MIX_OUT ** -0.5),
        'od_w_in': nrm((N_ODD, D, ODD_IN), D ** -0.5),
        'od_q_norm_g': gain((N_ODD, Q_LORA)),
        'od_w_uq': nrm((N_ODD, Q_LORA, MLA_HEADS * (MLA_NOPE + MLA_ROPE)), Q_LORA ** -0.5),
        'od_kv_norm_g': gain((N_ODD, KV_LORA)),
        'od_w_ukv': nrm((N_ODD, KV_LORA, MLA_HEADS * (MLA_NOPE + MLA_V)), KV_LORA ** -0.5),
        'od_w_pool': nrm((N_ODD, len(POOL_WINDOWS), POOL_GROUP, POOL_GROUP), POOL_GROUP ** -0.5),
        'od_b_pool': nrm((N_ODD, len(POOL_WINDOWS), POOL_GROUP), 0.02),
        'od_s_pool': gain((N_ODD, POOL_DIM)),
        'od_w_out': nrm((N_ODD, MIX_OUT, D), MIX_OUT ** -0.5),
        'moe_wg': nrm((DEPTH, D, N_GROUPS), D ** -0.5),
        'moe_bg': nrm((DEPTH, N_GROUPS), 0.01),
        'moe_we': nrm((DEPTH, D, N_EXPERTS), D ** -0.5),
        'moe_be': nrm((DEPTH, N_EXPERTS), 0.01),
        'moe_w1': nrm((DEPTH, N_EXPERTS, D, D_EXPERT), D ** -0.5),
        'moe_w3': nrm((DEPTH, N_EXPERTS, D, D_EXPERT), D ** -0.5),
        'moe_w2': nrm((DEPTH, N_EXPERTS, D_EXPERT, D), D_EXPERT ** -0.5),
        'final_g': gain((D,)),
    }


def reference(x, c, ctx, c_ctx, w_mod, b_mod, norm_g, ev_w_in, ev_conv_a_w, ev_conv_a_b, ev_ln_a_g, ev_ln_a_b,
              ev_conv_b_w, ev_w_out, od_w_in, od_q_norm_g, od_w_uq, od_kv_norm_g, od_w_ukv, od_w_pool, od_b_pool,
              od_s_pool, od_w_out, moe_wg, moe_bg, moe_we, moe_be, moe_w1, moe_w3, moe_w2, final_g):
    n_tok = x.shape[1]
    rows = n_tok // GRID_W
    pos_row = jnp.repeat(jnp.arange(rows, dtype=jnp.float32), GRID_W)
    pos_col = jnp.tile(jnp.arange(GRID_W, dtype=jnp.float32), rows)
    inv = ROPE_BASE ** (-jnp.arange(0, AXIS_DIM, 2, dtype=jnp.float32) / AXIS_DIM)
    ang = jnp.concatenate([pos_row[:, None] * inv, pos_col[:, None] * inv], axis=-1)
    cos, sin = jnp.cos(ang), jnp.sin(ang)
    s_lat = jax.nn.silu(c)[:, None, :]
    s_ctx = jax.nn.silu(c_ctx)[None, None, :]
    xc = ctx
    for i in range(DEPTH):
        j = i // 2
        last = i == DEPTH - 1
        odd = i % 2 == 1
        m_lat = jnp.split(s_lat @ w_mod[i] + b_mod[i], 6, axis=-1)
        hl = modulate(rms_norm(x, norm_g[i, 0]), m_lat[0], m_lat[1])
        if odd or not last:
            m_ctx = jnp.split(s_ctx @ w_mod[i] + b_mod[i], 6, axis=-1)
            hc = modulate(rms_norm(xc, norm_g[i, 0]), m_ctx[0], m_ctx[1])
        if odd:
            yl, yc = odd_mixer(hl, hc, cos, sin, od_w_in[j], od_q_norm_g[j], od_w_uq[j], od_kv_norm_g[j], od_w_ukv[j],
                               od_w_pool[j], od_b_pool[j], od_s_pool[j], od_w_out[j], not last)
        else:
            ev = (ev_w_in[j], ev_conv_a_w[j], ev_conv_a_b[j], ev_ln_a_g[j], ev_ln_a_b[j], ev_conv_b_w[j], ev_w_out[j])
            yl = even_mixer(hl, *ev)
            yc = None if last else even_mixer(hc, *ev)
        moe = (moe_wg[i], moe_bg[i], moe_we[i], moe_be[i], moe_w1[i], moe_w3[i], moe_w2[i])
        x = x + m_lat[2] * yl
        x = x + m_lat[5] * hier_moe(modulate(rms_norm(x, norm_g[i, 1]), m_lat[3], m_lat[4]), *moe)
        if not last:
            xc = xc + m_ctx[2] * yc
            xc = xc + m_ctx[5] * hier_moe(modulate(rms_norm(xc, norm_g[i, 1]), m_ctx[3], m_ctx[4]), *moe)
    return rms_norm(x, final_g)
```

```python
import functools

import jax
import jax.numpy as jnp
from jax import lax
from jax.experimental import pallas as pl
from jax.experimental.pallas import tpu as pltpu

F32 = jnp.float32
BF16 = jnp.bfloat16

D_MODEL = 1024
BATCH = 2
SEQ = 8192
DEPTH = 4
GRID_W = 64
CTX_LEN = 256
EPS = 1e-6
CONV_A_DIM = 512
CONV_A_WIDTH = 31
CONV_B_DIM = 512
CONV_B_WIDTH = 3
MLA_HEADS = 8
MLA_NOPE = 64
MLA_ROPE = 32
MLA_V = 64
Q_LORA = 384
KV_LORA = 256
MLA_SCALE = (MLA_NOPE + MLA_ROPE) ** -0.5
AXIS_DIM = MLA_ROPE // 2
AXIS_PAIRS = AXIS_DIM // 2
ROPE_BASE = 10000.0
POOL_WINDOWS = (2, 4, 8, 16)
POOL_GROUP = 128
POOL_DIM = POOL_GROUP * len(POOL_WINDOWS)
N_GROUPS = 4
EXPERTS_PER_GROUP = 4
N_EXPERTS = 16
D_EXPERT = 512

LANES = 128
SUBLANES = 8
VMEM_LIMIT = 48 * 1024 * 1024

ROW_TILE = 256
ROWS_PER_BATCH = CTX_LEN + SEQ
N_ROWS = BATCH * ROWS_PER_BATCH
TILES_PER_BATCH = ROWS_PER_BATCH // ROW_TILE
N_TILES = N_ROWS // ROW_TILE
HEAD_PAD = LANES
QK_W = MLA_HEADS * HEAD_PAD
ATT_W = MLA_HEADS * MLA_V
CONV_A_HALO = 16
SMALL_HALO = 8
ATT_TK = 768
MOE_TM = 768
ROUTE_W = LANES
GATE_LANE0 = N_GROUPS
MOD_ROWS = 8


def _mod_row(t):
    return jnp.where(t % TILES_PER_BATCH == 0, BATCH, t // TILES_PER_BATCH)


def _cparams(sem):
    return pltpu.CompilerParams(dimension_semantics=sem, vmem_limit_bytes=VMEM_LIMIT)


def _dot(a, b):
    return jnp.dot(a, b, preferred_element_type=F32)


def _sigmoid(x):
    return 1.0 / (1.0 + jnp.exp(-x))


def _silu(x):
    return x * _sigmoid(x)


def _norm_mod(x, g, shift, scale):
    y = x * lax.rsqrt(jnp.mean(x * x, axis=-1, keepdims=True) + EPS)
    return (y * g) * (1.0 + scale) + shift


def _mod_kernel(c_ref, w_ref, b_ref, o_ref):
    s = _silu(c_ref[...])
    o_ref[0] = jnp.dot(s, w_ref[0], preferred_element_type=F32,
                       precision=lax.Precision.HIGHEST) + b_ref[0]


def _modulation(cvec, w_mod, b_mod):
    nblk = 4
    bw = 6 * D_MODEL // nblk
    return pl.pallas_call(
        _mod_kernel,
        grid=(DEPTH, nblk),
        in_specs=[pl.BlockSpec((MOD_ROWS, D_MODEL), lambda i, j: (0, 0)),
                  pl.BlockSpec((1, D_MODEL, bw), lambda i, j: (i, 0, j)),
                  pl.BlockSpec((1, 1, bw), lambda i, j: (i, 0, j))],
        out_specs=pl.BlockSpec((1, MOD_ROWS, bw), lambda i, j: (i, 0, j)),
        out_shape=jax.ShapeDtypeStruct((DEPTH, MOD_ROWS, 6 * D_MODEL), F32),
        compiler_params=_cparams(("parallel", "parallel")),
        name="modulation",
    )(cvec, w_mod, b_mod.reshape(DEPTH, 1, 6 * D_MODEL))


def _route(h2, wr_hi_ref, wr_lo_ref, br_ref):
    hi = h2.astype(BF16)
    lo = (h2 - hi.astype(F32)).astype(BF16)
    logits = (_dot(hi, wr_hi_ref[...]) + _dot(hi, wr_lo_ref[...]) + _dot(lo, wr_hi_ref[...])
              + br_ref[...])
    lane = lax.broadcasted_iota(jnp.int32, logits.shape, 1)
    lane_f = lane.astype(F32)
    neg = -jnp.inf
    big = float(ROUTE_W)
    gl = jnp.where(lane < N_GROUPS, logits, neg)
    gmax = jnp.max(gl, axis=-1, keepdims=True)
    gidx = jnp.min(jnp.where(gl == gmax, lane_f, big), axis=-1, keepdims=True)
    g_w = 1.0 / jnp.sum(jnp.exp(gl - gmax), axis=-1, keepdims=True)
    egrp = ((lane - GATE_LANE0) // EXPERTS_PER_GROUP).astype(F32)
    in_g = (lane >= GATE_LANE0) & (lane < GATE_LANE0 + N_EXPERTS) & (egrp == gidx)
    el = jnp.where(in_g, logits, neg)
    v1 = jnp.max(el, axis=-1, keepdims=True)
    i1 = jnp.min(jnp.where(el == v1, lane_f, big), axis=-1, keepdims=True)
    el2 = jnp.where(lane_f == i1, neg, el)
    v2 = jnp.max(el2, axis=-1, keepdims=True)
    i2 = jnp.min(jnp.where(el2 == v2, lane_f, big), axis=-1, keepdims=True)
    e21 = jnp.exp(v2 - v1)
    w1 = 1.0 / (1.0 + e21)
    w2 = e21 * w1
    return jnp.where(lane_f == i1, w1 * g_w, jnp.where(lane_f == i2, w2 * g_w, 0.0))


def _residual_and_route(x, y, m, g1_ref, wr_hi_ref, wr_lo_ref, br_ref, xo_ref, h2_ref, gates_ref):
    x_new = x + m[2:3] * y
    xo_ref[...] = x_new
    h2 = _norm_mod(x_new, g1_ref[...], m[3:4], m[4:5])
    h2_ref[...] = h2.astype(BF16)
    gates_ref[...] = _route(h2, wr_hi_ref, wr_lo_ref, br_ref)


def _tile_spec(width):
    return pl.BlockSpec((ROW_TILE, width), lambda t: (t, 0))


def _full_spec(shape):
    nd = len(shape)
    return pl.BlockSpec(shape, lambda t: (0,) * nd)


def _mod_spec():
    return pl.BlockSpec((1, 6, D_MODEL), lambda t: (_mod_row(t), 0, 0))


def _halo_specs(width, halo):
    per_tile = ROW_TILE // halo
    last = N_ROWS // halo - 1
    prev = pl.BlockSpec((halo, width), lambda t: (jnp.maximum(t * per_tile - 1, 0), 0))
    nxt = pl.BlockSpec((halo, width), lambda t: (jnp.minimum((t + 1) * per_tile, last), 0))
    return prev, nxt


def _seq_flags():
    w = pl.program_id(0) % TILES_PER_BATCH
    is_start = (w == 0) | (w == 1)
    is_end = (w == 0) | (w == TILES_PER_BATCH - 1)
    seq_tile = jnp.maximum(w - 1, 0)
    return is_start, is_end, seq_tile, w == 0


_POST_OUT_SHAPES = (jax.ShapeDtypeStruct((N_ROWS, D_MODEL), F32),
                    jax.ShapeDtypeStruct((N_ROWS, D_MODEL), BF16),
                    jax.ShapeDtypeStruct((N_ROWS, ROUTE_W), F32))


def _post_out_specs():
    return (_tile_spec(D_MODEL), _tile_spec(D_MODEL), _tile_spec(ROUTE_W))


def _even_pre_kernel(x_ref, mod_ref, g_ref, w_ref, a_ref, gb_ref, gch_ref):
    m = mod_ref[0]
    h = _norm_mod(x_ref[...], g_ref[...], m[0:1], m[1:2])
    u = _dot(h.astype(BF16), w_ref[...])
    ca, cb = CONV_A_DIM, CONV_B_DIM
    a_ref[...] = u[:, 0:ca] * _sigmoid(u[:, ca:2 * ca])
    gb_ref[...] = u[:, 2 * ca:2 * ca + cb]
    gch_ref[...] = u[:, 2 * ca + cb:2 * ca + 2 * cb] * u[:, 2 * ca + 2 * cb:2 * ca + 3 * cb]


def _even_pre(x, mod_i, g0, w_in):
    n_in = w_in.shape[1]
    return pl.pallas_call(
        _even_pre_kernel,
        grid=(N_TILES,),
        in_specs=[_tile_spec(D_MODEL), _mod_spec(), _full_spec((1, D_MODEL)),
                  _full_spec((D_MODEL, n_in))],
        out_specs=(_tile_spec(CONV_A_DIM), _tile_spec(CONV_B_DIM), _tile_spec(CONV_B_DIM)),
        out_shape=(jax.ShapeDtypeStruct((N_ROWS, CONV_A_DIM), F32),
                   jax.ShapeDtypeStruct((N_ROWS, CONV_B_DIM), F32),
                   jax.ShapeDtypeStruct((N_ROWS, CONV_B_DIM), F32)),
        compiler_params=_cparams(("parallel",)),
        name="even_pre",
    )(x, mod_i, g0, w_in)


def _even_post_kernel(a_ref, ap_ref, an_ref, gch_ref, gp_ref, gn_ref, gb_ref,
                      cwa_ref, cba_ref, lng_ref, lnb_ref, cwb_ref, wout_ref,
                      x_ref, mod_ref, g1_ref, wr_hi_ref, wr_lo_ref, br_ref,
                      xo_ref, h2_ref, gates_ref, exta_ref, extb_ref):
    is_start, is_end, _, _ = _seq_flags()
    ha, hb = CONV_A_HALO, SMALL_HALO
    exta_ref[0:ha] = jnp.where(is_start, 0.0, ap_ref[...])
    exta_ref[ha:ha + ROW_TILE] = a_ref[...]
    exta_ref[ha + ROW_TILE:2 * ha + ROW_TILE] = jnp.where(is_end, 0.0, an_ref[...])
    extb_ref[0:hb] = jnp.where(is_start, 0.0, gp_ref[...])
    extb_ref[hb:hb + ROW_TILE] = gch_ref[...]
    extb_ref[hb + ROW_TILE:2 * hb + ROW_TILE] = jnp.where(is_end, 0.0, gn_ref[...])

    half_a = (CONV_A_WIDTH - 1) // 2
    acc = jnp.zeros((ROW_TILE, CONV_A_DIM), F32) + cba_ref[...]
    for k in range(CONV_A_WIDTH):
        off = ha - half_a + k
        acc = acc + cwa_ref[k:k + 1, :] * exta_ref[off:off + ROW_TILE, :]
    mu = jnp.mean(acc, axis=-1, keepdims=True)
    xc = acc - mu
    a = xc * lax.rsqrt(jnp.mean(xc * xc, axis=-1, keepdims=True) + EPS)
    a = _silu(a * lng_ref[...] + lnb_ref[...])

    half_b = (CONV_B_WIDTH - 1) // 2
    cb = jnp.zeros((ROW_TILE, CONV_B_DIM), F32)
    for k in range(CONV_B_WIDTH):
        off = hb - half_b + k
        cb = cb + cwb_ref[k:k + 1, :] * extb_ref[off:off + ROW_TILE, :]
    b = gb_ref[...] * cb

    y = (_dot(a.astype(BF16), wout_ref[0:CONV_A_DIM, :])
         + _dot(b.astype(BF16), wout_ref[CONV_A_DIM:CONV_A_DIM + CONV_B_DIM, :]))
    _residual_and_route(x_ref[...], y, mod_ref[0], g1_ref, wr_hi_ref, wr_lo_ref, br_ref,
                        xo_ref, h2_ref, gates_ref)


def _even_post(a_pre, gb, gch, cwa, cba, lng, lnb, cwb, w_out, x, mod_i, g1, wr_hi, wr_lo, br):
    ap_spec, an_spec = _halo_specs(CONV_A_DIM, CONV_A_HALO)
    gp_spec, gn_spec = _halo_specs(CONV_B_DIM, SMALL_HALO)
    return pl.pallas_call(
        _even_post_kernel,
        grid=(N_TILES,),
        in_specs=[_tile_spec(CONV_A_DIM), ap_spec, an_spec,
                  _tile_spec(CONV_B_DIM), gp_spec, gn_spec, _tile_spec(CONV_B_DIM),
                  _full_spec((CONV_A_WIDTH, CONV_A_DIM)), _full_spec((1, CONV_A_DIM)),
                  _full_spec((1, CONV_A_DIM)), _full_spec((1, CONV_A_DIM)),
                  _full_spec((CONV_B_WIDTH, CONV_B_DIM)), _full_spec((D_MODEL, D_MODEL)),
                  _tile_spec(D_MODEL), _mod_spec(), _full_spec((1, D_MODEL)),
                  _full_spec((D_MODEL, ROUTE_W)), _full_spec((D_MODEL, ROUTE_W)),
                  _full_spec((1, ROUTE_W))],
        out_specs=_post_out_specs(),
        out_shape=_POST_OUT_SHAPES,
        scratch_shapes=[pltpu.VMEM((ROW_TILE + 2 * CONV_A_HALO, CONV_A_DIM), F32),
                        pltpu.VMEM((ROW_TILE + 2 * SMALL_HALO, CONV_B_DIM), F32)],
        compiler_params=_cparams(("parallel",)),
        name="even_post",
    )(a_pre, a_pre, a_pre, gch, gch, gch, gb, cwa, cba, lng, lnb, cwb, w_out,
      x, mod_i, g1, wr_hi, wr_lo, br)


def _odd_pre_kernel(x_ref, mod_ref, g_ref, w_ref, gq_ref, wqm_ref, wqs_ref, gkv_ref, wk_ref, wv_ref,
                    cos_ref, sin_ref, q_ref, k_ref, v_ref, pool_ref):
    m = mod_ref[0]
    h = _norm_mod(x_ref[...], g_ref[...], m[0:1], m[1:2])
    u = _dot(h.astype(BF16), w_ref[...])
    o_kv = Q_LORA
    o_kr = o_kv + KV_LORA
    o_krs = o_kr + HEAD_PAD
    o_pool = o_krs + HEAD_PAD
    cos = cos_ref[...]
    sin = sin_ref[...]

    q_c = u[:, 0:o_kv]
    qn = (q_c * lax.rsqrt(jnp.mean(q_c * q_c, axis=-1, keepdims=True) + EPS) * gq_ref[...]).astype(BF16)
    q_main = _dot(qn, wqm_ref[...])
    q_swap = _dot(qn, wqs_ref[...])
    kv_c = u[:, o_kv:o_kr]
    kvn = (kv_c * lax.rsqrt(jnp.mean(kv_c * kv_c, axis=-1, keepdims=True) + EPS) * gkv_ref[...]).astype(BF16)
    k_nope = _dot(kvn, wk_ref[...])
    v_ref[...] = _dot(kvn, wv_ref[...]).astype(BF16)
    k_rope = u[:, o_kr:o_krs] * cos + u[:, o_krs:o_pool] * sin
    for hd in range(MLA_HEADS):
        sl = slice(hd * HEAD_PAD, (hd + 1) * HEAD_PAD)
        q_ref[:, sl] = ((q_main[:, sl] * cos + q_swap[:, sl] * sin) * MLA_SCALE).astype(BF16)
        k_ref[:, sl] = (k_nope[:, sl] + k_rope).astype(BF16)
    pool_ref[...] = u[:, o_pool:o_pool + POOL_DIM]


def _odd_pre(x, mod_i, g0, w_in, gq, wqm, wqs, gkv, wk, wv, cos_t, sin_t):
    n_in = w_in.shape[1]
    return pl.pallas_call(
        _odd_pre_kernel,
        grid=(N_TILES,),
        in_specs=[_tile_spec(D_MODEL), _mod_spec(), _full_spec((1, D_MODEL)),
                  _full_spec((D_MODEL, n_in)), _full_spec((1, Q_LORA)),
                  _full_spec((Q_LORA, QK_W)), _full_spec((Q_LORA, QK_W)),
                  _full_spec((1, KV_LORA)), _full_spec((KV_LORA, QK_W)), _full_spec((KV_LORA, QK_W)),
                  _tile_spec(HEAD_PAD), _tile_spec(HEAD_PAD)],
        out_specs=(_tile_spec(QK_W), _tile_spec(QK_W), _tile_spec(QK_W), _tile_spec(POOL_DIM)),
        out_shape=(jax.ShapeDtypeStruct((N_ROWS, QK_W), BF16),
                   jax.ShapeDtypeStruct((N_ROWS, QK_W), BF16),
                   jax.ShapeDtypeStruct((N_ROWS, QK_W), BF16),
                   jax.ShapeDtypeStruct((N_ROWS, POOL_DIM), F32)),
        compiler_params=_cparams(("parallel",)),
        name="odd_pre",
    )(x, mod_i, g0, w_in, gq, wqm, wqs, gkv, wk, wv, cos_t, sin_t)


def _attn_kernel(q_ref, k_ref, v_ref, o_ref, m_ref, l_ref, acc_ref):
    qi = pl.program_id(1)
    kj = pl.program_id(2)
    nk = pl.num_programs(2)

    @pl.when(kj == 0)
    def _():
        m_ref[...] = jnp.full(m_ref.shape, -jnp.inf, F32)
        l_ref[...] = jnp.zeros(l_ref.shape, F32)
        acc_ref[...] = jnp.zeros(acc_ref.shape, F32)

    def step(ctx_only):
        for hd in range(MLA_HEADS):
            sl = slice(hd * HEAD_PAD, (hd + 1) * HEAD_PAD)
            s = lax.dot_general(q_ref[:, sl], k_ref[:, sl], (((1,), (1,)), ((), ())),
                                preferred_element_type=F32)
            if ctx_only:
                col = lax.broadcasted_iota(jnp.int32, s.shape, 1)
                s = jnp.where(col < CTX_LEN, s, -jnp.inf)
            m_old = m_ref[hd]
            m_new = jnp.maximum(m_old, jnp.max(s, axis=-1, keepdims=True))
            alpha = jnp.exp(m_old - m_new)
            p = jnp.exp(s - m_new[:, 0:1])
            l_ref[hd] = alpha * l_ref[hd] + jnp.sum(p, axis=-1, keepdims=True)
            acc_ref[hd] = alpha * acc_ref[hd] + _dot(p.astype(BF16), v_ref[:, sl])
            m_ref[hd] = m_new

    @pl.when(qi != 0)
    def _():
        step(False)

    @pl.when((qi == 0) & (kj == 0))
    def _():
        step(True)

    @pl.when(kj == nk - 1)
    def _():
        for hd in range(MLA_HEADS):
            out = acc_ref[hd] / l_ref[hd]
            o_ref[:, hd * MLA_V:(hd + 1) * MLA_V] = out[:, 0:MLA_V]


def _attention(q, k, v):
    nk = ROWS_PER_BATCH // ATT_TK
    return pl.pallas_call(
        _attn_kernel,
        grid=(BATCH, TILES_PER_BATCH, nk),
        in_specs=[pl.BlockSpec((ROW_TILE, QK_W), lambda b, i, j: (b * TILES_PER_BATCH + i, 0)),
                  pl.BlockSpec((ATT_TK, QK_W), lambda b, i, j: (b * nk + j, 0)),
                  pl.BlockSpec((ATT_TK, QK_W), lambda b, i, j: (b * nk + j, 0))],
        out_specs=pl.BlockSpec((ROW_TILE, ATT_W), lambda b, i, j: (b * TILES_PER_BATCH + i, 0)),
        out_shape=jax.ShapeDtypeStruct((N_ROWS, ATT_W), F32),
        scratch_shapes=[pltpu.VMEM((MLA_HEADS, ROW_TILE, HEAD_PAD), F32),
                        pltpu.VMEM((MLA_HEADS, ROW_TILE, HEAD_PAD), F32),
                        pltpu.VMEM((MLA_HEADS, ROW_TILE, HEAD_PAD), F32)],
        compiler_params=_cparams(("parallel", "parallel", "arbitrary")),
        name="attention",
    )(q, k, v)


def _odd_post_kernel(att_ref, p_ref, pp_ref, pn_ref, wp_ref, bp_ref, sp_ref, wout_ref,
                     x_ref, mod_ref, g1_ref, wr_hi_ref, wr_lo_ref, br_ref,
                     xo_ref, h2_ref, gates_ref, ext_ref):
    is_start, is_end, seq_tile, is_ctx = _seq_flags()
    hp = SMALL_HALO
    ext_ref[0:hp] = jnp.where(is_start, 0.0, pp_ref[...])
    ext_ref[hp:hp + ROW_TILE] = p_ref[...]
    ext_ref[hp + ROW_TILE:2 * hp + ROW_TILE] = jnp.where(is_end, 0.0, pn_ref[...])
    seq_len = jnp.where(is_ctx, CTX_LEN, SEQ)
    pos = seq_tile * ROW_TILE + lax.broadcasted_iota(jnp.int32, (ROW_TILE, 1), 0)

    y = _dot(att_ref[...].astype(BF16), wout_ref[0:ATT_W, :])
    for g, w in enumerate(POOL_WINDOWS):
        sl = slice(g * POOL_GROUP, (g + 1) * POOL_GROUP)
        ssum = jnp.zeros((ROW_TILE, POOL_GROUP), F32)
        for d in range(-(w // 2), w - w // 2):
            ssum = ssum + ext_ref[hp + d:hp + d + ROW_TILE, sl]
        lo = jnp.maximum(pos - w // 2, 0)
        hi = jnp.minimum(pos - w // 2 + w, seq_len)
        cnt = (hi - lo).astype(F32)
        pm = ssum / cnt - p_ref[:, sl]
        yg = (_dot(pm.astype(BF16), wp_ref[g]) + bp_ref[g:g + 1, :]) * sp_ref[:, sl]
        y = y + _dot(yg.astype(BF16), wout_ref[ATT_W + g * POOL_GROUP:ATT_W + (g + 1) * POOL_GROUP, :])
    _residual_and_route(x_ref[...], y, mod_ref[0], g1_ref, wr_hi_ref, wr_lo_ref, br_ref,
                        xo_ref, h2_ref, gates_ref)


def _odd_post(att, pool_in, w_pool, b_pool, s_pool, w_out, x, mod_i, g1, wr_hi, wr_lo, br):
    pp_spec, pn_spec = _halo_specs(POOL_DIM, SMALL_HALO)
    ng = len(POOL_WINDOWS)
    return pl.pallas_call(
        _odd_post_kernel,
        grid=(N_TILES,),
        in_specs=[_tile_spec(ATT_W), _tile_spec(POOL_DIM), pp_spec, pn_spec,
                  _full_spec((ng, POOL_GROUP, POOL_GROUP)), _full_spec((ng, POOL_GROUP)),
                  _full_spec((1, POOL_DIM)), _full_spec((D_MODEL, D_MODEL)),
                  _tile_spec(D_MODEL), _mod_spec(), _full_spec((1, D_MODEL)),
                  _full_spec((D_MODEL, ROUTE_W)), _full_spec((D_MODEL, ROUTE_W)),
                  _full_spec((1, ROUTE_W))],
        out_specs=_post_out_specs(),
        out_shape=_POST_OUT_SHAPES,
        scratch_shapes=[pltpu.VMEM((ROW_TILE + 2 * SMALL_HALO, POOL_DIM), F32)],
        compiler_params=_cparams(("parallel",)),
        name="odd_post",
    )(att, pool_in, pool_in, pool_in, w_pool, b_pool, s_pool, w_out, x, mod_i, g1, wr_hi, wr_lo, br)


def _moe_dense_kernel(h2_ref, gates_ref, w1_ref, w3_ref, w2_ref, x_ref, mod_ref, o_ref, acc_ref):
    t = pl.program_id(0)
    e = pl.program_id(1)

    @pl.when(e == 0)
    def _():
        acc_ref[...] = jnp.zeros(acc_ref.shape, F32)

    h = h2_ref[...]
    hid = _silu(_dot(h, w1_ref[0])) * _dot(h, w3_ref[0])
    y = _dot(hid.astype(BF16), w2_ref[0])
    lane = lax.broadcasted_iota(jnp.int32, gates_ref.shape, 1)
    g = jnp.sum(jnp.where(lane == e + GATE_LANE0, gates_ref[...], 0.0), axis=-1, keepdims=True)
    acc_ref[...] += g * y

    @pl.when(e == N_EXPERTS - 1)
    def _():
        sub = MOE_TM // ROW_TILE
        for s in range(sub):
            row = _mod_row(t * sub + s)
            m = mod_ref[pl.ds(row, 1)]
            rs = slice(s * ROW_TILE, (s + 1) * ROW_TILE)
            o_ref[rs, :] = x_ref[rs, :] + m[0, 5:6] * acc_ref[rs, :]


def _moe_dense(h2, gates, w1, w3, w2, x, mod_i):
    nt = N_ROWS // MOE_TM
    return pl.pallas_call(
        _moe_dense_kernel,
        grid=(nt, N_EXPERTS),
        in_specs=[pl.BlockSpec((MOE_TM, D_MODEL), lambda t, e: (t, 0)),
                  pl.BlockSpec((MOE_TM, ROUTE_W), lambda t, e: (t, 0)),
                  pl.BlockSpec((1, D_MODEL, D_EXPERT), lambda t, e: (e, 0, 0)),
                  pl.BlockSpec((1, D_MODEL, D_EXPERT), lambda t, e: (e, 0, 0)),
                  pl.BlockSpec((1, D_EXPERT, D_MODEL), lambda t, e: (e, 0, 0)),
                  pl.BlockSpec((MOE_TM, D_MODEL), lambda t, e: (t, 0)),
                  pl.BlockSpec((MOD_ROWS, 6, D_MODEL), lambda t, e: (0, 0, 0))],
        out_specs=pl.BlockSpec((MOE_TM, D_MODEL), lambda t, e: (t, 0)),
        out_shape=jax.ShapeDtypeStruct((N_ROWS, D_MODEL), F32),
        scratch_shapes=[pltpu.VMEM((MOE_TM, D_MODEL), F32)],
        compiler_params=_cparams(("parallel", "arbitrary")),
        name="moe_dense",
    )(h2, gates, w1, w3, w2, x, mod_i)


def _final_kernel(x_ref, g_ref, o_ref):
    x = x_ref[...]
    o_ref[0] = x * lax.rsqrt(jnp.mean(x * x, axis=-1, keepdims=True) + EPS) * g_ref[...]


def _final_norm(x, g):
    lat_tiles = SEQ // ROW_TILE
    return pl.pallas_call(
        _final_kernel,
        grid=(BATCH, lat_tiles),
        in_specs=[pl.BlockSpec((ROW_TILE, D_MODEL), lambda b, i: (b * TILES_PER_BATCH + 1 + i, 0)),
                  pl.BlockSpec((1, D_MODEL), lambda b, i: (0, 0))],
        out_specs=pl.BlockSpec((1, ROW_TILE, D_MODEL), lambda b, i: (b, i, 0)),
        out_shape=jax.ShapeDtypeStruct((BATCH, SEQ, D_MODEL), F32),
        compiler_params=_cparams(("parallel", "parallel")),
        name="final_norm",
    )(x, g)


_SWAP32 = tuple(list(range(8, 16)) + list(range(0, 8)) + list(range(24, 32)) + list(range(16, 24)))


def _rope_tables():
    rows = SEQ // GRID_W
    pos_row = jnp.repeat(jnp.arange(rows, dtype=F32), GRID_W)
    pos_col = jnp.tile(jnp.arange(GRID_W, dtype=F32), rows)
    inv = ROPE_BASE ** (-jnp.arange(0, AXIS_DIM, 2, dtype=F32) / AXIS_DIM)
    ang = jnp.concatenate([pos_row[:, None] * inv, pos_col[:, None] * inv], axis=-1)
    cos, sin = jnp.cos(ang), jnp.sin(ang)
    p = AXIS_PAIRS
    c32 = jnp.concatenate([cos[:, :p], cos[:, :p], cos[:, p:], cos[:, p:]], axis=-1)
    s32 = jnp.concatenate([-sin[:, :p], sin[:, :p], -sin[:, p:], sin[:, p:]], axis=-1)
    pad = HEAD_PAD - MLA_NOPE - MLA_ROPE
    cos_lat = jnp.concatenate([jnp.ones((SEQ, MLA_NOPE), F32), c32, jnp.zeros((SEQ, pad), F32)], axis=-1)
    sin_lat = jnp.concatenate([jnp.zeros((SEQ, MLA_NOPE), F32), s32, jnp.zeros((SEQ, pad), F32)], axis=-1)
    cos_ctx = jnp.concatenate([jnp.ones((CTX_LEN, MLA_NOPE + MLA_ROPE), F32),
                               jnp.zeros((CTX_LEN, pad), F32)], axis=-1)
    sin_ctx = jnp.zeros((CTX_LEN, HEAD_PAD), F32)
    cos_t = jnp.concatenate([cos_ctx, cos_lat] * BATCH, axis=0)
    sin_t = jnp.concatenate([sin_ctx, sin_lat] * BATCH, axis=0)
    return cos_t, sin_t


def _odd_weights(w_in, w_uq, w_ukv):
    swap = jnp.array(_SWAP32)
    o_kv, o_kr, o_pool = Q_LORA, Q_LORA + KV_LORA, Q_LORA + KV_LORA + MLA_ROPE
    w_kr = w_in[:, o_kr:o_pool]
    pad_l = jnp.zeros((D_MODEL, MLA_NOPE), F32)
    pad_r = jnp.zeros((D_MODEL, HEAD_PAD - MLA_NOPE - MLA_ROPE), F32)
    w_in2 = jnp.concatenate([w_in[:, :o_kr], pad_l, w_kr, pad_r, pad_l, w_kr[:, swap], pad_r,
                             w_in[:, o_pool:]], axis=-1).astype(BF16)
    wq = w_uq.reshape(Q_LORA, MLA_HEADS, MLA_NOPE + MLA_ROPE)
    zq = jnp.zeros((Q_LORA, MLA_HEADS, HEAD_PAD - MLA_NOPE - MLA_ROPE), F32)
    wq_main = jnp.concatenate([wq, zq], axis=-1).reshape(Q_LORA, QK_W).astype(BF16)
    wq_swap = jnp.concatenate([jnp.zeros((Q_LORA, MLA_HEADS, MLA_NOPE), F32),
                               wq[:, :, MLA_NOPE:][:, :, swap], zq], axis=-1).reshape(Q_LORA, QK_W).astype(BF16)
    wkv = w_ukv.reshape(KV_LORA, MLA_HEADS, MLA_NOPE + MLA_V)
    zk = jnp.zeros((KV_LORA, MLA_HEADS, HEAD_PAD - MLA_NOPE), F32)
    wk = jnp.concatenate([wkv[:, :, :MLA_NOPE], zk], axis=-1).reshape(KV_LORA, QK_W).astype(BF16)
    zv = jnp.zeros((KV_LORA, MLA_HEADS, HEAD_PAD - MLA_V), F32)
    wv = jnp.concatenate([wkv[:, :, MLA_NOPE:], zv], axis=-1).reshape(KV_LORA, QK_W).astype(BF16)
    return w_in2, wq_main, wq_swap, wk, wv


def kernel(x, c, ctx, c_ctx, w_mod, b_mod, norm_g, ev_w_in, ev_conv_a_w, ev_conv_a_b, ev_ln_a_g, ev_ln_a_b,
           ev_conv_b_w, ev_w_out, od_w_in, od_q_norm_g, od_w_uq, od_kv_norm_g, od_w_ukv, od_w_pool, od_b_pool,
           od_s_pool, od_w_out, moe_wg, moe_bg, moe_we, moe_be, moe_w1, moe_w3, moe_w2, final_g):
    assert x.shape == (BATCH, SEQ, D_MODEL) and ctx.shape == (BATCH, CTX_LEN, D_MODEL)
    xs = jnp.concatenate([ctx[0], x[0], ctx[1], x[1]], axis=0)
    cvec = jnp.concatenate([c, c_ctx[None, :], jnp.zeros((MOD_ROWS - BATCH - 1, D_MODEL), F32)], axis=0)
    mod = _modulation(cvec, w_mod, b_mod).reshape(DEPTH, MOD_ROWS, 6, D_MODEL)
    cos_t, sin_t = _rope_tables()

    for i in range(DEPTH):
        j = i // 2
        mod_i = mod[i]
        g0 = norm_g[i, 0][None, :]
        g1 = norm_g[i, 1][None, :]
        wr = jnp.concatenate([moe_wg[i], moe_we[i],
                              jnp.zeros((D_MODEL, ROUTE_W - N_GROUPS - N_EXPERTS), F32)], axis=-1)
        wr_hi = wr.astype(BF16)
        wr_lo = (wr - wr_hi.astype(F32)).astype(BF16)
        br = jnp.concatenate([moe_bg[i], moe_be[i],
                              jnp.zeros((ROUTE_W - N_GROUPS - N_EXPERTS,), F32)])[None, :]
        if i % 2 == 0:
            a_pre, gb, gch = _even_pre(xs, mod_i, g0, ev_w_in[j].astype(BF16))
            xs, h2, gates = _even_post(a_pre, gb, gch, ev_conv_a_w[j], ev_conv_a_b[j][None, :],
                                       ev_ln_a_g[j][None, :], ev_ln_a_b[j][None, :], ev_conv_b_w[j],
                                       ev_w_out[j].astype(BF16), xs, mod_i, g1, wr_hi, wr_lo, br)
        else:
            w_in2, wq_main, wq_swap, wk, wv = _odd_weights(od_w_in[j], od_w_uq[j], od_w_ukv[j])
            q, k, v, pool_in = _odd_pre(xs, mod_i, g0, w_in2, od_q_norm_g[j][None, :], wq_main, wq_swap,
                                        od_kv_norm_g[j][None, :], wk, wv, cos_t, sin_t)
            att = _attention(q, k, v)
            xs, h2, gates = _odd_post(att, pool_in, od_w_pool[j].astype(BF16), od_b_pool[j],
                                      od_s_pool[j][None, :], od_w_out[j].astype(BF16), xs, mod_i, g1,
                                      wr_hi, wr_lo, br)
        xs = _moe_dense(h2, gates, moe_w1[i].astype(BF16), moe_w3[i].astype(BF16),
                        moe_w2[i].astype(BF16), xs, mod_i)
    return _final_norm(xs, final_g[None, :])
```

```python
import functools

import jax
import jax.numpy as jnp
from jax import lax
from jax.experimental import pallas as pl
from jax.experimental.pallas import tpu as pltpu

F32 = jnp.float32
BF16 = jnp.bfloat16

D_MODEL = 1024
BATCH = 2
SEQ = 8192
DEPTH = 4
GRID_W = 64
CTX_LEN = 256
EPS = 1e-6
CONV_A_DIM = 512
CONV_A_WIDTH = 31
CONV_B_DIM = 512
CONV_B_WIDTH = 3
MLA_HEADS = 8
MLA_NOPE = 64
MLA_ROPE = 32
MLA_V = 64
Q_LORA = 384
KV_LORA = 256
MLA_SCALE = (MLA_NOPE + MLA_ROPE) ** -0.5
AXIS_DIM = MLA_ROPE // 2
AXIS_PAIRS = AXIS_DIM // 2
ROPE_BASE = 10000.0
POOL_WINDOWS = (2, 4, 8, 16)
POOL_GROUP = 128
POOL_DIM = POOL_GROUP * len(POOL_WINDOWS)
N_GROUPS = 4
EXPERTS_PER_GROUP = 4
N_EXPERTS = 16
D_EXPERT = 512

LANES = 128
SUBLANES = 8
VMEM_LIMIT = 48 * 1024 * 1024

ROW_TILE = 256
ROWS_PER_BATCH = CTX_LEN + SEQ
N_ROWS = BATCH * ROWS_PER_BATCH
TILES_PER_BATCH = ROWS_PER_BATCH // ROW_TILE
N_TILES = N_ROWS // ROW_TILE
HEAD_PAD = LANES
QK_W = MLA_HEADS * HEAD_PAD
ATT_W = MLA_HEADS * MLA_V
V_AUG = MLA_V + 16
VT_W = MLA_HEADS * V_AUG
CONV_A_HALO = 16
SMALL_HALO = 8
ATT_TK = 2816
ATT_CHUNK = 128
LOG2E = 1.4426950408889634
MOE_TM = 768
ROUTE_W = LANES
GATE_LANE0 = N_GROUPS
MOD_ROWS = 8


def _mod_row(t):
    return jnp.where(t % TILES_PER_BATCH == 0, BATCH, t // TILES_PER_BATCH)


def _cparams(sem):
    return pltpu.CompilerParams(dimension_semantics=sem, vmem_limit_bytes=VMEM_LIMIT)


def _dot(a, b):
    return jnp.dot(a, b, preferred_element_type=F32)


def _sigmoid(x):
    return 1.0 / (1.0 + jnp.exp(-x))


def _silu(x):
    return x * _sigmoid(x)


def _norm_mod(x, g, shift, scale):
    y = x * lax.rsqrt(jnp.mean(x * x, axis=-1, keepdims=True) + EPS)
    return (y * g) * (1.0 + scale) + shift


def _mod_kernel(c_ref, w_ref, b_ref, o_ref):
    s = _silu(c_ref[...])
    o_ref[0] = jnp.dot(s, w_ref[0], preferred_element_type=F32,
                       precision=lax.Precision.HIGHEST) + b_ref[0]


def _modulation(cvec, w_mod, b_mod):
    nblk = 4
    bw = 6 * D_MODEL // nblk
    return pl.pallas_call(
        _mod_kernel,
        grid=(DEPTH, nblk),
        in_specs=[pl.BlockSpec((MOD_ROWS, D_MODEL), lambda i, j: (0, 0)),
                  pl.BlockSpec((1, D_MODEL, bw), lambda i, j: (i, 0, j)),
                  pl.BlockSpec((1, 1, bw), lambda i, j: (i, 0, j))],
        out_specs=pl.BlockSpec((1, MOD_ROWS, bw), lambda i, j: (i, 0, j)),
        out_shape=jax.ShapeDtypeStruct((DEPTH, MOD_ROWS, 6 * D_MODEL), F32),
        compiler_params=_cparams(("parallel", "parallel")),
        name="modulation",
    )(cvec, w_mod, b_mod.reshape(DEPTH, 1, 6 * D_MODEL))


def _route(h2, wr_hi_ref, wr_lo_ref, br_ref):
    hi = h2.astype(BF16)
    lo = (h2 - hi.astype(F32)).astype(BF16)
    logits = (_dot(hi, wr_hi_ref[...]) + _dot(hi, wr_lo_ref[...]) + _dot(lo, wr_hi_ref[...])
              + br_ref[...])
    lane = lax.broadcasted_iota(jnp.int32, logits.shape, 1)
    lane_f = lane.astype(F32)
    neg = -jnp.inf
    big = float(ROUTE_W)
    gl = jnp.where(lane < N_GROUPS, logits, neg)
    gmax = jnp.max(gl, axis=-1, keepdims=True)
    gidx = jnp.min(jnp.where(gl == gmax, lane_f, big), axis=-1, keepdims=True)
    g_w = 1.0 / jnp.sum(jnp.exp(gl - gmax), axis=-1, keepdims=True)
    egrp = ((lane - GATE_LANE0) // EXPERTS_PER_GROUP).astype(F32)
    in_g = (lane >= GATE_LANE0) & (lane < GATE_LANE0 + N_EXPERTS) & (egrp == gidx)
    el = jnp.where(in_g, logits, neg)
    v1 = jnp.max(el, axis=-1, keepdims=True)
    i1 = jnp.min(jnp.where(el == v1, lane_f, big), axis=-1, keepdims=True)
    el2 = jnp.where(lane_f == i1, neg, el)
    v2 = jnp.max(el2, axis=-1, keepdims=True)
    i2 = jnp.min(jnp.where(el2 == v2, lane_f, big), axis=-1, keepdims=True)
    e21 = jnp.exp(v2 - v1)
    w1 = 1.0 / (1.0 + e21)
    w2 = e21 * w1
    return jnp.where(lane_f == i1, w1 * g_w, jnp.where(lane_f == i2, w2 * g_w, 0.0))


def _residual_and_route(x, y, m, g1_ref, wr_hi_ref, wr_lo_ref, br_ref, xo_ref, h2_ref, gates_ref):
    x_new = x + m[2:3] * y
    xo_ref[...] = x_new
    h2 = _norm_mod(x_new, g1_ref[...], m[3:4], m[4:5])
    h2_ref[...] = h2.astype(BF16)
    gates_ref[...] = _route(h2, wr_hi_ref, wr_lo_ref, br_ref)


def _tile_spec(width):
    return pl.BlockSpec((ROW_TILE, width), lambda t: (t, 0))


def _full_spec(shape):
    nd = len(shape)
    return pl.BlockSpec(shape, lambda t: (0,) * nd)


def _mod_spec():
    return pl.BlockSpec((1, 6, D_MODEL), lambda t: (_mod_row(t), 0, 0))


def _halo_specs(width, halo):
    per_tile = ROW_TILE // halo
    last = N_ROWS // halo - 1
    prev = pl.BlockSpec((halo, width), lambda t: (jnp.maximum(t * per_tile - 1, 0), 0))
    nxt = pl.BlockSpec((halo, width), lambda t: (jnp.minimum((t + 1) * per_tile, last), 0))
    return prev, nxt


def _seq_flags():
    w = pl.program_id(0) % TILES_PER_BATCH
    is_start = (w == 0) | (w == 1)
    is_end = (w == 0) | (w == TILES_PER_BATCH - 1)
    seq_tile = jnp.maximum(w - 1, 0)
    return is_start, is_end, seq_tile, w == 0


_POST_OUT_SHAPES = (jax.ShapeDtypeStruct((N_ROWS, D_MODEL), F32),
                    jax.ShapeDtypeStruct((N_ROWS, D_MODEL), BF16),
                    jax.ShapeDtypeStruct((N_ROWS, ROUTE_W), F32))


def _post_out_specs():
    return (_tile_spec(D_MODEL), _tile_spec(D_MODEL), _tile_spec(ROUTE_W))


def _even_pre_kernel(x_ref, mod_ref, g_ref, w_ref, a_ref, gb_ref, gch_ref):
    m = mod_ref[0]
    h = _norm_mod(x_ref[...], g_ref[...], m[0:1], m[1:2])
    u = _dot(h.astype(BF16), w_ref[...])
    ca, cb = CONV_A_DIM, CONV_B_DIM
    a_ref[...] = u[:, 0:ca] * _sigmoid(u[:, ca:2 * ca])
    gb_ref[...] = u[:, 2 * ca:2 * ca + cb]
    gch_ref[...] = u[:, 2 * ca + cb:2 * ca + 2 * cb] * u[:, 2 * ca + 2 * cb:2 * ca + 3 * cb]


def _even_pre(x, mod_i, g0, w_in):
    n_in = w_in.shape[1]
    return pl.pallas_call(
        _even_pre_kernel,
        grid=(N_TILES,),
        in_specs=[_tile_spec(D_MODEL), _mod_spec(), _full_spec((1, D_MODEL)),
                  _full_spec((D_MODEL, n_in))],
        out_specs=(_tile_spec(CONV_A_DIM), _tile_spec(CONV_B_DIM), _tile_spec(CONV_B_DIM)),
        out_shape=(jax.ShapeDtypeStruct((N_ROWS, CONV_A_DIM), F32),
                   jax.ShapeDtypeStruct((N_ROWS, CONV_B_DIM), F32),
                   jax.ShapeDtypeStruct((N_ROWS, CONV_B_DIM), F32)),
        compiler_params=_cparams(("parallel",)),
        name="even_pre",
    )(x, mod_i, g0, w_in)


def _even_post_kernel(a_ref, ap_ref, an_ref, gch_ref, gp_ref, gn_ref, gb_ref,
                      cwa_ref, cba_ref, lng_ref, lnb_ref, cwb_ref, wout_ref,
                      x_ref, mod_ref, g1_ref, wr_hi_ref, wr_lo_ref, br_ref,
                      xo_ref, h2_ref, gates_ref, exta_ref, extb_ref):
    is_start, is_end, _, _ = _seq_flags()
    ha, hb = CONV_A_HALO, SMALL_HALO
    exta_ref[0:ha] = jnp.where(is_start, 0.0, ap_ref[...])
    exta_ref[ha:ha + ROW_TILE] = a_ref[...]
    exta_ref[ha + ROW_TILE:2 * ha + ROW_TILE] = jnp.where(is_end, 0.0, an_ref[...])
    extb_ref[0:hb] = jnp.where(is_start, 0.0, gp_ref[...])
    extb_ref[hb:hb + ROW_TILE] = gch_ref[...]
    extb_ref[hb + ROW_TILE:2 * hb + ROW_TILE] = jnp.where(is_end, 0.0, gn_ref[...])

    half_a = (CONV_A_WIDTH - 1) // 2
    acc = jnp.zeros((ROW_TILE, CONV_A_DIM), F32) + cba_ref[...]
    for k in range(CONV_A_WIDTH):
        off = ha - half_a + k
        acc = acc + cwa_ref[k:k + 1, :] * exta_ref[off:off + ROW_TILE, :]
    mu = jnp.mean(acc, axis=-1, keepdims=True)
    xc = acc - mu
    a = xc * lax.rsqrt(jnp.mean(xc * xc, axis=-1, keepdims=True) + EPS)
    a = _silu(a * lng_ref[...] + lnb_ref[...])

    half_b = (CONV_B_WIDTH - 1) // 2
    cb = jnp.zeros((ROW_TILE, CONV_B_DIM), F32)
    for k in range(CONV_B_WIDTH):
        off = hb - half_b + k
        cb = cb + cwb_ref[k:k + 1, :] * extb_ref[off:off + ROW_TILE, :]
    b = gb_ref[...] * cb

    y = (_dot(a.astype(BF16), wout_ref[0:CONV_A_DIM, :])
         + _dot(b.astype(BF16), wout_ref[CONV_A_DIM:CONV_A_DIM + CONV_B_DIM, :]))
    _residual_and_route(x_ref[...], y, mod_ref[0], g1_ref, wr_hi_ref, wr_lo_ref, br_ref,
                        xo_ref, h2_ref, gates_ref)


def _even_post(a_pre, gb, gch, cwa, cba, lng, lnb, cwb, w_out, x, mod_i, g1, wr_hi, wr_lo, br):
    ap_spec, an_spec = _halo_specs(CONV_A_DIM, CONV_A_HALO)
    gp_spec, gn_spec = _halo_specs(CONV_B_DIM, SMALL_HALO)
    return pl.pallas_call(
        _even_post_kernel,
        grid=(N_TILES,),
        in_specs=[_tile_spec(CONV_A_DIM), ap_spec, an_spec,
                  _tile_spec(CONV_B_DIM), gp_spec, gn_spec, _tile_spec(CONV_B_DIM),
                  _full_spec((CONV_A_WIDTH, CONV_A_DIM)), _full_spec((1, CONV_A_DIM)),
                  _full_spec((1, CONV_A_DIM)), _full_spec((1, CONV_A_DIM)),
                  _full_spec((CONV_B_WIDTH, CONV_B_DIM)), _full_spec((D_MODEL, D_MODEL)),
                  _tile_spec(D_MODEL), _mod_spec(), _full_spec((1, D_MODEL)),
                  _full_spec((D_MODEL, ROUTE_W)), _full_spec((D_MODEL, ROUTE_W)),
                  _full_spec((1, ROUTE_W))],
        out_specs=_post_out_specs(),
        out_shape=_POST_OUT_SHAPES,
        scratch_shapes=[pltpu.VMEM((ROW_TILE + 2 * CONV_A_HALO, CONV_A_DIM), F32),
                        pltpu.VMEM((ROW_TILE + 2 * SMALL_HALO, CONV_B_DIM), F32)],
        compiler_params=_cparams(("parallel",)),
        name="even_post",
    )(a_pre, a_pre, a_pre, gch, gch, gch, gb, cwa, cba, lng, lnb, cwb, w_out,
      x, mod_i, g1, wr_hi, wr_lo, br)


def _odd_pre_kernel(x_ref, mod_ref, g_ref, w_ref, gq_ref, wqm_ref, wqs_ref, gkv_ref, wk_ref, wv_ref,
                    vaug_ref, cos_ref, sin_ref, qt_ref, k_ref, vt_ref, pool_ref):
    m = mod_ref[0]
    h = _norm_mod(x_ref[...], g_ref[...], m[0:1], m[1:2])
    u = _dot(h.astype(BF16), w_ref[...])
    o_kv = Q_LORA
    o_kr = o_kv + KV_LORA
    o_krs = o_kr + HEAD_PAD
    o_pool = o_krs + HEAD_PAD
    cos = cos_ref[...]
    sin = sin_ref[...]

    q_c = u[:, 0:o_kv]
    qn = (q_c * lax.rsqrt(jnp.mean(q_c * q_c, axis=-1, keepdims=True) + EPS) * gq_ref[...]).astype(BF16)
    q_main = _dot(qn, wqm_ref[...])
    q_swap = _dot(qn, wqs_ref[...])
    kv_c = u[:, o_kv:o_kr]
    kvn = (kv_c * lax.rsqrt(jnp.mean(kv_c * kv_c, axis=-1, keepdims=True) + EPS) * gkv_ref[...]).astype(BF16)
    k_nope = _dot(kvn, wk_ref[...])
    v = _dot(kvn, wv_ref[...]) + vaug_ref[...]
    for cc in range(ROW_TILE // ATT_CHUNK):
        vt_ref[cc] = v[cc * ATT_CHUNK:(cc + 1) * ATT_CHUNK, :].T.astype(BF16)
    k_rope = u[:, o_kr:o_krs] * cos + u[:, o_krs:o_pool] * sin
    for hd in range(MLA_HEADS):
        sl = slice(hd * HEAD_PAD, (hd + 1) * HEAD_PAD)
        qh = (q_main[:, sl] * cos + q_swap[:, sl] * sin) * (MLA_SCALE * LOG2E)
        qt_ref[0, sl, :] = qh.T.astype(BF16)
        k_ref[:, sl] = (k_nope[:, sl] + k_rope).astype(BF16)
    pool_ref[...] = u[:, o_pool:o_pool + POOL_DIM]


def _odd_pre(x, mod_i, g0, w_in, gq, wqm, wqs, gkv, wk, wv, cos_t, sin_t):
    n_in = w_in.shape[1]
    vaug = jnp.zeros((MLA_HEADS, V_AUG), F32).at[:, MLA_V].set(1.0).reshape(1, VT_W)
    return pl.pallas_call(
        _odd_pre_kernel,
        grid=(N_TILES,),
        in_specs=[_tile_spec(D_MODEL), _mod_spec(), _full_spec((1, D_MODEL)),
                  _full_spec((D_MODEL, n_in)), _full_spec((1, Q_LORA)),
                  _full_spec((Q_LORA, QK_W)), _full_spec((Q_LORA, QK_W)),
                  _full_spec((1, KV_LORA)), _full_spec((KV_LORA, QK_W)), _full_spec((KV_LORA, VT_W)),
                  _full_spec((1, VT_W)),
                  _tile_spec(HEAD_PAD), _tile_spec(HEAD_PAD)],
        out_specs=(pl.BlockSpec((1, QK_W, ROW_TILE), lambda t: (t, 0, 0)),
                   _tile_spec(QK_W),
                   pl.BlockSpec((ROW_TILE // ATT_CHUNK, VT_W, ATT_CHUNK), lambda t: (t, 0, 0)),
                   _tile_spec(POOL_DIM)),
        out_shape=(jax.ShapeDtypeStruct((N_TILES, QK_W, ROW_TILE), BF16),
                   jax.ShapeDtypeStruct((N_ROWS, QK_W), BF16),
                   jax.ShapeDtypeStruct((N_ROWS // ATT_CHUNK, VT_W, ATT_CHUNK), BF16),
                   jax.ShapeDtypeStruct((N_ROWS, POOL_DIM), F32)),
        compiler_params=_cparams(("parallel",)),
        name="odd_pre",
    )(x, mod_i, g0, w_in, gq, wqm, wqs, gkv, wk, wv, vaug, cos_t, sin_t)


def _attn_kernel(qt_ref, k_ref, vt_ref, o_ref, m_ref, acc_ref, s_ref, p_ref, a_ref):
    qi = pl.program_id(1)
    kj = pl.program_id(2)
    nk = pl.num_programs(2)

    @pl.when(kj == 0)
    def _():
        m_ref[...] = jnp.full(m_ref.shape, -jnp.inf, F32)
        acc_ref[...] = jnp.zeros(acc_ref.shape, F32)

    ctx_chunks = jnp.where(kj == 0, CTX_LEN // ATT_CHUNK, 0)
    n_chunks = jnp.where(qi == 0, ctx_chunks, ATT_TK // ATT_CHUNK)

    def stage_scores(c, slot):
        r0 = pl.multiple_of(c * ATT_CHUNK, ATT_CHUNK)
        for hd in range(MLA_HEADS):
            sl = slice(hd * HEAD_PAD, (hd + 1) * HEAD_PAD)
            s_ref[slot, hd] = _dot(k_ref[pl.ds(r0, ATT_CHUNK), sl], qt_ref[0, sl, :])

    def stage_softmax(slot):
        for hd in range(MLA_HEADS):
            hs = slice(hd, hd + 1)
            s = s_ref[slot, hd]
            m_old = m_ref[hs, :]
            m_new = jnp.maximum(m_old, jnp.max(s, axis=0, keepdims=True))
            alpha = jnp.exp2(m_old - m_new)
            p = jnp.exp2(s - m_new)
            m_ref[hs, :] = m_new
            a_ref[slot, hs, :] = alpha
            p_ref[slot, hd] = p.astype(BF16)

    def stage_pv(c, slot):
        for hd in range(MLA_HEADS):
            vs = slice(hd * V_AUG, (hd + 1) * V_AUG)
            pv = _dot(vt_ref[c, vs, :], p_ref[slot, hd])
            acc_ref[vs, :] = a_ref[slot, hd:hd + 1, :] * acc_ref[vs, :] + pv

    @pl.when(n_chunks > 0)
    def _():
        stage_scores(0, 0)
        stage_scores(1, 1)
        stage_softmax(0)

        def trip(i, carry):
            t = 2 * i
            stage_scores(t, 0)
            stage_pv(t - 2, 0)
            stage_softmax(1)
            stage_scores(t + 1, 1)
            stage_pv(t - 1, 1)
            stage_softmax(0)
            return carry

        lax.fori_loop(1, n_chunks // 2, trip, 0)
        stage_pv(n_chunks - 2, 0)
        stage_softmax(1)
        stage_pv(n_chunks - 1, 1)

    @pl.when(kj == nk - 1)
    def _():
        outs = []
        for hd in range(MLA_HEADS):
            blk = acc_ref[hd * V_AUG:(hd + 1) * V_AUG, :]
            outs.append(blk[0:MLA_V, :] / blk[MLA_V:MLA_V + 1, :])
        o_ref[...] = jnp.concatenate(outs, axis=0).T


def _attention(qt, k, vt):
    nk = ROWS_PER_BATCH // ATT_TK
    cpb = ATT_TK // ATT_CHUNK
    return pl.pallas_call(
        _attn_kernel,
        grid=(BATCH, TILES_PER_BATCH, nk),
        in_specs=[pl.BlockSpec((1, QK_W, ROW_TILE), lambda b, i, j: (b * TILES_PER_BATCH + i, 0, 0)),
                  pl.BlockSpec((ATT_TK, QK_W), lambda b, i, j: (b * nk + j, 0)),
                  pl.BlockSpec((cpb, VT_W, ATT_CHUNK), lambda b, i, j: (b * nk + j, 0, 0))],
        out_specs=pl.BlockSpec((ROW_TILE, ATT_W), lambda b, i, j: (b * TILES_PER_BATCH + i, 0)),
        out_shape=jax.ShapeDtypeStruct((N_ROWS, ATT_W), F32),
        scratch_shapes=[pltpu.VMEM((MLA_HEADS, ROW_TILE), F32),
                        pltpu.VMEM((VT_W, ROW_TILE), F32),
                        pltpu.VMEM((2, MLA_HEADS, ATT_CHUNK, ROW_TILE), F32),
                        pltpu.VMEM((2, MLA_HEADS, ATT_CHUNK, ROW_TILE), BF16),
                        pltpu.VMEM((2, MLA_HEADS, ROW_TILE), F32)],
        compiler_params=_cparams(("parallel", "parallel", "arbitrary")),
        name="attention",
    )(qt, k, vt)


def _odd_post_kernel(att_ref, p_ref, pp_ref, pn_ref, wp_ref, bp_ref, sp_ref, wout_ref,
                     x_ref, mod_ref, g1_ref, wr_hi_ref, wr_lo_ref, br_ref,
                     xo_ref, h2_ref, gates_ref, ext_ref):
    is_start, is_end, seq_tile, is_ctx = _seq_flags()
    hp = SMALL_HALO
    ext_ref[0:hp] = jnp.where(is_start, 0.0, pp_ref[...])
    ext_ref[hp:hp + ROW_TILE] = p_ref[...]
    ext_ref[hp + ROW_TILE:2 * hp + ROW_TILE] = jnp.where(is_end, 0.0, pn_ref[...])
    seq_len = jnp.where(is_ctx, CTX_LEN, SEQ)
    pos = seq_tile * ROW_TILE + lax.broadcasted_iota(jnp.int32, (ROW_TILE, 1), 0)

    y = _dot(att_ref[...].astype(BF16), wout_ref[0:ATT_W, :])
    for g, w in enumerate(POOL_WINDOWS):
        sl = slice(g * POOL_GROUP, (g + 1) * POOL_GROUP)
        ssum = jnp.zeros((ROW_TILE, POOL_GROUP), F32)
        for d in range(-(w // 2), w - w // 2):
            ssum = ssum + ext_ref[hp + d:hp + d + ROW_TILE, sl]
        lo = jnp.maximum(pos - w // 2, 0)
        hi = jnp.minimum(pos - w // 2 + w, seq_len)
        cnt = (hi - lo).astype(F32)
        pm = ssum / cnt - p_ref[:, sl]
        yg = (_dot(pm.astype(BF16), wp_ref[g]) + bp_ref[g:g + 1, :]) * sp_ref[:, sl]
        y = y + _dot(yg.astype(BF16), wout_ref[ATT_W + g * POOL_GROUP:ATT_W + (g + 1) * POOL_GROUP, :])
    _residual_and_route(x_ref[...], y, mod_ref[0], g1_ref, wr_hi_ref, wr_lo_ref, br_ref,
                        xo_ref, h2_ref, gates_ref)


def _odd_post(att, pool_in, w_pool, b_pool, s_pool, w_out, x, mod_i, g1, wr_hi, wr_lo, br):
    pp_spec, pn_spec = _halo_specs(POOL_DIM, SMALL_HALO)
    ng = len(POOL_WINDOWS)
    return pl.pallas_call(
        _odd_post_kernel,
        grid=(N_TILES,),
        in_specs=[_tile_spec(ATT_W), _tile_spec(POOL_DIM), pp_spec, pn_spec,
                  _full_spec((ng, POOL_GROUP, POOL_GROUP)), _full_spec((ng, POOL_GROUP)),
                  _full_spec((1, POOL_DIM)), _full_spec((D_MODEL, D_MODEL)),
                  _tile_spec(D_MODEL), _mod_spec(), _full_spec((1, D_MODEL)),
                  _full_spec((D_MODEL, ROUTE_W)), _full_spec((D_MODEL, ROUTE_W)),
                  _full_spec((1, ROUTE_W))],
        out_specs=_post_out_specs(),
        out_shape=_POST_OUT_SHAPES,
        scratch_shapes=[pltpu.VMEM((ROW_TILE + 2 * SMALL_HALO, POOL_DIM), F32)],
        compiler_params=_cparams(("parallel",)),
        name="odd_post",
    )(att, pool_in, pool_in, pool_in, w_pool, b_pool, s_pool, w_out, x, mod_i, g1, wr_hi, wr_lo, br)


def _moe_dense_kernel(h2_ref, gates_ref, w1_ref, w3_ref, w2_ref, x_ref, mod_ref, o_ref, acc_ref):
    t = pl.program_id(0)
    e = pl.program_id(1)

    @pl.when(e == 0)
    def _():
        acc_ref[...] = jnp.zeros(acc_ref.shape, F32)

    h = h2_ref[...]
    hid = _silu(_dot(h, w1_ref[0])) * _dot(h, w3_ref[0])
    y = _dot(hid.astype(BF16), w2_ref[0])
    lane = lax.broadcasted_iota(jnp.int32, gates_ref.shape, 1)
    g = jnp.sum(jnp.where(lane == e + GATE_LANE0, gates_ref[...], 0.0), axis=-1, keepdims=True)
    acc_ref[...] += g * y

    @pl.when(e == N_EXPERTS - 1)
    def _():
        sub = MOE_TM // ROW_TILE
        for s in range(sub):
            row = _mod_row(t * sub + s)
            m = mod_ref[pl.ds(row, 1)]
            rs = slice(s * ROW_TILE, (s + 1) * ROW_TILE)
            o_ref[rs, :] = x_ref[rs, :] + m[0, 5:6] * acc_ref[rs, :]


def _moe_dense(h2, gates, w1, w3, w2, x, mod_i):
    nt = N_ROWS // MOE_TM
    return pl.pallas_call(
        _moe_dense_kernel,
        grid=(nt, N_EXPERTS),
        in_specs=[pl.BlockSpec((MOE_TM, D_MODEL), lambda t, e: (t, 0)),
                  pl.BlockSpec((MOE_TM, ROUTE_W), lambda t, e: (t, 0)),
                  pl.BlockSpec((1, D_MODEL, D_EXPERT), lambda t, e: (e, 0, 0)),
                  pl.BlockSpec((1, D_MODEL, D_EXPERT), lambda t, e: (e, 0, 0)),
                  pl.BlockSpec((1, D_EXPERT, D_MODEL), lambda t, e: (e, 0, 0)),
                  pl.BlockSpec((MOE_TM, D_MODEL), lambda t, e: (t, 0)),
                  pl.BlockSpec((MOD_ROWS, 6, D_MODEL), lambda t, e: (0, 0, 0))],
        out_specs=pl.BlockSpec((MOE_TM, D_MODEL), lambda t, e: (t, 0)),
        out_shape=jax.ShapeDtypeStruct((N_ROWS, D_MODEL), F32),
        scratch_shapes=[pltpu.VMEM((MOE_TM, D_MODEL), F32)],
        compiler_params=_cparams(("parallel", "arbitrary")),
        name="moe_dense",
    )(h2, gates, w1, w3, w2, x, mod_i)


def _final_kernel(x_ref, g_ref, o_ref):
    x = x_ref[...]
    o_ref[0] = x * lax.rsqrt(jnp.mean(x * x, axis=-1, keepdims=True) + EPS) * g_ref[...]


def _final_norm(x, g):
    lat_tiles = SEQ // ROW_TILE
    return pl.pallas_call(
        _final_kernel,
        grid=(BATCH, lat_tiles),
        in_specs=[pl.BlockSpec((ROW_TILE, D_MODEL), lambda b, i: (b * TILES_PER_BATCH + 1 + i, 0)),
                  pl.BlockSpec((1, D_MODEL), lambda b, i: (0, 0))],
        out_specs=pl.BlockSpec((1, ROW_TILE, D_MODEL), lambda b, i: (b, i, 0)),
        out_shape=jax.ShapeDtypeStruct((BATCH, SEQ, D_MODEL), F32),
        compiler_params=_cparams(("parallel", "parallel")),
        name="final_norm",
    )(x, g)


_SWAP32 = tuple(list(range(8, 16)) + list(range(0, 8)) + list(range(24, 32)) + list(range(16, 24)))


def _rope_tables():
    rows = SEQ // GRID_W
    pos_row = jnp.repeat(jnp.arange(rows, dtype=F32), GRID_W)
    pos_col = jnp.tile(jnp.arange(GRID_W, dtype=F32), rows)
    inv = ROPE_BASE ** (-jnp.arange(0, AXIS_DIM, 2, dtype=F32) / AXIS_DIM)
    ang = jnp.concatenate([pos_row[:, None] * inv, pos_col[:, None] * inv], axis=-1)
    cos, sin = jnp.cos(ang), jnp.sin(ang)
    p = AXIS_PAIRS
    c32 = jnp.concatenate([cos[:, :p], cos[:, :p], cos[:, p:], cos[:, p:]], axis=-1)
    s32 = jnp.concatenate([-sin[:, :p], sin[:, :p], -sin[:, p:], sin[:, p:]], axis=-1)
    pad = HEAD_PAD - MLA_NOPE - MLA_ROPE
    cos_lat = jnp.concatenate([jnp.ones((SEQ, MLA_NOPE), F32), c32, jnp.zeros((SEQ, pad), F32)], axis=-1)
    sin_lat = jnp.concatenate([jnp.zeros((SEQ, MLA_NOPE), F32), s32, jnp.zeros((SEQ, pad), F32)], axis=-1)
    cos_ctx = jnp.concatenate([jnp.ones((CTX_LEN, MLA_NOPE + MLA_ROPE), F32),
                               jnp.zeros((CTX_LEN, pad), F32)], axis=-1)
    sin_ctx = jnp.zeros((CTX_LEN, HEAD_PAD), F32)
    cos_t = jnp.concatenate([cos_ctx, cos_lat] * BATCH, axis=0)
    sin_t = jnp.concatenate([sin_ctx, sin_lat] * BATCH, axis=0)
    return cos_t, sin_t


def _odd_weights(w_in, w_uq, w_ukv):
    swap = jnp.array(_SWAP32)
    o_kv, o_kr, o_pool = Q_LORA, Q_LORA + KV_LORA, Q_LORA + KV_LORA + MLA_ROPE
    w_kr = w_in[:, o_kr:o_pool]
    pad_l = jnp.zeros((D_MODEL, MLA_NOPE), F32)
    pad_r = jnp.zeros((D_MODEL, HEAD_PAD - MLA_NOPE - MLA_ROPE), F32)
    w_in2 = jnp.concatenate([w_in[:, :o_kr], pad_l, w_kr, pad_r, pad_l, w_kr[:, swap], pad_r,
                             w_in[:, o_pool:]], axis=-1).astype(BF16)
    wq = w_uq.reshape(Q_LORA, MLA_HEADS, MLA_NOPE + MLA_ROPE)
    zq = jnp.zeros((Q_LORA, MLA_HEADS, HEAD_PAD - MLA_NOPE - MLA_ROPE), F32)
    wq_main = jnp.concatenate([wq, zq], axis=-1).reshape(Q_LORA, QK_W).astype(BF16)
    wq_swap = jnp.concatenate([jnp.zeros((Q_LORA, MLA_HEADS, MLA_NOPE), F32),
                               wq[:, :, MLA_NOPE:][:, :, swap], zq], axis=-1).reshape(Q_LORA, QK_W).astype(BF16)
    wkv = w_ukv.reshape(KV_LORA, MLA_HEADS, MLA_NOPE + MLA_V)
    zk = jnp.zeros((KV_LORA, MLA_HEADS, HEAD_PAD - MLA_NOPE), F32)
    wk = jnp.concatenate([wkv[:, :, :MLA_NOPE], zk], axis=-1).reshape(KV_LORA, QK_W).astype(BF16)
    zv = jnp.zeros((KV_LORA, MLA_HEADS, V_AUG - MLA_V), F32)
    wv = jnp.concatenate([wkv[:, :, MLA_NOPE:], zv], axis=-1).reshape(KV_LORA, VT_W).astype(BF16)
    return w_in2, wq_main, wq_swap, wk, wv


def kernel(x, c, ctx, c_ctx, w_mod, b_mod, norm_g, ev_w_in, ev_conv_a_w, ev_conv_a_b, ev_ln_a_g, ev_ln_a_b,
           ev_conv_b_w, ev_w_out, od_w_in, od_q_norm_g, od_w_uq, od_kv_norm_g, od_w_ukv, od_w_pool, od_b_pool,
           od_s_pool, od_w_out, moe_wg, moe_bg, moe_we, moe_be, moe_w1, moe_w3, moe_w2, final_g):
    assert x.shape == (BATCH, SEQ, D_MODEL) and ctx.shape == (BATCH, CTX_LEN, D_MODEL)
    xs = jnp.concatenate([ctx[0], x[0], ctx[1], x[1]], axis=0)
    cvec = jnp.concatenate([c, c_ctx[None, :], jnp.zeros((MOD_ROWS - BATCH - 1, D_MODEL), F32)], axis=0)
    mod = _modulation(cvec, w_mod, b_mod).reshape(DEPTH, MOD_ROWS, 6, D_MODEL)
    cos_t, sin_t = _rope_tables()

    for i in range(DEPTH):
        j = i // 2
        mod_i = mod[i]
        g0 = norm_g[i, 0][None, :]
        g1 = norm_g[i, 1][None, :]
        wr = jnp.concatenate([moe_wg[i], moe_we[i],
                              jnp.zeros((D_MODEL, ROUTE_W - N_GROUPS - N_EXPERTS), F32)], axis=-1)
        wr_hi = wr.astype(BF16)
        wr_lo = (wr - wr_hi.astype(F32)).astype(BF16)
        br = jnp.concatenate([moe_bg[i], moe_be[i],
                              jnp.zeros((ROUTE_W - N_GROUPS - N_EXPERTS,), F32)])[None, :]
        if i % 2 == 0:
            a_pre, gb, gch = _even_pre(xs, mod_i, g0, ev_w_in[j].astype(BF16))
            xs, h2, gates = _even_post(a_pre, gb, gch, ev_conv_a_w[j], ev_conv_a_b[j][None, :],
                                       ev_ln_a_g[j][None, :], ev_ln_a_b[j][None, :], ev_conv_b_w[j],
                                       ev_w_out[j].astype(BF16), xs, mod_i, g1, wr_hi, wr_lo, br)
        else:
            w_in2, wq_main, wq_swap, wk, wv = _odd_weights(od_w_in[j], od_w_uq[j], od_w_ukv[j])
            q, k, v, pool_in = _odd_pre(xs, mod_i, g0, w_in2, od_q_norm_g[j][None, :], wq_main, wq_swap,
                                        od_kv_norm_g[j][None, :], wk, wv, cos_t, sin_t)
            att = _attention(q, k, v)
            xs, h2, gates = _odd_post(att, pool_in, od_w_pool[j].astype(BF16), od_b_pool[j],
                                      od_s_pool[j][None, :], od_w_out[j].astype(BF16), xs, mod_i, g1,
                                      wr_hi, wr_lo, br)
        xs = _moe_dense(h2, gates, moe_w1[i].astype(BF16), moe_w3[i].astype(BF16),
                        moe_w2[i].astype(BF16), xs, mod_i)
    return _final_norm(xs, final_g[None, :])
```

```python
import functools

import jax
import jax.numpy as jnp
from jax import lax
from jax.experimental import pallas as pl
from jax.experimental.pallas import tpu as pltpu

F32 = jnp.float32
BF16 = jnp.bfloat16

D_MODEL = 1024
BATCH = 2
SEQ = 8192
DEPTH = 4
GRID_W = 64
CTX_LEN = 256
EPS = 1e-6
CONV_A_DIM = 512
CONV_A_WIDTH = 31
CONV_B_DIM = 512
CONV_B_WIDTH = 3
MLA_HEADS = 8
MLA_NOPE = 64
MLA_ROPE = 32
MLA_V = 64
Q_LORA = 384
KV_LORA = 256
MLA_SCALE = (MLA_NOPE + MLA_ROPE) ** -0.5
AXIS_DIM = MLA_ROPE // 2
AXIS_PAIRS = AXIS_DIM // 2
ROPE_BASE = 10000.0
POOL_WINDOWS = (2, 4, 8, 16)
POOL_GROUP = 128
POOL_DIM = POOL_GROUP * len(POOL_WINDOWS)
N_GROUPS = 4
EXPERTS_PER_GROUP = 4
N_EXPERTS = 16
D_EXPERT = 512

LANES = 128
SUBLANES = 8
VMEM_LIMIT = 48 * 1024 * 1024

ROW_TILE = 256
ROWS_PER_BATCH = CTX_LEN + SEQ
N_ROWS = BATCH * ROWS_PER_BATCH
TILES_PER_BATCH = ROWS_PER_BATCH // ROW_TILE
N_TILES = N_ROWS // ROW_TILE
HEAD_PAD = LANES
QK_W = MLA_HEADS * HEAD_PAD
ATT_W = MLA_HEADS * MLA_V
V_AUG = MLA_V + 16
VT_W = MLA_HEADS * V_AUG
CONV_A_HALO = 16
SMALL_HALO = 8
ATT_TK = 2816
ATT_CHUNK = 128
LOG2E = 1.4426950408889634
ROUTE_W = LANES
GATE_LANE0 = N_GROUPS
CLS_LANE = GATE_LANE0 + N_EXPERTS
RANK_LANE = CLS_LANE + 1
HX_W = D_MODEL + ROUTE_W
PAIRS_PER_GROUP = 6
N_CLASSES = N_GROUPS * PAIRS_PER_GROUP
PAIR_LO = (0, 0, 0, 1, 1, 2)
PAIR_HI = (1, 2, 3, 2, 3, 3)
MOE_TM = 128
MOE_TILES = -(-(N_ROWS + N_CLASSES * (MOE_TM - 1)) // MOE_TM)
MOE_ROWS = MOE_TILES * MOE_TM
MOD_ROWS = 8


def _mod_row(t):
    return jnp.where(t % TILES_PER_BATCH == 0, BATCH, t // TILES_PER_BATCH)


def _cparams(sem):
    return pltpu.CompilerParams(dimension_semantics=sem, vmem_limit_bytes=VMEM_LIMIT)


def _dot(a, b):
    return jnp.dot(a, b, preferred_element_type=F32)


def _sigmoid(x):
    return 1.0 / (1.0 + jnp.exp(-x))


def _silu(x):
    return x * _sigmoid(x)


def _norm_mod(x, g, shift, scale):
    y = x * lax.rsqrt(jnp.mean(x * x, axis=-1, keepdims=True) + EPS)
    return (y * g) * (1.0 + scale) + shift


def _mod_kernel(c_ref, w_ref, b_ref, o_ref):
    s = _silu(c_ref[...])
    o_ref[0] = jnp.dot(s, w_ref[0], preferred_element_type=F32,
                       precision=lax.Precision.HIGHEST) + b_ref[0]


def _modulation(cvec, w_mod, b_mod):
    nblk = 4
    bw = 6 * D_MODEL // nblk
    return pl.pallas_call(
        _mod_kernel,
        grid=(DEPTH, nblk),
        in_specs=[pl.BlockSpec((MOD_ROWS, D_MODEL), lambda i, j: (0, 0)),
                  pl.BlockSpec((1, D_MODEL, bw), lambda i, j: (i, 0, j)),
                  pl.BlockSpec((1, 1, bw), lambda i, j: (i, 0, j))],
        out_specs=pl.BlockSpec((1, MOD_ROWS, bw), lambda i, j: (i, 0, j)),
        out_shape=jax.ShapeDtypeStruct((DEPTH, MOD_ROWS, 6 * D_MODEL), F32),
        compiler_params=_cparams(("parallel", "parallel")),
        name="modulation",
    )(cvec, w_mod, b_mod.reshape(DEPTH, 1, 6 * D_MODEL))


def _route(h2, wr_hi_ref, wr_lo_ref, br_ref):
    hi = h2.astype(BF16)
    lo = (h2 - hi.astype(F32)).astype(BF16)
    logits = (_dot(hi, wr_hi_ref[...]) + _dot(hi, wr_lo_ref[...]) + _dot(lo, wr_hi_ref[...])
              + br_ref[...])
    lane = lax.broadcasted_iota(jnp.int32, logits.shape, 1)
    lane_f = lane.astype(F32)
    neg = -jnp.inf
    big = float(ROUTE_W)
    gl = jnp.where(lane < N_GROUPS, logits, neg)
    gmax = jnp.max(gl, axis=-1, keepdims=True)
    gidx = jnp.min(jnp.where(gl == gmax, lane_f, big), axis=-1, keepdims=True)
    g_w = 1.0 / jnp.sum(jnp.exp(gl - gmax), axis=-1, keepdims=True)
    egrp = ((lane - GATE_LANE0) // EXPERTS_PER_GROUP).astype(F32)
    in_g = (lane >= GATE_LANE0) & (lane < GATE_LANE0 + N_EXPERTS) & (egrp == gidx)
    el = jnp.where(in_g, logits, neg)
    v1 = jnp.max(el, axis=-1, keepdims=True)
    i1 = jnp.min(jnp.where(el == v1, lane_f, big), axis=-1, keepdims=True)
    el2 = jnp.where(lane_f == i1, neg, el)
    v2 = jnp.max(el2, axis=-1, keepdims=True)
    i2 = jnp.min(jnp.where(el2 == v2, lane_f, big), axis=-1, keepdims=True)
    e21 = jnp.exp(v2 - v1)
    w1 = 1.0 / (1.0 + e21)
    w2 = e21 * w1
    gates = jnp.where(lane_f == i1, w1 * g_w, jnp.where(lane_f == i2, w2 * g_w, 0.0))
    base = GATE_LANE0 + gidx * EXPERTS_PER_GROUP
    lo = jnp.minimum(i1, i2) - base
    hi = jnp.maximum(i1, i2) - base
    pair = jnp.where(lo == 0.0, 0.0, jnp.where(lo == 1.0, 3.0, 5.0)) + (hi - lo - 1.0)
    return gates, gidx * PAIRS_PER_GROUP + pair


def _residual_and_route(x, y, m, g1_ref, wr_hi_ref, wr_lo_ref, br_ref, xo_ref, hx_ref, cnt_ref):
    @pl.when(pl.program_id(0) == 0)
    def _():
        cnt_ref[...] = jnp.zeros(cnt_ref.shape, F32)

    x_new = x + m[2:3] * y
    xo_ref[...] = x_new
    h2 = _norm_mod(x_new, g1_ref[...], m[3:4], m[4:5])
    hx_ref[:, 0:D_MODEL] = h2
    gates, cls = _route(h2, wr_hi_ref, wr_lo_ref, br_ref)
    lane_f = lax.broadcasted_iota(jnp.int32, gates.shape, 1).astype(F32)
    onehot = jnp.where(lane_f == cls, 1.0, 0.0)
    row = lax.broadcasted_iota(jnp.int32, (ROW_TILE, ROW_TILE), 0)
    col = lax.broadcasted_iota(jnp.int32, (ROW_TILE, ROW_TILE), 1)
    earlier = jnp.where(col < row, 1.0, 0.0).astype(BF16)
    before = _dot(earlier, onehot.astype(BF16)) + cnt_ref[...]
    rank = jnp.sum(onehot * before, axis=-1, keepdims=True)
    cnt_ref[...] = cnt_ref[...] + jnp.sum(onehot, axis=0, keepdims=True)
    hx_ref[:, D_MODEL:HX_W] = jnp.where(lane_f == float(CLS_LANE), cls,
                                        jnp.where(lane_f == float(RANK_LANE), rank, gates))


def _tile_spec(width):
    return pl.BlockSpec((ROW_TILE, width), lambda t: (t, 0))


def _full_spec(shape):
    nd = len(shape)
    return pl.BlockSpec(shape, lambda t: (0,) * nd)


def _mod_spec():
    return pl.BlockSpec((1, 6, D_MODEL), lambda t: (_mod_row(t), 0, 0))


def _halo_specs(width, halo):
    per_tile = ROW_TILE // halo
    last = N_ROWS // halo - 1
    prev = pl.BlockSpec((halo, width), lambda t: (jnp.maximum(t * per_tile - 1, 0), 0))
    nxt = pl.BlockSpec((halo, width), lambda t: (jnp.minimum((t + 1) * per_tile, last), 0))
    return prev, nxt


def _seq_flags():
    w = pl.program_id(0) % TILES_PER_BATCH
    is_start = (w == 0) | (w == 1)
    is_end = (w == 0) | (w == TILES_PER_BATCH - 1)
    seq_tile = jnp.maximum(w - 1, 0)
    return is_start, is_end, seq_tile, w == 0


_POST_OUT_SHAPES = (jax.ShapeDtypeStruct((N_ROWS, D_MODEL), F32),
                    jax.ShapeDtypeStruct((N_ROWS, HX_W), F32))


def _post_out_specs():
    return (_tile_spec(D_MODEL), _tile_spec(HX_W))


def _class_count_scratch():
    return pltpu.VMEM((1, ROUTE_W), F32)


def _even_pre_kernel(x_ref, mod_ref, g_ref, w_ref, a_ref, gb_ref, gch_ref):
    m = mod_ref[0]
    h = _norm_mod(x_ref[...], g_ref[...], m[0:1], m[1:2])
    u = _dot(h.astype(BF16), w_ref[...])
    ca, cb = CONV_A_DIM, CONV_B_DIM
    a_ref[...] = u[:, 0:ca] * _sigmoid(u[:, ca:2 * ca])
    gb_ref[...] = u[:, 2 * ca:2 * ca + cb]
    gch_ref[...] = u[:, 2 * ca + cb:2 * ca + 2 * cb] * u[:, 2 * ca + 2 * cb:2 * ca + 3 * cb]


def _even_pre(x, mod_i, g0, w_in):
    n_in = w_in.shape[1]
    return pl.pallas_call(
        _even_pre_kernel,
        grid=(N_TILES,),
        in_specs=[_tile_spec(D_MODEL), _mod_spec(), _full_spec((1, D_MODEL)),
                  _full_spec((D_MODEL, n_in))],
        out_specs=(_tile_spec(CONV_A_DIM), _tile_spec(CONV_B_DIM), _tile_spec(CONV_B_DIM)),
        out_shape=(jax.ShapeDtypeStruct((N_ROWS, CONV_A_DIM), F32),
                   jax.ShapeDtypeStruct((N_ROWS, CONV_B_DIM), F32),
                   jax.ShapeDtypeStruct((N_ROWS, CONV_B_DIM), F32)),
        compiler_params=_cparams(("parallel",)),
        name="even_pre",
    )(x, mod_i, g0, w_in)


def _even_post_kernel(a_ref, ap_ref, an_ref, gch_ref, gp_ref, gn_ref, gb_ref,
                      cwa_ref, cba_ref, lng_ref, lnb_ref, cwb_ref, wout_ref,
                      x_ref, mod_ref, g1_ref, wr_hi_ref, wr_lo_ref, br_ref,
                      xo_ref, hx_ref, exta_ref, extb_ref, cnt_ref):
    is_start, is_end, _, _ = _seq_flags()
    ha, hb = CONV_A_HALO, SMALL_HALO
    exta_ref[0:ha] = jnp.where(is_start, 0.0, ap_ref[...])
    exta_ref[ha:ha + ROW_TILE] = a_ref[...]
    exta_ref[ha + ROW_TILE:2 * ha + ROW_TILE] = jnp.where(is_end, 0.0, an_ref[...])
    extb_ref[0:hb] = jnp.where(is_start, 0.0, gp_ref[...])
    extb_ref[hb:hb + ROW_TILE] = gch_ref[...]
    extb_ref[hb + ROW_TILE:2 * hb + ROW_TILE] = jnp.where(is_end, 0.0, gn_ref[...])

    half_a = (CONV_A_WIDTH - 1) // 2
    acc = jnp.zeros((ROW_TILE, CONV_A_DIM), F32) + cba_ref[...]
    for k in range(CONV_A_WIDTH):
        off = ha - half_a + k
        acc = acc + cwa_ref[k:k + 1, :] * exta_ref[off:off + ROW_TILE, :]
    mu = jnp.mean(acc, axis=-1, keepdims=True)
    xc = acc - mu
    a = xc * lax.rsqrt(jnp.mean(xc * xc, axis=-1, keepdims=True) + EPS)
    a = _silu(a * lng_ref[...] + lnb_ref[...])

    half_b = (CONV_B_WIDTH - 1) // 2
    cb = jnp.zeros((ROW_TILE, CONV_B_DIM), F32)
    for k in range(CONV_B_WIDTH):
        off = hb - half_b + k
        cb = cb + cwb_ref[k:k + 1, :] * extb_ref[off:off + ROW_TILE, :]
    b = gb_ref[...] * cb

    y = (_dot(a.astype(BF16), wout_ref[0:CONV_A_DIM, :])
         + _dot(b.astype(BF16), wout_ref[CONV_A_DIM:CONV_A_DIM + CONV_B_DIM, :]))
    _residual_and_route(x_ref[...], y, mod_ref[0], g1_ref, wr_hi_ref, wr_lo_ref, br_ref,
                        xo_ref, hx_ref, cnt_ref)


def _even_post(a_pre, gb, gch, cwa, cba, lng, lnb, cwb, w_out, x, mod_i, g1, wr_hi, wr_lo, br):
    ap_spec, an_spec = _halo_specs(CONV_A_DIM, CONV_A_HALO)
    gp_spec, gn_spec = _halo_specs(CONV_B_DIM, SMALL_HALO)
    return pl.pallas_call(
        _even_post_kernel,
        grid=(N_TILES,),
        in_specs=[_tile_spec(CONV_A_DIM), ap_spec, an_spec,
                  _tile_spec(CONV_B_DIM), gp_spec, gn_spec, _tile_spec(CONV_B_DIM),
                  _full_spec((CONV_A_WIDTH, CONV_A_DIM)), _full_spec((1, CONV_A_DIM)),
                  _full_spec((1, CONV_A_DIM)), _full_spec((1, CONV_A_DIM)),
                  _full_spec((CONV_B_WIDTH, CONV_B_DIM)), _full_spec((D_MODEL, D_MODEL)),
                  _tile_spec(D_MODEL), _mod_spec(), _full_spec((1, D_MODEL)),
                  _full_spec((D_MODEL, ROUTE_W)), _full_spec((D_MODEL, ROUTE_W)),
                  _full_spec((1, ROUTE_W))],
        out_specs=_post_out_specs(),
        out_shape=_POST_OUT_SHAPES,
        scratch_shapes=[pltpu.VMEM((ROW_TILE + 2 * CONV_A_HALO, CONV_A_DIM), F32),
                        pltpu.VMEM((ROW_TILE + 2 * SMALL_HALO, CONV_B_DIM), F32),
                        _class_count_scratch()],
        compiler_params=_cparams(("arbitrary",)),
        name="even_post",
    )(a_pre, a_pre, a_pre, gch, gch, gch, gb, cwa, cba, lng, lnb, cwb, w_out,
      x, mod_i, g1, wr_hi, wr_lo, br)


def _odd_pre_kernel(x_ref, mod_ref, g_ref, w_ref, gq_ref, wqm_ref, wqs_ref, gkv_ref, wk_ref, wv_ref,
                    vaug_ref, cos_ref, sin_ref, qt_ref, k_ref, vt_ref, pool_ref):
    m = mod_ref[0]
    h = _norm_mod(x_ref[...], g_ref[...], m[0:1], m[1:2])
    u = _dot(h.astype(BF16), w_ref[...])
    o_kv = Q_LORA
    o_kr = o_kv + KV_LORA
    o_krs = o_kr + HEAD_PAD
    o_pool = o_krs + HEAD_PAD
    cos = cos_ref[...]
    sin = sin_ref[...]

    q_c = u[:, 0:o_kv]
    qn = (q_c * lax.rsqrt(jnp.mean(q_c * q_c, axis=-1, keepdims=True) + EPS) * gq_ref[...]).astype(BF16)
    q_main = _dot(qn, wqm_ref[...])
    q_swap = _dot(qn, wqs_ref[...])
    kv_c = u[:, o_kv:o_kr]
    kvn = (kv_c * lax.rsqrt(jnp.mean(kv_c * kv_c, axis=-1, keepdims=True) + EPS) * gkv_ref[...]).astype(BF16)
    k_nope = _dot(kvn, wk_ref[...])
    v = _dot(kvn, wv_ref[...]) + vaug_ref[...]
    for cc in range(ROW_TILE // ATT_CHUNK):
        vt_ref[cc] = v[cc * ATT_CHUNK:(cc + 1) * ATT_CHUNK, :].T.astype(BF16)
    k_rope = u[:, o_kr:o_krs] * cos + u[:, o_krs:o_pool] * sin
    for hd in range(MLA_HEADS):
        sl = slice(hd * HEAD_PAD, (hd + 1) * HEAD_PAD)
        qh = (q_main[:, sl] * cos + q_swap[:, sl] * sin) * (MLA_SCALE * LOG2E)
        qt_ref[0, sl, :] = qh.T.astype(BF16)
        k_ref[:, sl] = (k_nope[:, sl] + k_rope).astype(BF16)
    pool_ref[...] = u[:, o_pool:o_pool + POOL_DIM]


def _odd_pre(x, mod_i, g0, w_in, gq, wqm, wqs, gkv, wk, wv, cos_t, sin_t):
    n_in = w_in.shape[1]
    vaug = jnp.zeros((MLA_HEADS, V_AUG), F32).at[:, MLA_V].set(1.0).reshape(1, VT_W)
    return pl.pallas_call(
        _odd_pre_kernel,
        grid=(N_TILES,),
        in_specs=[_tile_spec(D_MODEL), _mod_spec(), _full_spec((1, D_MODEL)),
                  _full_spec((D_MODEL, n_in)), _full_spec((1, Q_LORA)),
                  _full_spec((Q_LORA, QK_W)), _full_spec((Q_LORA, QK_W)),
                  _full_spec((1, KV_LORA)), _full_spec((KV_LORA, QK_W)), _full_spec((KV_LORA, VT_W)),
                  _full_spec((1, VT_W)),
                  _tile_spec(HEAD_PAD), _tile_spec(HEAD_PAD)],
        out_specs=(pl.BlockSpec((1, QK_W, ROW_TILE), lambda t: (t, 0, 0)),
                   _tile_spec(QK_W),
                   pl.BlockSpec((ROW_TILE // ATT_CHUNK, VT_W, ATT_CHUNK), lambda t: (t, 0, 0)),
                   _tile_spec(POOL_DIM)),
        out_shape=(jax.ShapeDtypeStruct((N_TILES, QK_W, ROW_TILE), BF16),
                   jax.ShapeDtypeStruct((N_ROWS, QK_W), BF16),
                   jax.ShapeDtypeStruct((N_ROWS // ATT_CHUNK, VT_W, ATT_CHUNK), BF16),
                   jax.ShapeDtypeStruct((N_ROWS, POOL_DIM), F32)),
        compiler_params=_cparams(("parallel",)),
        name="odd_pre",
    )(x, mod_i, g0, w_in, gq, wqm, wqs, gkv, wk, wv, vaug, cos_t, sin_t)


def _attn_kernel(qt_ref, k_ref, vt_ref, o_ref, m_ref, acc_ref, s_ref, p_ref, a_ref):
    qi = pl.program_id(1)
    kj = pl.program_id(2)
    nk = pl.num_programs(2)

    @pl.when(kj == 0)
    def _():
        m_ref[...] = jnp.full(m_ref.shape, -jnp.inf, F32)
        acc_ref[...] = jnp.zeros(acc_ref.shape, F32)

    ctx_chunks = jnp.where(kj == 0, CTX_LEN // ATT_CHUNK, 0)
    n_chunks = jnp.where(qi == 0, ctx_chunks, ATT_TK // ATT_CHUNK)

    def stage_scores(c, slot):
        r0 = pl.multiple_of(c * ATT_CHUNK, ATT_CHUNK)
        for hd in range(MLA_HEADS):
            sl = slice(hd * HEAD_PAD, (hd + 1) * HEAD_PAD)
            s_ref[slot, hd] = _dot(k_ref[pl.ds(r0, ATT_CHUNK), sl], qt_ref[0, sl, :])

    def stage_softmax(slot):
        for hd in range(MLA_HEADS):
            hs = slice(hd, hd + 1)
            s = s_ref[slot, hd]
            m_old = m_ref[hs, :]
            m_new = jnp.maximum(m_old, jnp.max(s, axis=0, keepdims=True))
            alpha = jnp.exp2(m_old - m_new)
            p = jnp.exp2(s - m_new)
            m_ref[hs, :] = m_new
            a_ref[slot, hs, :] = alpha
            p_ref[slot, hd] = p.astype(BF16)

    def stage_pv(c, slot):
        for hd in range(MLA_HEADS):
            vs = slice(hd * V_AUG, (hd + 1) * V_AUG)
            pv = _dot(vt_ref[c, vs, :], p_ref[slot, hd])
            acc_ref[vs, :] = a_ref[slot, hd:hd + 1, :] * acc_ref[vs, :] + pv

    @pl.when(n_chunks > 0)
    def _():
        stage_scores(0, 0)
        stage_scores(1, 1)
        stage_softmax(0)

        def trip(i, carry):
            t = 2 * i
            stage_scores(t, 0)
            stage_pv(t - 2, 0)
            stage_softmax(1)
            stage_scores(t + 1, 1)
            stage_pv(t - 1, 1)
            stage_softmax(0)
            return carry

        lax.fori_loop(1, n_chunks // 2, trip, 0)
        stage_pv(n_chunks - 2, 0)
        stage_softmax(1)
        stage_pv(n_chunks - 1, 1)

    @pl.when(kj == nk - 1)
    def _():
        outs = []
        for hd in range(MLA_HEADS):
            blk = acc_ref[hd * V_AUG:(hd + 1) * V_AUG, :]
            outs.append(blk[0:MLA_V, :] / blk[MLA_V:MLA_V + 1, :])
        o_ref[...] = jnp.concatenate(outs, axis=0).T


def _attention(qt, k, vt):
    nk = ROWS_PER_BATCH // ATT_TK
    cpb = ATT_TK // ATT_CHUNK
    return pl.pallas_call(
        _attn_kernel,
        grid=(BATCH, TILES_PER_BATCH, nk),
        in_specs=[pl.BlockSpec((1, QK_W, ROW_TILE), lambda b, i, j: (b * TILES_PER_BATCH + i, 0, 0)),
                  pl.BlockSpec((ATT_TK, QK_W), lambda b, i, j: (b * nk + j, 0)),
                  pl.BlockSpec((cpb, VT_W, ATT_CHUNK), lambda b, i, j: (b * nk + j, 0, 0))],
        out_specs=pl.BlockSpec((ROW_TILE, ATT_W), lambda b, i, j: (b * TILES_PER_BATCH + i, 0)),
        out_shape=jax.ShapeDtypeStruct((N_ROWS, ATT_W), F32),
        scratch_shapes=[pltpu.VMEM((MLA_HEADS, ROW_TILE), F32),
                        pltpu.VMEM((VT_W, ROW_TILE), F32),
                        pltpu.VMEM((2, MLA_HEADS, ATT_CHUNK, ROW_TILE), F32),
                        pltpu.VMEM((2, MLA_HEADS, ATT_CHUNK, ROW_TILE), BF16),
                        pltpu.VMEM((2, MLA_HEADS, ROW_TILE), F32)],
        compiler_params=_cparams(("parallel", "parallel", "arbitrary")),
        name="attention",
    )(qt, k, vt)


def _odd_post_kernel(att_ref, p_ref, pp_ref, pn_ref, wp_ref, bp_ref, sp_ref, wout_ref,
                     x_ref, mod_ref, g1_ref, wr_hi_ref, wr_lo_ref, br_ref,
                     xo_ref, hx_ref, ext_ref, cnt_ref):
    is_start, is_end, seq_tile, is_ctx = _seq_flags()
    hp = SMALL_HALO
    ext_ref[0:hp] = jnp.where(is_start, 0.0, pp_ref[...])
    ext_ref[hp:hp + ROW_TILE] = p_ref[...]
    ext_ref[hp + ROW_TILE:2 * hp + ROW_TILE] = jnp.where(is_end, 0.0, pn_ref[...])
    seq_len = jnp.where(is_ctx, CTX_LEN, SEQ)
    pos = seq_tile * ROW_TILE + lax.broadcasted_iota(jnp.int32, (ROW_TILE, 1), 0)

    y = _dot(att_ref[...].astype(BF16), wout_ref[0:ATT_W, :])
    for g, w in enumerate(POOL_WINDOWS):
        sl = slice(g * POOL_GROUP, (g + 1) * POOL_GROUP)
        ssum = jnp.zeros((ROW_TILE, POOL_GROUP), F32)
        for d in range(-(w // 2), w - w // 2):
            ssum = ssum + ext_ref[hp + d:hp + d + ROW_TILE, sl]
        lo = jnp.maximum(pos - w // 2, 0)
        hi = jnp.minimum(pos - w // 2 + w, seq_len)
        cnt = (hi - lo).astype(F32)
        pm = ssum / cnt - p_ref[:, sl]
        yg = (_dot(pm.astype(BF16), wp_ref[g]) + bp_ref[g:g + 1, :]) * sp_ref[:, sl]
        y = y + _dot(yg.astype(BF16), wout_ref[ATT_W + g * POOL_GROUP:ATT_W + (g + 1) * POOL_GROUP, :])
    _residual_and_route(x_ref[...], y, mod_ref[0], g1_ref, wr_hi_ref, wr_lo_ref, br_ref,
                        xo_ref, hx_ref, cnt_ref)


def _odd_post(att, pool_in, w_pool, b_pool, s_pool, w_out, x, mod_i, g1, wr_hi, wr_lo, br):
    pp_spec, pn_spec = _halo_specs(POOL_DIM, SMALL_HALO)
    ng = len(POOL_WINDOWS)
    return pl.pallas_call(
        _odd_post_kernel,
        grid=(N_TILES,),
        in_specs=[_tile_spec(ATT_W), _tile_spec(POOL_DIM), pp_spec, pn_spec,
                  _full_spec((ng, POOL_GROUP, POOL_GROUP)), _full_spec((ng, POOL_GROUP)),
                  _full_spec((1, POOL_DIM)), _full_spec((D_MODEL, D_MODEL)),
                  _tile_spec(D_MODEL), _mod_spec(), _full_spec((1, D_MODEL)),
                  _full_spec((D_MODEL, ROUTE_W)), _full_spec((D_MODEL, ROUTE_W)),
                  _full_spec((1, ROUTE_W))],
        out_specs=_post_out_specs(),
        out_shape=_POST_OUT_SHAPES,
        scratch_shapes=[pltpu.VMEM((ROW_TILE + 2 * SMALL_HALO, POOL_DIM), F32), _class_count_scratch()],
        compiler_params=_cparams(("arbitrary",)),
        name="odd_post",
    )(att, pool_in, pool_in, pool_in, w_pool, b_pool, s_pool, w_out, x, mod_i, g1, wr_hi, wr_lo, br)


def _moe_plan(hx):
    cls = hx[:, D_MODEL + CLS_LANE].astype(jnp.int32)
    rank = hx[:, D_MODEL + RANK_LANE].astype(jnp.int32)
    counts = jnp.sum((cls[:, None] == jnp.arange(N_CLASSES)[None, :]).astype(jnp.int32), axis=0)
    tiles = (counts + MOE_TM - 1) // MOE_TM
    tile_end = jnp.cumsum(tiles)
    dest = (tile_end - tiles)[cls] * MOE_TM + rank
    j = jnp.arange(MOE_TILES)
    used = tile_end[-1]
    j_eff = jnp.minimum(j, jnp.maximum(used - 1, 0))
    cls_of_tile = jnp.sum((j_eff[:, None] >= tile_end[None, :]).astype(jnp.int32), axis=1)
    cls_of_tile = jnp.minimum(cls_of_tile, N_CLASSES - 1)
    grp = cls_of_tile // PAIRS_PER_GROUP
    pair = cls_of_tile % PAIRS_PER_GROUP
    e_lo = grp * EXPERTS_PER_GROUP + jnp.array(PAIR_LO, jnp.int32)[pair]
    e_hi = grp * EXPERTS_PER_GROUP + jnp.array(PAIR_HI, jnp.int32)[pair]
    return dest, e_lo, e_hi, (j < used).astype(jnp.int32)


def _dispatch_kernel(dest_ref, hx_ref, init_ref, xs_ref, sem):
    del init_ref
    base = pl.program_id(0) * ROW_TILE

    def row_copy(r, d):
        return pltpu.make_async_copy(hx_ref.at[pl.ds(r, 1)], xs_ref.at[pl.ds(d, 1)], sem)

    def issue(r, carry):
        row_copy(r, dest_ref[base + r]).start()
        return carry

    def drain(r, carry):
        row_copy(r, 0).wait()
        return carry

    lax.fori_loop(0, ROW_TILE, issue, 0)
    lax.fori_loop(0, ROW_TILE, drain, 0)


def _dispatch(dest, hx):
    return pl.pallas_call(
        _dispatch_kernel,
        grid_spec=pltpu.PrefetchScalarGridSpec(
            num_scalar_prefetch=1,
            grid=(N_TILES,),
            in_specs=[pl.BlockSpec((ROW_TILE, HX_W), lambda t, dest: (t, 0)),
                      pl.BlockSpec(memory_space=pl.ANY)],
            out_specs=pl.BlockSpec(memory_space=pl.ANY),
            scratch_shapes=[pltpu.SemaphoreType.DMA(())]),
        out_shape=jax.ShapeDtypeStruct((MOE_ROWS, HX_W), F32),
        input_output_aliases={2: 0},
        compiler_params=_cparams(("arbitrary",)),
        name="moe_dispatch",
    )(dest, hx, jnp.zeros((MOE_ROWS, HX_W), F32))


def _moe_expert_kernel(elo_ref, ehi_ref, valid_ref, xs_ref, w1a_ref, w3a_ref, w2a_ref,
                       w1b_ref, w3b_ref, w2b_ref, ys_ref):
    j = pl.program_id(0)

    @pl.when(valid_ref[j] == 0)
    def _():
        ys_ref[...] = jnp.zeros(ys_ref.shape, F32)

    @pl.when(valid_ref[j] != 0)
    def _():
        h = xs_ref[:, 0:D_MODEL].astype(BF16)
        gates = xs_ref[:, D_MODEL:HX_W]
        lane = lax.broadcasted_iota(jnp.int32, gates.shape, 1)
        y = jnp.zeros((MOE_TM, D_MODEL), F32)
        for e_ref, w1_ref, w3_ref, w2_ref in ((elo_ref, w1a_ref, w3a_ref, w2a_ref),
                                              (ehi_ref, w1b_ref, w3b_ref, w2b_ref)):
            hid = _silu(_dot(h, w1_ref[0])) * _dot(h, w3_ref[0])
            g = jnp.sum(jnp.where(lane == e_ref[j] + GATE_LANE0, gates, 0.0), axis=-1, keepdims=True)
            y = y + g * _dot(hid.astype(BF16), w2_ref[0])
        ys_ref[...] = y


def _moe_experts(e_lo, e_hi, valid, xs_sorted, w1, w3, w2):
    up_lo = pl.BlockSpec((1, D_MODEL, D_EXPERT), lambda j, elo, ehi, valid: (elo[j], 0, 0))
    dn_lo = pl.BlockSpec((1, D_EXPERT, D_MODEL), lambda j, elo, ehi, valid: (elo[j], 0, 0))
    up_hi = pl.BlockSpec((1, D_MODEL, D_EXPERT), lambda j, elo, ehi, valid: (ehi[j], 0, 0))
    dn_hi = pl.BlockSpec((1, D_EXPERT, D_MODEL), lambda j, elo, ehi, valid: (ehi[j], 0, 0))
    return pl.pallas_call(
        _moe_expert_kernel,
        grid_spec=pltpu.PrefetchScalarGridSpec(
            num_scalar_prefetch=3,
            grid=(MOE_TILES,),
            in_specs=[pl.BlockSpec((MOE_TM, HX_W), lambda j, elo, ehi, valid: (j, 0)),
                      up_lo, up_lo, dn_lo, up_hi, up_hi, dn_hi],
            out_specs=pl.BlockSpec((MOE_TM, D_MODEL), lambda j, elo, ehi, valid: (j, 0))),
        out_shape=jax.ShapeDtypeStruct((MOE_ROWS, D_MODEL), F32),
        compiler_params=_cparams(("arbitrary",)),
        name="moe_experts",
    )(e_lo, e_hi, valid, xs_sorted, w1, w3, w2, w1, w3, w2)


def _combine_kernel(dest_ref, x_ref, mod_ref, ys_ref, o_ref, buf_ref, sem):
    base = pl.program_id(0) * ROW_TILE

    def row_copy(r, d):
        return pltpu.make_async_copy(ys_ref.at[pl.ds(d, 1)], buf_ref.at[pl.ds(r, 1)], sem)

    def issue(r, carry):
        row_copy(r, dest_ref[base + r]).start()
        return carry

    def drain(r, carry):
        row_copy(r, 0).wait()
        return carry

    lax.fori_loop(0, ROW_TILE, issue, 0)
    lax.fori_loop(0, ROW_TILE, drain, 0)
    o_ref[...] = x_ref[...] + mod_ref[0][5:6] * buf_ref[...]


def _combine(dest, x, mod_i, ys):
    return pl.pallas_call(
        _combine_kernel,
        grid_spec=pltpu.PrefetchScalarGridSpec(
            num_scalar_prefetch=1,
            grid=(N_TILES,),
            in_specs=[pl.BlockSpec((ROW_TILE, D_MODEL), lambda t, dest: (t, 0)),
                      pl.BlockSpec((1, 6, D_MODEL), lambda t, dest: (_mod_row(t), 0, 0)),
                      pl.BlockSpec(memory_space=pl.ANY)],
            out_specs=pl.BlockSpec((ROW_TILE, D_MODEL), lambda t, dest: (t, 0)),
            scratch_shapes=[pltpu.VMEM((ROW_TILE, D_MODEL), F32), pltpu.SemaphoreType.DMA(())]),
        out_shape=jax.ShapeDtypeStruct((N_ROWS, D_MODEL), F32),
        compiler_params=_cparams(("arbitrary",)),
        name="moe_combine",
    )(dest, x, mod_i, ys)


def _final_kernel(x_ref, g_ref, o_ref):
    x = x_ref[...]
    o_ref[0] = x * lax.rsqrt(jnp.mean(x * x, axis=-1, keepdims=True) + EPS) * g_ref[...]


def _final_norm(x, g):
    lat_tiles = SEQ // ROW_TILE
    return pl.pallas_call(
        _final_kernel,
        grid=(BATCH, lat_tiles),
        in_specs=[pl.BlockSpec((ROW_TILE, D_MODEL), lambda b, i: (b * TILES_PER_BATCH + 1 + i, 0)),
                  pl.BlockSpec((1, D_MODEL), lambda b, i: (0, 0))],
        out_specs=pl.BlockSpec((1, ROW_TILE, D_MODEL), lambda b, i: (b, i, 0)),
        out_shape=jax.ShapeDtypeStruct((BATCH, SEQ, D_MODEL), F32),
        compiler_params=_cparams(("parallel", "parallel")),
        name="final_norm",
    )(x, g)


_SWAP32 = tuple(list(range(8, 16)) + list(range(0, 8)) + list(range(24, 32)) + list(range(16, 24)))


def _rope_tables():
    rows = SEQ // GRID_W
    pos_row = jnp.repeat(jnp.arange(rows, dtype=F32), GRID_W)
    pos_col = jnp.tile(jnp.arange(GRID_W, dtype=F32), rows)
    inv = ROPE_BASE ** (-jnp.arange(0, AXIS_DIM, 2, dtype=F32) / AXIS_DIM)
    ang = jnp.concatenate([pos_row[:, None] * inv, pos_col[:, None] * inv], axis=-1)
    cos, sin = jnp.cos(ang), jnp.sin(ang)
    p = AXIS_PAIRS
    c32 = jnp.concatenate([cos[:, :p], cos[:, :p], cos[:, p:], cos[:, p:]], axis=-1)
    s32 = jnp.concatenate([-sin[:, :p], sin[:, :p], -sin[:, p:], sin[:, p:]], axis=-1)
    pad = HEAD_PAD - MLA_NOPE - MLA_ROPE
    cos_lat = jnp.concatenate([jnp.ones((SEQ, MLA_NOPE), F32), c32, jnp.zeros((SEQ, pad), F32)], axis=-1)
    sin_lat = jnp.concatenate([jnp.zeros((SEQ, MLA_NOPE), F32), s32, jnp.zeros((SEQ, pad), F32)], axis=-1)
    cos_ctx = jnp.concatenate([jnp.ones((CTX_LEN, MLA_NOPE + MLA_ROPE), F32),
                               jnp.zeros((CTX_LEN, pad), F32)], axis=-1)
    sin_ctx = jnp.zeros((CTX_LEN, HEAD_PAD), F32)
    cos_t = jnp.concatenate([cos_ctx, cos_lat] * BATCH, axis=0)
    sin_t = jnp.concatenate([sin_ctx, sin_lat] * BATCH, axis=0)
    return cos_t, sin_t


def _odd_weights(w_in, w_uq, w_ukv):
    swap = jnp.array(_SWAP32)
    o_kv, o_kr, o_pool = Q_LORA, Q_LORA + KV_LORA, Q_LORA + KV_LORA + MLA_ROPE
    w_kr = w_in[:, o_kr:o_pool]
    pad_l = jnp.zeros((D_MODEL, MLA_NOPE), F32)
    pad_r = jnp.zeros((D_MODEL, HEAD_PAD - MLA_NOPE - MLA_ROPE), F32)
    w_in2 = jnp.concatenate([w_in[:, :o_kr], pad_l, w_kr, pad_r, pad_l, w_kr[:, swap], pad_r,
                             w_in[:, o_pool:]], axis=-1).astype(BF16)
    wq = w_uq.reshape(Q_LORA, MLA_HEADS, MLA_NOPE + MLA_ROPE)
    zq = jnp.zeros((Q_LORA, MLA_HEADS, HEAD_PAD - MLA_NOPE - MLA_ROPE), F32)
    wq_main = jnp.concatenate([wq, zq], axis=-1).reshape(Q_LORA, QK_W).astype(BF16)
    wq_swap = jnp.concatenate([jnp.zeros((Q_LORA, MLA_HEADS, MLA_NOPE), F32),
                               wq[:, :, MLA_NOPE:][:, :, swap], zq], axis=-1).reshape(Q_LORA, QK_W).astype(BF16)
    wkv = w_ukv.reshape(KV_LORA, MLA_HEADS, MLA_NOPE + MLA_V)
    zk = jnp.zeros((KV_LORA, MLA_HEADS, HEAD_PAD - MLA_NOPE), F32)
    wk = jnp.concatenate([wkv[:, :, :MLA_NOPE], zk], axis=-1).reshape(KV_LORA, QK_W).astype(BF16)
    zv = jnp.zeros((KV_LORA, MLA_HEADS, V_AUG - MLA_V), F32)
    wv = jnp.concatenate([wkv[:, :, MLA_NOPE:], zv], axis=-1).reshape(KV_LORA, VT_W).astype(BF16)
    return w_in2, wq_main, wq_swap, wk, wv


def kernel(x, c, ctx, c_ctx, w_mod, b_mod, norm_g, ev_w_in, ev_conv_a_w, ev_conv_a_b, ev_ln_a_g, ev_ln_a_b,
           ev_conv_b_w, ev_w_out, od_w_in, od_q_norm_g, od_w_uq, od_kv_norm_g, od_w_ukv, od_w_pool, od_b_pool,
           od_s_pool, od_w_out, moe_wg, moe_bg, moe_we, moe_be, moe_w1, moe_w3, moe_w2, final_g):
    assert x.shape == (BATCH, SEQ, D_MODEL) and ctx.shape == (BATCH, CTX_LEN, D_MODEL)
    xs = jnp.concatenate([ctx[0], x[0], ctx[1], x[1]], axis=0)
    cvec = jnp.concatenate([c, c_ctx[None, :], jnp.zeros((MOD_ROWS - BATCH - 1, D_MODEL), F32)], axis=0)
    mod = _modulation(cvec, w_mod, b_mod).reshape(DEPTH, MOD_ROWS, 6, D_MODEL)
    cos_t, sin_t = _rope_tables()

    for i in range(DEPTH):
        j = i // 2
        mod_i = mod[i]
        g0 = norm_g[i, 0][None, :]
        g1 = norm_g[i, 1][None, :]
        wr = jnp.concatenate([moe_wg[i], moe_we[i],
                              jnp.zeros((D_MODEL, ROUTE_W - N_GROUPS - N_EXPERTS), F32)], axis=-1)
        wr_hi = wr.astype(BF16)
        wr_lo = (wr - wr_hi.astype(F32)).astype(BF16)
        br = jnp.concatenate([moe_bg[i], moe_be[i],
                              jnp.zeros((ROUTE_W - N_GROUPS - N_EXPERTS,), F32)])[None, :]
        if i % 2 == 0:
            a_pre, gb, gch = _even_pre(xs, mod_i, g0, ev_w_in[j].astype(BF16))
            xs, hx = _even_post(a_pre, gb, gch, ev_conv_a_w[j], ev_conv_a_b[j][None, :],
                                       ev_ln_a_g[j][None, :], ev_ln_a_b[j][None, :], ev_conv_b_w[j],
                                       ev_w_out[j].astype(BF16), xs, mod_i, g1, wr_hi, wr_lo, br)
        else:
            w_in2, wq_main, wq_swap, wk, wv = _odd_weights(od_w_in[j], od_w_uq[j], od_w_ukv[j])
            q, k, v, pool_in = _odd_pre(xs, mod_i, g0, w_in2, od_q_norm_g[j][None, :], wq_main, wq_swap,
                                        od_kv_norm_g[j][None, :], wk, wv, cos_t, sin_t)
            att = _attention(q, k, v)
            xs, hx = _odd_post(att, pool_in, od_w_pool[j].astype(BF16), od_b_pool[j],
                                      od_s_pool[j][None, :], od_w_out[j].astype(BF16), xs, mod_i, g1,
                                      wr_hi, wr_lo, br)
        dest, e_lo, e_hi, valid = _moe_plan(hx)
        ys = _moe_experts(e_lo, e_hi, valid, _dispatch(dest, hx), moe_w1[i].astype(BF16),
                          moe_w3[i].astype(BF16), moe_w2[i].astype(BF16))
        xs = _combine(dest, xs, mod_i, ys)
    return _final_norm(xs, final_g[None, :])
```

```python
import functools

import jax
import jax.numpy as jnp
from jax import lax
from jax.experimental import pallas as pl
from jax.experimental.pallas import tpu as pltpu

F32 = jnp.float32
BF16 = jnp.bfloat16

D_MODEL = 1024
BATCH = 2
SEQ = 8192
DEPTH = 4
GRID_W = 64
CTX_LEN = 256
EPS = 1e-6
CONV_A_DIM = 512
CONV_A_WIDTH = 31
CONV_B_DIM = 512
CONV_B_WIDTH = 3
MLA_HEADS = 8
MLA_NOPE = 64
MLA_ROPE = 32
MLA_V = 64
Q_LORA = 384
KV_LORA = 256
MLA_SCALE = (MLA_NOPE + MLA_ROPE) ** -0.5
AXIS_DIM = MLA_ROPE // 2
AXIS_PAIRS = AXIS_DIM // 2
ROPE_BASE = 10000.0
POOL_WINDOWS = (2, 4, 8, 16)
POOL_GROUP = 128
POOL_DIM = POOL_GROUP * len(POOL_WINDOWS)
N_GROUPS = 4
EXPERTS_PER_GROUP = 4
N_EXPERTS = 16
D_EXPERT = 512

LANES = 128
SUBLANES = 8
VMEM_LIMIT = 48 * 1024 * 1024

ROW_TILE = 256
ROWS_PER_BATCH = CTX_LEN + SEQ
N_ROWS = BATCH * ROWS_PER_BATCH
TILES_PER_BATCH = ROWS_PER_BATCH // ROW_TILE
N_TILES = N_ROWS // ROW_TILE
HEAD_PAD = LANES
QK_W = MLA_HEADS * HEAD_PAD
ATT_W = MLA_HEADS * MLA_V
V_AUG = MLA_V + 16
VT_W = MLA_HEADS * V_AUG
CONV_A_HALO = 16
SMALL_HALO = 8
ATT_TK = 2816
ATT_CHUNK = 128
LOG2E = 1.4426950408889634
ROUTE_W = LANES
GATE_LANE0 = N_GROUPS
CLS_LANE = GATE_LANE0 + N_EXPERTS
RANK_LANE = CLS_LANE + 1
GLO_LANE = RANK_LANE + 1
GHI_LANE = GLO_LANE + 1
HX_W = D_MODEL + ROUTE_W
PAIRS_PER_GROUP = 6
N_CLASSES = N_GROUPS * PAIRS_PER_GROUP
PAIR_LO = (0, 0, 0, 1, 1, 2)
PAIR_HI = (1, 2, 3, 2, 3, 3)
MOE_TM = 256
MOE_TILES = -(-(N_ROWS + N_CLASSES * (MOE_TM - 1)) // MOE_TM)
MOE_ROWS = MOE_TILES * MOE_TM
DISPATCH_ROWS = 1536
COMBINE_ROWS = 768
DMA_UNROLL = 8
MOD_ROWS = 8


def _mod_row(t):
    return jnp.where(t % TILES_PER_BATCH == 0, BATCH, t // TILES_PER_BATCH)


def _cparams(sem):
    return pltpu.CompilerParams(dimension_semantics=sem, vmem_limit_bytes=VMEM_LIMIT)


def _dot(a, b):
    return jnp.dot(a, b, preferred_element_type=F32)


def _sigmoid(x):
    return 1.0 / (1.0 + jnp.exp(-x))


def _silu(x):
    return x * _sigmoid(x)


def _norm_mod(x, g, shift, scale):
    y = x * lax.rsqrt(jnp.mean(x * x, axis=-1, keepdims=True) + EPS)
    return (y * g) * (1.0 + scale) + shift


def _mod_kernel(c_ref, w_ref, b_ref, o_ref):
    s = _silu(c_ref[...])
    o_ref[0] = jnp.dot(s, w_ref[0], preferred_element_type=F32,
                       precision=lax.Precision.HIGHEST) + b_ref[0]


def _modulation(cvec, w_mod, b_mod):
    nblk = 4
    bw = 6 * D_MODEL // nblk
    return pl.pallas_call(
        _mod_kernel,
        grid=(DEPTH, nblk),
        in_specs=[pl.BlockSpec((MOD_ROWS, D_MODEL), lambda i, j: (0, 0)),
                  pl.BlockSpec((1, D_MODEL, bw), lambda i, j: (i, 0, j)),
                  pl.BlockSpec((1, 1, bw), lambda i, j: (i, 0, j))],
        out_specs=pl.BlockSpec((1, MOD_ROWS, bw), lambda i, j: (i, 0, j)),
        out_shape=jax.ShapeDtypeStruct((DEPTH, MOD_ROWS, 6 * D_MODEL), F32),
        compiler_params=_cparams(("parallel", "parallel")),
        name="modulation",
    )(cvec, w_mod, b_mod.reshape(DEPTH, 1, 6 * D_MODEL))


def _route(h2, wr_hi_ref, wr_lo_ref, br_ref):
    hi = h2.astype(BF16)
    lo = (h2 - hi.astype(F32)).astype(BF16)
    logits = (_dot(hi, wr_hi_ref[...]) + _dot(hi, wr_lo_ref[...]) + _dot(lo, wr_hi_ref[...])
              + br_ref[...])
    lane = lax.broadcasted_iota(jnp.int32, logits.shape, 1)
    lane_f = lane.astype(F32)
    neg = -jnp.inf
    big = float(ROUTE_W)
    gl = jnp.where(lane < N_GROUPS, logits, neg)
    gmax = jnp.max(gl, axis=-1, keepdims=True)
    gidx = jnp.min(jnp.where(gl == gmax, lane_f, big), axis=-1, keepdims=True)
    g_w = 1.0 / jnp.sum(jnp.exp(gl - gmax), axis=-1, keepdims=True)
    egrp = ((lane - GATE_LANE0) // EXPERTS_PER_GROUP).astype(F32)
    in_g = (lane >= GATE_LANE0) & (lane < GATE_LANE0 + N_EXPERTS) & (egrp == gidx)
    el = jnp.where(in_g, logits, neg)
    v1 = jnp.max(el, axis=-1, keepdims=True)
    i1 = jnp.min(jnp.where(el == v1, lane_f, big), axis=-1, keepdims=True)
    el2 = jnp.where(lane_f == i1, neg, el)
    v2 = jnp.max(el2, axis=-1, keepdims=True)
    i2 = jnp.min(jnp.where(el2 == v2, lane_f, big), axis=-1, keepdims=True)
    e21 = jnp.exp(v2 - v1)
    w1 = 1.0 / (1.0 + e21)
    w2 = e21 * w1
    g1 = w1 * g_w
    g2 = w2 * g_w
    gates = jnp.where(lane_f == i1, g1, jnp.where(lane_f == i2, g2, 0.0))
    base = GATE_LANE0 + gidx * EXPERTS_PER_GROUP
    lo = jnp.minimum(i1, i2) - base
    hi = jnp.maximum(i1, i2) - base
    pair = jnp.where(lo == 0.0, 0.0, jnp.where(lo == 1.0, 3.0, 5.0)) + (hi - lo - 1.0)
    cls = gidx * PAIRS_PER_GROUP + pair
    g_lo = jnp.where(i1 < i2, g1, g2)
    g_hi = jnp.where(i1 < i2, g2, g1)
    return jnp.where(lane_f == float(GLO_LANE), g_lo, jnp.where(lane_f == float(GHI_LANE), g_hi, gates)), cls


def _residual_and_route(x, y, m, g1_ref, wr_hi_ref, wr_lo_ref, br_ref,
                        xo_ref, hx_ref, route_ref, counts_ref, cnt_ref):
    @pl.when(pl.program_id(0) == 0)
    def _():
        cnt_ref[...] = jnp.zeros(cnt_ref.shape, F32)

    x_new = x + m[2:3] * y
    xo_ref[...] = x_new
    h2 = _norm_mod(x_new, g1_ref[...], m[3:4], m[4:5])
    hx_ref[:, 0:D_MODEL] = h2
    route, cls = _route(h2, wr_hi_ref, wr_lo_ref, br_ref)
    lane_f = lax.broadcasted_iota(jnp.int32, route.shape, 1).astype(F32)
    onehot = jnp.where(lane_f == cls, 1.0, 0.0)
    row = lax.broadcasted_iota(jnp.int32, (ROW_TILE, ROW_TILE), 0)
    col = lax.broadcasted_iota(jnp.int32, (ROW_TILE, ROW_TILE), 1)
    earlier = jnp.where(col < row, 1.0, 0.0).astype(BF16)
    before = _dot(earlier, onehot.astype(BF16)) + cnt_ref[...]
    rank = jnp.sum(onehot * before, axis=-1, keepdims=True)
    cnt_ref[...] = cnt_ref[...] + jnp.sum(onehot, axis=0, keepdims=True)
    counts_ref[...] = cnt_ref[...]
    route = jnp.where(lane_f == float(CLS_LANE), cls, jnp.where(lane_f == float(RANK_LANE), rank, route))
    route_ref[...] = route
    hx_ref[:, D_MODEL:HX_W] = route


def _tile_spec(width):
    return pl.BlockSpec((ROW_TILE, width), lambda t: (t, 0))


def _full_spec(shape):
    nd = len(shape)
    return pl.BlockSpec(shape, lambda t: (0,) * nd)


def _mod_spec():
    return pl.BlockSpec((1, 6, D_MODEL), lambda t: (_mod_row(t), 0, 0))


def _halo_specs(width, halo):
    per_tile = ROW_TILE // halo
    last = N_ROWS // halo - 1
    prev = pl.BlockSpec((halo, width), lambda t: (jnp.maximum(t * per_tile - 1, 0), 0))
    nxt = pl.BlockSpec((halo, width), lambda t: (jnp.minimum((t + 1) * per_tile, last), 0))
    return prev, nxt


def _seq_flags():
    w = pl.program_id(0) % TILES_PER_BATCH
    is_start = (w == 0) | (w == 1)
    is_end = (w == 0) | (w == TILES_PER_BATCH - 1)
    seq_tile = jnp.maximum(w - 1, 0)
    return is_start, is_end, seq_tile, w == 0


_POST_OUT_SHAPES = (jax.ShapeDtypeStruct((N_ROWS, D_MODEL), F32),
                    jax.ShapeDtypeStruct((N_ROWS, HX_W), F32),
                    jax.ShapeDtypeStruct((N_ROWS, ROUTE_W), F32),
                    jax.ShapeDtypeStruct((1, ROUTE_W), F32))


def _post_out_specs():
    return (_tile_spec(D_MODEL),
            _tile_spec(HX_W),
            _tile_spec(ROUTE_W),
            pl.BlockSpec((1, ROUTE_W), lambda t: (0, 0)))


def _class_count_scratch():
    return pltpu.VMEM((1, ROUTE_W), F32)


def _even_pre_kernel(x_ref, mod_ref, g_ref, w_ref, a_ref, gb_ref, gch_ref):
    m = mod_ref[0]
    h = _norm_mod(x_ref[...], g_ref[...], m[0:1], m[1:2])
    u = _dot(h.astype(BF16), w_ref[...])
    ca, cb = CONV_A_DIM, CONV_B_DIM
    a_ref[...] = u[:, 0:ca] * _sigmoid(u[:, ca:2 * ca])
    gb_ref[...] = u[:, 2 * ca:2 * ca + cb]
    gch_ref[...] = u[:, 2 * ca + cb:2 * ca + 2 * cb] * u[:, 2 * ca + 2 * cb:2 * ca + 3 * cb]


def _even_pre(x, mod_i, g0, w_in):
    n_in = w_in.shape[1]
    return pl.pallas_call(
        _even_pre_kernel,
        grid=(N_TILES,),
        in_specs=[_tile_spec(D_MODEL), _mod_spec(), _full_spec((1, D_MODEL)),
                  _full_spec((D_MODEL, n_in))],
        out_specs=(_tile_spec(CONV_A_DIM), _tile_spec(CONV_B_DIM), _tile_spec(CONV_B_DIM)),
        out_shape=(jax.ShapeDtypeStruct((N_ROWS, CONV_A_DIM), F32),
                   jax.ShapeDtypeStruct((N_ROWS, CONV_B_DIM), F32),
                   jax.ShapeDtypeStruct((N_ROWS, CONV_B_DIM), F32)),
        compiler_params=_cparams(("parallel",)),
        name="even_pre",
    )(x, mod_i, g0, w_in)


def _even_post_kernel(a_ref, ap_ref, an_ref, gch_ref, gp_ref, gn_ref, gb_ref,
                      cwa_ref, cba_ref, lng_ref, lnb_ref, cwb_ref, wout_ref,
                      x_ref, mod_ref, g1_ref, wr_hi_ref, wr_lo_ref, br_ref,
                      xo_ref, hx_ref, route_ref, counts_ref, exta_ref, extb_ref, cnt_ref):
    is_start, is_end, _, _ = _seq_flags()
    ha, hb = CONV_A_HALO, SMALL_HALO
    exta_ref[0:ha] = jnp.where(is_start, 0.0, ap_ref[...])
    exta_ref[ha:ha + ROW_TILE] = a_ref[...]
    exta_ref[ha + ROW_TILE:2 * ha + ROW_TILE] = jnp.where(is_end, 0.0, an_ref[...])
    extb_ref[0:hb] = jnp.where(is_start, 0.0, gp_ref[...])
    extb_ref[hb:hb + ROW_TILE] = gch_ref[...]
    extb_ref[hb + ROW_TILE:2 * hb + ROW_TILE] = jnp.where(is_end, 0.0, gn_ref[...])

    half_a = (CONV_A_WIDTH - 1) // 2
    acc = jnp.zeros((ROW_TILE, CONV_A_DIM), F32) + cba_ref[...]
    for k in range(CONV_A_WIDTH):
        off = ha - half_a + k
        acc = acc + cwa_ref[k:k + 1, :] * exta_ref[off:off + ROW_TILE, :]
    mu = jnp.mean(acc, axis=-1, keepdims=True)
    xc = acc - mu
    a = xc * lax.rsqrt(jnp.mean(xc * xc, axis=-1, keepdims=True) + EPS)
    a = _silu(a * lng_ref[...] + lnb_ref[...])

    half_b = (CONV_B_WIDTH - 1) // 2
    cb = jnp.zeros((ROW_TILE, CONV_B_DIM), F32)
    for k in range(CONV_B_WIDTH):
        off = hb - half_b + k
        cb = cb + cwb_ref[k:k + 1, :] * extb_ref[off:off + ROW_TILE, :]
    b = gb_ref[...] * cb

    y = (_dot(a.astype(BF16), wout_ref[0:CONV_A_DIM, :])
         + _dot(b.astype(BF16), wout_ref[CONV_A_DIM:CONV_A_DIM + CONV_B_DIM, :]))
    _residual_and_route(x_ref[...], y, mod_ref[0], g1_ref, wr_hi_ref, wr_lo_ref, br_ref,
                        xo_ref, hx_ref, route_ref, counts_ref, cnt_ref)


def _even_post(a_pre, gb, gch, cwa, cba, lng, lnb, cwb, w_out, x, mod_i, g1, wr_hi, wr_lo, br):
    ap_spec, an_spec = _halo_specs(CONV_A_DIM, CONV_A_HALO)
    gp_spec, gn_spec = _halo_specs(CONV_B_DIM, SMALL_HALO)
    return pl.pallas_call(
        _even_post_kernel,
        grid=(N_TILES,),
        in_specs=[_tile_spec(CONV_A_DIM), ap_spec, an_spec,
                  _tile_spec(CONV_B_DIM), gp_spec, gn_spec, _tile_spec(CONV_B_DIM),
                  _full_spec((CONV_A_WIDTH, CONV_A_DIM)), _full_spec((1, CONV_A_DIM)),
                  _full_spec((1, CONV_A_DIM)), _full_spec((1, CONV_A_DIM)),
                  _full_spec((CONV_B_WIDTH, CONV_B_DIM)), _full_spec((D_MODEL, D_MODEL)),
                  _tile_spec(D_MODEL), _mod_spec(), _full_spec((1, D_MODEL)),
                  _full_spec((D_MODEL, ROUTE_W)), _full_spec((D_MODEL, ROUTE_W)),
                  _full_spec((1, ROUTE_W))],
        out_specs=_post_out_specs(),
        out_shape=_POST_OUT_SHAPES,
        scratch_shapes=[pltpu.VMEM((ROW_TILE + 2 * CONV_A_HALO, CONV_A_DIM), F32),
                        pltpu.VMEM((ROW_TILE + 2 * SMALL_HALO, CONV_B_DIM), F32),
                        _class_count_scratch()],
        compiler_params=_cparams(("arbitrary",)),
        name="even_post",
    )(a_pre, a_pre, a_pre, gch, gch, gch, gb, cwa, cba, lng, lnb, cwb, w_out,
      x, mod_i, g1, wr_hi, wr_lo, br)


def _odd_pre_kernel(x_ref, mod_ref, g_ref, w_ref, gq_ref, wqm_ref, wqs_ref, gkv_ref, wk_ref, wv_ref,
                    vaug_ref, cos_ref, sin_ref, qt_ref, k_ref, vt_ref, pool_ref):
    m = mod_ref[0]
    h = _norm_mod(x_ref[...], g_ref[...], m[0:1], m[1:2])
    u = _dot(h.astype(BF16), w_ref[...])
    o_kv = Q_LORA
    o_kr = o_kv + KV_LORA
    o_krs = o_kr + HEAD_PAD
    o_pool = o_krs + HEAD_PAD
    cos = cos_ref[...]
    sin = sin_ref[...]

    q_c = u[:, 0:o_kv]
    qn = (q_c * lax.rsqrt(jnp.mean(q_c * q_c, axis=-1, keepdims=True) + EPS) * gq_ref[...]).astype(BF16)
    q_main = _dot(qn, wqm_ref[...])
    q_swap = _dot(qn, wqs_ref[...])
    kv_c = u[:, o_kv:o_kr]
    kvn = (kv_c * lax.rsqrt(jnp.mean(kv_c * kv_c, axis=-1, keepdims=True) + EPS) * gkv_ref[...]).astype(BF16)
    k_nope = _dot(kvn, wk_ref[...])
    v = _dot(kvn, wv_ref[...]) + vaug_ref[...]
    for cc in range(ROW_TILE // ATT_CHUNK):
        vt_ref[cc] = v[cc * ATT_CHUNK:(cc + 1) * ATT_CHUNK, :].T.astype(BF16)
    k_rope = u[:, o_kr:o_krs] * cos + u[:, o_krs:o_pool] * sin
    for hd in range(MLA_HEADS):
        sl = slice(hd * HEAD_PAD, (hd + 1) * HEAD_PAD)
        qh = (q_main[:, sl] * cos + q_swap[:, sl] * sin) * (MLA_SCALE * LOG2E)
        qt_ref[0, sl, :] = qh.T.astype(BF16)
        k_ref[:, sl] = (k_nope[:, sl] + k_rope).astype(BF16)
    pool_ref[...] = u[:, o_pool:o_pool + POOL_DIM]


def _odd_pre(x, mod_i, g0, w_in, gq, wqm, wqs, gkv, wk, wv, cos_t, sin_t):
    n_in = w_in.shape[1]
    vaug = jnp.zeros((MLA_HEADS, V_AUG), F32).at[:, MLA_V].set(1.0).reshape(1, VT_W)
    return pl.pallas_call(
        _odd_pre_kernel,
        grid=(N_TILES,),
        in_specs=[_tile_spec(D_MODEL), _mod_spec(), _full_spec((1, D_MODEL)),
                  _full_spec((D_MODEL, n_in)), _full_spec((1, Q_LORA)),
                  _full_spec((Q_LORA, QK_W)), _full_spec((Q_LORA, QK_W)),
                  _full_spec((1, KV_LORA)), _full_spec((KV_LORA, QK_W)), _full_spec((KV_LORA, VT_W)),
                  _full_spec((1, VT_W)),
                  _tile_spec(HEAD_PAD), _tile_spec(HEAD_PAD)],
        out_specs=(pl.BlockSpec((1, QK_W, ROW_TILE), lambda t: (t, 0, 0)),
                   _tile_spec(QK_W),
                   pl.BlockSpec((ROW_TILE // ATT_CHUNK, VT_W, ATT_CHUNK), lambda t: (t, 0, 0)),
                   _tile_spec(POOL_DIM)),
        out_shape=(jax.ShapeDtypeStruct((N_TILES, QK_W, ROW_TILE), BF16),
                   jax.ShapeDtypeStruct((N_ROWS, QK_W), BF16),
                   jax.ShapeDtypeStruct((N_ROWS // ATT_CHUNK, VT_W, ATT_CHUNK), BF16),
                   jax.ShapeDtypeStruct((N_ROWS, POOL_DIM), F32)),
        compiler_params=_cparams(("parallel",)),
        name="odd_pre",
    )(x, mod_i, g0, w_in, gq, wqm, wqs, gkv, wk, wv, vaug, cos_t, sin_t)


def _attn_kernel(qt_ref, k_ref, vt_ref, o_ref, m_ref, acc_ref, s_ref, p_ref, a_ref):
    qi = pl.program_id(1)
    kj = pl.program_id(2)
    nk = pl.num_programs(2)

    @pl.when(kj == 0)
    def _():
        m_ref[...] = jnp.full(m_ref.shape, -jnp.inf, F32)
        acc_ref[...] = jnp.zeros(acc_ref.shape, F32)

    ctx_chunks = jnp.where(kj == 0, CTX_LEN // ATT_CHUNK, 0)
    n_chunks = jnp.where(qi == 0, ctx_chunks, ATT_TK // ATT_CHUNK)

    def stage_scores(c, slot):
        r0 = pl.multiple_of(c * ATT_CHUNK, ATT_CHUNK)
        for hd in range(MLA_HEADS):
            sl = slice(hd * HEAD_PAD, (hd + 1) * HEAD_PAD)
            s_ref[slot, hd] = _dot(k_ref[pl.ds(r0, ATT_CHUNK), sl], qt_ref[0, sl, :])

    def stage_softmax(slot):
        for hd in range(MLA_HEADS):
            hs = slice(hd, hd + 1)
            s = s_ref[slot, hd]
            m_old = m_ref[hs, :]
            m_new = jnp.maximum(m_old, jnp.max(s, axis=0, keepdims=True))
            alpha = jnp.exp2(m_old - m_new)
            p = jnp.exp2(s - m_new)
            m_ref[hs, :] = m_new
            a_ref[slot, hs, :] = alpha
            p_ref[slot, hd] = p.astype(BF16)

    def stage_pv(c, slot):
        for hd in range(MLA_HEADS):
            vs = slice(hd * V_AUG, (hd + 1) * V_AUG)
            pv = _dot(vt_ref[c, vs, :], p_ref[slot, hd])
            acc_ref[vs, :] = a_ref[slot, hd:hd + 1, :] * acc_ref[vs, :] + pv

    @pl.when(n_chunks > 0)
    def _():
        stage_scores(0, 0)
        stage_scores(1, 1)
        stage_softmax(0)

        def trip(i, carry):
            t = 2 * i
            stage_scores(t, 0)
            stage_pv(t - 2, 0)
            stage_softmax(1)
            stage_scores(t + 1, 1)
            stage_pv(t - 1, 1)
            stage_softmax(0)
            return carry

        lax.fori_loop(1, n_chunks // 2, trip, 0)
        stage_pv(n_chunks - 2, 0)
        stage_softmax(1)
        stage_pv(n_chunks - 1, 1)

    @pl.when(kj == nk - 1)
    def _():
        outs = []
        for hd in range(MLA_HEADS):
            blk = acc_ref[hd * V_AUG:(hd + 1) * V_AUG, :]
            outs.append(blk[0:MLA_V, :] / blk[MLA_V:MLA_V + 1, :])
        o_ref[...] = jnp.concatenate(outs, axis=0).T


def _attention(qt, k, vt):
    nk = ROWS_PER_BATCH // ATT_TK
    cpb = ATT_TK // ATT_CHUNK
    return pl.pallas_call(
        _attn_kernel,
        grid=(BATCH, TILES_PER_BATCH, nk),
        in_specs=[pl.BlockSpec((1, QK_W, ROW_TILE), lambda b, i, j: (b * TILES_PER_BATCH + i, 0, 0)),
                  pl.BlockSpec((ATT_TK, QK_W), lambda b, i, j: (b * nk + j, 0)),
                  pl.BlockSpec((cpb, VT_W, ATT_CHUNK), lambda b, i, j: (b * nk + j, 0, 0))],
        out_specs=pl.BlockSpec((ROW_TILE, ATT_W), lambda b, i, j: (b * TILES_PER_BATCH + i, 0)),
        out_shape=jax.ShapeDtypeStruct((N_ROWS, ATT_W), F32),
        scratch_shapes=[pltpu.VMEM((MLA_HEADS, ROW_TILE), F32),
                        pltpu.VMEM((VT_W, ROW_TILE), F32),
                        pltpu.VMEM((2, MLA_HEADS, ATT_CHUNK, ROW_TILE), F32),
                        pltpu.VMEM((2, MLA_HEADS, ATT_CHUNK, ROW_TILE), BF16),
                        pltpu.VMEM((2, MLA_HEADS, ROW_TILE), F32)],
        compiler_params=_cparams(("parallel", "parallel", "arbitrary")),
        name="attention",
    )(qt, k, vt)


def _odd_post_kernel(att_ref, p_ref, pp_ref, pn_ref, wp_ref, bp_ref, sp_ref, wout_ref,
                     x_ref, mod_ref, g1_ref, wr_hi_ref, wr_lo_ref, br_ref,
                     xo_ref, hx_ref, route_ref, counts_ref, ext_ref, cnt_ref):
    is_start, is_end, seq_tile, is_ctx = _seq_flags()
    hp = SMALL_HALO
    ext_ref[0:hp] = jnp.where(is_start, 0.0, pp_ref[...])
    ext_ref[hp:hp + ROW_TILE] = p_ref[...]
    ext_ref[hp + ROW_TILE:2 * hp + ROW_TILE] = jnp.where(is_end, 0.0, pn_ref[...])
    seq_len = jnp.where(is_ctx, CTX_LEN, SEQ)
    pos = seq_tile * ROW_TILE + lax.broadcasted_iota(jnp.int32, (ROW_TILE, 1), 0)

    y = _dot(att_ref[...].astype(BF16), wout_ref[0:ATT_W, :])
    for g, w in enumerate(POOL_WINDOWS):
        sl = slice(g * POOL_GROUP, (g + 1) * POOL_GROUP)
        ssum = jnp.zeros((ROW_TILE, POOL_GROUP), F32)
        for d in range(-(w // 2), w - w // 2):
            ssum = ssum + ext_ref[hp + d:hp + d + ROW_TILE, sl]
        lo = jnp.maximum(pos - w // 2, 0)
        hi = jnp.minimum(pos - w // 2 + w, seq_len)
        cnt = (hi - lo).astype(F32)
        pm = ssum / cnt - p_ref[:, sl]
        yg = (_dot(pm.astype(BF16), wp_ref[g]) + bp_ref[g:g + 1, :]) * sp_ref[:, sl]
        y = y + _dot(yg.astype(BF16), wout_ref[ATT_W + g * POOL_GROUP:ATT_W + (g + 1) * POOL_GROUP, :])
    _residual_and_route(x_ref[...], y, mod_ref[0], g1_ref, wr_hi_ref, wr_lo_ref, br_ref,
                        xo_ref, hx_ref, route_ref, counts_ref, cnt_ref)


def _odd_post(att, pool_in, w_pool, b_pool, s_pool, w_out, x, mod_i, g1, wr_hi, wr_lo, br):
    pp_spec, pn_spec = _halo_specs(POOL_DIM, SMALL_HALO)
    ng = len(POOL_WINDOWS)
    return pl.pallas_call(
        _odd_post_kernel,
        grid=(N_TILES,),
        in_specs=[_tile_spec(ATT_W), _tile_spec(POOL_DIM), pp_spec, pn_spec,
                  _full_spec((ng, POOL_GROUP, POOL_GROUP)), _full_spec((ng, POOL_GROUP)),
                  _full_spec((1, POOL_DIM)), _full_spec((D_MODEL, D_MODEL)),
                  _tile_spec(D_MODEL), _mod_spec(), _full_spec((1, D_MODEL)),
                  _full_spec((D_MODEL, ROUTE_W)), _full_spec((D_MODEL, ROUTE_W)),
                  _full_spec((1, ROUTE_W))],
        out_specs=_post_out_specs(),
        out_shape=_POST_OUT_SHAPES,
        scratch_shapes=[pltpu.VMEM((ROW_TILE + 2 * SMALL_HALO, POOL_DIM), F32), _class_count_scratch()],
        compiler_params=_cparams(("arbitrary",)),
        name="odd_post",
    )(att, pool_in, pool_in, pool_in, w_pool, b_pool, s_pool, w_out, x, mod_i, g1, wr_hi, wr_lo, br)


def _moe_plan(route, counts):
    cls = route[:, CLS_LANE].astype(jnp.int32)
    rank = route[:, RANK_LANE].astype(jnp.int32)
    tiles = (counts[0, :N_CLASSES].astype(jnp.int32) + MOE_TM - 1) // MOE_TM
    tile_end = jnp.cumsum(tiles)
    dest = (tile_end - tiles)[cls] * MOE_TM + rank
    j = jnp.arange(MOE_TILES)
    used = tile_end[-1]
    j_eff = jnp.minimum(j, jnp.maximum(used - 1, 0))
    cls_of_tile = jnp.sum((j_eff[:, None] >= tile_end[None, :]).astype(jnp.int32), axis=1)
    cls_of_tile = jnp.minimum(cls_of_tile, N_CLASSES - 1)
    grp = cls_of_tile // PAIRS_PER_GROUP
    pair = cls_of_tile % PAIRS_PER_GROUP
    e_lo = grp * EXPERTS_PER_GROUP + jnp.array(PAIR_LO, jnp.int32)[pair]
    e_hi = grp * EXPERTS_PER_GROUP + jnp.array(PAIR_HI, jnp.int32)[pair]
    return dest, e_lo, e_hi, (j < used).astype(jnp.int32)


def _move_rows(n_rows, base, dest_ref, row_copy, block_copy):
    def issue(i, carry):
        for u in range(DMA_UNROLL):
            r = i * DMA_UNROLL + u
            row_copy(r, dest_ref[base + r]).start()
        return carry

    lax.fori_loop(0, n_rows // DMA_UNROLL, issue, 0)
    block_copy.wait()


def _dispatch_kernel(dest_ref, hx_ref, init_ref, xs_ref, sem):
    del init_ref
    _move_rows(DISPATCH_ROWS, pl.program_id(0) * DISPATCH_ROWS, dest_ref,
               lambda r, d: pltpu.make_async_copy(hx_ref.at[pl.ds(r, 1)], xs_ref.at[pl.ds(d, 1)], sem),
               pltpu.make_async_copy(hx_ref, xs_ref.at[pl.ds(0, DISPATCH_ROWS)], sem))


def _dispatch(dest, hx):
    shape = (MOE_ROWS, HX_W)
    return pl.pallas_call(
        _dispatch_kernel,
        grid_spec=pltpu.PrefetchScalarGridSpec(
            num_scalar_prefetch=1,
            grid=(N_ROWS // DISPATCH_ROWS,),
            in_specs=[pl.BlockSpec((DISPATCH_ROWS, HX_W), lambda t, dest: (t, 0)),
                      pl.BlockSpec(memory_space=pl.ANY)],
            out_specs=pl.BlockSpec(memory_space=pl.ANY),
            scratch_shapes=[pltpu.SemaphoreType.DMA(())]),
        out_shape=jax.ShapeDtypeStruct(shape, F32),
        input_output_aliases={2: 0},
        compiler_params=_cparams(("arbitrary",)),
        name="moe_dispatch",
    )(dest, hx, jnp.zeros(shape, F32))


def _moe_expert_kernel(elo_ref, ehi_ref, valid_ref, xs_ref, w1a_ref, w3a_ref, w2a_ref,
                       w1b_ref, w3b_ref, w2b_ref, ys_ref):
    del elo_ref, ehi_ref
    j = pl.program_id(0)

    @pl.when(valid_ref[j] == 0)
    def _():
        ys_ref[...] = jnp.zeros(ys_ref.shape, F32)

    @pl.when(valid_ref[j] != 0)
    def _():
        h = xs_ref[:, 0:D_MODEL].astype(BF16)
        y = jnp.zeros((MOE_TM, D_MODEL), F32)
        for gate_lane, w1_ref, w3_ref, w2_ref in ((GLO_LANE, w1a_ref, w3a_ref, w2a_ref),
                                                  (GHI_LANE, w1b_ref, w3b_ref, w2b_ref)):
            hid = _silu(_dot(h, w1_ref[0])) * _dot(h, w3_ref[0])
            g = xs_ref[:, D_MODEL + gate_lane:D_MODEL + gate_lane + 1]
            y = y + g * _dot(hid.astype(BF16), w2_ref[0])
        ys_ref[...] = y


def _moe_experts(e_lo, e_hi, valid, xs_sorted, w1, w3, w2):
    up_lo = pl.BlockSpec((1, D_MODEL, D_EXPERT), lambda j, elo, ehi, valid: (elo[j], 0, 0))
    dn_lo = pl.BlockSpec((1, D_EXPERT, D_MODEL), lambda j, elo, ehi, valid: (elo[j], 0, 0))
    up_hi = pl.BlockSpec((1, D_MODEL, D_EXPERT), lambda j, elo, ehi, valid: (ehi[j], 0, 0))
    dn_hi = pl.BlockSpec((1, D_EXPERT, D_MODEL), lambda j, elo, ehi, valid: (ehi[j], 0, 0))
    return pl.pallas_call(
        _moe_expert_kernel,
        grid_spec=pltpu.PrefetchScalarGridSpec(
            num_scalar_prefetch=3,
            grid=(MOE_TILES,),
            in_specs=[pl.BlockSpec((MOE_TM, HX_W), lambda j, elo, ehi, valid: (j, 0)),
                      up_lo, up_lo, dn_lo, up_hi, up_hi, dn_hi],
            out_specs=pl.BlockSpec((MOE_TM, D_MODEL), lambda j, elo, ehi, valid: (j, 0))),
        out_shape=jax.ShapeDtypeStruct((MOE_ROWS, D_MODEL), F32),
        compiler_params=_cparams(("arbitrary",)),
        name="moe_experts",
    )(e_lo, e_hi, valid, xs_sorted, w1, w3, w2, w1, w3, w2)


def _combine_kernel(dest_ref, x_ref, mod_ref, ys_ref, o_ref, buf_ref, sem):
    t = pl.program_id(0)
    _move_rows(COMBINE_ROWS, t * COMBINE_ROWS, dest_ref,
               lambda r, d: pltpu.make_async_copy(ys_ref.at[pl.ds(d, 1)], buf_ref.at[pl.ds(r, 1)], sem),
               pltpu.make_async_copy(ys_ref.at[pl.ds(0, COMBINE_ROWS)], buf_ref, sem))
    sub = COMBINE_ROWS // ROW_TILE
    for b in range(sub):
        rs = slice(b * ROW_TILE, (b + 1) * ROW_TILE)
        m5 = mod_ref[pl.ds(_mod_row(t * sub + b), 1)][0, 5:6]
        o_ref[rs, :] = x_ref[rs, :] + m5 * buf_ref[rs, :]


def _combine(dest, x, mod_i, ys):
    return pl.pallas_call(
        _combine_kernel,
        grid_spec=pltpu.PrefetchScalarGridSpec(
            num_scalar_prefetch=1,
            grid=(N_ROWS // COMBINE_ROWS,),
            in_specs=[pl.BlockSpec((COMBINE_ROWS, D_MODEL), lambda t, dest: (t, 0)),
                      pl.BlockSpec((MOD_ROWS, 6, D_MODEL), lambda t, dest: (0, 0, 0)),
                      pl.BlockSpec(memory_space=pl.ANY)],
            out_specs=pl.BlockSpec((COMBINE_ROWS, D_MODEL), lambda t, dest: (t, 0)),
            scratch_shapes=[pltpu.VMEM((COMBINE_ROWS, D_MODEL), F32), pltpu.SemaphoreType.DMA(())]),
        out_shape=jax.ShapeDtypeStruct((N_ROWS, D_MODEL), F32),
        compiler_params=_cparams(("arbitrary",)),
        name="moe_combine",
    )(dest, x, mod_i, ys)


def _final_kernel(x_ref, g_ref, o_ref):
    x = x_ref[...]
    o_ref[0] = x * lax.rsqrt(jnp.mean(x * x, axis=-1, keepdims=True) + EPS) * g_ref[...]


def _final_norm(x, g):
    lat_tiles = SEQ // ROW_TILE
    return pl.pallas_call(
        _final_kernel,
        grid=(BATCH, lat_tiles),
        in_specs=[pl.BlockSpec((ROW_TILE, D_MODEL), lambda b, i: (b * TILES_PER_BATCH + 1 + i, 0)),
                  pl.BlockSpec((1, D_MODEL), lambda b, i: (0, 0))],
        out_specs=pl.BlockSpec((1, ROW_TILE, D_MODEL), lambda b, i: (b, i, 0)),
        out_shape=jax.ShapeDtypeStruct((BATCH, SEQ, D_MODEL), F32),
        compiler_params=_cparams(("parallel", "parallel")),
        name="final_norm",
    )(x, g)


_SWAP32 = tuple(list(range(8, 16)) + list(range(0, 8)) + list(range(24, 32)) + list(range(16, 24)))


def _rope_tables():
    rows = SEQ // GRID_W
    pos_row = jnp.repeat(jnp.arange(rows, dtype=F32), GRID_W)
    pos_col = jnp.tile(jnp.arange(GRID_W, dtype=F32), rows)
    inv = ROPE_BASE ** (-jnp.arange(0, AXIS_DIM, 2, dtype=F32) / AXIS_DIM)
    ang = jnp.concatenate([pos_row[:, None] * inv, pos_col[:, None] * inv], axis=-1)
    cos, sin = jnp.cos(ang), jnp.sin(ang)
    p = AXIS_PAIRS
    c32 = jnp.concatenate([cos[:, :p], cos[:, :p], cos[:, p:], cos[:, p:]], axis=-1)
    s32 = jnp.concatenate([-sin[:, :p], sin[:, :p], -sin[:, p:], sin[:, p:]], axis=-1)
    pad = HEAD_PAD - MLA_NOPE - MLA_ROPE
    cos_lat = jnp.concatenate([jnp.ones((SEQ, MLA_NOPE), F32), c32, jnp.zeros((SEQ, pad), F32)], axis=-1)
    sin_lat = jnp.concatenate([jnp.zeros((SEQ, MLA_NOPE), F32), s32, jnp.zeros((SEQ, pad), F32)], axis=-1)
    cos_ctx = jnp.concatenate([jnp.ones((CTX_LEN, MLA_NOPE + MLA_ROPE), F32),
                               jnp.zeros((CTX_LEN, pad), F32)], axis=-1)
    sin_ctx = jnp.zeros((CTX_LEN, HEAD_PAD), F32)
    cos_t = jnp.concatenate([cos_ctx, cos_lat] * BATCH, axis=0)
    sin_t = jnp.concatenate([sin_ctx, sin_lat] * BATCH, axis=0)
    return cos_t, sin_t


def _odd_weights(w_in, w_uq, w_ukv):
    swap = jnp.array(_SWAP32)
    o_kv, o_kr, o_pool = Q_LORA, Q_LORA + KV_LORA, Q_LORA + KV_LORA + MLA_ROPE
    w_kr = w_in[:, o_kr:o_pool]
    pad_l = jnp.zeros((D_MODEL, MLA_NOPE), F32)
    pad_r = jnp.zeros((D_MODEL, HEAD_PAD - MLA_NOPE - MLA_ROPE), F32)
    w_in2 = jnp.concatenate([w_in[:, :o_kr], pad_l, w_kr, pad_r, pad_l, w_kr[:, swap], pad_r,
                             w_in[:, o_pool:]], axis=-1).astype(BF16)
    wq = w_uq.reshape(Q_LORA, MLA_HEADS, MLA_NOPE + MLA_ROPE)
    zq = jnp.zeros((Q_LORA, MLA_HEADS, HEAD_PAD - MLA_NOPE - MLA_ROPE), F32)
    wq_main = jnp.concatenate([wq, zq], axis=-1).reshape(Q_LORA, QK_W).astype(BF16)
    wq_swap = jnp.concatenate([jnp.zeros((Q_LORA, MLA_HEADS, MLA_NOPE), F32),
                               wq[:, :, MLA_NOPE:][:, :, swap], zq], axis=-1).reshape(Q_LORA, QK_W).astype(BF16)
    wkv = w_ukv.reshape(KV_LORA, MLA_HEADS, MLA_NOPE + MLA_V)
    zk = jnp.zeros((KV_LORA, MLA_HEADS, HEAD_PAD - MLA_NOPE), F32)
    wk = jnp.concatenate([wkv[:, :, :MLA_NOPE], zk], axis=-1).reshape(KV_LORA, QK_W).astype(BF16)
    zv = jnp.zeros((KV_LORA, MLA_HEADS, V_AUG - MLA_V), F32)
    wv = jnp.concatenate([wkv[:, :, MLA_NOPE:], zv], axis=-1).reshape(KV_LORA, VT_W).astype(BF16)
    return w_in2, wq_main, wq_swap, wk, wv


def kernel(x, c, ctx, c_ctx, w_mod, b_mod, norm_g, ev_w_in, ev_conv_a_w, ev_conv_a_b, ev_ln_a_g, ev_ln_a_b,
           ev_conv_b_w, ev_w_out, od_w_in, od_q_norm_g, od_w_uq, od_kv_norm_g, od_w_ukv, od_w_pool, od_b_pool,
           od_s_pool, od_w_out, moe_wg, moe_bg, moe_we, moe_be, moe_w1, moe_w3, moe_w2, final_g):
    assert x.shape == (BATCH, SEQ, D_MODEL) and ctx.shape == (BATCH, CTX_LEN, D_MODEL)
    xs = jnp.concatenate([ctx[0], x[0], ctx[1], x[1]], axis=0)
    cvec = jnp.concatenate([c, c_ctx[None, :], jnp.zeros((MOD_ROWS - BATCH - 1, D_MODEL), F32)], axis=0)
    mod = _modulation(cvec, w_mod, b_mod).reshape(DEPTH, MOD_ROWS, 6, D_MODEL)
    cos_t, sin_t = _rope_tables()

    for i in range(DEPTH):
        j = i // 2
        mod_i = mod[i]
        g0 = norm_g[i, 0][None, :]
        g1 = norm_g[i, 1][None, :]
        wr = jnp.concatenate([moe_wg[i], moe_we[i],
                              jnp.zeros((D_MODEL, ROUTE_W - N_GROUPS - N_EXPERTS), F32)], axis=-1)
        wr_hi = wr.astype(BF16)
        wr_lo = (wr - wr_hi.astype(F32)).astype(BF16)
        br = jnp.concatenate([moe_bg[i], moe_be[i],
                              jnp.zeros((ROUTE_W - N_GROUPS - N_EXPERTS,), F32)])[None, :]
        if i % 2 == 0:
            a_pre, gb, gch = _even_pre(xs, mod_i, g0, ev_w_in[j].astype(BF16))
            xs, hx, route, counts = _even_post(a_pre, gb, gch, ev_conv_a_w[j], ev_conv_a_b[j][None, :],
                                       ev_ln_a_g[j][None, :], ev_ln_a_b[j][None, :], ev_conv_b_w[j],
                                       ev_w_out[j].astype(BF16), xs, mod_i, g1, wr_hi, wr_lo, br)
        else:
            w_in2, wq_main, wq_swap, wk, wv = _odd_weights(od_w_in[j], od_w_uq[j], od_w_ukv[j])
            q, k, v, pool_in = _odd_pre(xs, mod_i, g0, w_in2, od_q_norm_g[j][None, :], wq_main, wq_swap,
                                        od_kv_norm_g[j][None, :], wk, wv, cos_t, sin_t)
            att = _attention(q, k, v)
            xs, hx, route, counts = _odd_post(att, pool_in, od_w_pool[j].astype(BF16), od_b_pool[j],
                                      od_s_pool[j][None, :], od_w_out[j].astype(BF16), xs, mod_i, g1,
                                      wr_hi, wr_lo, br)
        dest, e_lo, e_hi, valid = _moe_plan(route, counts)
        ys = _moe_experts(e_lo, e_hi, valid, _dispatch(dest, hx), moe_w1[i].astype(BF16),
                          moe_w3[i].astype(BF16), moe_w2[i].astype(BF16))
        xs = _combine(dest, xs, mod_i, ys)
    return _final_norm(xs, final_g[None, :])
```

```python
import functools

import numpy as np
import jax
import jax.numpy as jnp
from jax import lax
from jax.experimental import pallas as pl
from jax.experimental.pallas import tpu as pltpu

F32 = jnp.float32
BF16 = jnp.bfloat16

D_MODEL = 1024
BATCH = 2
SEQ = 8192
DEPTH = 4
GRID_W = 64
CTX_LEN = 256
EPS = 1e-6
CONV_A_DIM = 512
CONV_A_WIDTH = 31
CONV_B_DIM = 512
CONV_B_WIDTH = 3
MLA_HEADS = 8
MLA_NOPE = 64
MLA_ROPE = 32
MLA_V = 64
Q_LORA = 384
KV_LORA = 256
MLA_SCALE = (MLA_NOPE + MLA_ROPE) ** -0.5
AXIS_DIM = MLA_ROPE // 2
AXIS_PAIRS = AXIS_DIM // 2
ROPE_BASE = 10000.0
POOL_WINDOWS = (2, 4, 8, 16)
POOL_GROUP = 128
POOL_DIM = POOL_GROUP * len(POOL_WINDOWS)
N_GROUPS = 4
EXPERTS_PER_GROUP = 4
N_EXPERTS = 16
D_EXPERT = 512

LANES = 128
SUBLANES = 8
VMEM_LIMIT = 48 * 1024 * 1024

ROW_TILE = 256
ROWS_PER_BATCH = CTX_LEN + SEQ
N_ROWS = BATCH * ROWS_PER_BATCH
TILES_PER_BATCH = ROWS_PER_BATCH // ROW_TILE
N_TILES = N_ROWS // ROW_TILE
HEAD_PAD = LANES
QK_W = MLA_HEADS * HEAD_PAD
ATT_W = MLA_HEADS * MLA_V
V_AUG = MLA_V + 16
VT_W = MLA_HEADS * V_AUG
CONV_A_HALO = 16
SMALL_HALO = 8
ATT_CHUNK = 128
ATT_TRIP = 4
LOG2E = 1.4426950408889634
ROUTE_W = LANES
GATE_LANE0 = N_GROUPS
CLS_LANE = GATE_LANE0 + N_EXPERTS
RANK_LANE = CLS_LANE + 1
GLO_LANE = RANK_LANE + 1
GHI_LANE = GLO_LANE + 1
META_ROW0 = (CLS_LANE // SUBLANES) * SUBLANES
HX_W = D_MODEL + ROUTE_W
PAIRS_PER_GROUP = 6
N_CLASSES = N_GROUPS * PAIRS_PER_GROUP
PAIR_LO = (0, 0, 0, 1, 1, 2)
PAIR_HI = (1, 2, 3, 2, 3, 3)
MOE_TM = 256
MOE_TILES = -(-(N_ROWS + N_CLASSES * (MOE_TM - 1)) // MOE_TM)
MOE_ROWS = MOE_TILES * MOE_TM
DISPATCH_ROWS = 1536
COMBINE_ROWS = 768
DMA_UNROLL = 8
MOD_ROWS = 8


def _mod_row(t):
    return jnp.where(t % TILES_PER_BATCH == 0, BATCH, t // TILES_PER_BATCH)


def _cparams(sem):
    return pltpu.CompilerParams(dimension_semantics=sem, vmem_limit_bytes=VMEM_LIMIT)


def _dot(a, b):
    return jnp.dot(a, b, preferred_element_type=F32)


def _sigmoid(x):
    return 1.0 / (1.0 + jnp.exp(-x))


def _silu(x):
    return x * _sigmoid(x)


def _norm_mod(x, g, shift, scale):
    y = x * lax.rsqrt(jnp.mean(x * x, axis=-1, keepdims=True) + EPS)
    return (y * g) * (1.0 + scale) + shift


def _mod_kernel(c_ref, w_ref, b_ref, o_ref):
    s = _silu(c_ref[...])
    o_ref[0] = jnp.dot(s, w_ref[0], preferred_element_type=F32,
                       precision=lax.Precision.HIGHEST) + b_ref[0]


def _modulation(cvec, w_mod, b_mod):
    nblk = 4
    bw = 6 * D_MODEL // nblk
    return pl.pallas_call(
        _mod_kernel,
        grid=(DEPTH, nblk),
        in_specs=[pl.BlockSpec((MOD_ROWS, D_MODEL), lambda i, j: (0, 0)),
                  pl.BlockSpec((1, D_MODEL, bw), lambda i, j: (i, 0, j)),
                  pl.BlockSpec((1, 1, bw), lambda i, j: (i, 0, j))],
        out_specs=pl.BlockSpec((1, MOD_ROWS, bw), lambda i, j: (i, 0, j)),
        out_shape=jax.ShapeDtypeStruct((DEPTH, MOD_ROWS, 6 * D_MODEL), F32),
        compiler_params=_cparams(("parallel", "parallel")),
        name="modulation",
    )(cvec, w_mod, b_mod.reshape(DEPTH, 1, 6 * D_MODEL))


def _route(h2, wr_hi_ref, wr_lo_ref, br_ref):
    hi = h2.astype(BF16)
    lo = (h2 - hi.astype(F32)).astype(BF16)
    logits = (_dot(hi, wr_hi_ref[...]) + _dot(hi, wr_lo_ref[...]) + _dot(lo, wr_hi_ref[...])
              + br_ref[...])
    lane = lax.broadcasted_iota(jnp.int32, logits.shape, 1)
    lane_f = lane.astype(F32)
    neg = -jnp.inf
    big = float(ROUTE_W)
    gl = jnp.where(lane < N_GROUPS, logits, neg)
    gmax = jnp.max(gl, axis=-1, keepdims=True)
    gidx = jnp.min(jnp.where(gl == gmax, lane_f, big), axis=-1, keepdims=True)
    g_w = 1.0 / jnp.sum(jnp.exp(gl - gmax), axis=-1, keepdims=True)
    egrp = ((lane - GATE_LANE0) // EXPERTS_PER_GROUP).astype(F32)
    in_g = (lane >= GATE_LANE0) & (lane < GATE_LANE0 + N_EXPERTS) & (egrp == gidx)
    el = jnp.where(in_g, logits, neg)
    v1 = jnp.max(el, axis=-1, keepdims=True)
    i1 = jnp.min(jnp.where(el == v1, lane_f, big), axis=-1, keepdims=True)
    el2 = jnp.where(lane_f == i1, neg, el)
    v2 = jnp.max(el2, axis=-1, keepdims=True)
    i2 = jnp.min(jnp.where(el2 == v2, lane_f, big), axis=-1, keepdims=True)
    e21 = jnp.exp(v2 - v1)
    w1 = 1.0 / (1.0 + e21)
    w2 = e21 * w1
    g1 = w1 * g_w
    g2 = w2 * g_w
    gates = jnp.where(lane_f == i1, g1, jnp.where(lane_f == i2, g2, 0.0))
    base = GATE_LANE0 + gidx * EXPERTS_PER_GROUP
    lo = jnp.minimum(i1, i2) - base
    hi = jnp.maximum(i1, i2) - base
    pair = jnp.where(lo == 0.0, 0.0, jnp.where(lo == 1.0, 3.0, 5.0)) + (hi - lo - 1.0)
    cls = gidx * PAIRS_PER_GROUP + pair
    g_lo = jnp.where(i1 < i2, g1, g2)
    g_hi = jnp.where(i1 < i2, g2, g1)
    return jnp.where(lane_f == float(GLO_LANE), g_lo, jnp.where(lane_f == float(GHI_LANE), g_hi, gates)), cls


def _residual_and_route(x, y, m, g1_ref, wr_hi_ref, wr_lo_ref, br_ref,
                        xo_ref, hx_ref, meta_ref, counts_ref, cnt_ref):
    @pl.when(pl.program_id(0) == 0)
    def _():
        cnt_ref[...] = jnp.zeros(cnt_ref.shape, F32)

    x_new = x + m[2:3] * y
    xo_ref[...] = x_new
    h2 = _norm_mod(x_new, g1_ref[...], m[3:4], m[4:5])
    hx_ref[:, 0:D_MODEL] = h2
    route, cls = _route(h2, wr_hi_ref, wr_lo_ref, br_ref)
    lane_f = lax.broadcasted_iota(jnp.int32, route.shape, 1).astype(F32)
    onehot = jnp.where(lane_f == cls, 1.0, 0.0)
    row = lax.broadcasted_iota(jnp.int32, (ROW_TILE, ROW_TILE), 0)
    col = lax.broadcasted_iota(jnp.int32, (ROW_TILE, ROW_TILE), 1)
    earlier = jnp.where(col < row, 1.0, 0.0).astype(BF16)
    before = _dot(earlier, onehot.astype(BF16)) + cnt_ref[...]
    rank = jnp.sum(onehot * before, axis=-1, keepdims=True)
    cnt_ref[...] = cnt_ref[...] + jnp.sum(onehot, axis=0, keepdims=True)
    counts_ref[...] = cnt_ref[...]
    route = jnp.where(lane_f == float(CLS_LANE), cls, jnp.where(lane_f == float(RANK_LANE), rank, route))
    hx_ref[:, D_MODEL:HX_W] = route
    meta_ref[0] = route.T[META_ROW0:META_ROW0 + SUBLANES, :]


def _tile_spec(width):
    return pl.BlockSpec((ROW_TILE, width), lambda t: (t, 0))


def _stream_specs(n_stream):
    if n_stream == 1:
        return [_tile_spec(D_MODEL)]
    return [pl.BlockSpec((1, ROW_TILE, D_MODEL),
                         lambda t: (t // TILES_PER_BATCH, jnp.maximum(t % TILES_PER_BATCH - 1, 0), 0)),
            pl.BlockSpec((1, CTX_LEN, D_MODEL), lambda t: (t // TILES_PER_BATCH, 0, 0))]


def _load_stream(refs):
    if len(refs) == 1:
        return refs[0][...]
    is_ctx = pl.program_id(0) % TILES_PER_BATCH == 0
    return jnp.where(is_ctx, refs[1][0], refs[0][0])


def _full_spec(shape):
    nd = len(shape)
    return pl.BlockSpec(shape, lambda t: (0,) * nd)


def _mod_spec():
    return pl.BlockSpec((1, 6, D_MODEL), lambda t: (_mod_row(t), 0, 0))


def _halo_specs(width, halo):
    per_tile = ROW_TILE // halo
    last = N_ROWS // halo - 1
    prev = pl.BlockSpec((halo, width), lambda t: (jnp.maximum(t * per_tile - 1, 0), 0))
    nxt = pl.BlockSpec((halo, width), lambda t: (jnp.minimum((t + 1) * per_tile, last), 0))
    return prev, nxt


def _seq_flags():
    w = pl.program_id(0) % TILES_PER_BATCH
    is_start = (w == 0) | (w == 1)
    is_end = (w == 0) | (w == TILES_PER_BATCH - 1)
    seq_tile = jnp.maximum(w - 1, 0)
    return is_start, is_end, seq_tile, w == 0


_POST_OUT_SHAPES = (jax.ShapeDtypeStruct((N_ROWS, D_MODEL), F32),
                    jax.ShapeDtypeStruct((N_ROWS, HX_W), F32),
                    jax.ShapeDtypeStruct((N_TILES, SUBLANES, ROW_TILE), F32),
                    jax.ShapeDtypeStruct((1, ROUTE_W), F32))


def _post_out_specs():
    return (_tile_spec(D_MODEL),
            _tile_spec(HX_W),
            pl.BlockSpec((1, SUBLANES, ROW_TILE), lambda t: (t, 0, 0)),
            pl.BlockSpec((1, ROUTE_W), lambda t: (0, 0)))


def _class_count_scratch():
    return pltpu.VMEM((1, ROUTE_W), F32)


def _even_pre_kernel(n_stream, *refs):
    mod_ref, g_ref, w_ref, a_ref, gb_ref, gch_ref = refs[n_stream:]
    m = mod_ref[0]
    h = _norm_mod(_load_stream(refs[:n_stream]), g_ref[...], m[0:1], m[1:2])
    u = _dot(h.astype(BF16), w_ref[...])
    ca, cb = CONV_A_DIM, CONV_B_DIM
    a_ref[...] = u[:, 0:ca] * _sigmoid(u[:, ca:2 * ca])
    gb_ref[...] = u[:, 2 * ca:2 * ca + cb]
    gch_ref[...] = u[:, 2 * ca + cb:2 * ca + 2 * cb] * u[:, 2 * ca + 2 * cb:2 * ca + 3 * cb]


def _even_pre(stream, mod_i, g0, w_in):
    n_in = w_in.shape[1]
    return pl.pallas_call(
        functools.partial(_even_pre_kernel, len(stream)),
        grid=(N_TILES,),
        in_specs=_stream_specs(len(stream)) + [_mod_spec(), _full_spec((1, D_MODEL)),
                                               _full_spec((D_MODEL, n_in))],
        out_specs=(_tile_spec(CONV_A_DIM), _tile_spec(CONV_B_DIM), _tile_spec(CONV_B_DIM)),
        out_shape=(jax.ShapeDtypeStruct((N_ROWS, CONV_A_DIM), F32),
                   jax.ShapeDtypeStruct((N_ROWS, CONV_B_DIM), F32),
                   jax.ShapeDtypeStruct((N_ROWS, CONV_B_DIM), F32)),
        compiler_params=_cparams(("parallel",)),
        name="even_pre",
    )(*stream, mod_i, g0, w_in)


def _even_post_kernel(n_stream, *refs):
    (a_ref, ap_ref, an_ref, gch_ref, gp_ref, gn_ref, gb_ref,
     cwa_ref, cba_ref, lng_ref, lnb_ref, cwb_ref, wout_ref,
     mod_ref, g1_ref, wr_hi_ref, wr_lo_ref, br_ref,
     xo_ref, hx_ref, meta_ref, counts_ref, exta_ref, extb_ref, cnt_ref) = refs[n_stream:]
    is_start, is_end, _, _ = _seq_flags()
    ha, hb = CONV_A_HALO, SMALL_HALO
    exta_ref[0:ha] = jnp.where(is_start, 0.0, ap_ref[...])
    exta_ref[ha:ha + ROW_TILE] = a_ref[...]
    exta_ref[ha + ROW_TILE:2 * ha + ROW_TILE] = jnp.where(is_end, 0.0, an_ref[...])
    extb_ref[0:hb] = jnp.where(is_start, 0.0, gp_ref[...])
    extb_ref[hb:hb + ROW_TILE] = gch_ref[...]
    extb_ref[hb + ROW_TILE:2 * hb + ROW_TILE] = jnp.where(is_end, 0.0, gn_ref[...])

    half_a = (CONV_A_WIDTH - 1) // 2
    acc = jnp.zeros((ROW_TILE, CONV_A_DIM), F32) + cba_ref[...]
    for k in range(CONV_A_WIDTH):
        off = ha - half_a + k
        acc = acc + cwa_ref[k:k + 1, :] * exta_ref[off:off + ROW_TILE, :]
    mu = jnp.mean(acc, axis=-1, keepdims=True)
    xc = acc - mu
    a = xc * lax.rsqrt(jnp.mean(xc * xc, axis=-1, keepdims=True) + EPS)
    a = _silu(a * lng_ref[...] + lnb_ref[...])

    half_b = (CONV_B_WIDTH - 1) // 2
    cb = jnp.zeros((ROW_TILE, CONV_B_DIM), F32)
    for k in range(CONV_B_WIDTH):
        off = hb - half_b + k
        cb = cb + cwb_ref[k:k + 1, :] * extb_ref[off:off + ROW_TILE, :]
    b = gb_ref[...] * cb

    y = (_dot(a.astype(BF16), wout_ref[0:CONV_A_DIM, :])
         + _dot(b.astype(BF16), wout_ref[CONV_A_DIM:CONV_A_DIM + CONV_B_DIM, :]))
    _residual_and_route(_load_stream(refs[:n_stream]), y, mod_ref[0], g1_ref, wr_hi_ref, wr_lo_ref, br_ref,
                        xo_ref, hx_ref, meta_ref, counts_ref, cnt_ref)


def _even_post(stream, a_pre, gb, gch, cwa, cba, lng, lnb, cwb, w_out, mod_i, g1, wr_hi, wr_lo, br):
    ap_spec, an_spec = _halo_specs(CONV_A_DIM, CONV_A_HALO)
    gp_spec, gn_spec = _halo_specs(CONV_B_DIM, SMALL_HALO)
    return pl.pallas_call(
        functools.partial(_even_post_kernel, len(stream)),
        grid=(N_TILES,),
        in_specs=_stream_specs(len(stream)) + [
                  _tile_spec(CONV_A_DIM), ap_spec, an_spec,
                  _tile_spec(CONV_B_DIM), gp_spec, gn_spec, _tile_spec(CONV_B_DIM),
                  _full_spec((CONV_A_WIDTH, CONV_A_DIM)), _full_spec((1, CONV_A_DIM)),
                  _full_spec((1, CONV_A_DIM)), _full_spec((1, CONV_A_DIM)),
                  _full_spec((CONV_B_WIDTH, CONV_B_DIM)), _full_spec((D_MODEL, D_MODEL)),
                  _mod_spec(), _full_spec((1, D_MODEL)),
                  _full_spec((D_MODEL, ROUTE_W)), _full_spec((D_MODEL, ROUTE_W)),
                  _full_spec((1, ROUTE_W))],
        out_specs=_post_out_specs(),
        out_shape=_POST_OUT_SHAPES,
        scratch_shapes=[pltpu.VMEM((ROW_TILE + 2 * CONV_A_HALO, CONV_A_DIM), F32),
                        pltpu.VMEM((ROW_TILE + 2 * SMALL_HALO, CONV_B_DIM), F32),
                        _class_count_scratch()],
        compiler_params=_cparams(("arbitrary",)),
        name="even_post",
    )(*stream, a_pre, a_pre, a_pre, gch, gch, gch, gb, cwa, cba, lng, lnb, cwb, w_out,
      mod_i, g1, wr_hi, wr_lo, br)


def _odd_pre_kernel(x_ref, mod_ref, g_ref, w_ref, gq_ref, wqm_ref, wqs_ref, gkv_ref, wk_ref, wv_ref,
                    vaug_ref, cos_ref, sin_ref, qt_ref, k_ref, vt_ref, pool_ref):
    m = mod_ref[0]
    h = _norm_mod(x_ref[...], g_ref[...], m[0:1], m[1:2])
    u = _dot(h.astype(BF16), w_ref[...])
    o_kv = Q_LORA
    o_kr = o_kv + KV_LORA
    o_krs = o_kr + HEAD_PAD
    o_pool = o_krs + HEAD_PAD
    cos = cos_ref[...]
    sin = sin_ref[...]

    q_c = u[:, 0:o_kv]
    qn = (q_c * lax.rsqrt(jnp.mean(q_c * q_c, axis=-1, keepdims=True) + EPS) * gq_ref[...]).astype(BF16)
    q_main = _dot(qn, wqm_ref[...])
    q_swap = _dot(qn, wqs_ref[...])
    kv_c = u[:, o_kv:o_kr]
    kvn = (kv_c * lax.rsqrt(jnp.mean(kv_c * kv_c, axis=-1, keepdims=True) + EPS) * gkv_ref[...]).astype(BF16)
    k_nope = _dot(kvn, wk_ref[...])
    v = _dot(kvn, wv_ref[...]) + vaug_ref[...]
    for cc in range(ROW_TILE // ATT_CHUNK):
        vt_ref[cc] = v[cc * ATT_CHUNK:(cc + 1) * ATT_CHUNK, :].T.astype(BF16)
    k_rope = u[:, o_kr:o_krs] * cos + u[:, o_krs:o_pool] * sin
    for hd in range(MLA_HEADS):
        sl = slice(hd * HEAD_PAD, (hd + 1) * HEAD_PAD)
        qh = (q_main[:, sl] * cos + q_swap[:, sl] * sin) * (MLA_SCALE * LOG2E)
        qt_ref[0, sl, :] = qh.T.astype(BF16)
        k_ref[:, sl] = (k_nope[:, sl] + k_rope).astype(BF16)
    pool_ref[...] = u[:, o_pool:o_pool + POOL_DIM]


def _odd_pre(x, mod_i, g0, w_in, gq, wqm, wqs, gkv, wk, wv, cos_t, sin_t):
    n_in = w_in.shape[1]
    vaug = jnp.zeros((MLA_HEADS, V_AUG), F32).at[:, MLA_V].set(1.0).reshape(1, VT_W)
    return pl.pallas_call(
        _odd_pre_kernel,
        grid=(N_TILES,),
        in_specs=[_tile_spec(D_MODEL), _mod_spec(), _full_spec((1, D_MODEL)),
                  _full_spec((D_MODEL, n_in)), _full_spec((1, Q_LORA)),
                  _full_spec((Q_LORA, QK_W)), _full_spec((Q_LORA, QK_W)),
                  _full_spec((1, KV_LORA)), _full_spec((KV_LORA, QK_W)), _full_spec((KV_LORA, VT_W)),
                  _full_spec((1, VT_W)),
                  pl.BlockSpec((ROW_TILE, HEAD_PAD), lambda t: (t % TILES_PER_BATCH, 0)),
                  pl.BlockSpec((ROW_TILE, HEAD_PAD), lambda t: (t % TILES_PER_BATCH, 0))],
        out_specs=(pl.BlockSpec((1, QK_W, ROW_TILE), lambda t: (t, 0, 0)),
                   _tile_spec(QK_W),
                   pl.BlockSpec((ROW_TILE // ATT_CHUNK, VT_W, ATT_CHUNK), lambda t: (t, 0, 0)),
                   _tile_spec(POOL_DIM)),
        out_shape=(jax.ShapeDtypeStruct((N_TILES, QK_W, ROW_TILE), BF16),
                   jax.ShapeDtypeStruct((N_ROWS, QK_W), BF16),
                   jax.ShapeDtypeStruct((N_ROWS // ATT_CHUNK, VT_W, ATT_CHUNK), BF16),
                   jax.ShapeDtypeStruct((N_ROWS, POOL_DIM), F32)),
        compiler_params=_cparams(("parallel",)),
        name="odd_pre",
    )(x, mod_i, g0, w_in, gq, wqm, wqs, gkv, wk, wv, vaug, cos_t, sin_t)


def _attn_kernel(qt_ref, k_ref, vt_ref, o_ref, m_ref, acc_ref, s_ref, p_ref, a_ref):
    m_ref[...] = jnp.full(m_ref.shape, -jnp.inf, F32)
    acc_ref[...] = jnp.zeros(acc_ref.shape, F32)

    n_chunks = jnp.where(pl.program_id(1) == 0, CTX_LEN // ATT_CHUNK, ROWS_PER_BATCH // ATT_CHUNK)

    def stage_scores(c, slot):
        r0 = pl.multiple_of(c * ATT_CHUNK, ATT_CHUNK)
        for hd in range(MLA_HEADS):
            sl = slice(hd * HEAD_PAD, (hd + 1) * HEAD_PAD)
            s_ref[slot, hd] = _dot(k_ref[pl.ds(r0, ATT_CHUNK), sl], qt_ref[0, sl, :])

    def stage_softmax(slot):
        for hd in range(MLA_HEADS):
            hs = slice(hd, hd + 1)
            s = s_ref[slot, hd]
            m_old = m_ref[hs, :]
            m_new = jnp.maximum(m_old, jnp.max(s, axis=0, keepdims=True))
            alpha = jnp.exp2(m_old - m_new)
            p = jnp.exp2(s - m_new)
            m_ref[hs, :] = m_new
            a_ref[slot, hs, :] = alpha
            p_ref[slot, hd] = p.astype(BF16)

    def stage_pv(c, slot):
        for hd in range(MLA_HEADS):
            vs = slice(hd * V_AUG, (hd + 1) * V_AUG)
            pv = _dot(vt_ref[c, vs, :], p_ref[slot, hd])
            acc_ref[vs, :] = a_ref[slot, hd:hd + 1, :] * acc_ref[vs, :] + pv

    stage_scores(0, 0)
    stage_scores(1, 1)
    stage_softmax(0)

    def trip(i, carry):
        t = 2 + ATT_TRIP * i
        for u in range(ATT_TRIP):
            stage_scores(t + u, u % 2)
            stage_pv(t + u - 2, u % 2)
            stage_softmax((u + 1) % 2)
        return carry

    lax.fori_loop(0, (n_chunks - 2) // ATT_TRIP, trip, 0)
    stage_pv(n_chunks - 2, 0)
    stage_softmax(1)
    stage_pv(n_chunks - 1, 1)

    outs = []
    for hd in range(MLA_HEADS):
        blk = acc_ref[hd * V_AUG:(hd + 1) * V_AUG, :]
        outs.append(blk[0:MLA_V, :] / blk[MLA_V:MLA_V + 1, :])
    o_ref[...] = jnp.concatenate(outs, axis=0).T


def _attention(qt, k, vt):
    cpb = ROWS_PER_BATCH // ATT_CHUNK
    assert (cpb - 2) % ATT_TRIP == 0 and CTX_LEN // ATT_CHUNK == 2
    return pl.pallas_call(
        _attn_kernel,
        grid=(BATCH, TILES_PER_BATCH),
        in_specs=[pl.BlockSpec((1, QK_W, ROW_TILE), lambda b, i: (b * TILES_PER_BATCH + i, 0, 0)),
                  pl.BlockSpec((ROWS_PER_BATCH, QK_W), lambda b, i: (b, 0), pipeline_mode=pl.Buffered(1)),
                  pl.BlockSpec((cpb, VT_W, ATT_CHUNK), lambda b, i: (b, 0, 0), pipeline_mode=pl.Buffered(1))],
        out_specs=pl.BlockSpec((ROW_TILE, ATT_W), lambda b, i: (b * TILES_PER_BATCH + i, 0)),
        out_shape=jax.ShapeDtypeStruct((N_ROWS, ATT_W), F32),
        scratch_shapes=[pltpu.VMEM((MLA_HEADS, ROW_TILE), F32),
                        pltpu.VMEM((VT_W, ROW_TILE), F32),
                        pltpu.VMEM((2, MLA_HEADS, ATT_CHUNK, ROW_TILE), F32),
                        pltpu.VMEM((2, MLA_HEADS, ATT_CHUNK, ROW_TILE), BF16),
                        pltpu.VMEM((2, MLA_HEADS, ROW_TILE), F32)],
        compiler_params=_cparams(("parallel", "parallel")),
        name="attention",
    )(qt, k, vt)


def _odd_post_kernel(att_ref, p_ref, pp_ref, pn_ref, wp_ref, bp_ref, sp_ref, wout_ref,
                     x_ref, mod_ref, g1_ref, wr_hi_ref, wr_lo_ref, br_ref,
                     xo_ref, hx_ref, meta_ref, counts_ref, ext_ref, cnt_ref):
    is_start, is_end, seq_tile, is_ctx = _seq_flags()
    hp = SMALL_HALO
    ext_ref[0:hp] = jnp.where(is_start, 0.0, pp_ref[...])
    ext_ref[hp:hp + ROW_TILE] = p_ref[...]
    ext_ref[hp + ROW_TILE:2 * hp + ROW_TILE] = jnp.where(is_end, 0.0, pn_ref[...])
    seq_len = jnp.where(is_ctx, CTX_LEN, SEQ)
    pos = seq_tile * ROW_TILE + lax.broadcasted_iota(jnp.int32, (ROW_TILE, 1), 0)

    y = _dot(att_ref[...].astype(BF16), wout_ref[0:ATT_W, :])
    for g, w in enumerate(POOL_WINDOWS):
        sl = slice(g * POOL_GROUP, (g + 1) * POOL_GROUP)
        ssum = jnp.zeros((ROW_TILE, POOL_GROUP), F32)
        for d in range(-(w // 2), w - w // 2):
            ssum = ssum + ext_ref[hp + d:hp + d + ROW_TILE, sl]
        lo = jnp.maximum(pos - w // 2, 0)
        hi = jnp.minimum(pos - w // 2 + w, seq_len)
        cnt = (hi - lo).astype(F32)
        pm = ssum / cnt - p_ref[:, sl]
        yg = (_dot(pm.astype(BF16), wp_ref[g]) + bp_ref[g:g + 1, :]) * sp_ref[:, sl]
        y = y + _dot(yg.astype(BF16), wout_ref[ATT_W + g * POOL_GROUP:ATT_W + (g + 1) * POOL_GROUP, :])
    _residual_and_route(x_ref[...], y, mod_ref[0], g1_ref, wr_hi_ref, wr_lo_ref, br_ref,
                        xo_ref, hx_ref, meta_ref, counts_ref, cnt_ref)


def _odd_post(att, pool_in, w_pool, b_pool, s_pool, w_out, x, mod_i, g1, wr_hi, wr_lo, br):
    pp_spec, pn_spec = _halo_specs(POOL_DIM, SMALL_HALO)
    ng = len(POOL_WINDOWS)
    return pl.pallas_call(
        _odd_post_kernel,
        grid=(N_TILES,),
        in_specs=[_tile_spec(ATT_W), _tile_spec(POOL_DIM), pp_spec, pn_spec,
                  _full_spec((ng, POOL_GROUP, POOL_GROUP)), _full_spec((ng, POOL_GROUP)),
                  _full_spec((1, POOL_DIM)), _full_spec((D_MODEL, D_MODEL)),
                  _tile_spec(D_MODEL), _mod_spec(), _full_spec((1, D_MODEL)),
                  _full_spec((D_MODEL, ROUTE_W)), _full_spec((D_MODEL, ROUTE_W)),
                  _full_spec((1, ROUTE_W))],
        out_specs=_post_out_specs(),
        out_shape=_POST_OUT_SHAPES,
        scratch_shapes=[pltpu.VMEM((ROW_TILE + 2 * SMALL_HALO, POOL_DIM), F32), _class_count_scratch()],
        compiler_params=_cparams(("arbitrary",)),
        name="odd_post",
    )(att, pool_in, pool_in, pool_in, w_pool, b_pool, s_pool, w_out, x, mod_i, g1, wr_hi, wr_lo, br)


def _moe_plan(meta, counts):
    cls = meta[:, CLS_LANE - META_ROW0, :].reshape(N_ROWS).astype(jnp.int32)
    rank = meta[:, RANK_LANE - META_ROW0, :].reshape(N_ROWS).astype(jnp.int32)
    tiles = (counts[0, :N_CLASSES].astype(jnp.int32) + MOE_TM - 1) // MOE_TM
    tile_end = jnp.cumsum(tiles)
    first_row = (tile_end - tiles) * MOE_TM
    dest = rank
    for c in range(N_CLASSES):
        dest = dest + jnp.where(cls == c, first_row[c], 0)
    j = jnp.arange(MOE_TILES)
    used = tile_end[-1]
    j_eff = jnp.minimum(j, jnp.maximum(used - 1, 0))
    cls_of_tile = jnp.sum((j_eff[:, None] >= tile_end[None, :]).astype(jnp.int32), axis=1)
    cls_of_tile = jnp.minimum(cls_of_tile, N_CLASSES - 1)
    grp = cls_of_tile // PAIRS_PER_GROUP
    pair = cls_of_tile % PAIRS_PER_GROUP
    e_lo = grp * EXPERTS_PER_GROUP + jnp.array(PAIR_LO, jnp.int32)[pair]
    e_hi = grp * EXPERTS_PER_GROUP + jnp.array(PAIR_HI, jnp.int32)[pair]
    return dest, e_lo, e_hi, (j < used).astype(jnp.int32)


def _move_rows(n_rows, base, dest_ref, row_copy, block_copy):
    def issue(i, carry):
        for u in range(DMA_UNROLL):
            r = i * DMA_UNROLL + u
            row_copy(r, dest_ref[base + r]).start()
        return carry

    lax.fori_loop(0, n_rows // DMA_UNROLL, issue, 0)
    block_copy.wait()


def _dispatch_kernel(dest_ref, hx_ref, init_ref, xs_ref, sem):
    del init_ref
    _move_rows(DISPATCH_ROWS, pl.program_id(0) * DISPATCH_ROWS, dest_ref,
               lambda r, d: pltpu.make_async_copy(hx_ref.at[pl.ds(r, 1)], xs_ref.at[pl.ds(d, 1)], sem),
               pltpu.make_async_copy(hx_ref, xs_ref.at[pl.ds(0, DISPATCH_ROWS)], sem))


def _dispatch(dest, hx):
    shape = (MOE_ROWS, HX_W)
    return pl.pallas_call(
        _dispatch_kernel,
        grid_spec=pltpu.PrefetchScalarGridSpec(
            num_scalar_prefetch=1,
            grid=(N_ROWS // DISPATCH_ROWS,),
            in_specs=[pl.BlockSpec((DISPATCH_ROWS, HX_W), lambda t, dest: (t, 0)),
                      pl.BlockSpec(memory_space=pl.ANY)],
            out_specs=pl.BlockSpec(memory_space=pl.ANY),
            scratch_shapes=[pltpu.SemaphoreType.DMA(())]),
        out_shape=jax.ShapeDtypeStruct(shape, F32),
        input_output_aliases={2: 0},
        compiler_params=_cparams(("arbitrary",)),
        name="moe_dispatch",
    )(dest, hx, jnp.zeros(shape, F32))


def _moe_expert_kernel(elo_ref, ehi_ref, valid_ref, xs_ref, w1a_ref, w3a_ref, w2a_ref,
                       w1b_ref, w3b_ref, w2b_ref, ys_ref):
    del elo_ref, ehi_ref
    j = pl.program_id(0)

    @pl.when(valid_ref[j] == 0)
    def _():
        ys_ref[...] = jnp.zeros(ys_ref.shape, F32)

    @pl.when(valid_ref[j] != 0)
    def _():
        h = xs_ref[:, 0:D_MODEL].astype(BF16)
        y = jnp.zeros((MOE_TM, D_MODEL), F32)
        for gate_lane, w1_ref, w3_ref, w2_ref in ((GLO_LANE, w1a_ref, w3a_ref, w2a_ref),
                                                  (GHI_LANE, w1b_ref, w3b_ref, w2b_ref)):
            hid = _silu(_dot(h, w1_ref[0, 0])) * _dot(h, w3_ref[0, 0])
            g = xs_ref[:, D_MODEL + gate_lane:D_MODEL + gate_lane + 1]
            y = y + g * _dot(hid.astype(BF16), w2_ref[0, 0])
        ys_ref[...] = y


def _moe_experts(e_lo, e_hi, valid, xs_sorted, w1, w3, w2, layer):
    up_lo = pl.BlockSpec((1, 1, D_MODEL, D_EXPERT), lambda j, elo, ehi, valid: (layer, elo[j], 0, 0))
    dn_lo = pl.BlockSpec((1, 1, D_EXPERT, D_MODEL), lambda j, elo, ehi, valid: (layer, elo[j], 0, 0))
    up_hi = pl.BlockSpec((1, 1, D_MODEL, D_EXPERT), lambda j, elo, ehi, valid: (layer, ehi[j], 0, 0))
    dn_hi = pl.BlockSpec((1, 1, D_EXPERT, D_MODEL), lambda j, elo, ehi, valid: (layer, ehi[j], 0, 0))
    return pl.pallas_call(
        _moe_expert_kernel,
        grid_spec=pltpu.PrefetchScalarGridSpec(
            num_scalar_prefetch=3,
            grid=(MOE_TILES,),
            in_specs=[pl.BlockSpec((MOE_TM, HX_W), lambda j, elo, ehi, valid: (j, 0)),
                      up_lo, up_lo, dn_lo, up_hi, up_hi, dn_hi],
            out_specs=pl.BlockSpec((MOE_TM, D_MODEL), lambda j, elo, ehi, valid: (j, 0))),
        out_shape=jax.ShapeDtypeStruct((MOE_ROWS, D_MODEL), F32),
        compiler_params=_cparams(("arbitrary",)),
        name="moe_experts",
    )(e_lo, e_hi, valid, xs_sorted, w1, w3, w2, w1, w3, w2)


def _combine_kernel(dest_ref, x_ref, mod_ref, ys_ref, o_ref, buf_ref, sem):
    t = pl.program_id(0)
    _move_rows(COMBINE_ROWS, t * COMBINE_ROWS, dest_ref,
               lambda r, d: pltpu.make_async_copy(ys_ref.at[pl.ds(d, 1)], buf_ref.at[pl.ds(r, 1)], sem),
               pltpu.make_async_copy(ys_ref.at[pl.ds(0, COMBINE_ROWS)], buf_ref, sem))
    sub = COMBINE_ROWS // ROW_TILE
    for b in range(sub):
        rs = slice(b * ROW_TILE, (b + 1) * ROW_TILE)
        m5 = mod_ref[pl.ds(_mod_row(t * sub + b), 1)][0, 5:6]
        o_ref[rs, :] = x_ref[rs, :] + m5 * buf_ref[rs, :]


def _combine(dest, x, mod_i, ys):
    return pl.pallas_call(
        _combine_kernel,
        grid_spec=pltpu.PrefetchScalarGridSpec(
            num_scalar_prefetch=1,
            grid=(N_ROWS // COMBINE_ROWS,),
            in_specs=[pl.BlockSpec((COMBINE_ROWS, D_MODEL), lambda t, dest: (t, 0)),
                      pl.BlockSpec((MOD_ROWS, 6, D_MODEL), lambda t, dest: (0, 0, 0)),
                      pl.BlockSpec(memory_space=pl.ANY)],
            out_specs=pl.BlockSpec((COMBINE_ROWS, D_MODEL), lambda t, dest: (t, 0)),
            scratch_shapes=[pltpu.VMEM((COMBINE_ROWS, D_MODEL), F32), pltpu.SemaphoreType.DMA(())]),
        out_shape=jax.ShapeDtypeStruct((N_ROWS, D_MODEL), F32),
        compiler_params=_cparams(("arbitrary",)),
        name="moe_combine",
    )(dest, x, mod_i, ys)


def _final_kernel(x_ref, g_ref, o_ref):
    x = x_ref[...]
    o_ref[0] = x * lax.rsqrt(jnp.mean(x * x, axis=-1, keepdims=True) + EPS) * g_ref[...]


def _final_norm(x, g):
    lat_tiles = SEQ // ROW_TILE
    return pl.pallas_call(
        _final_kernel,
        grid=(BATCH, lat_tiles),
        in_specs=[pl.BlockSpec((ROW_TILE, D_MODEL), lambda b, i: (b * TILES_PER_BATCH + 1 + i, 0)),
                  pl.BlockSpec((1, D_MODEL), lambda b, i: (0, 0))],
        out_specs=pl.BlockSpec((1, ROW_TILE, D_MODEL), lambda b, i: (b, i, 0)),
        out_shape=jax.ShapeDtypeStruct((BATCH, SEQ, D_MODEL), F32),
        compiler_params=_cparams(("parallel", "parallel")),
        name="final_norm",
    )(x, g)


_SWAP32 = tuple(list(range(8, 16)) + list(range(0, 8)) + list(range(24, 32)) + list(range(16, 24)))


def _rope_tables():
    f32 = np.float32
    rows = SEQ // GRID_W
    pos_row = np.repeat(np.arange(rows, dtype=f32), GRID_W)
    pos_col = np.tile(np.arange(GRID_W, dtype=f32), rows)
    inv = (f32(ROPE_BASE) ** (-np.arange(0, AXIS_DIM, 2, dtype=f32) / f32(AXIS_DIM))).astype(f32)
    ang = np.concatenate([pos_row[:, None] * inv, pos_col[:, None] * inv], axis=-1).astype(f32)
    cos, sin = np.cos(ang).astype(f32), np.sin(ang).astype(f32)
    p = AXIS_PAIRS
    c32 = np.concatenate([cos[:, :p], cos[:, :p], cos[:, p:], cos[:, p:]], axis=-1)
    s32 = np.concatenate([-sin[:, :p], sin[:, :p], -sin[:, p:], sin[:, p:]], axis=-1)
    pad = HEAD_PAD - MLA_NOPE - MLA_ROPE
    cos_lat = np.concatenate([np.ones((SEQ, MLA_NOPE), f32), c32, np.zeros((SEQ, pad), f32)], axis=-1)
    sin_lat = np.concatenate([np.zeros((SEQ, MLA_NOPE), f32), s32, np.zeros((SEQ, pad), f32)], axis=-1)
    cos_ctx = np.concatenate([np.ones((CTX_LEN, MLA_NOPE + MLA_ROPE), f32), np.zeros((CTX_LEN, pad), f32)], axis=-1)
    sin_ctx = np.zeros((CTX_LEN, HEAD_PAD), f32)
    return (jnp.asarray(np.concatenate([cos_ctx, cos_lat], axis=0)),
            jnp.asarray(np.concatenate([sin_ctx, sin_lat], axis=0)))


def _odd_weights(w_in, w_uq, w_ukv):
    swap = jnp.array(_SWAP32)
    o_kv, o_kr, o_pool = Q_LORA, Q_LORA + KV_LORA, Q_LORA + KV_LORA + MLA_ROPE
    w_kr = w_in[:, o_kr:o_pool]
    pad_l = jnp.zeros((D_MODEL, MLA_NOPE), F32)
    pad_r = jnp.zeros((D_MODEL, HEAD_PAD - MLA_NOPE - MLA_ROPE), F32)
    w_in2 = jnp.concatenate([w_in[:, :o_kr], pad_l, w_kr, pad_r, pad_l, w_kr[:, swap], pad_r,
                             w_in[:, o_pool:]], axis=-1).astype(BF16)
    wq = w_uq.reshape(Q_LORA, MLA_HEADS, MLA_NOPE + MLA_ROPE)
    zq = jnp.zeros((Q_LORA, MLA_HEADS, HEAD_PAD - MLA_NOPE - MLA_ROPE), F32)
    wq_main = jnp.concatenate([wq, zq], axis=-1).reshape(Q_LORA, QK_W).astype(BF16)
    wq_swap = jnp.concatenate([jnp.zeros((Q_LORA, MLA_HEADS, MLA_NOPE), F32),
                               wq[:, :, MLA_NOPE:][:, :, swap], zq], axis=-1).reshape(Q_LORA, QK_W).astype(BF16)
    wkv = w_ukv.reshape(KV_LORA, MLA_HEADS, MLA_NOPE + MLA_V)
    zk = jnp.zeros((KV_LORA, MLA_HEADS, HEAD_PAD - MLA_NOPE), F32)
    wk = jnp.concatenate([wkv[:, :, :MLA_NOPE], zk], axis=-1).reshape(KV_LORA, QK_W).astype(BF16)
    zv = jnp.zeros((KV_LORA, MLA_HEADS, V_AUG - MLA_V), F32)
    wv = jnp.concatenate([wkv[:, :, MLA_NOPE:], zv], axis=-1).reshape(KV_LORA, VT_W).astype(BF16)
    return w_in2, wq_main, wq_swap, wk, wv


def kernel(x, c, ctx, c_ctx, w_mod, b_mod, norm_g, ev_w_in, ev_conv_a_w, ev_conv_a_b, ev_ln_a_g, ev_ln_a_b,
           ev_conv_b_w, ev_w_out, od_w_in, od_q_norm_g, od_w_uq, od_kv_norm_g, od_w_ukv, od_w_pool, od_b_pool,
           od_s_pool, od_w_out, moe_wg, moe_bg, moe_we, moe_be, moe_w1, moe_w3, moe_w2, final_g):
    assert x.shape == (BATCH, SEQ, D_MODEL) and ctx.shape == (BATCH, CTX_LEN, D_MODEL)
    assert CTX_LEN == ROW_TILE
    stream = (x, ctx)
    cvec = jnp.concatenate([c, c_ctx[None, :], jnp.zeros((MOD_ROWS - BATCH - 1, D_MODEL), F32)], axis=0)
    mod = _modulation(cvec, w_mod, b_mod).reshape(DEPTH, MOD_ROWS, 6, D_MODEL)
    cos_t, sin_t = _rope_tables()
    w1_bf, w3_bf, w2_bf = moe_w1.astype(BF16), moe_w3.astype(BF16), moe_w2.astype(BF16)

    for i in range(DEPTH):
        j = i // 2
        mod_i = mod[i]
        g0 = norm_g[i, 0][None, :]
        g1 = norm_g[i, 1][None, :]
        wr = jnp.concatenate([moe_wg[i], moe_we[i],
                              jnp.zeros((D_MODEL, ROUTE_W - N_GROUPS - N_EXPERTS), F32)], axis=-1)
        wr_hi = wr.astype(BF16)
        wr_lo = (wr - wr_hi.astype(F32)).astype(BF16)
        br = jnp.concatenate([moe_bg[i], moe_be[i],
                              jnp.zeros((ROUTE_W - N_GROUPS - N_EXPERTS,), F32)])[None, :]
        if i % 2 == 0:
            a_pre, gb, gch = _even_pre(stream, mod_i, g0, ev_w_in[j].astype(BF16))
            xs, hx, meta, counts = _even_post(stream, a_pre, gb, gch, ev_conv_a_w[j], ev_conv_a_b[j][None, :],
                                              ev_ln_a_g[j][None, :], ev_ln_a_b[j][None, :], ev_conv_b_w[j],
                                              ev_w_out[j].astype(BF16), mod_i, g1, wr_hi, wr_lo, br)
        else:
            w_in2, wq_main, wq_swap, wk, wv = _odd_weights(od_w_in[j], od_w_uq[j], od_w_ukv[j])
            q, k, v, pool_in = _odd_pre(xs, mod_i, g0, w_in2, od_q_norm_g[j][None, :], wq_main, wq_swap,
                                        od_kv_norm_g[j][None, :], wk, wv, cos_t, sin_t)
            att = _attention(q, k, v)
            xs, hx, meta, counts = _odd_post(att, pool_in, od_w_pool[j].astype(BF16), od_b_pool[j],
                                      od_s_pool[j][None, :], od_w_out[j].astype(BF16), xs, mod_i, g1,
                                      wr_hi, wr_lo, br)
        dest, e_lo, e_hi, valid = _moe_plan(meta, counts)
        ys = _moe_experts(e_lo, e_hi, valid, _dispatch(dest, hx), w1_bf, w3_bf, w2_bf, i)
        xs = _combine(dest, xs, mod_i, ys)
        stream = (xs,)
    return _final_norm(xs, final_g[None, :])
```

```python
import functools

import numpy as np
import jax
import jax.numpy as jnp
from jax import lax
from jax.experimental import pallas as pl
from jax.experimental.pallas import tpu as pltpu

F32 = jnp.float32
BF16 = jnp.bfloat16

D_MODEL = 1024
BATCH = 2
SEQ = 8192
DEPTH = 4
GRID_W = 64
CTX_LEN = 256
EPS = 1e-6
CONV_A_DIM = 512
CONV_A_WIDTH = 31
CONV_B_DIM = 512
CONV_B_WIDTH = 3
MLA_HEADS = 8
MLA_NOPE = 64
MLA_ROPE = 32
MLA_V = 64
Q_LORA = 384
KV_LORA = 256
MLA_SCALE = (MLA_NOPE + MLA_ROPE) ** -0.5
AXIS_DIM = MLA_ROPE // 2
AXIS_PAIRS = AXIS_DIM // 2
ROPE_BASE = 10000.0
POOL_WINDOWS = (2, 4, 8, 16)
POOL_GROUP = 128
POOL_DIM = POOL_GROUP * len(POOL_WINDOWS)
N_GROUPS = 4
EXPERTS_PER_GROUP = 4
N_EXPERTS = 16
D_EXPERT = 512

LANES = 128
SUBLANES = 8
VMEM_LIMIT = 48 * 1024 * 1024

ROW_TILE = 256
ROWS_PER_BATCH = CTX_LEN + SEQ
N_ROWS = BATCH * ROWS_PER_BATCH
TILES_PER_BATCH = ROWS_PER_BATCH // ROW_TILE
N_TILES = N_ROWS // ROW_TILE
HEAD_PAD = LANES
QK_W = MLA_HEADS * HEAD_PAD
ATT_W = MLA_HEADS * MLA_V
V_AUG = MLA_V + 16
VT_W = MLA_HEADS * V_AUG
CONV_A_HALO = 16
SMALL_HALO = 8
ATT_CHUNK = 256
SOFTMAX_ROWS = 64
LOG2E = 1.4426950408889634
ROUTE_W = LANES
GATE_LANE0 = N_GROUPS
CLS_LANE = GATE_LANE0 + N_EXPERTS
RANK_LANE = CLS_LANE + 1
GLO_LANE = RANK_LANE + 1
GHI_LANE = GLO_LANE + 1
META_ROW0 = (CLS_LANE // SUBLANES) * SUBLANES
HX_W = D_MODEL + ROUTE_W
PAIRS_PER_GROUP = 6
N_CLASSES = N_GROUPS * PAIRS_PER_GROUP
PAIR_LO = (0, 0, 0, 1, 1, 2)
PAIR_HI = (1, 2, 3, 2, 3, 3)
MOE_TM = 256
MOE_TILES = -(-(N_ROWS + N_CLASSES * (MOE_TM - 1)) // MOE_TM)
MOE_ROWS = MOE_TILES * MOE_TM
DISPATCH_ROWS = 1536
COMBINE_ROWS = 768
DMA_UNROLL = 8
MOD_ROWS = 8


def _mod_row(t):
    return jnp.where(t % TILES_PER_BATCH == 0, BATCH, t // TILES_PER_BATCH)


def _cparams(sem):
    return pltpu.CompilerParams(dimension_semantics=sem, vmem_limit_bytes=VMEM_LIMIT)


def _dot(a, b):
    return jnp.dot(a, b, preferred_element_type=F32)


def _sigmoid(x):
    return 1.0 / (1.0 + jnp.exp(-x))


def _silu(x):
    return x * _sigmoid(x)


def _norm_mod(x, g, shift, scale):
    y = x * lax.rsqrt(jnp.mean(x * x, axis=-1, keepdims=True) + EPS)
    return (y * g) * (1.0 + scale) + shift


def _mod_kernel(c_ref, w_ref, b_ref, o_ref):
    s = _silu(c_ref[...])
    o_ref[0] = jnp.dot(s, w_ref[0], preferred_element_type=F32,
                       precision=lax.Precision.HIGHEST) + b_ref[0]


def _modulation(cvec, w_mod, b_mod):
    nblk = 4
    bw = 6 * D_MODEL // nblk
    return pl.pallas_call(
        _mod_kernel,
        grid=(DEPTH, nblk),
        in_specs=[pl.BlockSpec((MOD_ROWS, D_MODEL), lambda i, j: (0, 0)),
                  pl.BlockSpec((1, D_MODEL, bw), lambda i, j: (i, 0, j)),
                  pl.BlockSpec((1, 1, bw), lambda i, j: (i, 0, j))],
        out_specs=pl.BlockSpec((1, MOD_ROWS, bw), lambda i, j: (i, 0, j)),
        out_shape=jax.ShapeDtypeStruct((DEPTH, MOD_ROWS, 6 * D_MODEL), F32),
        compiler_params=_cparams(("parallel", "parallel")),
        name="modulation",
    )(cvec, w_mod, b_mod.reshape(DEPTH, 1, 6 * D_MODEL))


def _route(h2, wr_hi_ref, wr_lo_ref, br_ref):
    hi = h2.astype(BF16)
    lo = (h2 - hi.astype(F32)).astype(BF16)
    logits = (_dot(hi, wr_hi_ref[...]) + _dot(hi, wr_lo_ref[...]) + _dot(lo, wr_hi_ref[...])
              + br_ref[...])
    lane = lax.broadcasted_iota(jnp.int32, logits.shape, 1)
    lane_f = lane.astype(F32)
    neg = -jnp.inf
    big = float(ROUTE_W)
    gl = jnp.where(lane < N_GROUPS, logits, neg)
    gmax = jnp.max(gl, axis=-1, keepdims=True)
    gidx = jnp.min(jnp.where(gl == gmax, lane_f, big), axis=-1, keepdims=True)
    g_w = 1.0 / jnp.sum(jnp.exp(gl - gmax), axis=-1, keepdims=True)
    egrp = ((lane - GATE_LANE0) // EXPERTS_PER_GROUP).astype(F32)
    in_g = (lane >= GATE_LANE0) & (lane < GATE_LANE0 + N_EXPERTS) & (egrp == gidx)
    el = jnp.where(in_g, logits, neg)
    v1 = jnp.max(el, axis=-1, keepdims=True)
    i1 = jnp.min(jnp.where(el == v1, lane_f, big), axis=-1, keepdims=True)
    el2 = jnp.where(lane_f == i1, neg, el)
    v2 = jnp.max(el2, axis=-1, keepdims=True)
    i2 = jnp.min(jnp.where(el2 == v2, lane_f, big), axis=-1, keepdims=True)
    e21 = jnp.exp(v2 - v1)
    w1 = 1.0 / (1.0 + e21)
    w2 = e21 * w1
    g1 = w1 * g_w
    g2 = w2 * g_w
    gates = jnp.where(lane_f == i1, g1, jnp.where(lane_f == i2, g2, 0.0))
    base = GATE_LANE0 + gidx * EXPERTS_PER_GROUP
    lo = jnp.minimum(i1, i2) - base
    hi = jnp.maximum(i1, i2) - base
    pair = jnp.where(lo == 0.0, 0.0, jnp.where(lo == 1.0, 3.0, 5.0)) + (hi - lo - 1.0)
    cls = gidx * PAIRS_PER_GROUP + pair
    g_lo = jnp.where(i1 < i2, g1, g2)
    g_hi = jnp.where(i1 < i2, g2, g1)
    return jnp.where(lane_f == float(GLO_LANE), g_lo, jnp.where(lane_f == float(GHI_LANE), g_hi, gates)), cls


def _residual_and_route(x, y, m, g1_ref, wr_hi_ref, wr_lo_ref, br_ref,
                        xo_ref, hx_ref, meta_ref, counts_ref, cnt_ref):
    @pl.when(pl.program_id(0) == 0)
    def _():
        cnt_ref[...] = jnp.zeros(cnt_ref.shape, F32)

    x_new = x + m[2:3] * y
    xo_ref[...] = x_new
    h2 = _norm_mod(x_new, g1_ref[...], m[3:4], m[4:5])
    hx_ref[:, 0:D_MODEL] = h2
    route, cls = _route(h2, wr_hi_ref, wr_lo_ref, br_ref)
    lane_f = lax.broadcasted_iota(jnp.int32, route.shape, 1).astype(F32)
    onehot = jnp.where(lane_f == cls, 1.0, 0.0)
    row = lax.broadcasted_iota(jnp.int32, (ROW_TILE, ROW_TILE), 0)
    col = lax.broadcasted_iota(jnp.int32, (ROW_TILE, ROW_TILE), 1)
    earlier = jnp.where(col < row, 1.0, 0.0).astype(BF16)
    before = _dot(earlier, onehot.astype(BF16)) + cnt_ref[...]
    rank = jnp.sum(onehot * before, axis=-1, keepdims=True)
    cnt_ref[...] = cnt_ref[...] + jnp.sum(onehot, axis=0, keepdims=True)
    counts_ref[...] = cnt_ref[...]
    route = jnp.where(lane_f == float(CLS_LANE), cls, jnp.where(lane_f == float(RANK_LANE), rank, route))
    hx_ref[:, D_MODEL:HX_W] = route
    meta_ref[0] = route.T[META_ROW0:META_ROW0 + SUBLANES, :]


def _tile_spec(width):
    return pl.BlockSpec((ROW_TILE, width), lambda t: (t, 0))


def _stream_specs(n_stream):
    if n_stream == 1:
        return [_tile_spec(D_MODEL)]
    return [pl.BlockSpec((1, ROW_TILE, D_MODEL),
                         lambda t: (t // TILES_PER_BATCH, jnp.maximum(t % TILES_PER_BATCH - 1, 0), 0)),
            pl.BlockSpec((1, CTX_LEN, D_MODEL), lambda t: (t // TILES_PER_BATCH, 0, 0))]


def _load_stream(refs):
    if len(refs) == 1:
        return refs[0][...]
    is_ctx = pl.program_id(0) % TILES_PER_BATCH == 0
    return jnp.where(is_ctx, refs[1][0], refs[0][0])


def _full_spec(shape):
    nd = len(shape)
    return pl.BlockSpec(shape, lambda t: (0,) * nd)


def _mod_spec():
    return pl.BlockSpec((1, 6, D_MODEL), lambda t: (_mod_row(t), 0, 0))


def _halo_specs(width, halo):
    per_tile = ROW_TILE // halo
    last = N_ROWS // halo - 1
    prev = pl.BlockSpec((halo, width), lambda t: (jnp.maximum(t * per_tile - 1, 0), 0))
    nxt = pl.BlockSpec((halo, width), lambda t: (jnp.minimum((t + 1) * per_tile, last), 0))
    return prev, nxt


def _seq_flags():
    w = pl.program_id(0) % TILES_PER_BATCH
    is_start = (w == 0) | (w == 1)
    is_end = (w == 0) | (w == TILES_PER_BATCH - 1)
    seq_tile = jnp.maximum(w - 1, 0)
    return is_start, is_end, seq_tile, w == 0


_POST_OUT_SHAPES = (jax.ShapeDtypeStruct((N_ROWS, D_MODEL), F32),
                    jax.ShapeDtypeStruct((N_ROWS, HX_W), F32),
                    jax.ShapeDtypeStruct((N_TILES, SUBLANES, ROW_TILE), F32),
                    jax.ShapeDtypeStruct((1, ROUTE_W), F32))


def _post_out_specs():
    return (_tile_spec(D_MODEL),
            _tile_spec(HX_W),
            pl.BlockSpec((1, SUBLANES, ROW_TILE), lambda t: (t, 0, 0)),
            pl.BlockSpec((1, ROUTE_W), lambda t: (0, 0)))


def _class_count_scratch():
    return pltpu.VMEM((1, ROUTE_W), F32)


def _even_pre_kernel(n_stream, *refs):
    mod_ref, g_ref, w_ref, a_ref, gb_ref, gch_ref = refs[n_stream:]
    m = mod_ref[0]
    h = _norm_mod(_load_stream(refs[:n_stream]), g_ref[...], m[0:1], m[1:2])
    u = _dot(h.astype(BF16), w_ref[...])
    ca, cb = CONV_A_DIM, CONV_B_DIM
    a_ref[...] = u[:, 0:ca] * _sigmoid(u[:, ca:2 * ca])
    gb_ref[...] = u[:, 2 * ca:2 * ca + cb]
    gch_ref[...] = u[:, 2 * ca + cb:2 * ca + 2 * cb] * u[:, 2 * ca + 2 * cb:2 * ca + 3 * cb]


def _even_pre(stream, mod_i, g0, w_in):
    n_in = w_in.shape[1]
    return pl.pallas_call(
        functools.partial(_even_pre_kernel, len(stream)),
        grid=(N_TILES,),
        in_specs=_stream_specs(len(stream)) + [_mod_spec(), _full_spec((1, D_MODEL)),
                                               _full_spec((D_MODEL, n_in))],
        out_specs=(_tile_spec(CONV_A_DIM), _tile_spec(CONV_B_DIM), _tile_spec(CONV_B_DIM)),
        out_shape=(jax.ShapeDtypeStruct((N_ROWS, CONV_A_DIM), F32),
                   jax.ShapeDtypeStruct((N_ROWS, CONV_B_DIM), F32),
                   jax.ShapeDtypeStruct((N_ROWS, CONV_B_DIM), F32)),
        compiler_params=_cparams(("parallel",)),
        name="even_pre",
    )(*stream, mod_i, g0, w_in)


def _even_post_kernel(n_stream, *refs):
    (a_ref, ap_ref, an_ref, gch_ref, gp_ref, gn_ref, gb_ref,
     cwa_ref, cba_ref, lng_ref, lnb_ref, cwb_ref, wout_ref,
     mod_ref, g1_ref, wr_hi_ref, wr_lo_ref, br_ref,
     xo_ref, hx_ref, meta_ref, counts_ref, exta_ref, extb_ref, cnt_ref) = refs[n_stream:]
    is_start, is_end, _, _ = _seq_flags()
    ha, hb = CONV_A_HALO, SMALL_HALO
    exta_ref[0:ha] = jnp.where(is_start, 0.0, ap_ref[...])
    exta_ref[ha:ha + ROW_TILE] = a_ref[...]
    exta_ref[ha + ROW_TILE:2 * ha + ROW_TILE] = jnp.where(is_end, 0.0, an_ref[...])
    extb_ref[0:hb] = jnp.where(is_start, 0.0, gp_ref[...])
    extb_ref[hb:hb + ROW_TILE] = gch_ref[...]
    extb_ref[hb + ROW_TILE:2 * hb + ROW_TILE] = jnp.where(is_end, 0.0, gn_ref[...])

    half_a = (CONV_A_WIDTH - 1) // 2
    lead = ha - half_a
    span = ROW_TILE + SUBLANES
    assert 0 <= lead and lead + SUBLANES - 1 <= SUBLANES
    parts = []
    for cblk in range(CONV_A_DIM // LANES):
        cs = slice(cblk * LANES, (cblk + 1) * LANES)
        acc = jnp.zeros((ROW_TILE, LANES), F32) + cba_ref[:, cs]
        for b in range(SUBLANES):
            z = jnp.zeros((span, LANES), F32)
            for k in range(b, CONV_A_WIDTH, SUBLANES):
                z = z + cwa_ref[k:k + 1, cs] * exta_ref[k - b:k - b + span, cs]
            acc = acc + z[lead + b:lead + b + ROW_TILE, :]
        parts.append(acc)
    acc = jnp.concatenate(parts, axis=1)
    mu = jnp.mean(acc, axis=-1, keepdims=True)
    xc = acc - mu
    a = xc * lax.rsqrt(jnp.mean(xc * xc, axis=-1, keepdims=True) + EPS)
    a = _silu(a * lng_ref[...] + lnb_ref[...])

    half_b = (CONV_B_WIDTH - 1) // 2
    cb = jnp.zeros((ROW_TILE, CONV_B_DIM), F32)
    for k in range(CONV_B_WIDTH):
        off = hb - half_b + k
        cb = cb + cwb_ref[k:k + 1, :] * extb_ref[off:off + ROW_TILE, :]
    b = gb_ref[...] * cb

    y = (_dot(a.astype(BF16), wout_ref[0:CONV_A_DIM, :])
         + _dot(b.astype(BF16), wout_ref[CONV_A_DIM:CONV_A_DIM + CONV_B_DIM, :]))
    _residual_and_route(_load_stream(refs[:n_stream]), y, mod_ref[0], g1_ref, wr_hi_ref, wr_lo_ref, br_ref,
                        xo_ref, hx_ref, meta_ref, counts_ref, cnt_ref)


def _even_post(stream, a_pre, gb, gch, cwa, cba, lng, lnb, cwb, w_out, mod_i, g1, wr_hi, wr_lo, br):
    ap_spec, an_spec = _halo_specs(CONV_A_DIM, CONV_A_HALO)
    gp_spec, gn_spec = _halo_specs(CONV_B_DIM, SMALL_HALO)
    return pl.pallas_call(
        functools.partial(_even_post_kernel, len(stream)),
        grid=(N_TILES,),
        in_specs=_stream_specs(len(stream)) + [
                  _tile_spec(CONV_A_DIM), ap_spec, an_spec,
                  _tile_spec(CONV_B_DIM), gp_spec, gn_spec, _tile_spec(CONV_B_DIM),
                  _full_spec((CONV_A_WIDTH, CONV_A_DIM)), _full_spec((1, CONV_A_DIM)),
                  _full_spec((1, CONV_A_DIM)), _full_spec((1, CONV_A_DIM)),
                  _full_spec((CONV_B_WIDTH, CONV_B_DIM)), _full_spec((D_MODEL, D_MODEL)),
                  _mod_spec(), _full_spec((1, D_MODEL)),
                  _full_spec((D_MODEL, ROUTE_W)), _full_spec((D_MODEL, ROUTE_W)),
                  _full_spec((1, ROUTE_W))],
        out_specs=_post_out_specs(),
        out_shape=_POST_OUT_SHAPES,
        scratch_shapes=[pltpu.VMEM((ROW_TILE + 2 * CONV_A_HALO, CONV_A_DIM), F32),
                        pltpu.VMEM((ROW_TILE + 2 * SMALL_HALO, CONV_B_DIM), F32),
                        _class_count_scratch()],
        compiler_params=_cparams(("arbitrary",)),
        name="even_post",
    )(*stream, a_pre, a_pre, a_pre, gch, gch, gch, gb, cwa, cba, lng, lnb, cwb, w_out,
      mod_i, g1, wr_hi, wr_lo, br)


def _odd_pre_kernel(x_ref, mod_ref, g_ref, w_ref, gq_ref, wqm_ref, wqs_ref, gkv_ref, wk_ref, wv_ref,
                    vaug_ref, cos_ref, sin_ref, qt_ref, k_ref, vt_ref, pool_ref):
    m = mod_ref[0]
    h = _norm_mod(x_ref[...], g_ref[...], m[0:1], m[1:2])
    u = _dot(h.astype(BF16), w_ref[...])
    o_kv = Q_LORA
    o_kr = o_kv + KV_LORA
    o_krs = o_kr + HEAD_PAD
    o_pool = o_krs + HEAD_PAD
    cos = cos_ref[...]
    sin = sin_ref[...]

    q_c = u[:, 0:o_kv]
    qn = (q_c * lax.rsqrt(jnp.mean(q_c * q_c, axis=-1, keepdims=True) + EPS) * gq_ref[...]).astype(BF16)
    q_main = _dot(qn, wqm_ref[...])
    q_swap = _dot(qn, wqs_ref[...])
    kv_c = u[:, o_kv:o_kr]
    kvn = (kv_c * lax.rsqrt(jnp.mean(kv_c * kv_c, axis=-1, keepdims=True) + EPS) * gkv_ref[...]).astype(BF16)
    k_nope = _dot(kvn, wk_ref[...])
    v = _dot(kvn, wv_ref[...]) + vaug_ref[...]
    for cc in range(ROW_TILE // ATT_CHUNK):
        vt_ref[cc] = v[cc * ATT_CHUNK:(cc + 1) * ATT_CHUNK, :].T.astype(BF16)
    k_rope = u[:, o_kr:o_krs] * cos + u[:, o_krs:o_pool] * sin
    for hd in range(MLA_HEADS):
        sl = slice(hd * HEAD_PAD, (hd + 1) * HEAD_PAD)
        qh = (q_main[:, sl] * cos + q_swap[:, sl] * sin) * (MLA_SCALE * LOG2E)
        qt_ref[0, sl, :] = qh.T.astype(BF16)
        k_ref[:, sl] = (k_nope[:, sl] + k_rope).astype(BF16)
    pool_ref[...] = u[:, o_pool:o_pool + POOL_DIM]


def _odd_pre(x, mod_i, g0, w_in, gq, wqm, wqs, gkv, wk, wv, cos_t, sin_t):
    n_in = w_in.shape[1]
    vaug = jnp.zeros((MLA_HEADS, V_AUG), F32).at[:, MLA_V].set(1.0).reshape(1, VT_W)
    return pl.pallas_call(
        _odd_pre_kernel,
        grid=(N_TILES,),
        in_specs=[_tile_spec(D_MODEL), _mod_spec(), _full_spec((1, D_MODEL)),
                  _full_spec((D_MODEL, n_in)), _full_spec((1, Q_LORA)),
                  _full_spec((Q_LORA, QK_W)), _full_spec((Q_LORA, QK_W)),
                  _full_spec((1, KV_LORA)), _full_spec((KV_LORA, QK_W)), _full_spec((KV_LORA, VT_W)),
                  _full_spec((1, VT_W)),
                  pl.BlockSpec((ROW_TILE, HEAD_PAD), lambda t: (t % TILES_PER_BATCH, 0)),
                  pl.BlockSpec((ROW_TILE, HEAD_PAD), lambda t: (t % TILES_PER_BATCH, 0))],
        out_specs=(pl.BlockSpec((1, QK_W, ROW_TILE), lambda t: (t, 0, 0)),
                   _tile_spec(QK_W),
                   pl.BlockSpec((ROW_TILE // ATT_CHUNK, VT_W, ATT_CHUNK), lambda t: (t, 0, 0)),
                   _tile_spec(POOL_DIM)),
        out_shape=(jax.ShapeDtypeStruct((N_TILES, QK_W, ROW_TILE), BF16),
                   jax.ShapeDtypeStruct((N_ROWS, QK_W), BF16),
                   jax.ShapeDtypeStruct((N_ROWS // ATT_CHUNK, VT_W, ATT_CHUNK), BF16),
                   jax.ShapeDtypeStruct((N_ROWS, POOL_DIM), F32)),
        compiler_params=_cparams(("parallel",)),
        name="odd_pre",
    )(x, mod_i, g0, w_in, gq, wqm, wqs, gkv, wk, wv, vaug, cos_t, sin_t)


def _attn_kernel(qt_ref, k_ref, vt_ref, o_ref, m_ref, acc_ref, s_ref, p_ref, a_ref):
    m_ref[...] = jnp.full(m_ref.shape, -jnp.inf, F32)
    acc_ref[...] = jnp.zeros(acc_ref.shape, F32)

    def stage_scores(c, slot):
        r0 = pl.multiple_of(c * ATT_CHUNK, ATT_CHUNK)
        for hd in range(MLA_HEADS):
            sl = slice(hd * HEAD_PAD, (hd + 1) * HEAD_PAD)
            s_ref[slot, hd] = _dot(k_ref[pl.ds(r0, ATT_CHUNK), sl], qt_ref[0, sl, :])

    def stage_softmax(slot):
        for hd in range(MLA_HEADS):
            hs = slice(hd, hd + 1)
            m_old = m_ref[hs, :]
            m_new = jnp.maximum(m_old, jnp.max(s_ref[slot, hd], axis=0, keepdims=True))
            m_ref[hs, :] = m_new
            a_ref[slot, hs, :] = jnp.exp2(m_old - m_new)
            for r0 in range(0, ATT_CHUNK, SOFTMAX_ROWS):
                rows = slice(r0, r0 + SOFTMAX_ROWS)
                p_ref[slot, hd, rows, :] = jnp.exp2(s_ref[slot, hd, rows, :] - m_new).astype(BF16)

    def stage_pv(c, slot):
        for hd in range(MLA_HEADS):
            vs = slice(hd * V_AUG, (hd + 1) * V_AUG)
            pv = _dot(vt_ref[c, vs, :], p_ref[slot, hd])
            acc_ref[vs, :] = a_ref[slot, hd:hd + 1, :] * acc_ref[vs, :] + pv

    def steady(c, slot):
        stage_scores(c, slot)
        stage_pv(c - 2, slot)
        stage_softmax(1 - slot)

    is_ctx = pl.program_id(1) == 0

    @pl.when(is_ctx)
    def _():
        stage_scores(0, 0)
        stage_softmax(0)
        stage_pv(0, 0)

    @pl.when(jnp.logical_not(is_ctx))
    def _():
        n = ROWS_PER_BATCH // ATT_CHUNK
        stage_scores(0, 0)
        stage_scores(1, 1)
        stage_softmax(0)
        first = 2 + (n - 2) % 2
        for c in range(2, first):
            steady(c, c % 2)

        def trip(i, carry):
            c = first + 2 * i
            steady(c, first % 2)
            steady(c + 1, 1 - first % 2)
            return carry

        lax.fori_loop(0, (n - first) // 2, trip, 0)
        stage_pv(n - 2, (n - 2) % 2)
        stage_softmax((n - 1) % 2)
        stage_pv(n - 1, (n - 1) % 2)

    outs = []
    for hd in range(MLA_HEADS):
        blk = acc_ref[hd * V_AUG:(hd + 1) * V_AUG, :]
        outs.append(blk[0:MLA_V, :] / blk[MLA_V:MLA_V + 1, :])
    o_ref[...] = jnp.concatenate(outs, axis=0).T


def _attention(qt, k, vt):
    cpb = ROWS_PER_BATCH // ATT_CHUNK
    assert cpb >= 3 and CTX_LEN == ATT_CHUNK
    return pl.pallas_call(
        _attn_kernel,
        grid=(BATCH, TILES_PER_BATCH),
        in_specs=[pl.BlockSpec((1, QK_W, ROW_TILE), lambda b, i: (b * TILES_PER_BATCH + i, 0, 0)),
                  pl.BlockSpec((ROWS_PER_BATCH, QK_W), lambda b, i: (b, 0), pipeline_mode=pl.Buffered(1)),
                  pl.BlockSpec((cpb, VT_W, ATT_CHUNK), lambda b, i: (b, 0, 0), pipeline_mode=pl.Buffered(1))],
        out_specs=pl.BlockSpec((ROW_TILE, ATT_W), lambda b, i: (b * TILES_PER_BATCH + i, 0)),
        out_shape=jax.ShapeDtypeStruct((N_ROWS, ATT_W), F32),
        scratch_shapes=[pltpu.VMEM((MLA_HEADS, ROW_TILE), F32),
                        pltpu.VMEM((VT_W, ROW_TILE), F32),
                        pltpu.VMEM((2, MLA_HEADS, ATT_CHUNK, ROW_TILE), F32),
                        pltpu.VMEM((2, MLA_HEADS, ATT_CHUNK, ROW_TILE), BF16),
                        pltpu.VMEM((2, MLA_HEADS, ROW_TILE), F32)],
        compiler_params=_cparams(("parallel", "parallel")),
        name="attention",
    )(qt, k, vt)


def _odd_post_kernel(att_ref, p_ref, pp_ref, pn_ref, wp_ref, bp_ref, sp_ref, wout_ref,
                     x_ref, mod_ref, g1_ref, wr_hi_ref, wr_lo_ref, br_ref,
                     xo_ref, hx_ref, meta_ref, counts_ref, ext_ref, cnt_ref):
    is_start, is_end, seq_tile, is_ctx = _seq_flags()
    hp = SMALL_HALO
    ext_ref[0:hp] = jnp.where(is_start, 0.0, pp_ref[...])
    ext_ref[hp:hp + ROW_TILE] = p_ref[...]
    ext_ref[hp + ROW_TILE:2 * hp + ROW_TILE] = jnp.where(is_end, 0.0, pn_ref[...])
    seq_len = jnp.where(is_ctx, CTX_LEN, SEQ)
    pos = seq_tile * ROW_TILE + lax.broadcasted_iota(jnp.int32, (ROW_TILE, 1), 0)

    y = _dot(att_ref[...].astype(BF16), wout_ref[0:ATT_W, :])
    for g, w in enumerate(POOL_WINDOWS):
        sl = slice(g * POOL_GROUP, (g + 1) * POOL_GROUP)
        ssum = jnp.zeros((ROW_TILE, POOL_GROUP), F32)
        for d in range(-(w // 2), w - w // 2):
            ssum = ssum + ext_ref[hp + d:hp + d + ROW_TILE, sl]
        lo = jnp.maximum(pos - w // 2, 0)
        hi = jnp.minimum(pos - w // 2 + w, seq_len)
        cnt = (hi - lo).astype(F32)
        pm = ssum / cnt - p_ref[:, sl]
        yg = (_dot(pm.astype(BF16), wp_ref[g]) + bp_ref[g:g + 1, :]) * sp_ref[:, sl]
        y = y + _dot(yg.astype(BF16), wout_ref[ATT_W + g * POOL_GROUP:ATT_W + (g + 1) * POOL_GROUP, :])
    _residual_and_route(x_ref[...], y, mod_ref[0], g1_ref, wr_hi_ref, wr_lo_ref, br_ref,
                        xo_ref, hx_ref, meta_ref, counts_ref, cnt_ref)


def _odd_post(att, pool_in, w_pool, b_pool, s_pool, w_out, x, mod_i, g1, wr_hi, wr_lo, br):
    pp_spec, pn_spec = _halo_specs(POOL_DIM, SMALL_HALO)
    ng = len(POOL_WINDOWS)
    return pl.pallas_call(
        _odd_post_kernel,
        grid=(N_TILES,),
        in_specs=[_tile_spec(ATT_W), _tile_spec(POOL_DIM), pp_spec, pn_spec,
                  _full_spec((ng, POOL_GROUP, POOL_GROUP)), _full_spec((ng, POOL_GROUP)),
                  _full_spec((1, POOL_DIM)), _full_spec((D_MODEL, D_MODEL)),
                  _tile_spec(D_MODEL), _mod_spec(), _full_spec((1, D_MODEL)),
                  _full_spec((D_MODEL, ROUTE_W)), _full_spec((D_MODEL, ROUTE_W)),
                  _full_spec((1, ROUTE_W))],
        out_specs=_post_out_specs(),
        out_shape=_POST_OUT_SHAPES,
        scratch_shapes=[pltpu.VMEM((ROW_TILE + 2 * SMALL_HALO, POOL_DIM), F32), _class_count_scratch()],
        compiler_params=_cparams(("arbitrary",)),
        name="odd_post",
    )(att, pool_in, pool_in, pool_in, w_pool, b_pool, s_pool, w_out, x, mod_i, g1, wr_hi, wr_lo, br)


def _moe_plan(meta, counts):
    cls = meta[:, CLS_LANE - META_ROW0, :].reshape(N_ROWS).astype(jnp.int32)
    rank = meta[:, RANK_LANE - META_ROW0, :].reshape(N_ROWS).astype(jnp.int32)
    tiles = (counts[0, :N_CLASSES].astype(jnp.int32) + MOE_TM - 1) // MOE_TM
    tile_end = jnp.cumsum(tiles)
    first_row = (tile_end - tiles) * MOE_TM
    dest = rank
    for c in range(N_CLASSES):
        dest = dest + jnp.where(cls == c, first_row[c], 0)
    j = jnp.arange(MOE_TILES)
    used = tile_end[-1]
    j_eff = jnp.minimum(j, jnp.maximum(used - 1, 0))
    cls_of_tile = jnp.sum((j_eff[:, None] >= tile_end[None, :]).astype(jnp.int32), axis=1)
    cls_of_tile = jnp.minimum(cls_of_tile, N_CLASSES - 1)
    grp = cls_of_tile // PAIRS_PER_GROUP
    pair = cls_of_tile % PAIRS_PER_GROUP
    e_lo = grp * EXPERTS_PER_GROUP + jnp.array(PAIR_LO, jnp.int32)[pair]
    e_hi = grp * EXPERTS_PER_GROUP + jnp.array(PAIR_HI, jnp.int32)[pair]
    return dest, e_lo, e_hi, (j < used).astype(jnp.int32)


def _move_rows(n_rows, base, dest_ref, row_copy, block_copy):
    def issue(i, carry):
        for u in range(DMA_UNROLL):
            r = i * DMA_UNROLL + u
            row_copy(r, dest_ref[base + r]).start()
        return carry

    lax.fori_loop(0, n_rows // DMA_UNROLL, issue, 0)
    block_copy.wait()


def _dispatch_kernel(dest_ref, hx_ref, init_ref, xs_ref, sem):
    del init_ref
    _move_rows(DISPATCH_ROWS, pl.program_id(0) * DISPATCH_ROWS, dest_ref,
               lambda r, d: pltpu.make_async_copy(hx_ref.at[pl.ds(r, 1)], xs_ref.at[pl.ds(d, 1)], sem),
               pltpu.make_async_copy(hx_ref, xs_ref.at[pl.ds(0, DISPATCH_ROWS)], sem))


def _dispatch(dest, hx):
    shape = (MOE_ROWS, HX_W)
    return pl.pallas_call(
        _dispatch_kernel,
        grid_spec=pltpu.PrefetchScalarGridSpec(
            num_scalar_prefetch=1,
            grid=(N_ROWS // DISPATCH_ROWS,),
            in_specs=[pl.BlockSpec((DISPATCH_ROWS, HX_W), lambda t, dest: (t, 0)),
                      pl.BlockSpec(memory_space=pl.ANY)],
            out_specs=pl.BlockSpec(memory_space=pl.ANY),
            scratch_shapes=[pltpu.SemaphoreType.DMA(())]),
        out_shape=jax.ShapeDtypeStruct(shape, F32),
        input_output_aliases={2: 0},
        compiler_params=_cparams(("arbitrary",)),
        name="moe_dispatch",
    )(dest, hx, jnp.zeros(shape, F32))


def _moe_expert_kernel(elo_ref, ehi_ref, valid_ref, xs_ref, w1a_ref, w3a_ref, w2a_ref,
                       w1b_ref, w3b_ref, w2b_ref, ys_ref):
    del elo_ref, ehi_ref
    j = pl.program_id(0)

    @pl.when(valid_ref[j] == 0)
    def _():
        ys_ref[...] = jnp.zeros(ys_ref.shape, F32)

    @pl.when(valid_ref[j] != 0)
    def _():
        h = xs_ref[:, 0:D_MODEL].astype(BF16)
        y = jnp.zeros((MOE_TM, D_MODEL), F32)
        for gate_lane, w1_ref, w3_ref, w2_ref in ((GLO_LANE, w1a_ref, w3a_ref, w2a_ref),
                                                  (GHI_LANE, w1b_ref, w3b_ref, w2b_ref)):
            hid = _silu(_dot(h, w1_ref[0, 0])) * _dot(h, w3_ref[0, 0])
            g = xs_ref[:, D_MODEL + gate_lane:D_MODEL + gate_lane + 1]
            y = y + g * _dot(hid.astype(BF16), w2_ref[0, 0])
        ys_ref[...] = y


def _moe_experts(e_lo, e_hi, valid, xs_sorted, w1, w3, w2, layer):
    up_lo = pl.BlockSpec((1, 1, D_MODEL, D_EXPERT), lambda j, elo, ehi, valid: (layer, elo[j], 0, 0))
    dn_lo = pl.BlockSpec((1, 1, D_EXPERT, D_MODEL), lambda j, elo, ehi, valid: (layer, elo[j], 0, 0))
    up_hi = pl.BlockSpec((1, 1, D_MODEL, D_EXPERT), lambda j, elo, ehi, valid: (layer, ehi[j], 0, 0))
    dn_hi = pl.BlockSpec((1, 1, D_EXPERT, D_MODEL), lambda j, elo, ehi, valid: (layer, ehi[j], 0, 0))
    return pl.pallas_call(
        _moe_expert_kernel,
        grid_spec=pltpu.PrefetchScalarGridSpec(
            num_scalar_prefetch=3,
            grid=(MOE_TILES,),
            in_specs=[pl.BlockSpec((MOE_TM, HX_W), lambda j, elo, ehi, valid: (j, 0)),
                      up_lo, up_lo, dn_lo, up_hi, up_hi, dn_hi],
            out_specs=pl.BlockSpec((MOE_TM, D_MODEL), lambda j, elo, ehi, valid: (j, 0))),
        out_shape=jax.ShapeDtypeStruct((MOE_ROWS, D_MODEL), F32),
        compiler_params=_cparams(("arbitrary",)),
        name="moe_experts",
    )(e_lo, e_hi, valid, xs_sorted, w1, w3, w2, w1, w3, w2)


def _combine_kernel(dest_ref, x_ref, mod_ref, ys_ref, o_ref, buf_ref, sem):
    t = pl.program_id(0)
    _move_rows(COMBINE_ROWS, t * COMBINE_ROWS, dest_ref,
               lambda r, d: pltpu.make_async_copy(ys_ref.at[pl.ds(d, 1)], buf_ref.at[pl.ds(r, 1)], sem),
               pltpu.make_async_copy(ys_ref.at[pl.ds(0, COMBINE_ROWS)], buf_ref, sem))
    sub = COMBINE_ROWS // ROW_TILE
    for b in range(sub):
        rs = slice(b * ROW_TILE, (b + 1) * ROW_TILE)
        m5 = mod_ref[pl.ds(_mod_row(t * sub + b), 1)][0, 5:6]
        o_ref[rs, :] = x_ref[rs, :] + m5 * buf_ref[rs, :]


def _combine(dest, x, mod_i, ys):
    return pl.pallas_call(
        _combine_kernel,
        grid_spec=pltpu.PrefetchScalarGridSpec(
            num_scalar_prefetch=1,
            grid=(N_ROWS // COMBINE_ROWS,),
            in_specs=[pl.BlockSpec((COMBINE_ROWS, D_MODEL), lambda t, dest: (t, 0)),
                      pl.BlockSpec((MOD_ROWS, 6, D_MODEL), lambda t, dest: (0, 0, 0)),
                      pl.BlockSpec(memory_space=pl.ANY)],
            out_specs=pl.BlockSpec((COMBINE_ROWS, D_MODEL), lambda t, dest: (t, 0)),
            scratch_shapes=[pltpu.VMEM((COMBINE_ROWS, D_MODEL), F32), pltpu.SemaphoreType.DMA(())]),
        out_shape=jax.ShapeDtypeStruct((N_ROWS, D_MODEL), F32),
        compiler_params=_cparams(("arbitrary",)),
        name="moe_combine",
    )(dest, x, mod_i, ys)


def _final_kernel(x_ref, g_ref, o_ref):
    x = x_ref[...]
    o_ref[0] = x * lax.rsqrt(jnp.mean(x * x, axis=-1, keepdims=True) + EPS) * g_ref[...]


def _final_norm(x, g):
    lat_tiles = SEQ // ROW_TILE
    return pl.pallas_call(
        _final_kernel,
        grid=(BATCH, lat_tiles),
        in_specs=[pl.BlockSpec((ROW_TILE, D_MODEL), lambda b, i: (b * TILES_PER_BATCH + 1 + i, 0)),
                  pl.BlockSpec((1, D_MODEL), lambda b, i: (0, 0))],
        out_specs=pl.BlockSpec((1, ROW_TILE, D_MODEL), lambda b, i: (b, i, 0)),
        out_shape=jax.ShapeDtypeStruct((BATCH, SEQ, D_MODEL), F32),
        compiler_params=_cparams(("parallel", "parallel")),
        name="final_norm",
    )(x, g)


_SWAP32 = tuple(list(range(8, 16)) + list(range(0, 8)) + list(range(24, 32)) + list(range(16, 24)))


def _rope_tables():
    f32 = np.float32
    rows = SEQ // GRID_W
    pos_row = np.repeat(np.arange(rows, dtype=f32), GRID_W)
    pos_col = np.tile(np.arange(GRID_W, dtype=f32), rows)
    inv = (f32(ROPE_BASE) ** (-np.arange(0, AXIS_DIM, 2, dtype=f32) / f32(AXIS_DIM))).astype(f32)
    ang = np.concatenate([pos_row[:, None] * inv, pos_col[:, None] * inv], axis=-1).astype(f32)
    cos, sin = np.cos(ang).astype(f32), np.sin(ang).astype(f32)
    p = AXIS_PAIRS
    c32 = np.concatenate([cos[:, :p], cos[:, :p], cos[:, p:], cos[:, p:]], axis=-1)
    s32 = np.concatenate([-sin[:, :p], sin[:, :p], -sin[:, p:], sin[:, p:]], axis=-1)
    pad = HEAD_PAD - MLA_NOPE - MLA_ROPE
    cos_lat = np.concatenate([np.ones((SEQ, MLA_NOPE), f32), c32, np.zeros((SEQ, pad), f32)], axis=-1)
    sin_lat = np.concatenate([np.zeros((SEQ, MLA_NOPE), f32), s32, np.zeros((SEQ, pad), f32)], axis=-1)
    cos_ctx = np.concatenate([np.ones((CTX_LEN, MLA_NOPE + MLA_ROPE), f32), np.zeros((CTX_LEN, pad), f32)], axis=-1)
    sin_ctx = np.zeros((CTX_LEN, HEAD_PAD), f32)
    return (jnp.asarray(np.concatenate([cos_ctx, cos_lat], axis=0)),
            jnp.asarray(np.concatenate([sin_ctx, sin_lat], axis=0)))


def _odd_weights(w_in, w_uq, w_ukv):
    swap = jnp.array(_SWAP32)
    o_kv, o_kr, o_pool = Q_LORA, Q_LORA + KV_LORA, Q_LORA + KV_LORA + MLA_ROPE
    w_kr = w_in[:, o_kr:o_pool]
    pad_l = jnp.zeros((D_MODEL, MLA_NOPE), F32)
    pad_r = jnp.zeros((D_MODEL, HEAD_PAD - MLA_NOPE - MLA_ROPE), F32)
    w_in2 = jnp.concatenate([w_in[:, :o_kr], pad_l, w_kr, pad_r, pad_l, w_kr[:, swap], pad_r,
                             w_in[:, o_pool:]], axis=-1).astype(BF16)
    wq = w_uq.reshape(Q_LORA, MLA_HEADS, MLA_NOPE + MLA_ROPE)
    zq = jnp.zeros((Q_LORA, MLA_HEADS, HEAD_PAD - MLA_NOPE - MLA_ROPE), F32)
    wq_main = jnp.concatenate([wq, zq], axis=-1).reshape(Q_LORA, QK_W).astype(BF16)
    wq_swap = jnp.concatenate([jnp.zeros((Q_LORA, MLA_HEADS, MLA_NOPE), F32),
                               wq[:, :, MLA_NOPE:][:, :, swap], zq], axis=-1).reshape(Q_LORA, QK_W).astype(BF16)
    wkv = w_ukv.reshape(KV_LORA, MLA_HEADS, MLA_NOPE + MLA_V)
    zk = jnp.zeros((KV_LORA, MLA_HEADS, HEAD_PAD - MLA_NOPE), F32)
    wk = jnp.concatenate([wkv[:, :, :MLA_NOPE], zk], axis=-1).reshape(KV_LORA, QK_W).astype(BF16)
    zv = jnp.zeros((KV_LORA, MLA_HEADS, V_AUG - MLA_V), F32)
    wv = jnp.concatenate([wkv[:, :, MLA_NOPE:], zv], axis=-1).reshape(KV_LORA, VT_W).astype(BF16)
    return w_in2, wq_main, wq_swap, wk, wv


def kernel(x, c, ctx, c_ctx, w_mod, b_mod, norm_g, ev_w_in, ev_conv_a_w, ev_conv_a_b, ev_ln_a_g, ev_ln_a_b,
           ev_conv_b_w, ev_w_out, od_w_in, od_q_norm_g, od_w_uq, od_kv_norm_g, od_w_ukv, od_w_pool, od_b_pool,
           od_s_pool, od_w_out, moe_wg, moe_bg, moe_we, moe_be, moe_w1, moe_w3, moe_w2, final_g):
    assert x.shape == (BATCH, SEQ, D_MODEL) and ctx.shape == (BATCH, CTX_LEN, D_MODEL)
    assert CTX_LEN == ROW_TILE
    stream = (x, ctx)
    cvec = jnp.concatenate([c, c_ctx[None, :], jnp.zeros((MOD_ROWS - BATCH - 1, D_MODEL), F32)], axis=0)
    mod = _modulation(cvec, w_mod, b_mod).reshape(DEPTH, MOD_ROWS, 6, D_MODEL)
    cos_t, sin_t = _rope_tables()
    w1_bf, w3_bf, w2_bf = moe_w1.astype(BF16), moe_w3.astype(BF16), moe_w2.astype(BF16)

    for i in range(DEPTH):
        j = i // 2
        mod_i = mod[i]
        g0 = norm_g[i, 0][None, :]
        g1 = norm_g[i, 1][None, :]
        wr = jnp.concatenate([moe_wg[i], moe_we[i],
                              jnp.zeros((D_MODEL, ROUTE_W - N_GROUPS - N_EXPERTS), F32)], axis=-1)
        wr_hi = wr.astype(BF16)
        wr_lo = (wr - wr_hi.astype(F32)).astype(BF16)
        br = jnp.concatenate([moe_bg[i], moe_be[i],
                              jnp.zeros((ROUTE_W - N_GROUPS - N_EXPERTS,), F32)])[None, :]
        if i % 2 == 0:
            a_pre, gb, gch = _even_pre(stream, mod_i, g0, ev_w_in[j].astype(BF16))
            xs, hx, meta, counts = _even_post(stream, a_pre, gb, gch, ev_conv_a_w[j], ev_conv_a_b[j][None, :],
                                              ev_ln_a_g[j][None, :], ev_ln_a_b[j][None, :], ev_conv_b_w[j],
                                              ev_w_out[j].astype(BF16), mod_i, g1, wr_hi, wr_lo, br)
        else:
            w_in2, wq_main, wq_swap, wk, wv = _odd_weights(od_w_in[j], od_w_uq[j], od_w_ukv[j])
            q, k, v, pool_in = _odd_pre(xs, mod_i, g0, w_in2, od_q_norm_g[j][None, :], wq_main, wq_swap,
                                        od_kv_norm_g[j][None, :], wk, wv, cos_t, sin_t)
            att = _attention(q, k, v)
            xs, hx, meta, counts = _odd_post(att, pool_in, od_w_pool[j].astype(BF16), od_b_pool[j],
                                      od_s_pool[j][None, :], od_w_out[j].astype(BF16), xs, mod_i, g1,
                                      wr_hi, wr_lo, br)
        dest, e_lo, e_hi, valid = _moe_plan(meta, counts)
        ys = _moe_experts(e_lo, e_hi, valid, _dispatch(dest, hx), w1_bf, w3_bf, w2_bf, i)
        xs = _combine(dest, xs, mod_i, ys)
        stream = (xs,)
    return _final_norm(xs, final_g[None, :])
```

```python
import functools

import numpy as np
import jax
import jax.numpy as jnp
from jax import lax
from jax.experimental import pallas as pl
from jax.experimental.pallas import tpu as pltpu

F32 = jnp.float32
BF16 = jnp.bfloat16

D_MODEL = 1024
BATCH = 2
SEQ = 8192
DEPTH = 4
GRID_W = 64
CTX_LEN = 256
EPS = 1e-6
CONV_A_DIM = 512
CONV_A_WIDTH = 31
CONV_B_DIM = 512
CONV_B_WIDTH = 3
MLA_HEADS = 8
MLA_NOPE = 64
MLA_ROPE = 32
MLA_V = 64
Q_LORA = 384
KV_LORA = 256
MLA_SCALE = (MLA_NOPE + MLA_ROPE) ** -0.5
AXIS_DIM = MLA_ROPE // 2
AXIS_PAIRS = AXIS_DIM // 2
ROPE_BASE = 10000.0
POOL_WINDOWS = (2, 4, 8, 16)
POOL_GROUP = 128
POOL_DIM = POOL_GROUP * len(POOL_WINDOWS)
N_GROUPS = 4
EXPERTS_PER_GROUP = 4
N_EXPERTS = 16
D_EXPERT = 512

LANES = 128
SUBLANES = 8
VMEM_LIMIT = 48 * 1024 * 1024

ROW_TILE = 256
ROWS_PER_BATCH = CTX_LEN + SEQ
N_ROWS = BATCH * ROWS_PER_BATCH
TILES_PER_BATCH = ROWS_PER_BATCH // ROW_TILE
N_TILES = N_ROWS // ROW_TILE
HEAD_PAD = LANES
QK_W = MLA_HEADS * HEAD_PAD
ATT_W = MLA_HEADS * MLA_V
V_AUG = MLA_V + 16
VT_W = MLA_HEADS * V_AUG
CONV_A_HALO = 16
SMALL_HALO = 8
ATT_CHUNK = 256
SOFTMAX_ROWS = 64
ATT_TRIP = 2
LOG2E = 1.4426950408889634
ROUTE_W = LANES
GATE_LANE0 = N_GROUPS
CLS_LANE = GATE_LANE0 + N_EXPERTS
RANK_LANE = CLS_LANE + 1
GLO_LANE = RANK_LANE + 1
GHI_LANE = GLO_LANE + 1
META_ROW0 = (CLS_LANE // SUBLANES) * SUBLANES
HX_W = D_MODEL + ROUTE_W
PAIRS_PER_GROUP = 6
N_CLASSES = N_GROUPS * PAIRS_PER_GROUP
PAIR_LO = (0, 0, 0, 1, 1, 2)
PAIR_HI = (1, 2, 3, 2, 3, 3)
MOE_TM = 256
MOE_TILES = -(-(N_ROWS + N_CLASSES * (MOE_TM - 1)) // MOE_TM)
MOE_ROWS = MOE_TILES * MOE_TM
COMBINE_ROWS = 1536
DMA_UNROLL = 8
MOD_ROWS = 8


def _mod_row(t):
    return jnp.where(t % TILES_PER_BATCH == 0, BATCH, t // TILES_PER_BATCH)


def _cparams(sem):
    return pltpu.CompilerParams(dimension_semantics=sem, vmem_limit_bytes=VMEM_LIMIT)


def _dot(a, b):
    return jnp.dot(a, b, preferred_element_type=F32)


def _sigmoid(x):
    return 1.0 / (1.0 + jnp.exp(-x))


def _silu(x):
    return x * _sigmoid(x)


def _norm_mod(x, g, shift, scale):
    y = x * lax.rsqrt(jnp.mean(x * x, axis=-1, keepdims=True) + EPS)
    return (y * g) * (1.0 + scale) + shift


def _mod_kernel(c_ref, w_ref, b_ref, o_ref):
    s = _silu(c_ref[...])
    o_ref[0] = jnp.dot(s, w_ref[0], preferred_element_type=F32,
                       precision=lax.Precision.HIGHEST) + b_ref[0]


def _modulation(cvec, w_mod, b_mod):
    nblk = 4
    bw = 6 * D_MODEL // nblk
    return pl.pallas_call(
        _mod_kernel,
        grid=(DEPTH, nblk),
        in_specs=[pl.BlockSpec((MOD_ROWS, D_MODEL), lambda i, j: (0, 0)),
                  pl.BlockSpec((1, D_MODEL, bw), lambda i, j: (i, 0, j)),
                  pl.BlockSpec((1, 1, bw), lambda i, j: (i, 0, j))],
        out_specs=pl.BlockSpec((1, MOD_ROWS, bw), lambda i, j: (i, 0, j)),
        out_shape=jax.ShapeDtypeStruct((DEPTH, MOD_ROWS, 6 * D_MODEL), F32),
        compiler_params=_cparams(("parallel", "parallel")),
        name="modulation",
    )(cvec, w_mod, b_mod.reshape(DEPTH, 1, 6 * D_MODEL))


def _route(h2, wr_ref, br_ref):
    hi = h2.astype(BF16)
    lo = (h2 - hi.astype(F32)).astype(BF16)
    both = _dot(hi, wr_ref[...])
    logits = (both[:, 0:ROUTE_W] + both[:, ROUTE_W:2 * ROUTE_W] + _dot(lo, wr_ref[:, 0:ROUTE_W])
              + br_ref[...])
    lane = lax.broadcasted_iota(jnp.int32, logits.shape, 1)
    lane_f = lane.astype(F32)
    neg = -jnp.inf
    big = float(ROUTE_W)
    gl = jnp.where(lane < N_GROUPS, logits, neg)
    gmax = jnp.max(gl, axis=-1, keepdims=True)
    gidx = jnp.min(jnp.where(gl == gmax, lane_f, big), axis=-1, keepdims=True)
    g_w = 1.0 / jnp.sum(jnp.exp(gl - gmax), axis=-1, keepdims=True)
    egrp = ((lane - GATE_LANE0) // EXPERTS_PER_GROUP).astype(F32)
    in_g = (lane >= GATE_LANE0) & (lane < GATE_LANE0 + N_EXPERTS) & (egrp == gidx)
    el = jnp.where(in_g, logits, neg)
    v1 = jnp.max(el, axis=-1, keepdims=True)
    i1 = jnp.min(jnp.where(el == v1, lane_f, big), axis=-1, keepdims=True)
    el2 = jnp.where(lane_f == i1, neg, el)
    v2 = jnp.max(el2, axis=-1, keepdims=True)
    i2 = jnp.min(jnp.where(el2 == v2, lane_f, big), axis=-1, keepdims=True)
    e21 = jnp.exp(v2 - v1)
    w1 = 1.0 / (1.0 + e21)
    w2 = e21 * w1
    g1 = w1 * g_w
    g2 = w2 * g_w
    gates = jnp.where(lane_f == i1, g1, jnp.where(lane_f == i2, g2, 0.0))
    base = GATE_LANE0 + gidx * EXPERTS_PER_GROUP
    lo = jnp.minimum(i1, i2) - base
    hi = jnp.maximum(i1, i2) - base
    pair = jnp.where(lo == 0.0, 0.0, jnp.where(lo == 1.0, 3.0, 5.0)) + (hi - lo - 1.0)
    cls = gidx * PAIRS_PER_GROUP + pair
    g_lo = jnp.where(i1 < i2, g1, g2)
    g_hi = jnp.where(i1 < i2, g2, g1)
    return jnp.where(lane_f == float(GLO_LANE), g_lo, jnp.where(lane_f == float(GHI_LANE), g_hi, gates)), cls


def _residual_and_route(x, y, m, g1_ref, wr_ref, br_ref,
                        xo_ref, hx_ref, meta_ref, counts_ref, cnt_ref):
    @pl.when(pl.program_id(0) == 0)
    def _():
        cnt_ref[...] = jnp.zeros(cnt_ref.shape, F32)

    x_new = x + m[2:3] * y
    xo_ref[...] = x_new
    h2 = _norm_mod(x_new, g1_ref[...], m[3:4], m[4:5])
    hx_ref[:, 0:D_MODEL] = h2
    route, cls = _route(h2, wr_ref, br_ref)
    lane_f = lax.broadcasted_iota(jnp.int32, route.shape, 1).astype(F32)
    onehot = jnp.where(lane_f == cls, 1.0, 0.0)
    row = lax.broadcasted_iota(jnp.int32, (ROW_TILE, ROW_TILE), 0)
    col = lax.broadcasted_iota(jnp.int32, (ROW_TILE, ROW_TILE), 1)
    earlier = jnp.where(col < row, 1.0, 0.0).astype(BF16)
    before = _dot(earlier, onehot.astype(BF16)) + cnt_ref[...]
    rank = jnp.sum(onehot * before, axis=-1, keepdims=True)
    cnt_ref[...] = cnt_ref[...] + jnp.sum(onehot, axis=0, keepdims=True)
    counts_ref[...] = cnt_ref[...]
    route = jnp.where(lane_f == float(CLS_LANE), cls, jnp.where(lane_f == float(RANK_LANE), rank, route))
    hx_ref[:, D_MODEL:HX_W] = route
    meta_ref[0] = route.T[META_ROW0:META_ROW0 + SUBLANES, :]


def _tile_spec(width):
    return pl.BlockSpec((ROW_TILE, width), lambda t: (t, 0))


def _stream_specs(n_stream):
    if n_stream == 1:
        return [_tile_spec(D_MODEL)]
    return [pl.BlockSpec((1, ROW_TILE, D_MODEL),
                         lambda t: (t // TILES_PER_BATCH, jnp.maximum(t % TILES_PER_BATCH - 1, 0), 0)),
            pl.BlockSpec((1, CTX_LEN, D_MODEL), lambda t: (t // TILES_PER_BATCH, 0, 0))]


def _load_stream(refs):
    if len(refs) == 1:
        return refs[0][...]
    is_ctx = pl.program_id(0) % TILES_PER_BATCH == 0
    return jnp.where(is_ctx, refs[1][0], refs[0][0])


def _full_spec(shape):
    nd = len(shape)
    return pl.BlockSpec(shape, lambda t: (0,) * nd)


def _mod_spec():
    return pl.BlockSpec((1, 6, D_MODEL), lambda t: (_mod_row(t), 0, 0))


def _halo_specs(width, halo):
    per_tile = ROW_TILE // halo
    last = N_ROWS // halo - 1
    prev = pl.BlockSpec((halo, width), lambda t: (jnp.maximum(t * per_tile - 1, 0), 0))
    nxt = pl.BlockSpec((halo, width), lambda t: (jnp.minimum((t + 1) * per_tile, last), 0))
    return prev, nxt


def _seq_flags():
    w = pl.program_id(0) % TILES_PER_BATCH
    is_start = (w == 0) | (w == 1)
    is_end = (w == 0) | (w == TILES_PER_BATCH - 1)
    seq_tile = jnp.maximum(w - 1, 0)
    return is_start, is_end, seq_tile, w == 0


_POST_OUT_SHAPES = (jax.ShapeDtypeStruct((N_ROWS, D_MODEL), F32),
                    jax.ShapeDtypeStruct((N_ROWS, HX_W), F32),
                    jax.ShapeDtypeStruct((N_TILES, SUBLANES, ROW_TILE), F32),
                    jax.ShapeDtypeStruct((1, ROUTE_W), F32))


def _post_out_specs():
    return (_tile_spec(D_MODEL),
            _tile_spec(HX_W),
            pl.BlockSpec((1, SUBLANES, ROW_TILE), lambda t: (t, 0, 0)),
            pl.BlockSpec((1, ROUTE_W), lambda t: (0, 0)))


def _class_count_scratch():
    return pltpu.VMEM((1, ROUTE_W), F32)


def _even_pre_kernel(n_stream, *refs):
    mod_ref, g_ref, w_ref, a_ref, gb_ref, gch_ref = refs[n_stream:]
    m = mod_ref[0]
    h = _norm_mod(_load_stream(refs[:n_stream]), g_ref[...], m[0:1], m[1:2])
    u = _dot(h.astype(BF16), w_ref[...])
    ca, cb = CONV_A_DIM, CONV_B_DIM
    a_ref[...] = u[:, 0:ca] * _sigmoid(u[:, ca:2 * ca])
    gb_ref[...] = u[:, 2 * ca:2 * ca + cb]
    gch_ref[...] = u[:, 2 * ca + cb:2 * ca + 2 * cb] * u[:, 2 * ca + 2 * cb:2 * ca + 3 * cb]


def _even_pre(stream, mod_i, g0, w_in):
    n_in = w_in.shape[1]
    return pl.pallas_call(
        functools.partial(_even_pre_kernel, len(stream)),
        grid=(N_TILES,),
        in_specs=_stream_specs(len(stream)) + [_mod_spec(), _full_spec((1, D_MODEL)),
                                               _full_spec((D_MODEL, n_in))],
        out_specs=(_tile_spec(CONV_A_DIM), _tile_spec(CONV_B_DIM), _tile_spec(CONV_B_DIM)),
        out_shape=(jax.ShapeDtypeStruct((N_ROWS, CONV_A_DIM), F32),
                   jax.ShapeDtypeStruct((N_ROWS, CONV_B_DIM), F32),
                   jax.ShapeDtypeStruct((N_ROWS, CONV_B_DIM), F32)),
        compiler_params=_cparams(("parallel",)),
        name="even_pre",
    )(*stream, mod_i, g0, w_in)


def _even_post_kernel(n_stream, *refs):
    (a_ref, ap_ref, an_ref, gch_ref, gp_ref, gn_ref, gb_ref,
     cwa_ref, cba_ref, lng_ref, lnb_ref, cwb_ref, wout_ref,
     mod_ref, g1_ref, wr_ref, br_ref,
     xo_ref, hx_ref, meta_ref, counts_ref, exta_ref, extb_ref, cnt_ref) = refs[n_stream:]
    is_start, is_end, _, _ = _seq_flags()
    ha, hb = CONV_A_HALO, SMALL_HALO
    exta_ref[0:ha] = jnp.where(is_start, 0.0, ap_ref[...])
    exta_ref[ha:ha + ROW_TILE] = a_ref[...]
    exta_ref[ha + ROW_TILE:2 * ha + ROW_TILE] = jnp.where(is_end, 0.0, an_ref[...])
    extb_ref[0:hb] = jnp.where(is_start, 0.0, gp_ref[...])
    extb_ref[hb:hb + ROW_TILE] = gch_ref[...]
    extb_ref[hb + ROW_TILE:2 * hb + ROW_TILE] = jnp.where(is_end, 0.0, gn_ref[...])

    half_a = (CONV_A_WIDTH - 1) // 2
    lead = ha - half_a
    span = ROW_TILE + SUBLANES
    assert 0 <= lead and lead + SUBLANES - 1 <= SUBLANES
    parts = []
    for cblk in range(CONV_A_DIM // LANES):
        cs = slice(cblk * LANES, (cblk + 1) * LANES)
        acc = jnp.zeros((ROW_TILE, LANES), F32) + cba_ref[:, cs]
        for b in range(SUBLANES):
            z = jnp.zeros((span, LANES), F32)
            for k in range(b, CONV_A_WIDTH, SUBLANES):
                z = z + cwa_ref[k:k + 1, cs] * exta_ref[k - b:k - b + span, cs]
            acc = acc + z[lead + b:lead + b + ROW_TILE, :]
        parts.append(acc)
    acc = jnp.concatenate(parts, axis=1)
    mu = jnp.mean(acc, axis=-1, keepdims=True)
    xc = acc - mu
    a = xc * lax.rsqrt(jnp.mean(xc * xc, axis=-1, keepdims=True) + EPS)
    a = _silu(a * lng_ref[...] + lnb_ref[...])

    half_b = (CONV_B_WIDTH - 1) // 2
    cb = jnp.zeros((ROW_TILE, CONV_B_DIM), F32)
    for k in range(CONV_B_WIDTH):
        off = hb - half_b + k
        cb = cb + cwb_ref[k:k + 1, :] * extb_ref[off:off + ROW_TILE, :]
    b = gb_ref[...] * cb

    y = (_dot(a.astype(BF16), wout_ref[0:CONV_A_DIM, :])
         + _dot(b.astype(BF16), wout_ref[CONV_A_DIM:CONV_A_DIM + CONV_B_DIM, :]))
    _residual_and_route(_load_stream(refs[:n_stream]), y, mod_ref[0], g1_ref, wr_ref, br_ref,
                        xo_ref, hx_ref, meta_ref, counts_ref, cnt_ref)


def _even_post(stream, a_pre, gb, gch, cwa, cba, lng, lnb, cwb, w_out, mod_i, g1, wr, br):
    ap_spec, an_spec = _halo_specs(CONV_A_DIM, CONV_A_HALO)
    gp_spec, gn_spec = _halo_specs(CONV_B_DIM, SMALL_HALO)
    return pl.pallas_call(
        functools.partial(_even_post_kernel, len(stream)),
        grid=(N_TILES,),
        in_specs=_stream_specs(len(stream)) + [
                  _tile_spec(CONV_A_DIM), ap_spec, an_spec,
                  _tile_spec(CONV_B_DIM), gp_spec, gn_spec, _tile_spec(CONV_B_DIM),
                  _full_spec((CONV_A_WIDTH, CONV_A_DIM)), _full_spec((1, CONV_A_DIM)),
                  _full_spec((1, CONV_A_DIM)), _full_spec((1, CONV_A_DIM)),
                  _full_spec((CONV_B_WIDTH, CONV_B_DIM)), _full_spec((D_MODEL, D_MODEL)),
                  _mod_spec(), _full_spec((1, D_MODEL)),
                  _full_spec((D_MODEL, 2 * ROUTE_W)),
                  _full_spec((1, ROUTE_W))],
        out_specs=_post_out_specs(),
        out_shape=_POST_OUT_SHAPES,
        scratch_shapes=[pltpu.VMEM((ROW_TILE + 2 * CONV_A_HALO, CONV_A_DIM), F32),
                        pltpu.VMEM((ROW_TILE + 2 * SMALL_HALO, CONV_B_DIM), F32),
                        _class_count_scratch()],
        compiler_params=_cparams(("arbitrary",)),
        name="even_post",
    )(*stream, a_pre, a_pre, a_pre, gch, gch, gch, gb, cwa, cba, lng, lnb, cwb, w_out,
      mod_i, g1, wr, br)


def _odd_pre_kernel(x_ref, mod_ref, g_ref, w_ref, gq_ref, wqm_ref, wqs_ref, gkv_ref, wk_ref, wv_ref,
                    vaug_ref, cos_ref, sin_ref, qt_ref, k_ref, vt_ref, pool_ref):
    m = mod_ref[0]
    h = _norm_mod(x_ref[...], g_ref[...], m[0:1], m[1:2])
    u = _dot(h.astype(BF16), w_ref[...])
    o_kv = Q_LORA
    o_kr = o_kv + KV_LORA
    o_krs = o_kr + HEAD_PAD
    o_pool = o_krs + HEAD_PAD
    cos = cos_ref[...]
    sin = sin_ref[...]

    q_c = u[:, 0:o_kv]
    qn = (q_c * lax.rsqrt(jnp.mean(q_c * q_c, axis=-1, keepdims=True) + EPS) * gq_ref[...]).astype(BF16)
    q_main = _dot(qn, wqm_ref[...])
    q_swap = _dot(qn, wqs_ref[...])
    kv_c = u[:, o_kv:o_kr]
    kvn = (kv_c * lax.rsqrt(jnp.mean(kv_c * kv_c, axis=-1, keepdims=True) + EPS) * gkv_ref[...]).astype(BF16)
    k_nope = _dot(kvn, wk_ref[...])
    v = _dot(kvn, wv_ref[...]) + vaug_ref[...]
    for cc in range(ROW_TILE // ATT_CHUNK):
        vt_ref[cc] = v[cc * ATT_CHUNK:(cc + 1) * ATT_CHUNK, :].T.astype(BF16)
    k_rope = u[:, o_kr:o_krs] * cos + u[:, o_krs:o_pool] * sin
    for hd in range(MLA_HEADS):
        sl = slice(hd * HEAD_PAD, (hd + 1) * HEAD_PAD)
        qh = (q_main[:, sl] * cos + q_swap[:, sl] * sin) * (MLA_SCALE * LOG2E)
        qt_ref[0, sl, :] = qh.T.astype(BF16)
        k_ref[:, sl] = (k_nope[:, sl] + k_rope).astype(BF16)
    pool_ref[...] = u[:, o_pool:o_pool + POOL_DIM]


def _odd_pre(x, mod_i, g0, w_in, gq, wqm, wqs, gkv, wk, wv, cos_t, sin_t):
    n_in = w_in.shape[1]
    vaug = jnp.zeros((MLA_HEADS, V_AUG), F32).at[:, MLA_V].set(1.0).reshape(1, VT_W)
    return pl.pallas_call(
        _odd_pre_kernel,
        grid=(N_TILES,),
        in_specs=[_tile_spec(D_MODEL), _mod_spec(), _full_spec((1, D_MODEL)),
                  _full_spec((D_MODEL, n_in)), _full_spec((1, Q_LORA)),
                  _full_spec((Q_LORA, QK_W)), _full_spec((Q_LORA, QK_W)),
                  _full_spec((1, KV_LORA)), _full_spec((KV_LORA, QK_W)), _full_spec((KV_LORA, VT_W)),
                  _full_spec((1, VT_W)),
                  pl.BlockSpec((ROW_TILE, HEAD_PAD), lambda t: (t % TILES_PER_BATCH, 0)),
                  pl.BlockSpec((ROW_TILE, HEAD_PAD), lambda t: (t % TILES_PER_BATCH, 0))],
        out_specs=(pl.BlockSpec((1, QK_W, ROW_TILE), lambda t: (t, 0, 0)),
                   _tile_spec(QK_W),
                   pl.BlockSpec((ROW_TILE // ATT_CHUNK, VT_W, ATT_CHUNK), lambda t: (t, 0, 0)),
                   _tile_spec(POOL_DIM)),
        out_shape=(jax.ShapeDtypeStruct((N_TILES, QK_W, ROW_TILE), BF16),
                   jax.ShapeDtypeStruct((N_ROWS, QK_W), BF16),
                   jax.ShapeDtypeStruct((N_ROWS // ATT_CHUNK, VT_W, ATT_CHUNK), BF16),
                   jax.ShapeDtypeStruct((N_ROWS, POOL_DIM), F32)),
        compiler_params=_cparams(("parallel",)),
        name="odd_pre",
    )(x, mod_i, g0, w_in, gq, wqm, wqs, gkv, wk, wv, vaug, cos_t, sin_t)


def _attn_kernel(qt_ref, k_ref, vt_ref, o_ref, m_ref, acc_ref, s_ref, p_ref, a_ref):
    m_ref[...] = jnp.full(m_ref.shape, -jnp.inf, F32)
    acc_ref[...] = jnp.zeros(acc_ref.shape, F32)

    def scores_head(c, slot, hd):
        r0 = pl.multiple_of(c * ATT_CHUNK, ATT_CHUNK)
        sl = slice(hd * HEAD_PAD, (hd + 1) * HEAD_PAD)
        s_ref[slot, hd] = _dot(k_ref[pl.ds(r0, ATT_CHUNK), sl], qt_ref[0, sl, :])

    def softmax_head(slot, hd):
        hs = slice(hd, hd + 1)
        m_old = m_ref[hs, :]
        m_new = jnp.maximum(m_old, jnp.max(s_ref[slot, hd], axis=0, keepdims=True))
        m_ref[hs, :] = m_new
        a_ref[slot, hs, :] = jnp.exp2(m_old - m_new)
        for r0 in range(0, ATT_CHUNK, SOFTMAX_ROWS):
            rows = slice(r0, r0 + SOFTMAX_ROWS)
            p_ref[slot, hd, rows, :] = jnp.exp2(s_ref[slot, hd, rows, :] - m_new).astype(BF16)

    def pv_head(c, slot, hd):
        vs = slice(hd * V_AUG, (hd + 1) * V_AUG)
        pv = _dot(vt_ref[c, vs, :], p_ref[slot, hd])
        acc_ref[vs, :] = a_ref[slot, hd:hd + 1, :] * acc_ref[vs, :] + pv

    def stage_scores(c, slot):
        for hd in range(MLA_HEADS):
            scores_head(c, slot, hd)

    def stage_softmax(slot):
        for hd in range(MLA_HEADS):
            softmax_head(slot, hd)

    def stage_pv(c, slot):
        for hd in range(MLA_HEADS):
            pv_head(c, slot, hd)

    def steady(c, slot):
        stage_scores(c, slot)
        stage_pv(c - 2, slot)
        stage_softmax(1 - slot)

    is_ctx = pl.program_id(1) == 0

    @pl.when(is_ctx)
    def _():
        stage_scores(0, 0)
        stage_softmax(0)
        stage_pv(0, 0)

    @pl.when(jnp.logical_not(is_ctx))
    def _():
        n = ROWS_PER_BATCH // ATT_CHUNK
        stage_scores(0, 0)
        stage_scores(1, 1)
        stage_softmax(0)
        first = 2 + (n - 2) % ATT_TRIP
        for c in range(2, first):
            steady(c, c % 2)

        def trip(i, carry):
            c = first + ATT_TRIP * i
            for u in range(ATT_TRIP):
                steady(c + u, (first + u) % 2)
            return carry

        lax.fori_loop(0, (n - first) // ATT_TRIP, trip, 0)
        stage_pv(n - 2, (n - 2) % 2)
        stage_softmax((n - 1) % 2)
        stage_pv(n - 1, (n - 1) % 2)

    outs = []
    for hd in range(MLA_HEADS):
        blk = acc_ref[hd * V_AUG:(hd + 1) * V_AUG, :]
        outs.append(blk[0:MLA_V, :] / blk[MLA_V:MLA_V + 1, :])
    o_ref[...] = jnp.concatenate(outs, axis=0).T


def _attention(qt, k, vt):
    cpb = ROWS_PER_BATCH // ATT_CHUNK
    assert cpb >= 3 and CTX_LEN == ATT_CHUNK
    return pl.pallas_call(
        _attn_kernel,
        grid=(BATCH, TILES_PER_BATCH),
        in_specs=[pl.BlockSpec((1, QK_W, ROW_TILE), lambda b, i: (b * TILES_PER_BATCH + i, 0, 0)),
                  pl.BlockSpec((ROWS_PER_BATCH, QK_W), lambda b, i: (b, 0), pipeline_mode=pl.Buffered(1)),
                  pl.BlockSpec((cpb, VT_W, ATT_CHUNK), lambda b, i: (b, 0, 0), pipeline_mode=pl.Buffered(1))],
        out_specs=pl.BlockSpec((ROW_TILE, ATT_W), lambda b, i: (b * TILES_PER_BATCH + i, 0)),
        out_shape=jax.ShapeDtypeStruct((N_ROWS, ATT_W), F32),
        scratch_shapes=[pltpu.VMEM((MLA_HEADS, ROW_TILE), F32),
                        pltpu.VMEM((VT_W, ROW_TILE), F32),
                        pltpu.VMEM((2, MLA_HEADS, ATT_CHUNK, ROW_TILE), F32),
                        pltpu.VMEM((2, MLA_HEADS, ATT_CHUNK, ROW_TILE), BF16),
                        pltpu.VMEM((2, MLA_HEADS, ROW_TILE), F32)],
        compiler_params=_cparams(("parallel", "parallel")),
        name="attention",
    )(qt, k, vt)


def _odd_post_kernel(att_ref, p_ref, pp_ref, pn_ref, wp_ref, bp_ref, sp_ref, wout_ref,
                     x_ref, mod_ref, g1_ref, wr_ref, br_ref,
                     xo_ref, hx_ref, meta_ref, counts_ref, ext_ref, cnt_ref):
    is_start, is_end, seq_tile, is_ctx = _seq_flags()
    hp = SMALL_HALO
    ext_ref[0:hp] = jnp.where(is_start, 0.0, pp_ref[...])
    ext_ref[hp:hp + ROW_TILE] = p_ref[...]
    ext_ref[hp + ROW_TILE:2 * hp + ROW_TILE] = jnp.where(is_end, 0.0, pn_ref[...])
    seq_len = jnp.where(is_ctx, CTX_LEN, SEQ)
    pos = seq_tile * ROW_TILE + lax.broadcasted_iota(jnp.int32, (ROW_TILE, 1), 0)

    pooled = []
    for g, w in enumerate(POOL_WINDOWS):
        sl = slice(g * POOL_GROUP, (g + 1) * POOL_GROUP)
        ssum = jnp.zeros((ROW_TILE, POOL_GROUP), F32)
        for d in range(-(w // 2), w - w // 2):
            ssum = ssum + ext_ref[hp + d:hp + d + ROW_TILE, sl]
        lo = jnp.maximum(pos - w // 2, 0)
        hi = jnp.minimum(pos - w // 2 + w, seq_len)
        cnt = (hi - lo).astype(F32)
        pm = ssum / cnt - p_ref[:, sl]
        pooled.append(((_dot(pm.astype(BF16), wp_ref[g]) + bp_ref[g:g + 1, :]) * sp_ref[:, sl]).astype(BF16))
    mixed = jnp.concatenate([att_ref[...].astype(BF16)] + pooled, axis=1)
    y = _dot(mixed, wout_ref[...])
    _residual_and_route(x_ref[...], y, mod_ref[0], g1_ref, wr_ref, br_ref,
                        xo_ref, hx_ref, meta_ref, counts_ref, cnt_ref)


def _odd_post(att, pool_in, w_pool, b_pool, s_pool, w_out, x, mod_i, g1, wr, br):
    pp_spec, pn_spec = _halo_specs(POOL_DIM, SMALL_HALO)
    ng = len(POOL_WINDOWS)
    return pl.pallas_call(
        _odd_post_kernel,
        grid=(N_TILES,),
        in_specs=[_tile_spec(ATT_W), _tile_spec(POOL_DIM), pp_spec, pn_spec,
                  _full_spec((ng, POOL_GROUP, POOL_GROUP)), _full_spec((ng, POOL_GROUP)),
                  _full_spec((1, POOL_DIM)), _full_spec((D_MODEL, D_MODEL)),
                  _tile_spec(D_MODEL), _mod_spec(), _full_spec((1, D_MODEL)),
                  _full_spec((D_MODEL, 2 * ROUTE_W)),
                  _full_spec((1, ROUTE_W))],
        out_specs=_post_out_specs(),
        out_shape=_POST_OUT_SHAPES,
        scratch_shapes=[pltpu.VMEM((ROW_TILE + 2 * SMALL_HALO, POOL_DIM), F32), _class_count_scratch()],
        compiler_params=_cparams(("arbitrary",)),
        name="odd_post",
    )(att, pool_in, pool_in, pool_in, w_pool, b_pool, s_pool, w_out, x, mod_i, g1, wr, br)


def _moe_plan(meta, counts):
    cls = meta[:, CLS_LANE - META_ROW0, :].reshape(N_ROWS).astype(jnp.int32)
    rank = meta[:, RANK_LANE - META_ROW0, :].reshape(N_ROWS).astype(jnp.int32)
    tiles = (counts[0, :N_CLASSES].astype(jnp.int32) + MOE_TM - 1) // MOE_TM
    tile_end = jnp.cumsum(tiles)
    first_row = (tile_end - tiles) * MOE_TM
    dest = rank
    for c in range(N_CLASSES):
        dest = dest + jnp.where(cls == c, first_row[c], 0)
    j = jnp.arange(MOE_TILES)
    used = tile_end[-1]
    j_eff = jnp.minimum(j, jnp.maximum(used - 1, 0))
    cls_of_tile = jnp.sum((j_eff[:, None] >= tile_end[None, :]).astype(jnp.int32), axis=1)
    cls_of_tile = jnp.minimum(cls_of_tile, N_CLASSES - 1)
    grp = cls_of_tile // PAIRS_PER_GROUP
    pair = cls_of_tile % PAIRS_PER_GROUP
    e_lo = grp * EXPERTS_PER_GROUP + jnp.array(PAIR_LO, jnp.int32)[pair]
    e_hi = grp * EXPERTS_PER_GROUP + jnp.array(PAIR_HI, jnp.int32)[pair]
    src = jnp.zeros((MOE_ROWS,), jnp.int32).at[dest].set(jnp.arange(N_ROWS, dtype=jnp.int32))
    return dest, src, e_lo, e_hi, (j < used).astype(jnp.int32)


def _move_rows(n_rows, base, dest_ref, row_copy, block_copy):
    def issue(i, carry):
        for u in range(DMA_UNROLL):
            r = i * DMA_UNROLL + u
            row_copy(r, dest_ref[base + r]).start()
        return carry

    lax.fori_loop(0, n_rows // DMA_UNROLL, issue, 0)
    block_copy.wait()


def _moe_expert_kernel(elo_ref, ehi_ref, valid_ref, src_ref, hx_ref, w1a_ref, w3a_ref, w2a_ref,
                       w1b_ref, w3b_ref, w2b_ref, ys_ref, buf0_ref, buf1_ref, sem0, sem1):
    del elo_ref, ehi_ref
    j = pl.program_id(0)

    def request(tile, buf_ref, sem, unrolled):
        def row_copy(r):
            return pltpu.make_async_copy(hx_ref.at[pl.ds(src_ref[tile * MOE_TM + r], 1)],
                                         buf_ref.at[pl.ds(r, 1)], sem)
        if unrolled:
            for r in range(MOE_TM):
                row_copy(r).start()
        else:
            def issue(i, carry):
                for u in range(DMA_UNROLL):
                    row_copy(i * DMA_UNROLL + u).start()
                return carry
            lax.fori_loop(0, MOE_TM // DMA_UNROLL, issue, 0)

    def arrived(buf_ref, sem):
        pltpu.make_async_copy(hx_ref.at[pl.ds(0, MOE_TM)], buf_ref, sem).wait()

    def step(cur_ref, cur_sem, nxt_ref, nxt_sem):
        nxt = jnp.where(j + 1 == MOE_TILES, 0, j + 1)
        arrived(cur_ref, cur_sem)

        @pl.when(valid_ref[j] == 0)
        def _():
            request(nxt, nxt_ref, nxt_sem, unrolled=False)
            ys_ref[...] = jnp.zeros(ys_ref.shape, F32)

        @pl.when(valid_ref[j] != 0)
        def _():
            request(nxt, nxt_ref, nxt_sem, unrolled=True)
            h = cur_ref[:, 0:D_MODEL].astype(BF16)
            y = jnp.zeros((MOE_TM, D_MODEL), F32)
            for gate_lane, w1_ref, w3_ref, w2_ref in ((GLO_LANE, w1a_ref, w3a_ref, w2a_ref),
                                                      (GHI_LANE, w1b_ref, w3b_ref, w2b_ref)):
                hid = _silu(_dot(h, w1_ref[0, 0])) * _dot(h, w3_ref[0, 0])
                g = cur_ref[:, D_MODEL + gate_lane:D_MODEL + gate_lane + 1]
                y = y + g * _dot(hid.astype(BF16), w2_ref[0, 0])
            ys_ref[...] = y

        @pl.when(j + 1 == MOE_TILES)
        def _():
            arrived(nxt_ref, nxt_sem)

    @pl.when(j == 0)
    def _():
        request(0, buf0_ref, sem0, unrolled=False)

    @pl.when(j % 2 == 0)
    def _():
        step(buf0_ref, sem0, buf1_ref, sem1)

    @pl.when(j % 2 == 1)
    def _():
        step(buf1_ref, sem1, buf0_ref, sem0)


def _moe_experts(e_lo, e_hi, valid, src, hx, w1, w3, w2, layer):
    up_lo = pl.BlockSpec((1, 1, D_MODEL, D_EXPERT), lambda j, elo, ehi, valid, src: (layer, elo[j], 0, 0))
    dn_lo = pl.BlockSpec((1, 1, D_EXPERT, D_MODEL), lambda j, elo, ehi, valid, src: (layer, elo[j], 0, 0))
    up_hi = pl.BlockSpec((1, 1, D_MODEL, D_EXPERT), lambda j, elo, ehi, valid, src: (layer, ehi[j], 0, 0))
    dn_hi = pl.BlockSpec((1, 1, D_EXPERT, D_MODEL), lambda j, elo, ehi, valid, src: (layer, ehi[j], 0, 0))
    return pl.pallas_call(
        _moe_expert_kernel,
        grid_spec=pltpu.PrefetchScalarGridSpec(
            num_scalar_prefetch=4,
            grid=(MOE_TILES,),
            in_specs=[pl.BlockSpec(memory_space=pl.ANY),
                      up_lo, up_lo, dn_lo, up_hi, up_hi, dn_hi],
            out_specs=pl.BlockSpec((MOE_TM, D_MODEL), lambda j, elo, ehi, valid, src: (j, 0)),
            scratch_shapes=[pltpu.VMEM((MOE_TM, HX_W), F32), pltpu.VMEM((MOE_TM, HX_W), F32),
                            pltpu.SemaphoreType.DMA(()), pltpu.SemaphoreType.DMA(())]),
        out_shape=jax.ShapeDtypeStruct((MOE_ROWS, D_MODEL), F32),
        compiler_params=_cparams(("arbitrary",)),
        name="moe_experts",
    )(e_lo, e_hi, valid, src, hx, w1, w3, w2, w1, w3, w2)


def _combine_kernel(dest_ref, x_ref, mod_ref, ys_ref, o_ref, buf_ref, sem):
    t = pl.program_id(0)
    _move_rows(COMBINE_ROWS, t * COMBINE_ROWS, dest_ref,
               lambda r, d: pltpu.make_async_copy(ys_ref.at[pl.ds(d, 1)], buf_ref.at[pl.ds(r, 1)], sem),
               pltpu.make_async_copy(ys_ref.at[pl.ds(0, COMBINE_ROWS)], buf_ref, sem))
    sub = COMBINE_ROWS // ROW_TILE
    for b in range(sub):
        rs = slice(b * ROW_TILE, (b + 1) * ROW_TILE)
        m5 = mod_ref[pl.ds(_mod_row(t * sub + b), 1)][0, 5:6]
        o_ref[rs, :] = x_ref[rs, :] + m5 * buf_ref[rs, :]


def _combine(dest, x, mod_i, ys):
    return pl.pallas_call(
        _combine_kernel,
        grid_spec=pltpu.PrefetchScalarGridSpec(
            num_scalar_prefetch=1,
            grid=(N_ROWS // COMBINE_ROWS,),
            in_specs=[pl.BlockSpec((COMBINE_ROWS, D_MODEL), lambda t, dest: (t, 0)),
                      pl.BlockSpec((MOD_ROWS, 6, D_MODEL), lambda t, dest: (0, 0, 0)),
                      pl.BlockSpec(memory_space=pl.ANY)],
            out_specs=pl.BlockSpec((COMBINE_ROWS, D_MODEL), lambda t, dest: (t, 0)),
            scratch_shapes=[pltpu.VMEM((COMBINE_ROWS, D_MODEL), F32), pltpu.SemaphoreType.DMA(())]),
        out_shape=jax.ShapeDtypeStruct((N_ROWS, D_MODEL), F32),
        compiler_params=_cparams(("arbitrary",)),
        name="moe_combine",
    )(dest, x, mod_i, ys)


def _final_kernel(x_ref, g_ref, o_ref):
    x = x_ref[...]
    o_ref[0] = x * lax.rsqrt(jnp.mean(x * x, axis=-1, keepdims=True) + EPS) * g_ref[...]


def _final_norm(x, g):
    lat_tiles = SEQ // ROW_TILE
    return pl.pallas_call(
        _final_kernel,
        grid=(BATCH, lat_tiles),
        in_specs=[pl.BlockSpec((ROW_TILE, D_MODEL), lambda b, i: (b * TILES_PER_BATCH + 1 + i, 0)),
                  pl.BlockSpec((1, D_MODEL), lambda b, i: (0, 0))],
        out_specs=pl.BlockSpec((1, ROW_TILE, D_MODEL), lambda b, i: (b, i, 0)),
        out_shape=jax.ShapeDtypeStruct((BATCH, SEQ, D_MODEL), F32),
        compiler_params=_cparams(("parallel", "parallel")),
        name="final_norm",
    )(x, g)


_SWAP32 = tuple(list(range(8, 16)) + list(range(0, 8)) + list(range(24, 32)) + list(range(16, 24)))


def _rope_tables():
    f32 = np.float32
    rows = SEQ // GRID_W
    pos_row = np.repeat(np.arange(rows, dtype=f32), GRID_W)
    pos_col = np.tile(np.arange(GRID_W, dtype=f32), rows)
    inv = (f32(ROPE_BASE) ** (-np.arange(0, AXIS_DIM, 2, dtype=f32) / f32(AXIS_DIM))).astype(f32)
    ang = np.concatenate([pos_row[:, None] * inv, pos_col[:, None] * inv], axis=-1).astype(f32)
    cos, sin = np.cos(ang).astype(f32), np.sin(ang).astype(f32)
    p = AXIS_PAIRS
    c32 = np.concatenate([cos[:, :p], cos[:, :p], cos[:, p:], cos[:, p:]], axis=-1)
    s32 = np.concatenate([-sin[:, :p], sin[:, :p], -sin[:, p:], sin[:, p:]], axis=-1)
    pad = HEAD_PAD - MLA_NOPE - MLA_ROPE
    cos_lat = np.concatenate([np.ones((SEQ, MLA_NOPE), f32), c32, np.zeros((SEQ, pad), f32)], axis=-1)
    sin_lat = np.concatenate([np.zeros((SEQ, MLA_NOPE), f32), s32, np.zeros((SEQ, pad), f32)], axis=-1)
    cos_ctx = np.concatenate([np.ones((CTX_LEN, MLA_NOPE + MLA_ROPE), f32), np.zeros((CTX_LEN, pad), f32)], axis=-1)
    sin_ctx = np.zeros((CTX_LEN, HEAD_PAD), f32)
    return (jnp.asarray(np.concatenate([cos_ctx, cos_lat], axis=0)),
            jnp.asarray(np.concatenate([sin_ctx, sin_lat], axis=0)))


def _odd_weights(w_in, w_uq, w_ukv):
    swap = jnp.array(_SWAP32)
    o_kv, o_kr, o_pool = Q_LORA, Q_LORA + KV_LORA, Q_LORA + KV_LORA + MLA_ROPE
    w_kr = w_in[:, o_kr:o_pool]
    pad_l = jnp.zeros((D_MODEL, MLA_NOPE), F32)
    pad_r = jnp.zeros((D_MODEL, HEAD_PAD - MLA_NOPE - MLA_ROPE), F32)
    w_in2 = jnp.concatenate([w_in[:, :o_kr], pad_l, w_kr, pad_r, pad_l, w_kr[:, swap], pad_r,
                             w_in[:, o_pool:]], axis=-1).astype(BF16)
    wq = w_uq.reshape(Q_LORA, MLA_HEADS, MLA_NOPE + MLA_ROPE)
    zq = jnp.zeros((Q_LORA, MLA_HEADS, HEAD_PAD - MLA_NOPE - MLA_ROPE), F32)
    wq_main = jnp.concatenate([wq, zq], axis=-1).reshape(Q_LORA, QK_W).astype(BF16)
    wq_swap = jnp.concatenate([jnp.zeros((Q_LORA, MLA_HEADS, MLA_NOPE), F32),
                               wq[:, :, MLA_NOPE:][:, :, swap], zq], axis=-1).reshape(Q_LORA, QK_W).astype(BF16)
    wkv = w_ukv.reshape(KV_LORA, MLA_HEADS, MLA_NOPE + MLA_V)
    zk = jnp.zeros((KV_LORA, MLA_HEADS, HEAD_PAD - MLA_NOPE), F32)
    wk = jnp.concatenate([wkv[:, :, :MLA_NOPE], zk], axis=-1).reshape(KV_LORA, QK_W).astype(BF16)
    zv = jnp.zeros((KV_LORA, MLA_HEADS, V_AUG - MLA_V), F32)
    wv = jnp.concatenate([wkv[:, :, MLA_NOPE:], zv], axis=-1).reshape(KV_LORA, VT_W).astype(BF16)
    return w_in2, wq_main, wq_swap, wk, wv


def kernel(x, c, ctx, c_ctx, w_mod, b_mod, norm_g, ev_w_in, ev_conv_a_w, ev_conv_a_b, ev_ln_a_g, ev_ln_a_b,
           ev_conv_b_w, ev_w_out, od_w_in, od_q_norm_g, od_w_uq, od_kv_norm_g, od_w_ukv, od_w_pool, od_b_pool,
           od_s_pool, od_w_out, moe_wg, moe_bg, moe_we, moe_be, moe_w1, moe_w3, moe_w2, final_g):
    assert x.shape == (BATCH, SEQ, D_MODEL) and ctx.shape == (BATCH, CTX_LEN, D_MODEL)
    assert CTX_LEN == ROW_TILE
    stream = (x, ctx)
    cvec = jnp.concatenate([c, c_ctx[None, :], jnp.zeros((MOD_ROWS - BATCH - 1, D_MODEL), F32)], axis=0)
    mod = _modulation(cvec, w_mod, b_mod).reshape(DEPTH, MOD_ROWS, 6, D_MODEL)
    cos_t, sin_t = _rope_tables()
    w1_bf, w3_bf, w2_bf = moe_w1.astype(BF16), moe_w3.astype(BF16), moe_w2.astype(BF16)

    for i in range(DEPTH):
        j = i // 2
        mod_i = mod[i]
        g0 = norm_g[i, 0][None, :]
        g1 = norm_g[i, 1][None, :]
        wr = jnp.concatenate([moe_wg[i], moe_we[i],
                              jnp.zeros((D_MODEL, ROUTE_W - N_GROUPS - N_EXPERTS), F32)], axis=-1)
        wr_hi = wr.astype(BF16)
        wr = jnp.concatenate([wr_hi, (wr - wr_hi.astype(F32)).astype(BF16)], axis=-1)
        br = jnp.concatenate([moe_bg[i], moe_be[i],
                              jnp.zeros((ROUTE_W - N_GROUPS - N_EXPERTS,), F32)])[None, :]
        if i % 2 == 0:
            a_pre, gb, gch = _even_pre(stream, mod_i, g0, ev_w_in[j].astype(BF16))
            xs, hx, meta, counts = _even_post(stream, a_pre, gb, gch, ev_conv_a_w[j], ev_conv_a_b[j][None, :],
                                              ev_ln_a_g[j][None, :], ev_ln_a_b[j][None, :], ev_conv_b_w[j],
                                              ev_w_out[j].astype(BF16), mod_i, g1, wr, br)
        else:
            w_in2, wq_main, wq_swap, wk, wv = _odd_weights(od_w_in[j], od_w_uq[j], od_w_ukv[j])
            q, k, v, pool_in = _odd_pre(xs, mod_i, g0, w_in2, od_q_norm_g[j][None, :], wq_main, wq_swap,
                                        od_kv_norm_g[j][None, :], wk, wv, cos_t, sin_t)
            att = _attention(q, k, v)
            xs, hx, meta, counts = _odd_post(att, pool_in, od_w_pool[j].astype(BF16), od_b_pool[j],
                                      od_s_pool[j][None, :], od_w_out[j].astype(BF16), xs, mod_i, g1,
                                      wr, br)
        dest, src, e_lo, e_hi, valid = _moe_plan(meta, counts)
        ys = _moe_experts(e_lo, e_hi, valid, src, hx, w1_bf, w3_bf, w2_bf, i)
        xs = _combine(dest, xs, mod_i, ys)
        stream = (xs,)
    return _final_norm(xs, final_g[None, :])
```

```python
import functools

import numpy as np
import jax
import jax.numpy as jnp
from jax import lax
from jax.experimental import pallas as pl
from jax.experimental.pallas import tpu as pltpu

F32 = jnp.float32
BF16 = jnp.bfloat16

D_MODEL = 1024
BATCH = 2
SEQ = 8192
DEPTH = 4
GRID_W = 64
CTX_LEN = 256
EPS = 1e-6
CONV_A_DIM = 512
CONV_A_WIDTH = 31
CONV_B_DIM = 512
CONV_B_WIDTH = 3
MLA_HEADS = 8
MLA_NOPE = 64
MLA_ROPE = 32
MLA_V = 64
Q_LORA = 384
KV_LORA = 256
MLA_SCALE = (MLA_NOPE + MLA_ROPE) ** -0.5
AXIS_DIM = MLA_ROPE // 2
AXIS_PAIRS = AXIS_DIM // 2
ROPE_BASE = 10000.0
POOL_WINDOWS = (2, 4, 8, 16)
POOL_GROUP = 128
POOL_DIM = POOL_GROUP * len(POOL_WINDOWS)
N_GROUPS = 4
EXPERTS_PER_GROUP = 4
N_EXPERTS = 16
D_EXPERT = 512

LANES = 128
SUBLANES = 8
VMEM_LIMIT = 48 * 1024 * 1024

ROW_TILE = 256
ROWS_PER_BATCH = CTX_LEN + SEQ
N_ROWS = BATCH * ROWS_PER_BATCH
TILES_PER_BATCH = ROWS_PER_BATCH // ROW_TILE
N_TILES = N_ROWS // ROW_TILE
HEAD_PAD = LANES
QK_W = MLA_HEADS * HEAD_PAD
ATT_W = MLA_HEADS * MLA_V
V_AUG = MLA_V + 16
VT_W = MLA_HEADS * V_AUG
CONV_A_HALO = 16
SMALL_HALO = 8
ATT_CHUNK = 256
SOFTMAX_ROWS = 64
ATT_TRIP = 2
LOG2E = 1.4426950408889634
ROUTE_W = LANES
GATE_LANE0 = N_GROUPS
CLS_LANE = GATE_LANE0 + N_EXPERTS
RANK_LANE = CLS_LANE + 1
GLO_LANE = RANK_LANE + 1
GHI_LANE = GLO_LANE + 1
META_ROW0 = (CLS_LANE // SUBLANES) * SUBLANES
HX_W = D_MODEL + ROUTE_W
PAIRS_PER_GROUP = 6
N_CLASSES = N_GROUPS * PAIRS_PER_GROUP
PAIR_LO = (0, 0, 0, 1, 1, 2)
PAIR_HI = (1, 2, 3, 2, 3, 3)
MOE_TM = 256
MOE_TILES = -(-(N_ROWS + N_CLASSES * (MOE_TM - 1)) // MOE_TM)
MOE_ROWS = MOE_TILES * MOE_TM
DISPATCH_ROWS = 2816
COMBINE_ROWS = 1536
DMA_UNROLL = 8
MOD_ROWS = 8


def _mod_row(t):
    return jnp.where(t % TILES_PER_BATCH == 0, BATCH, t // TILES_PER_BATCH)


def _cparams(sem):
    return pltpu.CompilerParams(dimension_semantics=sem, vmem_limit_bytes=VMEM_LIMIT)


def _dot(a, b):
    return jnp.dot(a, b, preferred_element_type=F32)


def _sigmoid(x):
    return 1.0 / (1.0 + jnp.exp(-x))


def _silu(x):
    return x * _sigmoid(x)


def _norm_mod(x, g, shift, scale):
    y = x * lax.rsqrt(jnp.mean(x * x, axis=-1, keepdims=True) + EPS)
    return (y * g) * (1.0 + scale) + shift


def _mod_kernel(c_ref, w_ref, b_ref, o_ref):
    s = _silu(c_ref[...])
    o_ref[0] = jnp.dot(s, w_ref[0], preferred_element_type=F32,
                       precision=lax.Precision.HIGHEST) + b_ref[0]


def _modulation(cvec, w_mod, b_mod):
    nblk = 4
    bw = 6 * D_MODEL // nblk
    return pl.pallas_call(
        _mod_kernel,
        grid=(DEPTH, nblk),
        in_specs=[pl.BlockSpec((MOD_ROWS, D_MODEL), lambda i, j: (0, 0)),
                  pl.BlockSpec((1, D_MODEL, bw), lambda i, j: (i, 0, j)),
                  pl.BlockSpec((1, 1, bw), lambda i, j: (i, 0, j))],
        out_specs=pl.BlockSpec((1, MOD_ROWS, bw), lambda i, j: (i, 0, j)),
        out_shape=jax.ShapeDtypeStruct((DEPTH, MOD_ROWS, 6 * D_MODEL), F32),
        compiler_params=_cparams(("parallel", "parallel")),
        name="modulation",
    )(cvec, w_mod, b_mod.reshape(DEPTH, 1, 6 * D_MODEL))


def _route(h2, wr_ref, br_ref):
    hi = h2.astype(BF16)
    lo = (h2 - hi.astype(F32)).astype(BF16)
    both = _dot(hi, wr_ref[...])
    logits = (both[:, 0:ROUTE_W] + both[:, ROUTE_W:2 * ROUTE_W] + _dot(lo, wr_ref[:, 0:ROUTE_W])
              + br_ref[...])
    lane = lax.broadcasted_iota(jnp.int32, logits.shape, 1)
    lane_f = lane.astype(F32)
    neg = -jnp.inf
    big = float(ROUTE_W)
    gl = jnp.where(lane < N_GROUPS, logits, neg)
    gmax = jnp.max(gl, axis=-1, keepdims=True)
    gidx = jnp.min(jnp.where(gl == gmax, lane_f, big), axis=-1, keepdims=True)
    g_w = 1.0 / jnp.sum(jnp.exp(gl - gmax), axis=-1, keepdims=True)
    egrp = ((lane - GATE_LANE0) // EXPERTS_PER_GROUP).astype(F32)
    in_g = (lane >= GATE_LANE0) & (lane < GATE_LANE0 + N_EXPERTS) & (egrp == gidx)
    el = jnp.where(in_g, logits, neg)
    v1 = jnp.max(el, axis=-1, keepdims=True)
    i1 = jnp.min(jnp.where(el == v1, lane_f, big), axis=-1, keepdims=True)
    el2 = jnp.where(lane_f == i1, neg, el)
    v2 = jnp.max(el2, axis=-1, keepdims=True)
    i2 = jnp.min(jnp.where(el2 == v2, lane_f, big), axis=-1, keepdims=True)
    e21 = jnp.exp(v2 - v1)
    w1 = 1.0 / (1.0 + e21)
    w2 = e21 * w1
    g1 = w1 * g_w
    g2 = w2 * g_w
    gates = jnp.where(lane_f == i1, g1, jnp.where(lane_f == i2, g2, 0.0))
    base = GATE_LANE0 + gidx * EXPERTS_PER_GROUP
    lo = jnp.minimum(i1, i2) - base
    hi = jnp.maximum(i1, i2) - base
    pair = jnp.where(lo == 0.0, 0.0, jnp.where(lo == 1.0, 3.0, 5.0)) + (hi - lo - 1.0)
    cls = gidx * PAIRS_PER_GROUP + pair
    g_lo = jnp.where(i1 < i2, g1, g2)
    g_hi = jnp.where(i1 < i2, g2, g1)
    return jnp.where(lane_f == float(GLO_LANE), g_lo, jnp.where(lane_f == float(GHI_LANE), g_hi, gates)), cls


def _residual_and_route(x, y, m, g1_ref, wr_ref, br_ref,
                        xo_ref, hx_ref, meta_ref, counts_ref, cnt_ref):
    @pl.when(pl.program_id(0) == 0)
    def _():
        cnt_ref[...] = jnp.zeros(cnt_ref.shape, F32)

    x_new = x + m[2:3] * y
    xo_ref[...] = x_new
    h2 = _norm_mod(x_new, g1_ref[...], m[3:4], m[4:5])
    hx_ref[:, 0:D_MODEL] = h2
    route, cls = _route(h2, wr_ref, br_ref)
    lane_f = lax.broadcasted_iota(jnp.int32, route.shape, 1).astype(F32)
    onehot = jnp.where(lane_f == cls, 1.0, 0.0)
    row = lax.broadcasted_iota(jnp.int32, (ROW_TILE, ROW_TILE), 0)
    col = lax.broadcasted_iota(jnp.int32, (ROW_TILE, ROW_TILE), 1)
    earlier = jnp.where(col < row, 1.0, 0.0).astype(BF16)
    before = _dot(earlier, onehot.astype(BF16)) + cnt_ref[...]
    rank = jnp.sum(onehot * before, axis=-1, keepdims=True)
    cnt_ref[...] = cnt_ref[...] + jnp.sum(onehot, axis=0, keepdims=True)
    counts_ref[...] = cnt_ref[...]
    route = jnp.where(lane_f == float(CLS_LANE), cls, jnp.where(lane_f == float(RANK_LANE), rank, route))
    hx_ref[:, D_MODEL:HX_W] = route
    meta_ref[0] = route.T[META_ROW0:META_ROW0 + SUBLANES, :]


def _tile_spec(width):
    return pl.BlockSpec((ROW_TILE, width), lambda t: (t, 0))


def _stream_specs(n_stream):
    if n_stream == 1:
        return [_tile_spec(D_MODEL)]
    return [pl.BlockSpec((1, ROW_TILE, D_MODEL),
                         lambda t: (t // TILES_PER_BATCH, jnp.maximum(t % TILES_PER_BATCH - 1, 0), 0)),
            pl.BlockSpec((1, CTX_LEN, D_MODEL), lambda t: (t // TILES_PER_BATCH, 0, 0))]


def _load_stream(refs):
    if len(refs) == 1:
        return refs[0][...]
    is_ctx = pl.program_id(0) % TILES_PER_BATCH == 0
    return jnp.where(is_ctx, refs[1][0], refs[0][0])


def _full_spec(shape):
    nd = len(shape)
    return pl.BlockSpec(shape, lambda t: (0,) * nd)


def _mod_spec():
    return pl.BlockSpec((1, 6, D_MODEL), lambda t: (_mod_row(t), 0, 0))


def _halo_specs(width, halo):
    per_tile = ROW_TILE // halo
    last = N_ROWS // halo - 1
    prev = pl.BlockSpec((halo, width), lambda t: (jnp.maximum(t * per_tile - 1, 0), 0))
    nxt = pl.BlockSpec((halo, width), lambda t: (jnp.minimum((t + 1) * per_tile, last), 0))
    return prev, nxt


def _seq_flags():
    w = pl.program_id(0) % TILES_PER_BATCH
    is_start = (w == 0) | (w == 1)
    is_end = (w == 0) | (w == TILES_PER_BATCH - 1)
    seq_tile = jnp.maximum(w - 1, 0)
    return is_start, is_end, seq_tile, w == 0


_POST_OUT_SHAPES = (jax.ShapeDtypeStruct((N_ROWS, D_MODEL), F32),
                    jax.ShapeDtypeStruct((N_ROWS, HX_W), F32),
                    jax.ShapeDtypeStruct((N_TILES, SUBLANES, ROW_TILE), F32),
                    jax.ShapeDtypeStruct((1, ROUTE_W), F32))


def _post_out_specs():
    return (_tile_spec(D_MODEL),
            _tile_spec(HX_W),
            pl.BlockSpec((1, SUBLANES, ROW_TILE), lambda t: (t, 0, 0)),
            pl.BlockSpec((1, ROUTE_W), lambda t: (0, 0)))


def _class_count_scratch():
    return pltpu.VMEM((1, ROUTE_W), F32)


def _even_pre_kernel(n_stream, *refs):
    mod_ref, g_ref, w_ref, a_ref, gb_ref, gch_ref = refs[n_stream:]
    m = mod_ref[0]
    h = _norm_mod(_load_stream(refs[:n_stream]), g_ref[...], m[0:1], m[1:2])
    u = _dot(h.astype(BF16), w_ref[...])
    ca, cb = CONV_A_DIM, CONV_B_DIM
    a_ref[...] = u[:, 0:ca] * _sigmoid(u[:, ca:2 * ca])
    gb_ref[...] = u[:, 2 * ca:2 * ca + cb]
    gch_ref[...] = u[:, 2 * ca + cb:2 * ca + 2 * cb] * u[:, 2 * ca + 2 * cb:2 * ca + 3 * cb]


def _even_pre(stream, mod_i, g0, w_in):
    n_in = w_in.shape[1]
    return pl.pallas_call(
        functools.partial(_even_pre_kernel, len(stream)),
        grid=(N_TILES,),
        in_specs=_stream_specs(len(stream)) + [_mod_spec(), _full_spec((1, D_MODEL)),
                                               _full_spec((D_MODEL, n_in))],
        out_specs=(_tile_spec(CONV_A_DIM), _tile_spec(CONV_B_DIM), _tile_spec(CONV_B_DIM)),
        out_shape=(jax.ShapeDtypeStruct((N_ROWS, CONV_A_DIM), F32),
                   jax.ShapeDtypeStruct((N_ROWS, CONV_B_DIM), F32),
                   jax.ShapeDtypeStruct((N_ROWS, CONV_B_DIM), F32)),
        compiler_params=_cparams(("parallel",)),
        name="even_pre",
    )(*stream, mod_i, g0, w_in)


def _even_post_kernel(n_stream, *refs):
    (a_ref, ap_ref, an_ref, gch_ref, gp_ref, gn_ref, gb_ref,
     cwa_ref, cba_ref, lng_ref, lnb_ref, cwb_ref, wout_ref,
     mod_ref, g1_ref, wr_ref, br_ref,
     xo_ref, hx_ref, meta_ref, counts_ref, exta_ref, extb_ref, cnt_ref) = refs[n_stream:]
    is_start, is_end, _, _ = _seq_flags()
    ha, hb = CONV_A_HALO, SMALL_HALO
    exta_ref[0:ha] = jnp.where(is_start, 0.0, ap_ref[...])
    exta_ref[ha:ha + ROW_TILE] = a_ref[...]
    exta_ref[ha + ROW_TILE:2 * ha + ROW_TILE] = jnp.where(is_end, 0.0, an_ref[...])
    extb_ref[0:hb] = jnp.where(is_start, 0.0, gp_ref[...])
    extb_ref[hb:hb + ROW_TILE] = gch_ref[...]
    extb_ref[hb + ROW_TILE:2 * hb + ROW_TILE] = jnp.where(is_end, 0.0, gn_ref[...])

    half_a = (CONV_A_WIDTH - 1) // 2
    lead = ha - half_a
    span = ROW_TILE + SUBLANES
    assert 0 <= lead and lead + SUBLANES - 1 <= SUBLANES
    parts = []
    for cblk in range(CONV_A_DIM // LANES):
        cs = slice(cblk * LANES, (cblk + 1) * LANES)
        acc = jnp.zeros((ROW_TILE, LANES), F32) + cba_ref[:, cs]
        for b in range(SUBLANES):
            z = jnp.zeros((span, LANES), F32)
            for k in range(b, CONV_A_WIDTH, SUBLANES):
                z = z + cwa_ref[k:k + 1, cs] * exta_ref[k - b:k - b + span, cs]
            acc = acc + z[lead + b:lead + b + ROW_TILE, :]
        parts.append(acc)
    acc = jnp.concatenate(parts, axis=1)
    mu = jnp.mean(acc, axis=-1, keepdims=True)
    xc = acc - mu
    a = xc * lax.rsqrt(jnp.mean(xc * xc, axis=-1, keepdims=True) + EPS)
    a = _silu(a * lng_ref[...] + lnb_ref[...])

    half_b = (CONV_B_WIDTH - 1) // 2
    cb = jnp.zeros((ROW_TILE, CONV_B_DIM), F32)
    for k in range(CONV_B_WIDTH):
        off = hb - half_b + k
        cb = cb + cwb_ref[k:k + 1, :] * extb_ref[off:off + ROW_TILE, :]
    b = gb_ref[...] * cb

    y = (_dot(a.astype(BF16), wout_ref[0:CONV_A_DIM, :])
         + _dot(b.astype(BF16), wout_ref[CONV_A_DIM:CONV_A_DIM + CONV_B_DIM, :]))
    _residual_and_route(_load_stream(refs[:n_stream]), y, mod_ref[0], g1_ref, wr_ref, br_ref,
                        xo_ref, hx_ref, meta_ref, counts_ref, cnt_ref)


def _even_post(stream, a_pre, gb, gch, cwa, cba, lng, lnb, cwb, w_out, mod_i, g1, wr, br):
    ap_spec, an_spec = _halo_specs(CONV_A_DIM, CONV_A_HALO)
    gp_spec, gn_spec = _halo_specs(CONV_B_DIM, SMALL_HALO)
    return pl.pallas_call(
        functools.partial(_even_post_kernel, len(stream)),
        grid=(N_TILES,),
        in_specs=_stream_specs(len(stream)) + [
                  _tile_spec(CONV_A_DIM), ap_spec, an_spec,
                  _tile_spec(CONV_B_DIM), gp_spec, gn_spec, _tile_spec(CONV_B_DIM),
                  _full_spec((CONV_A_WIDTH, CONV_A_DIM)), _full_spec((1, CONV_A_DIM)),
                  _full_spec((1, CONV_A_DIM)), _full_spec((1, CONV_A_DIM)),
                  _full_spec((CONV_B_WIDTH, CONV_B_DIM)), _full_spec((D_MODEL, D_MODEL)),
                  _mod_spec(), _full_spec((1, D_MODEL)),
                  _full_spec((D_MODEL, 2 * ROUTE_W)),
                  _full_spec((1, ROUTE_W))],
        out_specs=_post_out_specs(),
        out_shape=_POST_OUT_SHAPES,
        scratch_shapes=[pltpu.VMEM((ROW_TILE + 2 * CONV_A_HALO, CONV_A_DIM), F32),
                        pltpu.VMEM((ROW_TILE + 2 * SMALL_HALO, CONV_B_DIM), F32),
                        _class_count_scratch()],
        compiler_params=_cparams(("arbitrary",)),
        name="even_post",
    )(*stream, a_pre, a_pre, a_pre, gch, gch, gch, gb, cwa, cba, lng, lnb, cwb, w_out,
      mod_i, g1, wr, br)


def _odd_pre_kernel(x_ref, mod_ref, g_ref, w_ref, gq_ref, wqm_ref, wqs_ref, gkv_ref, wk_ref, wv_ref,
                    vaug_ref, cos_ref, sin_ref, qt_ref, k_ref, vt_ref, pool_ref):
    m = mod_ref[0]
    h = _norm_mod(x_ref[...], g_ref[...], m[0:1], m[1:2])
    u = _dot(h.astype(BF16), w_ref[...])
    o_kv = Q_LORA
    o_kr = o_kv + KV_LORA
    o_krs = o_kr + HEAD_PAD
    o_pool = o_krs + HEAD_PAD
    cos = cos_ref[...]
    sin = sin_ref[...]

    q_c = u[:, 0:o_kv]
    qn = (q_c * lax.rsqrt(jnp.mean(q_c * q_c, axis=-1, keepdims=True) + EPS) * gq_ref[...]).astype(BF16)
    q_main = _dot(qn, wqm_ref[...])
    q_swap = _dot(qn, wqs_ref[...])
    kv_c = u[:, o_kv:o_kr]
    kvn = (kv_c * lax.rsqrt(jnp.mean(kv_c * kv_c, axis=-1, keepdims=True) + EPS) * gkv_ref[...]).astype(BF16)
    k_nope = _dot(kvn, wk_ref[...])
    v = _dot(kvn, wv_ref[...]) + vaug_ref[...]
    for cc in range(ROW_TILE // ATT_CHUNK):
        vt_ref[cc] = v[cc * ATT_CHUNK:(cc + 1) * ATT_CHUNK, :].T.astype(BF16)
    k_rope = u[:, o_kr:o_krs] * cos + u[:, o_krs:o_pool] * sin
    for hd in range(MLA_HEADS):
        sl = slice(hd * HEAD_PAD, (hd + 1) * HEAD_PAD)
        qh = (q_main[:, sl] * cos + q_swap[:, sl] * sin) * (MLA_SCALE * LOG2E)
        qt_ref[0, sl, :] = qh.T.astype(BF16)
        k_ref[:, sl] = (k_nope[:, sl] + k_rope).astype(BF16)
    pool_ref[...] = u[:, o_pool:o_pool + POOL_DIM]


def _odd_pre(x, mod_i, g0, w_in, gq, wqm, wqs, gkv, wk, wv, cos_t, sin_t):
    n_in = w_in.shape[1]
    vaug = jnp.zeros((MLA_HEADS, V_AUG), F32).at[:, MLA_V].set(1.0).reshape(1, VT_W)
    return pl.pallas_call(
        _odd_pre_kernel,
        grid=(N_TILES,),
        in_specs=[_tile_spec(D_MODEL), _mod_spec(), _full_spec((1, D_MODEL)),
                  _full_spec((D_MODEL, n_in)), _full_spec((1, Q_LORA)),
                  _full_spec((Q_LORA, QK_W)), _full_spec((Q_LORA, QK_W)),
                  _full_spec((1, KV_LORA)), _full_spec((KV_LORA, QK_W)), _full_spec((KV_LORA, VT_W)),
                  _full_spec((1, VT_W)),
                  pl.BlockSpec((ROW_TILE, HEAD_PAD), lambda t: (t % TILES_PER_BATCH, 0)),
                  pl.BlockSpec((ROW_TILE, HEAD_PAD), lambda t: (t % TILES_PER_BATCH, 0))],
        out_specs=(pl.BlockSpec((1, QK_W, ROW_TILE), lambda t: (t, 0, 0)),
                   _tile_spec(QK_W),
                   pl.BlockSpec((ROW_TILE // ATT_CHUNK, VT_W, ATT_CHUNK), lambda t: (t, 0, 0)),
                   _tile_spec(POOL_DIM)),
        out_shape=(jax.ShapeDtypeStruct((N_TILES, QK_W, ROW_TILE), BF16),
                   jax.ShapeDtypeStruct((N_ROWS, QK_W), BF16),
                   jax.ShapeDtypeStruct((N_ROWS // ATT_CHUNK, VT_W, ATT_CHUNK), BF16),
                   jax.ShapeDtypeStruct((N_ROWS, POOL_DIM), F32)),
        compiler_params=_cparams(("parallel",)),
        name="odd_pre",
    )(x, mod_i, g0, w_in, gq, wqm, wqs, gkv, wk, wv, vaug, cos_t, sin_t)


def _attn_kernel(qt_ref, k_ref, vt_ref, o_ref, m_ref, acc_ref, s_ref, p_ref, a_ref):
    m_ref[...] = jnp.full(m_ref.shape, -jnp.inf, F32)
    acc_ref[...] = jnp.zeros(acc_ref.shape, F32)

    def scores_head(c, slot, hd):
        r0 = pl.multiple_of(c * ATT_CHUNK, ATT_CHUNK)
        sl = slice(hd * HEAD_PAD, (hd + 1) * HEAD_PAD)
        s_ref[slot, hd] = _dot(k_ref[pl.ds(r0, ATT_CHUNK), sl], qt_ref[0, sl, :])

    def softmax_head(slot, hd):
        hs = slice(hd, hd + 1)
        m_old = m_ref[hs, :]
        m_new = jnp.maximum(m_old, jnp.max(s_ref[slot, hd], axis=0, keepdims=True))
        m_ref[hs, :] = m_new
        a_ref[slot, hs, :] = jnp.exp2(m_old - m_new)
        for r0 in range(0, ATT_CHUNK, SOFTMAX_ROWS):
            rows = slice(r0, r0 + SOFTMAX_ROWS)
            p_ref[slot, hd, rows, :] = jnp.exp2(s_ref[slot, hd, rows, :] - m_new).astype(BF16)

    def pv_head(c, slot, hd):
        vs = slice(hd * V_AUG, (hd + 1) * V_AUG)
        pv = _dot(vt_ref[c, vs, :], p_ref[slot, hd])
        acc_ref[vs, :] = a_ref[slot, hd:hd + 1, :] * acc_ref[vs, :] + pv

    def stage_scores(c, slot):
        for hd in range(MLA_HEADS):
            scores_head(c, slot, hd)

    def stage_softmax(slot):
        for hd in range(MLA_HEADS):
            softmax_head(slot, hd)

    def stage_pv(c, slot):
        for hd in range(MLA_HEADS):
            pv_head(c, slot, hd)

    def steady(c, slot):
        stage_scores(c, slot)
        stage_pv(c - 2, slot)
        stage_softmax(1 - slot)

    is_ctx = pl.program_id(1) == 0

    @pl.when(is_ctx)
    def _():
        stage_scores(0, 0)
        stage_softmax(0)
        stage_pv(0, 0)

    @pl.when(jnp.logical_not(is_ctx))
    def _():
        n = ROWS_PER_BATCH // ATT_CHUNK
        stage_scores(0, 0)
        stage_scores(1, 1)
        stage_softmax(0)
        first = 2 + (n - 2) % ATT_TRIP
        for c in range(2, first):
            steady(c, c % 2)

        def trip(i, carry):
            c = first + ATT_TRIP * i
            for u in range(ATT_TRIP):
                steady(c + u, (first + u) % 2)
            return carry

        lax.fori_loop(0, (n - first) // ATT_TRIP, trip, 0)
        stage_pv(n - 2, (n - 2) % 2)
        stage_softmax((n - 1) % 2)
        stage_pv(n - 1, (n - 1) % 2)

    outs = []
    for hd in range(MLA_HEADS):
        blk = acc_ref[hd * V_AUG:(hd + 1) * V_AUG, :]
        outs.append(blk[0:MLA_V, :] / blk[MLA_V:MLA_V + 1, :])
    o_ref[...] = jnp.concatenate(outs, axis=0).T


def _attention(qt, k, vt):
    cpb = ROWS_PER_BATCH // ATT_CHUNK
    assert cpb >= 3 and CTX_LEN == ATT_CHUNK
    return pl.pallas_call(
        _attn_kernel,
        grid=(BATCH, TILES_PER_BATCH),
        in_specs=[pl.BlockSpec((1, QK_W, ROW_TILE), lambda b, i: (b * TILES_PER_BATCH + i, 0, 0)),
                  pl.BlockSpec((ROWS_PER_BATCH, QK_W), lambda b, i: (b, 0), pipeline_mode=pl.Buffered(1)),
                  pl.BlockSpec((cpb, VT_W, ATT_CHUNK), lambda b, i: (b, 0, 0), pipeline_mode=pl.Buffered(1))],
        out_specs=pl.BlockSpec((ROW_TILE, ATT_W), lambda b, i: (b * TILES_PER_BATCH + i, 0)),
        out_shape=jax.ShapeDtypeStruct((N_ROWS, ATT_W), F32),
        scratch_shapes=[pltpu.VMEM((MLA_HEADS, ROW_TILE), F32),
                        pltpu.VMEM((VT_W, ROW_TILE), F32),
                        pltpu.VMEM((2, MLA_HEADS, ATT_CHUNK, ROW_TILE), F32),
                        pltpu.VMEM((2, MLA_HEADS, ATT_CHUNK, ROW_TILE), BF16),
                        pltpu.VMEM((2, MLA_HEADS, ROW_TILE), F32)],
        compiler_params=_cparams(("parallel", "parallel")),
        name="attention",
    )(qt, k, vt)


def _odd_post_kernel(att_ref, p_ref, pp_ref, pn_ref, wp_ref, bp_ref, sp_ref, wout_ref,
                     x_ref, mod_ref, g1_ref, wr_ref, br_ref,
                     xo_ref, hx_ref, meta_ref, counts_ref, ext_ref, cnt_ref):
    is_start, is_end, seq_tile, is_ctx = _seq_flags()
    hp = SMALL_HALO
    ext_ref[0:hp] = jnp.where(is_start, 0.0, pp_ref[...])
    ext_ref[hp:hp + ROW_TILE] = p_ref[...]
    ext_ref[hp + ROW_TILE:2 * hp + ROW_TILE] = jnp.where(is_end, 0.0, pn_ref[...])
    seq_len = jnp.where(is_ctx, CTX_LEN, SEQ)
    pos = seq_tile * ROW_TILE + lax.broadcasted_iota(jnp.int32, (ROW_TILE, 1), 0)

    pooled = []
    for g, w in enumerate(POOL_WINDOWS):
        sl = slice(g * POOL_GROUP, (g + 1) * POOL_GROUP)
        ssum = jnp.zeros((ROW_TILE, POOL_GROUP), F32)
        for d in range(-(w // 2), w - w // 2):
            ssum = ssum + ext_ref[hp + d:hp + d + ROW_TILE, sl]
        lo = jnp.maximum(pos - w // 2, 0)
        hi = jnp.minimum(pos - w // 2 + w, seq_len)
        cnt = (hi - lo).astype(F32)
        pm = ssum / cnt - p_ref[:, sl]
        pooled.append(((_dot(pm.astype(BF16), wp_ref[g]) + bp_ref[g:g + 1, :]) * sp_ref[:, sl]).astype(BF16))
    mixed = jnp.concatenate([att_ref[...].astype(BF16)] + pooled, axis=1)
    y = _dot(mixed, wout_ref[...])
    _residual_and_route(x_ref[...], y, mod_ref[0], g1_ref, wr_ref, br_ref,
                        xo_ref, hx_ref, meta_ref, counts_ref, cnt_ref)


def _odd_post(att, pool_in, w_pool, b_pool, s_pool, w_out, x, mod_i, g1, wr, br):
    pp_spec, pn_spec = _halo_specs(POOL_DIM, SMALL_HALO)
    ng = len(POOL_WINDOWS)
    return pl.pallas_call(
        _odd_post_kernel,
        grid=(N_TILES,),
        in_specs=[_tile_spec(ATT_W), _tile_spec(POOL_DIM), pp_spec, pn_spec,
                  _full_spec((ng, POOL_GROUP, POOL_GROUP)), _full_spec((ng, POOL_GROUP)),
                  _full_spec((1, POOL_DIM)), _full_spec((D_MODEL, D_MODEL)),
                  _tile_spec(D_MODEL), _mod_spec(), _full_spec((1, D_MODEL)),
                  _full_spec((D_MODEL, 2 * ROUTE_W)),
                  _full_spec((1, ROUTE_W))],
        out_specs=_post_out_specs(),
        out_shape=_POST_OUT_SHAPES,
        scratch_shapes=[pltpu.VMEM((ROW_TILE + 2 * SMALL_HALO, POOL_DIM), F32), _class_count_scratch()],
        compiler_params=_cparams(("arbitrary",)),
        name="odd_post",
    )(att, pool_in, pool_in, pool_in, w_pool, b_pool, s_pool, w_out, x, mod_i, g1, wr, br)


def _moe_plan(meta, counts):
    cls = meta[:, CLS_LANE - META_ROW0, :].reshape(N_ROWS).astype(jnp.int32)
    rank = meta[:, RANK_LANE - META_ROW0, :].reshape(N_ROWS).astype(jnp.int32)
    tiles = (counts[0, :N_CLASSES].astype(jnp.int32) + MOE_TM - 1) // MOE_TM
    tile_end = jnp.cumsum(tiles)
    first_row = (tile_end - tiles) * MOE_TM
    dest = rank
    for c in range(N_CLASSES):
        dest = dest + jnp.where(cls == c, first_row[c], 0)
    j = jnp.arange(MOE_TILES)
    used = tile_end[-1]
    j_eff = jnp.minimum(j, jnp.maximum(used - 1, 0))
    cls_of_tile = jnp.sum((j_eff[:, None] >= tile_end[None, :]).astype(jnp.int32), axis=1)
    cls_of_tile = jnp.minimum(cls_of_tile, N_CLASSES - 1)
    grp = cls_of_tile // PAIRS_PER_GROUP
    pair = cls_of_tile % PAIRS_PER_GROUP
    e_lo = grp * EXPERTS_PER_GROUP + jnp.array(PAIR_LO, jnp.int32)[pair]
    e_hi = grp * EXPERTS_PER_GROUP + jnp.array(PAIR_HI, jnp.int32)[pair]
    return dest, e_lo, e_hi, (j < used).astype(jnp.int32)


def _move_rows(n_rows, base, dest_ref, row_copy, block_copy):
    def issue(i, carry):
        for u in range(DMA_UNROLL):
            r = i * DMA_UNROLL + u
            row_copy(r, dest_ref[base + r]).start(priority=u % 2)
        return carry

    lax.fori_loop(0, n_rows // DMA_UNROLL, issue, 0)
    block_copy.wait()


def _dispatch_kernel(dest_ref, hx_ref, init_ref, xs_ref, sem):
    del init_ref
    _move_rows(DISPATCH_ROWS, pl.program_id(0) * DISPATCH_ROWS, dest_ref,
               lambda r, d: pltpu.make_async_copy(hx_ref.at[pl.ds(r, 1)], xs_ref.at[pl.ds(d, 1)], sem),
               pltpu.make_async_copy(hx_ref, xs_ref.at[pl.ds(0, DISPATCH_ROWS)], sem))


def _dispatch(dest, hx):
    shape = (MOE_ROWS, HX_W)
    return pl.pallas_call(
        _dispatch_kernel,
        grid_spec=pltpu.PrefetchScalarGridSpec(
            num_scalar_prefetch=1,
            grid=(N_ROWS // DISPATCH_ROWS,),
            in_specs=[pl.BlockSpec((DISPATCH_ROWS, HX_W), lambda t, dest: (t, 0)),
                      pl.BlockSpec(memory_space=pl.ANY)],
            out_specs=pl.BlockSpec(memory_space=pl.ANY),
            scratch_shapes=[pltpu.SemaphoreType.DMA(())]),
        out_shape=jax.ShapeDtypeStruct(shape, F32),
        input_output_aliases={2: 0},
        compiler_params=_cparams(("arbitrary",)),
        name="moe_dispatch",
    )(dest, hx, jnp.zeros(shape, F32))


def _moe_expert_kernel(elo_ref, ehi_ref, valid_ref, xs_ref, w1a_ref, w3a_ref, w2a_ref,
                       w1b_ref, w3b_ref, w2b_ref, ys_ref):
    del elo_ref, ehi_ref
    j = pl.program_id(0)

    @pl.when(valid_ref[j] == 0)
    def _():
        ys_ref[...] = jnp.zeros(ys_ref.shape, F32)

    @pl.when(valid_ref[j] != 0)
    def _():
        h = xs_ref[:, 0:D_MODEL].astype(BF16)
        y = jnp.zeros((MOE_TM, D_MODEL), F32)
        for gate_lane, w1_ref, w3_ref, w2_ref in ((GLO_LANE, w1a_ref, w3a_ref, w2a_ref),
                                                  (GHI_LANE, w1b_ref, w3b_ref, w2b_ref)):
            hid = _silu(_dot(h, w1_ref[0, 0])) * _dot(h, w3_ref[0, 0])
            g = xs_ref[:, D_MODEL + gate_lane:D_MODEL + gate_lane + 1]
            y = y + g * _dot(hid.astype(BF16), w2_ref[0, 0])
        ys_ref[...] = y


def _moe_experts(e_lo, e_hi, valid, xs_sorted, w1, w3, w2, layer):
    up_lo = pl.BlockSpec((1, 1, D_MODEL, D_EXPERT), lambda j, elo, ehi, valid: (layer, elo[j], 0, 0))
    dn_lo = pl.BlockSpec((1, 1, D_EXPERT, D_MODEL), lambda j, elo, ehi, valid: (layer, elo[j], 0, 0))
    up_hi = pl.BlockSpec((1, 1, D_MODEL, D_EXPERT), lambda j, elo, ehi, valid: (layer, ehi[j], 0, 0))
    dn_hi = pl.BlockSpec((1, 1, D_EXPERT, D_MODEL), lambda j, elo, ehi, valid: (layer, ehi[j], 0, 0))
    return pl.pallas_call(
        _moe_expert_kernel,
        grid_spec=pltpu.PrefetchScalarGridSpec(
            num_scalar_prefetch=3,
            grid=(MOE_TILES,),
            in_specs=[pl.BlockSpec((MOE_TM, HX_W), lambda j, elo, ehi, valid: (j, 0)),
                      up_lo, up_lo, dn_lo, up_hi, up_hi, dn_hi],
            out_specs=pl.BlockSpec((MOE_TM, D_MODEL), lambda j, elo, ehi, valid: (j, 0))),
        out_shape=jax.ShapeDtypeStruct((MOE_ROWS, D_MODEL), F32),
        compiler_params=_cparams(("arbitrary",)),
        name="moe_experts",
    )(e_lo, e_hi, valid, xs_sorted, w1, w3, w2, w1, w3, w2)


def _combine_kernel(dest_ref, x_ref, mod_ref, ys_ref, o_ref, buf_ref, sem):
    t = pl.program_id(0)
    _move_rows(COMBINE_ROWS, t * COMBINE_ROWS, dest_ref,
               lambda r, d: pltpu.make_async_copy(ys_ref.at[pl.ds(d, 1)], buf_ref.at[pl.ds(r, 1)], sem),
               pltpu.make_async_copy(ys_ref.at[pl.ds(0, COMBINE_ROWS)], buf_ref, sem))
    sub = COMBINE_ROWS // ROW_TILE
    for b in range(sub):
        rs = slice(b * ROW_TILE, (b + 1) * ROW_TILE)
        m5 = mod_ref[pl.ds(_mod_row(t * sub + b), 1)][0, 5:6]
        o_ref[rs, :] = x_ref[rs, :] + m5 * buf_ref[rs, :]


def _combine(dest, x, mod_i, ys):
    return pl.pallas_call(
        _combine_kernel,
        grid_spec=pltpu.PrefetchScalarGridSpec(
            num_scalar_prefetch=1,
            grid=(N_ROWS // COMBINE_ROWS,),
            in_specs=[pl.BlockSpec((COMBINE_ROWS, D_MODEL), lambda t, dest: (t, 0)),
                      pl.BlockSpec((MOD_ROWS, 6, D_MODEL), lambda t, dest: (0, 0, 0)),
                      pl.BlockSpec(memory_space=pl.ANY)],
            out_specs=pl.BlockSpec((COMBINE_ROWS, D_MODEL), lambda t, dest: (t, 0)),
            scratch_shapes=[pltpu.VMEM((COMBINE_ROWS, D_MODEL), F32), pltpu.SemaphoreType.DMA(())]),
        out_shape=jax.ShapeDtypeStruct((N_ROWS, D_MODEL), F32),
        compiler_params=_cparams(("arbitrary",)),
        name="moe_combine",
    )(dest, x, mod_i, ys)


def _final_kernel(x_ref, g_ref, o_ref):
    x = x_ref[...]
    o_ref[0] = x * lax.rsqrt(jnp.mean(x * x, axis=-1, keepdims=True) + EPS) * g_ref[...]


def _final_norm(x, g):
    lat_tiles = SEQ // ROW_TILE
    return pl.pallas_call(
        _final_kernel,
        grid=(BATCH, lat_tiles),
        in_specs=[pl.BlockSpec((ROW_TILE, D_MODEL), lambda b, i: (b * TILES_PER_BATCH + 1 + i, 0)),
                  pl.BlockSpec((1, D_MODEL), lambda b, i: (0, 0))],
        out_specs=pl.BlockSpec((1, ROW_TILE, D_MODEL), lambda b, i: (b, i, 0)),
        out_shape=jax.ShapeDtypeStruct((BATCH, SEQ, D_MODEL), F32),
        compiler_params=_cparams(("parallel", "parallel")),
        name="final_norm",
    )(x, g)


_SWAP32 = tuple(list(range(8, 16)) + list(range(0, 8)) + list(range(24, 32)) + list(range(16, 24)))


def _rope_tables():
    f32 = np.float32
    rows = SEQ // GRID_W
    pos_row = np.repeat(np.arange(rows, dtype=f32), GRID_W)
    pos_col = np.tile(np.arange(GRID_W, dtype=f32), rows)
    inv = (f32(ROPE_BASE) ** (-np.arange(0, AXIS_DIM, 2, dtype=f32) / f32(AXIS_DIM))).astype(f32)
    ang = np.concatenate([pos_row[:, None] * inv, pos_col[:, None] * inv], axis=-1).astype(f32)
    cos, sin = np.cos(ang).astype(f32), np.sin(ang).astype(f32)
    p = AXIS_PAIRS
    c32 = np.concatenate([cos[:, :p], cos[:, :p], cos[:, p:], cos[:, p:]], axis=-1)
    s32 = np.concatenate([-sin[:, :p], sin[:, :p], -sin[:, p:], sin[:, p:]], axis=-1)
    pad = HEAD_PAD - MLA_NOPE - MLA_ROPE
    cos_lat = np.concatenate([np.ones((SEQ, MLA_NOPE), f32), c32, np.zeros((SEQ, pad), f32)], axis=-1)
    sin_lat = np.concatenate([np.zeros((SEQ, MLA_NOPE), f32), s32, np.zeros((SEQ, pad), f32)], axis=-1)
    cos_ctx = np.concatenate([np.ones((CTX_LEN, MLA_NOPE + MLA_ROPE), f32), np.zeros((CTX_LEN, pad), f32)], axis=-1)
    sin_ctx = np.zeros((CTX_LEN, HEAD_PAD), f32)
    return (jnp.asarray(np.concatenate([cos_ctx, cos_lat], axis=0)),
            jnp.asarray(np.concatenate([sin_ctx, sin_lat], axis=0)))


def _odd_weights(w_in, w_uq, w_ukv):
    swap = jnp.array(_SWAP32)
    o_kv, o_kr, o_pool = Q_LORA, Q_LORA + KV_LORA, Q_LORA + KV_LORA + MLA_ROPE
    w_kr = w_in[:, o_kr:o_pool]
    pad_l = jnp.zeros((D_MODEL, MLA_NOPE), F32)
    pad_r = jnp.zeros((D_MODEL, HEAD_PAD - MLA_NOPE - MLA_ROPE), F32)
    w_in2 = jnp.concatenate([w_in[:, :o_kr], pad_l, w_kr, pad_r, pad_l, w_kr[:, swap], pad_r,
                             w_in[:, o_pool:]], axis=-1).astype(BF16)
    wq = w_uq.reshape(Q_LORA, MLA_HEADS, MLA_NOPE + MLA_ROPE)
    zq = jnp.zeros((Q_LORA, MLA_HEADS, HEAD_PAD - MLA_NOPE - MLA_ROPE), F32)
    wq_main = jnp.concatenate([wq, zq], axis=-1).reshape(Q_LORA, QK_W).astype(BF16)
    wq_swap = jnp.concatenate([jnp.zeros((Q_LORA, MLA_HEADS, MLA_NOPE), F32),
                               wq[:, :, MLA_NOPE:][:, :, swap], zq], axis=-1).reshape(Q_LORA, QK_W).astype(BF16)
    wkv = w_ukv.reshape(KV_LORA, MLA_HEADS, MLA_NOPE + MLA_V)
    zk = jnp.zeros((KV_LORA, MLA_HEADS, HEAD_PAD - MLA_NOPE), F32)
    wk = jnp.concatenate([wkv[:, :, :MLA_NOPE], zk], axis=-1).reshape(KV_LORA, QK_W).astype(BF16)
    zv = jnp.zeros((KV_LORA, MLA_HEADS, V_AUG - MLA_V), F32)
    wv = jnp.concatenate([wkv[:, :, MLA_NOPE:], zv], axis=-1).reshape(KV_LORA, VT_W).astype(BF16)
    return w_in2, wq_main, wq_swap, wk, wv


def kernel(x, c, ctx, c_ctx, w_mod, b_mod, norm_g, ev_w_in, ev_conv_a_w, ev_conv_a_b, ev_ln_a_g, ev_ln_a_b,
           ev_conv_b_w, ev_w_out, od_w_in, od_q_norm_g, od_w_uq, od_kv_norm_g, od_w_ukv, od_w_pool, od_b_pool,
           od_s_pool, od_w_out, moe_wg, moe_bg, moe_we, moe_be, moe_w1, moe_w3, moe_w2, final_g):
    assert x.shape == (BATCH, SEQ, D_MODEL) and ctx.shape == (BATCH, CTX_LEN, D_MODEL)
    assert CTX_LEN == ROW_TILE
    stream = (x, ctx)
    cvec = jnp.concatenate([c, c_ctx[None, :], jnp.zeros((MOD_ROWS - BATCH - 1, D_MODEL), F32)], axis=0)
    mod = _modulation(cvec, w_mod, b_mod).reshape(DEPTH, MOD_ROWS, 6, D_MODEL)
    cos_t, sin_t = _rope_tables()
    w1_bf, w3_bf, w2_bf = moe_w1.astype(BF16), moe_w3.astype(BF16), moe_w2.astype(BF16)

    for i in range(DEPTH):
        j = i // 2
        mod_i = mod[i]
        g0 = norm_g[i, 0][None, :]
        g1 = norm_g[i, 1][None, :]
        wr = jnp.concatenate([moe_wg[i], moe_we[i],
                              jnp.zeros((D_MODEL, ROUTE_W - N_GROUPS - N_EXPERTS), F32)], axis=-1)
        wr_hi = wr.astype(BF16)
        wr = jnp.concatenate([wr_hi, (wr - wr_hi.astype(F32)).astype(BF16)], axis=-1)
        br = jnp.concatenate([moe_bg[i], moe_be[i],
                              jnp.zeros((ROUTE_W - N_GROUPS - N_EXPERTS,), F32)])[None, :]
        if i % 2 == 0:
            a_pre, gb, gch = _even_pre(stream, mod_i, g0, ev_w_in[j].astype(BF16))
            xs, hx, meta, counts = _even_post(stream, a_pre, gb, gch, ev_conv_a_w[j], ev_conv_a_b[j][None, :],
                                              ev_ln_a_g[j][None, :], ev_ln_a_b[j][None, :], ev_conv_b_w[j],
                                              ev_w_out[j].astype(BF16), mod_i, g1, wr, br)
        else:
            w_in2, wq_main, wq_swap, wk, wv = _odd_weights(od_w_in[j], od_w_uq[j], od_w_ukv[j])
            q, k, v, pool_in = _odd_pre(xs, mod_i, g0, w_in2, od_q_norm_g[j][None, :], wq_main, wq_swap,
                                        od_kv_norm_g[j][None, :], wk, wv, cos_t, sin_t)
            att = _attention(q, k, v)
            xs, hx, meta, counts = _odd_post(att, pool_in, od_w_pool[j].astype(BF16), od_b_pool[j],
                                      od_s_pool[j][None, :], od_w_out[j].astype(BF16), xs, mod_i, g1,
                                      wr, br)
        dest, e_lo, e_hi, valid = _moe_plan(meta, counts)
        ys = _moe_experts(e_lo, e_hi, valid, _dispatch(dest, hx), w1_bf, w3_bf, w2_bf, i)
        xs = _combine(dest, xs, mod_i, ys)
        stream = (xs,)
    return _final_norm(xs, final_g[None, :])
```

```python
import functools

import numpy as np
import jax
import jax.numpy as jnp
from jax import lax
from jax.experimental import pallas as pl
from jax.experimental.pallas import tpu as pltpu

F32 = jnp.float32
BF16 = jnp.bfloat16

D_MODEL = 1024
BATCH = 2
SEQ = 8192
DEPTH = 4
GRID_W = 64
CTX_LEN = 256
EPS = 1e-6
CONV_A_DIM = 512
CONV_A_WIDTH = 31
CONV_B_DIM = 512
CONV_B_WIDTH = 3
MLA_HEADS = 8
MLA_NOPE = 64
MLA_ROPE = 32
MLA_V = 64
Q_LORA = 384
KV_LORA = 256
MLA_SCALE = (MLA_NOPE + MLA_ROPE) ** -0.5
AXIS_DIM = MLA_ROPE // 2
AXIS_PAIRS = AXIS_DIM // 2
ROPE_BASE = 10000.0
POOL_WINDOWS = (2, 4, 8, 16)
POOL_GROUP = 128
POOL_DIM = POOL_GROUP * len(POOL_WINDOWS)
N_GROUPS = 4
EXPERTS_PER_GROUP = 4
N_EXPERTS = 16
D_EXPERT = 512

LANES = 128
SUBLANES = 8
VMEM_LIMIT = 48 * 1024 * 1024

ROW_TILE = 256
ROWS_PER_BATCH = CTX_LEN + SEQ
N_ROWS = BATCH * ROWS_PER_BATCH
TILES_PER_BATCH = ROWS_PER_BATCH // ROW_TILE
N_TILES = N_ROWS // ROW_TILE
HEAD_PAD = LANES
QK_W = MLA_HEADS * HEAD_PAD
ATT_W = MLA_HEADS * MLA_V
V_AUG = MLA_V + 16
VT_W = MLA_HEADS * V_AUG
CONV_A_HALO = 16
SMALL_HALO = 8
ATT_CHUNK = 256
SOFTMAX_ROWS = 64
ATT_TRIP = 2
LOG2E = 1.4426950408889634
ROUTE_W = LANES
GATE_LANE0 = N_GROUPS
CLS_LANE = GATE_LANE0 + N_EXPERTS
RANK_LANE = CLS_LANE + 1
GLO_LANE = RANK_LANE + 1
GHI_LANE = GLO_LANE + 1
META_ROW0 = (CLS_LANE // SUBLANES) * SUBLANES
HX_W = D_MODEL + ROUTE_W
PAIRS_PER_GROUP = 6
N_CLASSES = N_GROUPS * PAIRS_PER_GROUP
PAIR_A = (0, 0, 0, 1, 1, 3)
PAIR_B = (1, 2, 3, 3, 2, 2)
PAIR_A_IS_HIGHER = (0, 0, 0, 0, 0, 1)
MOE_TM = 256
MOE_TILES = -(-(N_ROWS + N_CLASSES * (MOE_TM - 1)) // MOE_TM)
MOE_ROWS = MOE_TILES * MOE_TM
DISPATCH_ROWS = 2816
COMBINE_ROWS = 1536
DMA_UNROLL = 8
MOD_ROWS = 8


def _mod_row(t):
    return jnp.where(t % TILES_PER_BATCH == 0, BATCH, t // TILES_PER_BATCH)


def _cparams(sem):
    return pltpu.CompilerParams(dimension_semantics=sem, vmem_limit_bytes=VMEM_LIMIT)


def _dot(a, b):
    return jnp.dot(a, b, preferred_element_type=F32)


def _sigmoid(x):
    return 1.0 / (1.0 + jnp.exp(-x))


def _silu(x):
    return x * _sigmoid(x)


def _norm_mod(x, g, shift, scale):
    y = x * lax.rsqrt(jnp.mean(x * x, axis=-1, keepdims=True) + EPS)
    return (y * g) * (1.0 + scale) + shift


def _mod_kernel(c_ref, w_ref, b_ref, o_ref):
    s = _silu(c_ref[...])
    o_ref[0] = jnp.dot(s, w_ref[0], preferred_element_type=F32,
                       precision=lax.Precision.HIGHEST) + b_ref[0]


def _modulation(cvec, w_mod, b_mod):
    nblk = 4
    bw = 6 * D_MODEL // nblk
    return pl.pallas_call(
        _mod_kernel,
        grid=(DEPTH, nblk),
        in_specs=[pl.BlockSpec((MOD_ROWS, D_MODEL), lambda i, j: (0, 0)),
                  pl.BlockSpec((1, D_MODEL, bw), lambda i, j: (i, 0, j)),
                  pl.BlockSpec((1, 1, bw), lambda i, j: (i, 0, j))],
        out_specs=pl.BlockSpec((1, MOD_ROWS, bw), lambda i, j: (i, 0, j)),
        out_shape=jax.ShapeDtypeStruct((DEPTH, MOD_ROWS, 6 * D_MODEL), F32),
        compiler_params=_cparams(("parallel", "parallel")),
        name="modulation",
    )(cvec, w_mod, b_mod.reshape(DEPTH, 1, 6 * D_MODEL))


def _route(h2, wr_ref, br_ref):
    hi = h2.astype(BF16)
    lo = (h2 - hi.astype(F32)).astype(BF16)
    both = _dot(hi, wr_ref[...])
    logits = (both[:, 0:ROUTE_W] + both[:, ROUTE_W:2 * ROUTE_W] + _dot(lo, wr_ref[:, 0:ROUTE_W])
              + br_ref[...])
    lane = lax.broadcasted_iota(jnp.int32, logits.shape, 1)
    lane_f = lane.astype(F32)
    neg = -jnp.inf
    big = float(ROUTE_W)
    gl = jnp.where(lane < N_GROUPS, logits, neg)
    gmax = jnp.max(gl, axis=-1, keepdims=True)
    gidx = jnp.min(jnp.where(gl == gmax, lane_f, big), axis=-1, keepdims=True)
    g_w = 1.0 / jnp.sum(jnp.exp(gl - gmax), axis=-1, keepdims=True)
    egrp = ((lane - GATE_LANE0) // EXPERTS_PER_GROUP).astype(F32)
    in_g = (lane >= GATE_LANE0) & (lane < GATE_LANE0 + N_EXPERTS) & (egrp == gidx)
    el = jnp.where(in_g, logits, neg)
    v1 = jnp.max(el, axis=-1, keepdims=True)
    i1 = jnp.min(jnp.where(el == v1, lane_f, big), axis=-1, keepdims=True)
    el2 = jnp.where(lane_f == i1, neg, el)
    v2 = jnp.max(el2, axis=-1, keepdims=True)
    i2 = jnp.min(jnp.where(el2 == v2, lane_f, big), axis=-1, keepdims=True)
    e21 = jnp.exp(v2 - v1)
    w1 = 1.0 / (1.0 + e21)
    w2 = e21 * w1
    g1 = w1 * g_w
    g2 = w2 * g_w
    gates = jnp.where(lane_f == i1, g1, jnp.where(lane_f == i2, g2, 0.0))
    base = GATE_LANE0 + gidx * EXPERTS_PER_GROUP
    lo = jnp.minimum(i1, i2) - base
    hi = jnp.maximum(i1, i2) - base
    pair = jnp.where(lo == 0.0, hi - 1.0, jnp.where(lo == 1.0, 6.0 - hi, 5.0))
    cls = gidx * PAIRS_PER_GROUP + pair
    g_lo = jnp.where(i1 < i2, g1, g2)
    g_hi = jnp.where(i1 < i2, g2, g1)
    return jnp.where(lane_f == float(GLO_LANE), g_lo, jnp.where(lane_f == float(GHI_LANE), g_hi, gates)), cls


def _residual_and_route(x, y, m, g1_ref, wr_ref, br_ref,
                        xo_ref, hx_ref, meta_ref, counts_ref, cnt_ref):
    @pl.when(pl.program_id(0) == 0)
    def _():
        cnt_ref[...] = jnp.zeros(cnt_ref.shape, F32)

    x_new = x + m[2:3] * y
    xo_ref[...] = x_new
    h2 = _norm_mod(x_new, g1_ref[...], m[3:4], m[4:5])
    hx_ref[:, 0:D_MODEL] = h2
    route, cls = _route(h2, wr_ref, br_ref)
    lane_f = lax.broadcasted_iota(jnp.int32, route.shape, 1).astype(F32)
    onehot = jnp.where(lane_f == cls, 1.0, 0.0)
    row = lax.broadcasted_iota(jnp.int32, (ROW_TILE, ROW_TILE), 0)
    col = lax.broadcasted_iota(jnp.int32, (ROW_TILE, ROW_TILE), 1)
    earlier = jnp.where(col < row, 1.0, 0.0).astype(BF16)
    before = _dot(earlier, onehot.astype(BF16)) + cnt_ref[...]
    rank = jnp.sum(onehot * before, axis=-1, keepdims=True)
    cnt_ref[...] = cnt_ref[...] + jnp.sum(onehot, axis=0, keepdims=True)
    counts_ref[...] = cnt_ref[...]
    route = jnp.where(lane_f == float(CLS_LANE), cls, jnp.where(lane_f == float(RANK_LANE), rank, route))
    hx_ref[:, D_MODEL:HX_W] = route
    meta_ref[0] = route.T[META_ROW0:META_ROW0 + SUBLANES, :]


def _tile_spec(width):
    return pl.BlockSpec((ROW_TILE, width), lambda t: (t, 0))


def _stream_specs(n_stream):
    if n_stream == 1:
        return [_tile_spec(D_MODEL)]
    return [pl.BlockSpec((1, ROW_TILE, D_MODEL),
                         lambda t: (t // TILES_PER_BATCH, jnp.maximum(t % TILES_PER_BATCH - 1, 0), 0)),
            pl.BlockSpec((1, CTX_LEN, D_MODEL), lambda t: (t // TILES_PER_BATCH, 0, 0))]


def _load_stream(refs):
    if len(refs) == 1:
        return refs[0][...]
    is_ctx = pl.program_id(0) % TILES_PER_BATCH == 0
    return jnp.where(is_ctx, refs[1][0], refs[0][0])


def _full_spec(shape):
    nd = len(shape)
    return pl.BlockSpec(shape, lambda t: (0,) * nd)


def _mod_spec():
    return pl.BlockSpec((1, 6, D_MODEL), lambda t: (_mod_row(t), 0, 0))


def _halo_specs(width, halo):
    per_tile = ROW_TILE // halo
    last = N_ROWS // halo - 1
    prev = pl.BlockSpec((halo, width), lambda t: (jnp.maximum(t * per_tile - 1, 0), 0))
    nxt = pl.BlockSpec((halo, width), lambda t: (jnp.minimum((t + 1) * per_tile, last), 0))
    return prev, nxt


def _seq_flags():
    w = pl.program_id(0) % TILES_PER_BATCH
    is_start = (w == 0) | (w == 1)
    is_end = (w == 0) | (w == TILES_PER_BATCH - 1)
    seq_tile = jnp.maximum(w - 1, 0)
    return is_start, is_end, seq_tile, w == 0


_POST_OUT_SHAPES = (jax.ShapeDtypeStruct((N_ROWS, D_MODEL), F32),
                    jax.ShapeDtypeStruct((N_ROWS, HX_W), F32),
                    jax.ShapeDtypeStruct((N_TILES, SUBLANES, ROW_TILE), F32),
                    jax.ShapeDtypeStruct((1, ROUTE_W), F32))


def _post_out_specs():
    return (_tile_spec(D_MODEL),
            _tile_spec(HX_W),
            pl.BlockSpec((1, SUBLANES, ROW_TILE), lambda t: (t, 0, 0)),
            pl.BlockSpec((1, ROUTE_W), lambda t: (0, 0)))


def _class_count_scratch():
    return pltpu.VMEM((1, ROUTE_W), F32)


def _even_pre_kernel(n_stream, *refs):
    mod_ref, g_ref, w_ref, a_ref, gb_ref, gch_ref = refs[n_stream:]
    m = mod_ref[0]
    h = _norm_mod(_load_stream(refs[:n_stream]), g_ref[...], m[0:1], m[1:2])
    u = _dot(h.astype(BF16), w_ref[...])
    ca, cb = CONV_A_DIM, CONV_B_DIM
    a_ref[...] = u[:, 0:ca] * _sigmoid(u[:, ca:2 * ca])
    gb_ref[...] = u[:, 2 * ca:2 * ca + cb]
    gch_ref[...] = u[:, 2 * ca + cb:2 * ca + 2 * cb] * u[:, 2 * ca + 2 * cb:2 * ca + 3 * cb]


def _even_pre(stream, mod_i, g0, w_in):
    n_in = w_in.shape[1]
    return pl.pallas_call(
        functools.partial(_even_pre_kernel, len(stream)),
        grid=(N_TILES,),
        in_specs=_stream_specs(len(stream)) + [_mod_spec(), _full_spec((1, D_MODEL)),
                                               _full_spec((D_MODEL, n_in))],
        out_specs=(_tile_spec(CONV_A_DIM), _tile_spec(CONV_B_DIM), _tile_spec(CONV_B_DIM)),
        out_shape=(jax.ShapeDtypeStruct((N_ROWS, CONV_A_DIM), F32),
                   jax.ShapeDtypeStruct((N_ROWS, CONV_B_DIM), F32),
                   jax.ShapeDtypeStruct((N_ROWS, CONV_B_DIM), F32)),
        compiler_params=_cparams(("parallel",)),
        name="even_pre",
    )(*stream, mod_i, g0, w_in)


def _even_post_kernel(n_stream, *refs):
    (a_ref, ap_ref, an_ref, gch_ref, gp_ref, gn_ref, gb_ref,
     cwa_ref, cba_ref, lng_ref, lnb_ref, cwb_ref, wout_ref,
     mod_ref, g1_ref, wr_ref, br_ref,
     xo_ref, hx_ref, meta_ref, counts_ref, exta_ref, extb_ref, cnt_ref) = refs[n_stream:]
    is_start, is_end, _, _ = _seq_flags()
    ha, hb = CONV_A_HALO, SMALL_HALO
    exta_ref[0:ha] = jnp.where(is_start, 0.0, ap_ref[...])
    exta_ref[ha:ha + ROW_TILE] = a_ref[...]
    exta_ref[ha + ROW_TILE:2 * ha + ROW_TILE] = jnp.where(is_end, 0.0, an_ref[...])
    extb_ref[0:hb] = jnp.where(is_start, 0.0, gp_ref[...])
    extb_ref[hb:hb + ROW_TILE] = gch_ref[...]
    extb_ref[hb + ROW_TILE:2 * hb + ROW_TILE] = jnp.where(is_end, 0.0, gn_ref[...])

    half_a = (CONV_A_WIDTH - 1) // 2
    lead = ha - half_a
    span = ROW_TILE + SUBLANES
    assert 0 <= lead and lead + SUBLANES - 1 <= SUBLANES
    parts = []
    for cblk in range(CONV_A_DIM // LANES):
        cs = slice(cblk * LANES, (cblk + 1) * LANES)
        acc = jnp.zeros((ROW_TILE, LANES), F32) + cba_ref[:, cs]
        for b in range(SUBLANES):
            z = jnp.zeros((span, LANES), F32)
            for k in range(b, CONV_A_WIDTH, SUBLANES):
                z = z + cwa_ref[k:k + 1, cs] * exta_ref[k - b:k - b + span, cs]
            acc = acc + z[lead + b:lead + b + ROW_TILE, :]
        parts.append(acc)
    acc = jnp.concatenate(parts, axis=1)
    mu = jnp.mean(acc, axis=-1, keepdims=True)
    xc = acc - mu
    a = xc * lax.rsqrt(jnp.mean(xc * xc, axis=-1, keepdims=True) + EPS)
    a = _silu(a * lng_ref[...] + lnb_ref[...])

    half_b = (CONV_B_WIDTH - 1) // 2
    cb = jnp.zeros((ROW_TILE, CONV_B_DIM), F32)
    for k in range(CONV_B_WIDTH):
        off = hb - half_b + k
        cb = cb + cwb_ref[k:k + 1, :] * extb_ref[off:off + ROW_TILE, :]
    b = gb_ref[...] * cb

    y = (_dot(a.astype(BF16), wout_ref[0:CONV_A_DIM, :])
         + _dot(b.astype(BF16), wout_ref[CONV_A_DIM:CONV_A_DIM + CONV_B_DIM, :]))
    _residual_and_route(_load_stream(refs[:n_stream]), y, mod_ref[0], g1_ref, wr_ref, br_ref,
                        xo_ref, hx_ref, meta_ref, counts_ref, cnt_ref)


def _even_post(stream, a_pre, gb, gch, cwa, cba, lng, lnb, cwb, w_out, mod_i, g1, wr, br):
    ap_spec, an_spec = _halo_specs(CONV_A_DIM, CONV_A_HALO)
    gp_spec, gn_spec = _halo_specs(CONV_B_DIM, SMALL_HALO)
    return pl.pallas_call(
        functools.partial(_even_post_kernel, len(stream)),
        grid=(N_TILES,),
        in_specs=_stream_specs(len(stream)) + [
                  _tile_spec(CONV_A_DIM), ap_spec, an_spec,
                  _tile_spec(CONV_B_DIM), gp_spec, gn_spec, _tile_spec(CONV_B_DIM),
                  _full_spec((CONV_A_WIDTH, CONV_A_DIM)), _full_spec((1, CONV_A_DIM)),
                  _full_spec((1, CONV_A_DIM)), _full_spec((1, CONV_A_DIM)),
                  _full_spec((CONV_B_WIDTH, CONV_B_DIM)), _full_spec((D_MODEL, D_MODEL)),
                  _mod_spec(), _full_spec((1, D_MODEL)),
                  _full_spec((D_MODEL, 2 * ROUTE_W)),
                  _full_spec((1, ROUTE_W))],
        out_specs=_post_out_specs(),
        out_shape=_POST_OUT_SHAPES,
        scratch_shapes=[pltpu.VMEM((ROW_TILE + 2 * CONV_A_HALO, CONV_A_DIM), F32),
                        pltpu.VMEM((ROW_TILE + 2 * SMALL_HALO, CONV_B_DIM), F32),
                        _class_count_scratch()],
        compiler_params=_cparams(("arbitrary",)),
        name="even_post",
    )(*stream, a_pre, a_pre, a_pre, gch, gch, gch, gb, cwa, cba, lng, lnb, cwb, w_out,
      mod_i, g1, wr, br)


def _odd_pre_kernel(x_ref, mod_ref, g_ref, w_ref, gq_ref, wqm_ref, wqs_ref, gkv_ref, wk_ref, wv_ref,
                    vaug_ref, cos_ref, sin_ref, qt_ref, k_ref, vt_ref, pool_ref):
    m = mod_ref[0]
    h = _norm_mod(x_ref[...], g_ref[...], m[0:1], m[1:2])
    u = _dot(h.astype(BF16), w_ref[...])
    o_kv = Q_LORA
    o_kr = o_kv + KV_LORA
    o_krs = o_kr + HEAD_PAD
    o_pool = o_krs + HEAD_PAD
    cos = cos_ref[...]
    sin = sin_ref[...]

    q_c = u[:, 0:o_kv]
    qn = (q_c * lax.rsqrt(jnp.mean(q_c * q_c, axis=-1, keepdims=True) + EPS) * gq_ref[...]).astype(BF16)
    q_main = _dot(qn, wqm_ref[...])
    q_swap = _dot(qn, wqs_ref[...])
    kv_c = u[:, o_kv:o_kr]
    kvn = (kv_c * lax.rsqrt(jnp.mean(kv_c * kv_c, axis=-1, keepdims=True) + EPS) * gkv_ref[...]).astype(BF16)
    k_nope = _dot(kvn, wk_ref[...])
    v = _dot(kvn, wv_ref[...]) + vaug_ref[...]
    for cc in range(ROW_TILE // ATT_CHUNK):
        vt_ref[cc] = v[cc * ATT_CHUNK:(cc + 1) * ATT_CHUNK, :].T.astype(BF16)
    k_rope = u[:, o_kr:o_krs] * cos + u[:, o_krs:o_pool] * sin
    for hd in range(MLA_HEADS):
        sl = slice(hd * HEAD_PAD, (hd + 1) * HEAD_PAD)
        qh = (q_main[:, sl] * cos + q_swap[:, sl] * sin) * (MLA_SCALE * LOG2E)
        qt_ref[0, sl, :] = qh.T.astype(BF16)
        k_ref[:, sl] = (k_nope[:, sl] + k_rope).astype(BF16)
    pool_ref[...] = u[:, o_pool:o_pool + POOL_DIM]


def _odd_pre(x, mod_i, g0, w_in, gq, wqm, wqs, gkv, wk, wv, cos_t, sin_t):
    n_in = w_in.shape[1]
    vaug = jnp.zeros((MLA_HEADS, V_AUG), F32).at[:, MLA_V].set(1.0).reshape(1, VT_W)
    return pl.pallas_call(
        _odd_pre_kernel,
        grid=(N_TILES,),
        in_specs=[_tile_spec(D_MODEL), _mod_spec(), _full_spec((1, D_MODEL)),
                  _full_spec((D_MODEL, n_in)), _full_spec((1, Q_LORA)),
                  _full_spec((Q_LORA, QK_W)), _full_spec((Q_LORA, QK_W)),
                  _full_spec((1, KV_LORA)), _full_spec((KV_LORA, QK_W)), _full_spec((KV_LORA, VT_W)),
                  _full_spec((1, VT_W)),
                  pl.BlockSpec((ROW_TILE, HEAD_PAD), lambda t: (t % TILES_PER_BATCH, 0)),
                  pl.BlockSpec((ROW_TILE, HEAD_PAD), lambda t: (t % TILES_PER_BATCH, 0))],
        out_specs=(pl.BlockSpec((1, QK_W, ROW_TILE), lambda t: (t, 0, 0)),
                   _tile_spec(QK_W),
                   pl.BlockSpec((ROW_TILE // ATT_CHUNK, VT_W, ATT_CHUNK), lambda t: (t, 0, 0)),
                   _tile_spec(POOL_DIM)),
        out_shape=(jax.ShapeDtypeStruct((N_TILES, QK_W, ROW_TILE), BF16),
                   jax.ShapeDtypeStruct((N_ROWS, QK_W), BF16),
                   jax.ShapeDtypeStruct((N_ROWS // ATT_CHUNK, VT_W, ATT_CHUNK), BF16),
                   jax.ShapeDtypeStruct((N_ROWS, POOL_DIM), F32)),
        compiler_params=_cparams(("parallel",)),
        name="odd_pre",
    )(x, mod_i, g0, w_in, gq, wqm, wqs, gkv, wk, wv, vaug, cos_t, sin_t)


def _attn_kernel(qt_ref, k_ref, vt_ref, o_ref, m_ref, acc_ref, s_ref, p_ref, a_ref):
    m_ref[...] = jnp.full(m_ref.shape, -jnp.inf, F32)
    acc_ref[...] = jnp.zeros(acc_ref.shape, F32)

    def scores_head(c, slot, hd):
        r0 = pl.multiple_of(c * ATT_CHUNK, ATT_CHUNK)
        sl = slice(hd * HEAD_PAD, (hd + 1) * HEAD_PAD)
        s_ref[slot, hd] = _dot(k_ref[pl.ds(r0, ATT_CHUNK), sl], qt_ref[0, sl, :])

    def softmax_head(slot, hd):
        hs = slice(hd, hd + 1)
        m_old = m_ref[hs, :]
        m_new = jnp.maximum(m_old, jnp.max(s_ref[slot, hd], axis=0, keepdims=True))
        m_ref[hs, :] = m_new
        a_ref[slot, hs, :] = jnp.exp2(m_old - m_new)
        for r0 in range(0, ATT_CHUNK, SOFTMAX_ROWS):
            rows = slice(r0, r0 + SOFTMAX_ROWS)
            p_ref[slot, hd, rows, :] = jnp.exp2(s_ref[slot, hd, rows, :] - m_new).astype(BF16)

    def pv_head(c, slot, hd):
        vs = slice(hd * V_AUG, (hd + 1) * V_AUG)
        pv = _dot(vt_ref[c, vs, :], p_ref[slot, hd])
        acc_ref[vs, :] = a_ref[slot, hd:hd + 1, :] * acc_ref[vs, :] + pv

    def stage_scores(c, slot):
        for hd in range(MLA_HEADS):
            scores_head(c, slot, hd)

    def stage_softmax(slot):
        for hd in range(MLA_HEADS):
            softmax_head(slot, hd)

    def stage_pv(c, slot):
        for hd in range(MLA_HEADS):
            pv_head(c, slot, hd)

    def steady(c, slot):
        stage_scores(c, slot)
        stage_pv(c - 2, slot)
        stage_softmax(1 - slot)

    is_ctx = pl.program_id(1) == 0

    @pl.when(is_ctx)
    def _():
        stage_scores(0, 0)
        stage_softmax(0)
        stage_pv(0, 0)

    @pl.when(jnp.logical_not(is_ctx))
    def _():
        n = ROWS_PER_BATCH // ATT_CHUNK
        stage_scores(0, 0)
        stage_scores(1, 1)
        stage_softmax(0)
        first = 2 + (n - 2) % ATT_TRIP
        for c in range(2, first):
            steady(c, c % 2)

        def trip(i, carry):
            c = first + ATT_TRIP * i
            for u in range(ATT_TRIP):
                steady(c + u, (first + u) % 2)
            return carry

        lax.fori_loop(0, (n - first) // ATT_TRIP, trip, 0)
        stage_pv(n - 2, (n - 2) % 2)
        stage_softmax((n - 1) % 2)
        stage_pv(n - 1, (n - 1) % 2)

    outs = []
    for hd in range(MLA_HEADS):
        blk = acc_ref[hd * V_AUG:(hd + 1) * V_AUG, :]
        outs.append(blk[0:MLA_V, :] / blk[MLA_V:MLA_V + 1, :])
    o_ref[...] = jnp.concatenate(outs, axis=0).T


def _attention(qt, k, vt):
    cpb = ROWS_PER_BATCH // ATT_CHUNK
    assert cpb >= 3 and CTX_LEN == ATT_CHUNK
    return pl.pallas_call(
        _attn_kernel,
        grid=(BATCH, TILES_PER_BATCH),
        in_specs=[pl.BlockSpec((1, QK_W, ROW_TILE), lambda b, i: (b * TILES_PER_BATCH + i, 0, 0)),
                  pl.BlockSpec((ROWS_PER_BATCH, QK_W), lambda b, i: (b, 0), pipeline_mode=pl.Buffered(1)),
                  pl.BlockSpec((cpb, VT_W, ATT_CHUNK), lambda b, i: (b, 0, 0), pipeline_mode=pl.Buffered(1))],
        out_specs=pl.BlockSpec((ROW_TILE, ATT_W), lambda b, i: (b * TILES_PER_BATCH + i, 0)),
        out_shape=jax.ShapeDtypeStruct((N_ROWS, ATT_W), F32),
        scratch_shapes=[pltpu.VMEM((MLA_HEADS, ROW_TILE), F32),
                        pltpu.VMEM((VT_W, ROW_TILE), F32),
                        pltpu.VMEM((2, MLA_HEADS, ATT_CHUNK, ROW_TILE), F32),
                        pltpu.VMEM((2, MLA_HEADS, ATT_CHUNK, ROW_TILE), BF16),
                        pltpu.VMEM((2, MLA_HEADS, ROW_TILE), F32)],
        compiler_params=_cparams(("parallel", "parallel")),
        name="attention",
    )(qt, k, vt)


def _odd_post_kernel(att_ref, p_ref, pp_ref, pn_ref, wp_ref, bp_ref, sp_ref, wout_ref,
                     x_ref, mod_ref, g1_ref, wr_ref, br_ref,
                     xo_ref, hx_ref, meta_ref, counts_ref, ext_ref, cnt_ref):
    is_start, is_end, seq_tile, is_ctx = _seq_flags()
    hp = SMALL_HALO
    ext_ref[0:hp] = jnp.where(is_start, 0.0, pp_ref[...])
    ext_ref[hp:hp + ROW_TILE] = p_ref[...]
    ext_ref[hp + ROW_TILE:2 * hp + ROW_TILE] = jnp.where(is_end, 0.0, pn_ref[...])
    seq_len = jnp.where(is_ctx, CTX_LEN, SEQ)
    pos = seq_tile * ROW_TILE + lax.broadcasted_iota(jnp.int32, (ROW_TILE, 1), 0)

    pooled = []
    for g, w in enumerate(POOL_WINDOWS):
        sl = slice(g * POOL_GROUP, (g + 1) * POOL_GROUP)
        ssum = jnp.zeros((ROW_TILE, POOL_GROUP), F32)
        for d in range(-(w // 2), w - w // 2):
            ssum = ssum + ext_ref[hp + d:hp + d + ROW_TILE, sl]
        lo = jnp.maximum(pos - w // 2, 0)
        hi = jnp.minimum(pos - w // 2 + w, seq_len)
        cnt = (hi - lo).astype(F32)
        pm = ssum / cnt - p_ref[:, sl]
        pooled.append(((_dot(pm.astype(BF16), wp_ref[g]) + bp_ref[g:g + 1, :]) * sp_ref[:, sl]).astype(BF16))
    mixed = jnp.concatenate([att_ref[...].astype(BF16)] + pooled, axis=1)
    y = _dot(mixed, wout_ref[...])
    _residual_and_route(x_ref[...], y, mod_ref[0], g1_ref, wr_ref, br_ref,
                        xo_ref, hx_ref, meta_ref, counts_ref, cnt_ref)


def _odd_post(att, pool_in, w_pool, b_pool, s_pool, w_out, x, mod_i, g1, wr, br):
    pp_spec, pn_spec = _halo_specs(POOL_DIM, SMALL_HALO)
    ng = len(POOL_WINDOWS)
    return pl.pallas_call(
        _odd_post_kernel,
        grid=(N_TILES,),
        in_specs=[_tile_spec(ATT_W), _tile_spec(POOL_DIM), pp_spec, pn_spec,
                  _full_spec((ng, POOL_GROUP, POOL_GROUP)), _full_spec((ng, POOL_GROUP)),
                  _full_spec((1, POOL_DIM)), _full_spec((D_MODEL, D_MODEL)),
                  _tile_spec(D_MODEL), _mod_spec(), _full_spec((1, D_MODEL)),
                  _full_spec((D_MODEL, 2 * ROUTE_W)),
                  _full_spec((1, ROUTE_W))],
        out_specs=_post_out_specs(),
        out_shape=_POST_OUT_SHAPES,
        scratch_shapes=[pltpu.VMEM((ROW_TILE + 2 * SMALL_HALO, POOL_DIM), F32), _class_count_scratch()],
        compiler_params=_cparams(("arbitrary",)),
        name="odd_post",
    )(att, pool_in, pool_in, pool_in, w_pool, b_pool, s_pool, w_out, x, mod_i, g1, wr, br)


def _moe_plan(meta, counts):
    cls = meta[:, CLS_LANE - META_ROW0, :].reshape(N_ROWS).astype(jnp.int32)
    rank = meta[:, RANK_LANE - META_ROW0, :].reshape(N_ROWS).astype(jnp.int32)
    tiles = (counts[0, :N_CLASSES].astype(jnp.int32) + MOE_TM - 1) // MOE_TM
    tile_end = jnp.cumsum(tiles)
    first_row = (tile_end - tiles) * MOE_TM
    dest = rank
    for c in range(N_CLASSES):
        dest = dest + jnp.where(cls == c, first_row[c], 0)
    j = jnp.arange(MOE_TILES)
    used = tile_end[-1]
    j_eff = jnp.minimum(j, jnp.maximum(used - 1, 0))
    cls_of_tile = jnp.sum((j_eff[:, None] >= tile_end[None, :]).astype(jnp.int32), axis=1)
    cls_of_tile = jnp.minimum(cls_of_tile, N_CLASSES - 1)
    grp = cls_of_tile // PAIRS_PER_GROUP
    pair = cls_of_tile % PAIRS_PER_GROUP
    e_a = grp * EXPERTS_PER_GROUP + jnp.array(PAIR_A, jnp.int32)[pair]
    e_b = grp * EXPERTS_PER_GROUP + jnp.array(PAIR_B, jnp.int32)[pair]
    status = jnp.where(j < used, 1 + jnp.array(PAIR_A_IS_HIGHER, jnp.int32)[pair], 0)
    return dest, e_a, e_b, status


def _move_rows(n_rows, base, dest_ref, row_copy, block_copy):
    def issue(i, carry):
        for u in range(DMA_UNROLL):
            r = i * DMA_UNROLL + u
            row_copy(r, dest_ref[base + r]).start(priority=u % 2)
        return carry

    lax.fori_loop(0, n_rows // DMA_UNROLL, issue, 0)
    block_copy.wait()


def _dispatch_kernel(dest_ref, hx_ref, init_ref, xs_ref, sem):
    del init_ref
    _move_rows(DISPATCH_ROWS, pl.program_id(0) * DISPATCH_ROWS, dest_ref,
               lambda r, d: pltpu.make_async_copy(hx_ref.at[pl.ds(r, 1)], xs_ref.at[pl.ds(d, 1)], sem),
               pltpu.make_async_copy(hx_ref, xs_ref.at[pl.ds(0, DISPATCH_ROWS)], sem))


def _dispatch(dest, hx, init):
    shape = (MOE_ROWS, HX_W)
    return pl.pallas_call(
        _dispatch_kernel,
        grid_spec=pltpu.PrefetchScalarGridSpec(
            num_scalar_prefetch=1,
            grid=(N_ROWS // DISPATCH_ROWS,),
            in_specs=[pl.BlockSpec((DISPATCH_ROWS, HX_W), lambda t, dest: (t, 0)),
                      pl.BlockSpec(memory_space=pl.ANY)],
            out_specs=pl.BlockSpec(memory_space=pl.ANY),
            scratch_shapes=[pltpu.SemaphoreType.DMA(())]),
        out_shape=jax.ShapeDtypeStruct(shape, F32),
        input_output_aliases={2: 0},
        compiler_params=_cparams(("arbitrary",)),
        name="moe_dispatch",
    )(dest, hx, init)


def _moe_expert_kernel(ea_ref, eb_ref, status_ref, xs_ref, w1a_ref, w3a_ref, w2a_ref,
                       w1b_ref, w3b_ref, w2b_ref, ys_ref):
    del ea_ref, eb_ref
    status = status_ref[pl.program_id(0)]

    @pl.when(status == 0)
    def _():
        ys_ref[...] = jnp.zeros(ys_ref.shape, F32)

    @pl.when(status != 0)
    def _():
        h = xs_ref[:, 0:D_MODEL].astype(BF16)
        g_lo = xs_ref[:, D_MODEL + GLO_LANE:D_MODEL + GLO_LANE + 1]
        g_hi = xs_ref[:, D_MODEL + GHI_LANE:D_MODEL + GHI_LANE + 1]
        a_is_higher = status == 2
        y = jnp.zeros((MOE_TM, D_MODEL), F32)
        for g, w1_ref, w3_ref, w2_ref in ((jnp.where(a_is_higher, g_hi, g_lo), w1a_ref, w3a_ref, w2a_ref),
                                          (jnp.where(a_is_higher, g_lo, g_hi), w1b_ref, w3b_ref, w2b_ref)):
            hid = _silu(_dot(h, w1_ref[0, 0])) * _dot(h, w3_ref[0, 0])
            y = y + g * _dot(hid.astype(BF16), w2_ref[0, 0])
        ys_ref[...] = y


def _moe_experts(e_a, e_b, status, xs_sorted, w1, w3, w2, layer):
    up_a = pl.BlockSpec((1, 1, D_MODEL, D_EXPERT), lambda j, ea, eb, status: (layer, ea[j], 0, 0))
    dn_a = pl.BlockSpec((1, 1, D_EXPERT, D_MODEL), lambda j, ea, eb, status: (layer, ea[j], 0, 0))
    up_b = pl.BlockSpec((1, 1, D_MODEL, D_EXPERT), lambda j, ea, eb, status: (layer, eb[j], 0, 0))
    dn_b = pl.BlockSpec((1, 1, D_EXPERT, D_MODEL), lambda j, ea, eb, status: (layer, eb[j], 0, 0))
    return pl.pallas_call(
        _moe_expert_kernel,
        grid_spec=pltpu.PrefetchScalarGridSpec(
            num_scalar_prefetch=3,
            grid=(MOE_TILES,),
            in_specs=[pl.BlockSpec((MOE_TM, HX_W), lambda j, ea, eb, status: (j, 0)),
                      up_a, up_a, dn_a, up_b, up_b, dn_b],
            out_specs=pl.BlockSpec((MOE_TM, D_MODEL), lambda j, ea, eb, status: (j, 0))),
        out_shape=jax.ShapeDtypeStruct((MOE_ROWS, D_MODEL), F32),
        compiler_params=_cparams(("arbitrary",)),
        name="moe_experts",
    )(e_a, e_b, status, xs_sorted, w1, w3, w2, w1, w3, w2)


def _combine_kernel(dest_ref, x_ref, mod_ref, ys_ref, o_ref, buf_ref, sem):
    t = pl.program_id(0)
    _move_rows(COMBINE_ROWS, t * COMBINE_ROWS, dest_ref,
               lambda r, d: pltpu.make_async_copy(ys_ref.at[pl.ds(d, 1)], buf_ref.at[pl.ds(r, 1)], sem),
               pltpu.make_async_copy(ys_ref.at[pl.ds(0, COMBINE_ROWS)], buf_ref, sem))
    sub = COMBINE_ROWS // ROW_TILE
    for b in range(sub):
        rs = slice(b * ROW_TILE, (b + 1) * ROW_TILE)
        m5 = mod_ref[pl.ds(_mod_row(t * sub + b), 1)][0, 5:6]
        o_ref[rs, :] = x_ref[rs, :] + m5 * buf_ref[rs, :]


def _combine(dest, x, mod_i, ys):
    return pl.pallas_call(
        _combine_kernel,
        grid_spec=pltpu.PrefetchScalarGridSpec(
            num_scalar_prefetch=1,
            grid=(N_ROWS // COMBINE_ROWS,),
            in_specs=[pl.BlockSpec((COMBINE_ROWS, D_MODEL), lambda t, dest: (t, 0)),
                      pl.BlockSpec((MOD_ROWS, 6, D_MODEL), lambda t, dest: (0, 0, 0)),
                      pl.BlockSpec(memory_space=pl.ANY)],
            out_specs=pl.BlockSpec((COMBINE_ROWS, D_MODEL), lambda t, dest: (t, 0)),
            scratch_shapes=[pltpu.VMEM((COMBINE_ROWS, D_MODEL), F32), pltpu.SemaphoreType.DMA(())]),
        out_shape=jax.ShapeDtypeStruct((N_ROWS, D_MODEL), F32),
        compiler_params=_cparams(("arbitrary",)),
        name="moe_combine",
    )(dest, x, mod_i, ys)


def _combine_final_kernel(dest_ref, x_ref, mod_ref, g_ref, ys_ref, o_ref, buf_ref, sem):
    _move_rows(ROW_TILE, pl.program_id(0) * ROW_TILE, dest_ref,
               lambda r, d: pltpu.make_async_copy(ys_ref.at[pl.ds(d, 1)], buf_ref.at[pl.ds(r, 1)], sem),
               pltpu.make_async_copy(ys_ref.at[pl.ds(0, ROW_TILE)], buf_ref, sem))
    x = x_ref[...] + mod_ref[0][5:6] * buf_ref[...]
    o_ref[0] = x * lax.rsqrt(jnp.mean(x * x, axis=-1, keepdims=True) + EPS) * g_ref[...]


def _combine_final(dest, x, mod_i, ys, g):
    def out_map(t, dest):
        return (t // TILES_PER_BATCH, jnp.maximum(t % TILES_PER_BATCH - 1, 0), 0)

    return pl.pallas_call(
        _combine_final_kernel,
        grid_spec=pltpu.PrefetchScalarGridSpec(
            num_scalar_prefetch=1,
            grid=(N_TILES,),
            in_specs=[pl.BlockSpec((ROW_TILE, D_MODEL), lambda t, dest: (t, 0)),
                      pl.BlockSpec((1, 6, D_MODEL), lambda t, dest: (_mod_row(t), 0, 0)),
                      pl.BlockSpec((1, D_MODEL), lambda t, dest: (0, 0)),
                      pl.BlockSpec(memory_space=pl.ANY)],
            out_specs=pl.BlockSpec((1, ROW_TILE, D_MODEL), out_map),
            scratch_shapes=[pltpu.VMEM((ROW_TILE, D_MODEL), F32), pltpu.SemaphoreType.DMA(())]),
        out_shape=jax.ShapeDtypeStruct((BATCH, SEQ, D_MODEL), F32),
        compiler_params=_cparams(("arbitrary",)),
        name="moe_combine_final",
    )(dest, x, mod_i, g, ys)


_SWAP32 = tuple(list(range(8, 16)) + list(range(0, 8)) + list(range(24, 32)) + list(range(16, 24)))


def _rope_tables():
    f32 = np.float32
    rows = SEQ // GRID_W
    pos_row = np.repeat(np.arange(rows, dtype=f32), GRID_W)
    pos_col = np.tile(np.arange(GRID_W, dtype=f32), rows)
    inv = (f32(ROPE_BASE) ** (-np.arange(0, AXIS_DIM, 2, dtype=f32) / f32(AXIS_DIM))).astype(f32)
    ang = np.concatenate([pos_row[:, None] * inv, pos_col[:, None] * inv], axis=-1).astype(f32)
    cos, sin = np.cos(ang).astype(f32), np.sin(ang).astype(f32)
    p = AXIS_PAIRS
    c32 = np.concatenate([cos[:, :p], cos[:, :p], cos[:, p:], cos[:, p:]], axis=-1)
    s32 = np.concatenate([-sin[:, :p], sin[:, :p], -sin[:, p:], sin[:, p:]], axis=-1)
    pad = HEAD_PAD - MLA_NOPE - MLA_ROPE
    cos_lat = np.concatenate([np.ones((SEQ, MLA_NOPE), f32), c32, np.zeros((SEQ, pad), f32)], axis=-1)
    sin_lat = np.concatenate([np.zeros((SEQ, MLA_NOPE), f32), s32, np.zeros((SEQ, pad), f32)], axis=-1)
    cos_ctx = np.concatenate([np.ones((CTX_LEN, MLA_NOPE + MLA_ROPE), f32), np.zeros((CTX_LEN, pad), f32)], axis=-1)
    sin_ctx = np.zeros((CTX_LEN, HEAD_PAD), f32)
    return (jnp.asarray(np.concatenate([cos_ctx, cos_lat], axis=0)),
            jnp.asarray(np.concatenate([sin_ctx, sin_lat], axis=0)))


def _odd_weights(w_in, w_uq, w_ukv):
    swap = jnp.array(_SWAP32)
    o_kv, o_kr, o_pool = Q_LORA, Q_LORA + KV_LORA, Q_LORA + KV_LORA + MLA_ROPE
    w_kr = w_in[:, o_kr:o_pool]
    pad_l = jnp.zeros((D_MODEL, MLA_NOPE), F32)
    pad_r = jnp.zeros((D_MODEL, HEAD_PAD - MLA_NOPE - MLA_ROPE), F32)
    w_in2 = jnp.concatenate([w_in[:, :o_kr], pad_l, w_kr, pad_r, pad_l, w_kr[:, swap], pad_r,
                             w_in[:, o_pool:]], axis=-1).astype(BF16)
    wq = w_uq.reshape(Q_LORA, MLA_HEADS, MLA_NOPE + MLA_ROPE)
    zq = jnp.zeros((Q_LORA, MLA_HEADS, HEAD_PAD - MLA_NOPE - MLA_ROPE), F32)
    wq_main = jnp.concatenate([wq, zq], axis=-1).reshape(Q_LORA, QK_W).astype(BF16)
    wq_swap = jnp.concatenate([jnp.zeros((Q_LORA, MLA_HEADS, MLA_NOPE), F32),
                               wq[:, :, MLA_NOPE:][:, :, swap], zq], axis=-1).reshape(Q_LORA, QK_W).astype(BF16)
    wkv = w_ukv.reshape(KV_LORA, MLA_HEADS, MLA_NOPE + MLA_V)
    zk = jnp.zeros((KV_LORA, MLA_HEADS, HEAD_PAD - MLA_NOPE), F32)
    wk = jnp.concatenate([wkv[:, :, :MLA_NOPE], zk], axis=-1).reshape(KV_LORA, QK_W).astype(BF16)
    zv = jnp.zeros((KV_LORA, MLA_HEADS, V_AUG - MLA_V), F32)
    wv = jnp.concatenate([wkv[:, :, MLA_NOPE:], zv], axis=-1).reshape(KV_LORA, VT_W).astype(BF16)
    return w_in2, wq_main, wq_swap, wk, wv


def kernel(x, c, ctx, c_ctx, w_mod, b_mod, norm_g, ev_w_in, ev_conv_a_w, ev_conv_a_b, ev_ln_a_g, ev_ln_a_b,
           ev_conv_b_w, ev_w_out, od_w_in, od_q_norm_g, od_w_uq, od_kv_norm_g, od_w_ukv, od_w_pool, od_b_pool,
           od_s_pool, od_w_out, moe_wg, moe_bg, moe_we, moe_be, moe_w1, moe_w3, moe_w2, final_g):
    assert x.shape == (BATCH, SEQ, D_MODEL) and ctx.shape == (BATCH, CTX_LEN, D_MODEL)
    assert CTX_LEN == ROW_TILE
    stream = (x, ctx)
    cvec = jnp.concatenate([c, c_ctx[None, :], jnp.zeros((MOD_ROWS - BATCH - 1, D_MODEL), F32)], axis=0)
    mod = _modulation(cvec, w_mod, b_mod).reshape(DEPTH, MOD_ROWS, 6, D_MODEL)
    cos_t, sin_t = _rope_tables()
    w1_bf, w3_bf, w2_bf = moe_w1.astype(BF16), moe_w3.astype(BF16), moe_w2.astype(BF16)
    sorted_rows = jnp.zeros((MOE_ROWS, HX_W), F32)

    for i in range(DEPTH):
        j = i // 2
        mod_i = mod[i]
        g0 = norm_g[i, 0][None, :]
        g1 = norm_g[i, 1][None, :]
        wr = jnp.concatenate([moe_wg[i], moe_we[i],
                              jnp.zeros((D_MODEL, ROUTE_W - N_GROUPS - N_EXPERTS), F32)], axis=-1)
        wr_hi = wr.astype(BF16)
        wr = jnp.concatenate([wr_hi, (wr - wr_hi.astype(F32)).astype(BF16)], axis=-1)
        br = jnp.concatenate([moe_bg[i], moe_be[i],
                              jnp.zeros((ROUTE_W - N_GROUPS - N_EXPERTS,), F32)])[None, :]
        if i % 2 == 0:
            a_pre, gb, gch = _even_pre(stream, mod_i, g0, ev_w_in[j].astype(BF16))
            xs, hx, meta, counts = _even_post(stream, a_pre, gb, gch, ev_conv_a_w[j], ev_conv_a_b[j][None, :],
                                              ev_ln_a_g[j][None, :], ev_ln_a_b[j][None, :], ev_conv_b_w[j],
                                              ev_w_out[j].astype(BF16), mod_i, g1, wr, br)
        else:
            w_in2, wq_main, wq_swap, wk, wv = _odd_weights(od_w_in[j], od_w_uq[j], od_w_ukv[j])
            q, k, v, pool_in = _odd_pre(xs, mod_i, g0, w_in2, od_q_norm_g[j][None, :], wq_main, wq_swap,
                                        od_kv_norm_g[j][None, :], wk, wv, cos_t, sin_t)
            att = _attention(q, k, v)
            xs, hx, meta, counts = _odd_post(att, pool_in, od_w_pool[j].astype(BF16), od_b_pool[j],
                                      od_s_pool[j][None, :], od_w_out[j].astype(BF16), xs, mod_i, g1,
                                      wr, br)
        dest, e_a, e_b, status = _moe_plan(meta, counts)
        sorted_rows = _dispatch(dest, hx, sorted_rows)
        ys = _moe_experts(e_a, e_b, status, sorted_rows, w1_bf, w3_bf, w2_bf, i)
        if i == DEPTH - 1:
            return _combine_final(dest, xs, mod_i, ys, final_g[None, :])
        xs = _combine(dest, xs, mod_i, ys)
        stream = (xs,)
```

```python
import functools

import numpy as np
import jax
import jax.numpy as jnp
from jax import lax
from jax.experimental import pallas as pl
from jax.experimental.pallas import tpu as pltpu

F32 = jnp.float32
BF16 = jnp.bfloat16

D_MODEL = 1024
BATCH = 2
SEQ = 8192
DEPTH = 4
GRID_W = 64
CTX_LEN = 256
EPS = 1e-6
CONV_A_DIM = 512
CONV_A_WIDTH = 31
CONV_B_DIM = 512
CONV_B_WIDTH = 3
MLA_HEADS = 8
MLA_NOPE = 64
MLA_ROPE = 32
MLA_V = 64
Q_LORA = 384
KV_LORA = 256
MLA_SCALE = (MLA_NOPE + MLA_ROPE) ** -0.5
AXIS_DIM = MLA_ROPE // 2
AXIS_PAIRS = AXIS_DIM // 2
ROPE_BASE = 10000.0
POOL_WINDOWS = (2, 4, 8, 16)
POOL_GROUP = 128
POOL_DIM = POOL_GROUP * len(POOL_WINDOWS)
N_GROUPS = 4
EXPERTS_PER_GROUP = 4
N_EXPERTS = 16
D_EXPERT = 512

LANES = 128
SUBLANES = 8
VMEM_LIMIT = 48 * 1024 * 1024

ROW_TILE = 256
ROWS_PER_BATCH = CTX_LEN + SEQ
N_ROWS = BATCH * ROWS_PER_BATCH
TILES_PER_BATCH = ROWS_PER_BATCH // ROW_TILE
N_TILES = N_ROWS // ROW_TILE
HEAD_PAD = LANES
QK_W = MLA_HEADS * HEAD_PAD
ATT_W = MLA_HEADS * MLA_V
V_AUG = MLA_V + 16
VT_W = MLA_HEADS * V_AUG
CONV_A_HALO = 16
SMALL_HALO = 8
ATT_CHUNK = 256
SOFTMAX_ROWS = 64
ATT_TRIP = 2
LOG2E = 1.4426950408889634
ROUTE_W = LANES
GATE_LANE0 = N_GROUPS
CLS_LANE = GATE_LANE0 + N_EXPERTS
RANK_LANE = CLS_LANE + 1
GLO_LANE = RANK_LANE + 1
GHI_LANE = GLO_LANE + 1
META_ROW0 = (CLS_LANE // SUBLANES) * SUBLANES
HX_W = D_MODEL + ROUTE_W
PAIRS_PER_GROUP = 6
N_CLASSES = N_GROUPS * PAIRS_PER_GROUP
PAIR_A = (0, 0, 0, 1, 1, 3)
PAIR_B = (1, 2, 3, 3, 2, 2)
PAIR_A_IS_HIGHER = (0, 0, 0, 0, 0, 1)
MOE_TM = 256
MOE_TILES = -(-(N_ROWS + N_CLASSES * (MOE_TM - 1)) // MOE_TM)
MOE_ROWS = MOE_TILES * MOE_TM
DISPATCH_ROWS = 2816
COMBINE_ROWS = 1536
DMA_UNROLL = 8
MOD_ROWS = 8


def _mod_row(t):
    return jnp.where(t % TILES_PER_BATCH == 0, BATCH, t // TILES_PER_BATCH)


def _cparams(sem):
    return pltpu.CompilerParams(dimension_semantics=sem, vmem_limit_bytes=VMEM_LIMIT)


def _dot(a, b):
    return jnp.dot(a, b, preferred_element_type=F32)


def _sigmoid(x):
    return 1.0 / (1.0 + jnp.exp(-x))


def _silu(x):
    return x * _sigmoid(x)


def _norm_mod(x, g, shift, scale):
    y = x * lax.rsqrt(jnp.mean(x * x, axis=-1, keepdims=True) + EPS)
    return (y * g) * (1.0 + scale) + shift


def _mod_kernel(c_ref, w_ref, b_ref, o_ref):
    s = _silu(c_ref[...])
    o_ref[0] = jnp.dot(s, w_ref[0], preferred_element_type=F32,
                       precision=lax.Precision.HIGHEST) + b_ref[0]


def _modulation(cvec, w_mod, b_mod):
    nblk = 4
    bw = 6 * D_MODEL // nblk
    return pl.pallas_call(
        _mod_kernel,
        grid=(DEPTH, nblk),
        in_specs=[pl.BlockSpec((MOD_ROWS, D_MODEL), lambda i, j: (0, 0)),
                  pl.BlockSpec((1, D_MODEL, bw), lambda i, j: (i, 0, j)),
                  pl.BlockSpec((1, 1, bw), lambda i, j: (i, 0, j))],
        out_specs=pl.BlockSpec((1, MOD_ROWS, bw), lambda i, j: (i, 0, j)),
        out_shape=jax.ShapeDtypeStruct((DEPTH, MOD_ROWS, 6 * D_MODEL), F32),
        compiler_params=_cparams(("parallel", "parallel")),
        name="modulation",
    )(cvec, w_mod, b_mod.reshape(DEPTH, 1, 6 * D_MODEL))


def _route(h2, wr_ref, br_ref):
    hi = h2.astype(BF16)
    lo = (h2 - hi.astype(F32)).astype(BF16)
    both = _dot(hi, wr_ref[...])
    logits = (both[:, 0:ROUTE_W] + both[:, ROUTE_W:2 * ROUTE_W] + _dot(lo, wr_ref[:, 0:ROUTE_W])
              + br_ref[...])
    lane = lax.broadcasted_iota(jnp.int32, logits.shape, 1)
    lane_f = lane.astype(F32)
    neg = -jnp.inf
    big = float(ROUTE_W)
    gl = jnp.where(lane < N_GROUPS, logits, neg)
    gmax = jnp.max(gl, axis=-1, keepdims=True)
    gidx = jnp.min(jnp.where(gl == gmax, lane_f, big), axis=-1, keepdims=True)
    g_w = 1.0 / jnp.sum(jnp.exp(gl - gmax), axis=-1, keepdims=True)
    egrp = ((lane - GATE_LANE0) // EXPERTS_PER_GROUP).astype(F32)
    in_g = (lane >= GATE_LANE0) & (lane < GATE_LANE0 + N_EXPERTS) & (egrp == gidx)
    el = jnp.where(in_g, logits, neg)
    v1 = jnp.max(el, axis=-1, keepdims=True)
    i1 = jnp.min(jnp.where(el == v1, lane_f, big), axis=-1, keepdims=True)
    el2 = jnp.where(lane_f == i1, neg, el)
    v2 = jnp.max(el2, axis=-1, keepdims=True)
    i2 = jnp.min(jnp.where(el2 == v2, lane_f, big), axis=-1, keepdims=True)
    e21 = jnp.exp(v2 - v1)
    w1 = 1.0 / (1.0 + e21)
    w2 = e21 * w1
    g1 = w1 * g_w
    g2 = w2 * g_w
    gates = jnp.where(lane_f == i1, g1, jnp.where(lane_f == i2, g2, 0.0))
    base = GATE_LANE0 + gidx * EXPERTS_PER_GROUP
    lo = jnp.minimum(i1, i2) - base
    hi = jnp.maximum(i1, i2) - base
    pair = jnp.where(lo == 0.0, hi - 1.0, jnp.where(lo == 1.0, 6.0 - hi, 5.0))
    cls = gidx * PAIRS_PER_GROUP + pair
    g_lo = jnp.where(i1 < i2, g1, g2)
    g_hi = jnp.where(i1 < i2, g2, g1)
    return jnp.where(lane_f == float(GLO_LANE), g_lo, jnp.where(lane_f == float(GHI_LANE), g_hi, gates)), cls


def _residual_and_route(x, y, m, g1_ref, wr_ref, br_ref,
                        xo_ref, hx_ref, meta_ref, counts_ref, cnt_ref):
    @pl.when(pl.program_id(0) == 0)
    def _():
        cnt_ref[...] = jnp.zeros(cnt_ref.shape, F32)

    x_new = x + m[2:3] * y
    xo_ref[...] = x_new
    h2 = _norm_mod(x_new, g1_ref[...], m[3:4], m[4:5])
    hx_ref[:, 0:D_MODEL] = h2
    route, cls = _route(h2, wr_ref, br_ref)
    lane_f = lax.broadcasted_iota(jnp.int32, route.shape, 1).astype(F32)
    onehot = jnp.where(lane_f == cls, 1.0, 0.0)
    row = lax.broadcasted_iota(jnp.int32, (ROW_TILE, ROW_TILE), 0)
    col = lax.broadcasted_iota(jnp.int32, (ROW_TILE, ROW_TILE), 1)
    earlier = jnp.where(col < row, 1.0, 0.0).astype(BF16)
    before = _dot(earlier, onehot.astype(BF16)) + cnt_ref[...]
    rank = jnp.sum(onehot * before, axis=-1, keepdims=True)
    cnt_ref[...] = cnt_ref[...] + jnp.sum(onehot, axis=0, keepdims=True)
    counts_ref[...] = cnt_ref[...]
    route = jnp.where(lane_f == float(CLS_LANE), cls, jnp.where(lane_f == float(RANK_LANE), rank, route))
    hx_ref[:, D_MODEL:HX_W] = route
    meta_ref[0] = route.T[META_ROW0:META_ROW0 + SUBLANES, :]


def _tile_spec(width):
    return pl.BlockSpec((ROW_TILE, width), lambda t: (t, 0))


def _stream_specs(n_stream):
    if n_stream == 1:
        return [_tile_spec(D_MODEL)]
    return [pl.BlockSpec((1, ROW_TILE, D_MODEL),
                         lambda t: (t // TILES_PER_BATCH, jnp.maximum(t % TILES_PER_BATCH - 1, 0), 0)),
            pl.BlockSpec((1, CTX_LEN, D_MODEL), lambda t: (t // TILES_PER_BATCH, 0, 0))]


def _load_stream(refs):
    if len(refs) == 1:
        return refs[0][...]
    is_ctx = pl.program_id(0) % TILES_PER_BATCH == 0
    return jnp.where(is_ctx, refs[1][0], refs[0][0])


def _full_spec(shape):
    nd = len(shape)
    return pl.BlockSpec(shape, lambda t: (0,) * nd)


def _mod_spec():
    return pl.BlockSpec((1, 6, D_MODEL), lambda t: (_mod_row(t), 0, 0))


def _halo_specs(width, halo):
    per_tile = ROW_TILE // halo
    last = N_ROWS // halo - 1
    prev = pl.BlockSpec((halo, width), lambda t: (jnp.maximum(t * per_tile - 1, 0), 0))
    nxt = pl.BlockSpec((halo, width), lambda t: (jnp.minimum((t + 1) * per_tile, last), 0))
    return prev, nxt


def _seq_flags():
    w = pl.program_id(0) % TILES_PER_BATCH
    is_start = (w == 0) | (w == 1)
    is_end = (w == 0) | (w == TILES_PER_BATCH - 1)
    seq_tile = jnp.maximum(w - 1, 0)
    return is_start, is_end, seq_tile, w == 0


_POST_OUT_SHAPES = (jax.ShapeDtypeStruct((N_ROWS, D_MODEL), F32),
                    jax.ShapeDtypeStruct((N_ROWS, HX_W), F32),
                    jax.ShapeDtypeStruct((N_TILES, SUBLANES, ROW_TILE), F32),
                    jax.ShapeDtypeStruct((1, ROUTE_W), F32))


def _post_out_specs():
    return (_tile_spec(D_MODEL),
            _tile_spec(HX_W),
            pl.BlockSpec((1, SUBLANES, ROW_TILE), lambda t: (t, 0, 0)),
            pl.BlockSpec((1, ROUTE_W), lambda t: (0, 0)))


def _class_count_scratch():
    return pltpu.VMEM((1, ROUTE_W), F32)


def _even_pre_kernel(n_stream, *refs):
    mod_ref, g_ref, w_ref, a_ref, gb_ref, gch_ref = refs[n_stream:]
    m = mod_ref[0]
    h = _norm_mod(_load_stream(refs[:n_stream]), g_ref[...], m[0:1], m[1:2])
    u = _dot(h.astype(BF16), w_ref[...])
    ca, cb = CONV_A_DIM, CONV_B_DIM
    a_ref[...] = u[:, 0:ca] * _sigmoid(u[:, ca:2 * ca])
    gb_ref[...] = u[:, 2 * ca:2 * ca + cb]
    gch_ref[...] = u[:, 2 * ca + cb:2 * ca + 2 * cb] * u[:, 2 * ca + 2 * cb:2 * ca + 3 * cb]


def _even_pre(stream, mod_i, g0, w_in):
    n_in = w_in.shape[1]
    return pl.pallas_call(
        functools.partial(_even_pre_kernel, len(stream)),
        grid=(N_TILES,),
        in_specs=_stream_specs(len(stream)) + [_mod_spec(), _full_spec((1, D_MODEL)),
                                               _full_spec((D_MODEL, n_in))],
        out_specs=(_tile_spec(CONV_A_DIM), _tile_spec(CONV_B_DIM), _tile_spec(CONV_B_DIM)),
        out_shape=(jax.ShapeDtypeStruct((N_ROWS, CONV_A_DIM), F32),
                   jax.ShapeDtypeStruct((N_ROWS, CONV_B_DIM), F32),
                   jax.ShapeDtypeStruct((N_ROWS, CONV_B_DIM), F32)),
        compiler_params=_cparams(("parallel",)),
        name="even_pre",
    )(*stream, mod_i, g0, w_in)


def _even_post_kernel(n_stream, *refs):
    (a_ref, ap_ref, an_ref, gch_ref, gp_ref, gn_ref, gb_ref,
     cwa_ref, cba_ref, lng_ref, lnb_ref, cwb_ref, wout_ref,
     mod_ref, g1_ref, wr_ref, br_ref,
     xo_ref, hx_ref, meta_ref, counts_ref, exta_ref, extb_ref, cnt_ref) = refs[n_stream:]
    is_start, is_end, _, _ = _seq_flags()
    ha, hb = CONV_A_HALO, SMALL_HALO
    exta_ref[0:ha] = jnp.where(is_start, 0.0, ap_ref[...])
    exta_ref[ha:ha + ROW_TILE] = a_ref[...]
    exta_ref[ha + ROW_TILE:2 * ha + ROW_TILE] = jnp.where(is_end, 0.0, an_ref[...])
    extb_ref[0:hb] = jnp.where(is_start, 0.0, gp_ref[...])
    extb_ref[hb:hb + ROW_TILE] = gch_ref[...]
    extb_ref[hb + ROW_TILE:2 * hb + ROW_TILE] = jnp.where(is_end, 0.0, gn_ref[...])

    half_a = (CONV_A_WIDTH - 1) // 2
    lead = ha - half_a
    span = ROW_TILE + SUBLANES
    assert 0 <= lead and lead + SUBLANES - 1 <= SUBLANES
    parts = []
    for cblk in range(CONV_A_DIM // LANES):
        cs = slice(cblk * LANES, (cblk + 1) * LANES)
        acc = jnp.zeros((ROW_TILE, LANES), F32) + cba_ref[:, cs]
        for b in range(SUBLANES):
            z = jnp.zeros((span, LANES), F32)
            for k in range(b, CONV_A_WIDTH, SUBLANES):
                z = z + cwa_ref[k:k + 1, cs] * exta_ref[k - b:k - b + span, cs]
            acc = acc + z[lead + b:lead + b + ROW_TILE, :]
        parts.append(acc)
    acc = jnp.concatenate(parts, axis=1)
    mu = jnp.mean(acc, axis=-1, keepdims=True)
    xc = acc - mu
    a = xc * lax.rsqrt(jnp.mean(xc * xc, axis=-1, keepdims=True) + EPS)
    a = _silu(a * lng_ref[...] + lnb_ref[...])

    half_b = (CONV_B_WIDTH - 1) // 2
    cb = jnp.zeros((ROW_TILE, CONV_B_DIM), F32)
    for k in range(CONV_B_WIDTH):
        off = hb - half_b + k
        cb = cb + cwb_ref[k:k + 1, :] * extb_ref[off:off + ROW_TILE, :]
    b = gb_ref[...] * cb

    y = (_dot(a.astype(BF16), wout_ref[0:CONV_A_DIM, :])
         + _dot(b.astype(BF16), wout_ref[CONV_A_DIM:CONV_A_DIM + CONV_B_DIM, :]))
    _residual_and_route(_load_stream(refs[:n_stream]), y, mod_ref[0], g1_ref, wr_ref, br_ref,
                        xo_ref, hx_ref, meta_ref, counts_ref, cnt_ref)


def _even_post(stream, a_pre, gb, gch, cwa, cba, lng, lnb, cwb, w_out, mod_i, g1, wr, br):
    ap_spec, an_spec = _halo_specs(CONV_A_DIM, CONV_A_HALO)
    gp_spec, gn_spec = _halo_specs(CONV_B_DIM, SMALL_HALO)
    return pl.pallas_call(
        functools.partial(_even_post_kernel, len(stream)),
        grid=(N_TILES,),
        in_specs=_stream_specs(len(stream)) + [
                  _tile_spec(CONV_A_DIM), ap_spec, an_spec,
                  _tile_spec(CONV_B_DIM), gp_spec, gn_spec, _tile_spec(CONV_B_DIM),
                  _full_spec((CONV_A_WIDTH, CONV_A_DIM)), _full_spec((1, CONV_A_DIM)),
                  _full_spec((1, CONV_A_DIM)), _full_spec((1, CONV_A_DIM)),
                  _full_spec((CONV_B_WIDTH, CONV_B_DIM)), _full_spec((D_MODEL, D_MODEL)),
                  _mod_spec(), _full_spec((1, D_MODEL)),
                  _full_spec((D_MODEL, 2 * ROUTE_W)),
                  _full_spec((1, ROUTE_W))],
        out_specs=_post_out_specs(),
        out_shape=_POST_OUT_SHAPES,
        scratch_shapes=[pltpu.VMEM((ROW_TILE + 2 * CONV_A_HALO, CONV_A_DIM), F32),
                        pltpu.VMEM((ROW_TILE + 2 * SMALL_HALO, CONV_B_DIM), F32),
                        _class_count_scratch()],
        compiler_params=_cparams(("arbitrary",)),
        name="even_post",
    )(*stream, a_pre, a_pre, a_pre, gch, gch, gch, gb, cwa, cba, lng, lnb, cwb, w_out,
      mod_i, g1, wr, br)


def _odd_pre_kernel(x_ref, mod_ref, g_ref, w_ref, gq_ref, wqm_ref, wqs_ref, gkv_ref, wk_ref, wv_ref,
                    vaug_ref, cos_ref, sin_ref, qt_ref, k_ref, vt_ref, pool_ref):
    m = mod_ref[0]
    h = _norm_mod(x_ref[...], g_ref[...], m[0:1], m[1:2])
    u = _dot(h.astype(BF16), w_ref[...])
    o_kv = Q_LORA
    o_kr = o_kv + KV_LORA
    o_krs = o_kr + HEAD_PAD
    o_pool = o_krs + HEAD_PAD
    cos = cos_ref[...]
    sin = sin_ref[...]

    q_c = u[:, 0:o_kv]
    qn = (q_c * lax.rsqrt(jnp.mean(q_c * q_c, axis=-1, keepdims=True) + EPS) * gq_ref[...]).astype(BF16)
    q_main = _dot(qn, wqm_ref[...])
    q_swap = _dot(qn, wqs_ref[...])
    kv_c = u[:, o_kv:o_kr]
    kvn = (kv_c * lax.rsqrt(jnp.mean(kv_c * kv_c, axis=-1, keepdims=True) + EPS) * gkv_ref[...]).astype(BF16)
    k_nope = _dot(kvn, wk_ref[...])
    v = _dot(kvn, wv_ref[...]) + vaug_ref[...]
    for cc in range(ROW_TILE // ATT_CHUNK):
        vt_ref[cc] = v[cc * ATT_CHUNK:(cc + 1) * ATT_CHUNK, :].T.astype(BF16)
    k_rope = u[:, o_kr:o_krs] * cos + u[:, o_krs:o_pool] * sin
    for hd in range(MLA_HEADS):
        sl = slice(hd * HEAD_PAD, (hd + 1) * HEAD_PAD)
        qh = (q_main[:, sl] * cos + q_swap[:, sl] * sin) * (MLA_SCALE * LOG2E)
        qt_ref[0, sl, :] = qh.T.astype(BF16)
        k_ref[:, sl] = (k_nope[:, sl] + k_rope).astype(BF16)
    pool_ref[...] = u[:, o_pool:o_pool + POOL_DIM]


def _odd_pre(x, mod_i, g0, w_in, gq, wqm, wqs, gkv, wk, wv, cos_t, sin_t):
    n_in = w_in.shape[1]
    vaug = jnp.zeros((MLA_HEADS, V_AUG), F32).at[:, MLA_V].set(1.0).reshape(1, VT_W)
    return pl.pallas_call(
        _odd_pre_kernel,
        grid=(N_TILES,),
        in_specs=[_tile_spec(D_MODEL), _mod_spec(), _full_spec((1, D_MODEL)),
                  _full_spec((D_MODEL, n_in)), _full_spec((1, Q_LORA)),
                  _full_spec((Q_LORA, QK_W)), _full_spec((Q_LORA, QK_W)),
                  _full_spec((1, KV_LORA)), _full_spec((KV_LORA, QK_W)), _full_spec((KV_LORA, VT_W)),
                  _full_spec((1, VT_W)),
                  pl.BlockSpec((ROW_TILE, HEAD_PAD), lambda t: (t % TILES_PER_BATCH, 0)),
                  pl.BlockSpec((ROW_TILE, HEAD_PAD), lambda t: (t % TILES_PER_BATCH, 0))],
        out_specs=(pl.BlockSpec((1, QK_W, ROW_TILE), lambda t: (t, 0, 0)),
                   _tile_spec(QK_W),
                   pl.BlockSpec((ROW_TILE // ATT_CHUNK, VT_W, ATT_CHUNK), lambda t: (t, 0, 0)),
                   _tile_spec(POOL_DIM)),
        out_shape=(jax.ShapeDtypeStruct((N_TILES, QK_W, ROW_TILE), BF16),
                   jax.ShapeDtypeStruct((N_ROWS, QK_W), BF16),
                   jax.ShapeDtypeStruct((N_ROWS // ATT_CHUNK, VT_W, ATT_CHUNK), BF16),
                   jax.ShapeDtypeStruct((N_ROWS, POOL_DIM), F32)),
        compiler_params=_cparams(("parallel",)),
        name="odd_pre",
    )(x, mod_i, g0, w_in, gq, wqm, wqs, gkv, wk, wv, vaug, cos_t, sin_t)


def _attn_kernel(qt_ref, k_ref, vt_ref, o_ref, m_ref, acc_ref, s_ref, p_ref, a_ref):
    m_ref[...] = jnp.full(m_ref.shape, -jnp.inf, F32)
    acc_ref[...] = jnp.zeros(acc_ref.shape, F32)

    def scores_head(c, slot, hd):
        r0 = pl.multiple_of(c * ATT_CHUNK, ATT_CHUNK)
        sl = slice(hd * HEAD_PAD, (hd + 1) * HEAD_PAD)
        s_ref[slot, hd] = _dot(k_ref[pl.ds(r0, ATT_CHUNK), sl], qt_ref[0, sl, :])

    def softmax_head(slot, hd):
        hs = slice(hd, hd + 1)
        m_old = m_ref[hs, :]
        m_new = jnp.maximum(m_old, jnp.max(s_ref[slot, hd], axis=0, keepdims=True))
        m_ref[hs, :] = m_new
        a_ref[slot, hs, :] = jnp.exp2(m_old - m_new)
        for r0 in range(0, ATT_CHUNK, SOFTMAX_ROWS):
            rows = slice(r0, r0 + SOFTMAX_ROWS)
            p_ref[slot, hd, rows, :] = jnp.exp2(s_ref[slot, hd, rows, :] - m_new).astype(BF16)

    def pv_head(c, slot, hd):
        vs = slice(hd * V_AUG, (hd + 1) * V_AUG)
        pv = _dot(vt_ref[c, vs, :], p_ref[slot, hd])
        acc_ref[vs, :] = a_ref[slot, hd:hd + 1, :] * acc_ref[vs, :] + pv

    def stage_scores(c, slot):
        for hd in range(MLA_HEADS):
            scores_head(c, slot, hd)

    def stage_softmax(slot):
        for hd in range(MLA_HEADS):
            softmax_head(slot, hd)

    def stage_pv(c, slot):
        for hd in range(MLA_HEADS):
            pv_head(c, slot, hd)

    def steady(c, slot):
        stage_scores(c, slot)
        stage_pv(c - 2, slot)
        stage_softmax(1 - slot)

    is_ctx = pl.program_id(1) == 0

    @pl.when(is_ctx)
    def _():
        stage_scores(0, 0)
        stage_softmax(0)
        stage_pv(0, 0)

    @pl.when(jnp.logical_not(is_ctx))
    def _():
        n = ROWS_PER_BATCH // ATT_CHUNK
        stage_scores(0, 0)
        stage_scores(1, 1)
        stage_softmax(0)
        first = 2 + (n - 2) % ATT_TRIP
        for c in range(2, first):
            steady(c, c % 2)

        def trip(i, carry):
            c = first + ATT_TRIP * i
            for u in range(ATT_TRIP):
                steady(c + u, (first + u) % 2)
            return carry

        lax.fori_loop(0, (n - first) // ATT_TRIP, trip, 0)
        stage_pv(n - 2, (n - 2) % 2)
        stage_softmax((n - 1) % 2)
        stage_pv(n - 1, (n - 1) % 2)

    outs = []
    for hd in range(MLA_HEADS):
        blk = acc_ref[hd * V_AUG:(hd + 1) * V_AUG, :]
        outs.append(blk[0:MLA_V, :] / blk[MLA_V:MLA_V + 1, :])
    o_ref[...] = jnp.concatenate(outs, axis=0).T


def _attention(qt, k, vt):
    cpb = ROWS_PER_BATCH // ATT_CHUNK
    assert cpb >= 3 and CTX_LEN == ATT_CHUNK
    return pl.pallas_call(
        _attn_kernel,
        grid=(BATCH, TILES_PER_BATCH),
        in_specs=[pl.BlockSpec((1, QK_W, ROW_TILE), lambda b, i: (b * TILES_PER_BATCH + i, 0, 0)),
                  pl.BlockSpec((ROWS_PER_BATCH, QK_W), lambda b, i: (b, 0), pipeline_mode=pl.Buffered(1)),
                  pl.BlockSpec((cpb, VT_W, ATT_CHUNK), lambda b, i: (b, 0, 0), pipeline_mode=pl.Buffered(1))],
        out_specs=pl.BlockSpec((ROW_TILE, ATT_W), lambda b, i: (b * TILES_PER_BATCH + i, 0)),
        out_shape=jax.ShapeDtypeStruct((N_ROWS, ATT_W), F32),
        scratch_shapes=[pltpu.VMEM((MLA_HEADS, ROW_TILE), F32),
                        pltpu.VMEM((VT_W, ROW_TILE), F32),
                        pltpu.VMEM((2, MLA_HEADS, ATT_CHUNK, ROW_TILE), F32),
                        pltpu.VMEM((2, MLA_HEADS, ATT_CHUNK, ROW_TILE), BF16),
                        pltpu.VMEM((2, MLA_HEADS, ROW_TILE), F32)],
        compiler_params=_cparams(("parallel", "parallel")),
        name="attention",
    )(qt, k, vt)


def _odd_post_kernel(att_ref, p_ref, pp_ref, pn_ref, wp_ref, bp_ref, sp_ref, wout_ref,
                     x_ref, mod_ref, g1_ref, wr_ref, br_ref,
                     xo_ref, hx_ref, meta_ref, counts_ref, ext_ref, cnt_ref):
    is_start, is_end, seq_tile, is_ctx = _seq_flags()
    hp = SMALL_HALO
    ext_ref[0:hp] = jnp.where(is_start, 0.0, pp_ref[...])
    ext_ref[hp:hp + ROW_TILE] = p_ref[...]
    ext_ref[hp + ROW_TILE:2 * hp + ROW_TILE] = jnp.where(is_end, 0.0, pn_ref[...])
    seq_len = jnp.where(is_ctx, CTX_LEN, SEQ)
    pos = seq_tile * ROW_TILE + lax.broadcasted_iota(jnp.int32, (ROW_TILE, 1), 0)

    pooled = []
    for g, w in enumerate(POOL_WINDOWS):
        sl = slice(g * POOL_GROUP, (g + 1) * POOL_GROUP)
        ssum = jnp.zeros((ROW_TILE, POOL_GROUP), F32)
        for d in range(-(w // 2), w - w // 2):
            ssum = ssum + ext_ref[hp + d:hp + d + ROW_TILE, sl]
        lo = jnp.maximum(pos - w // 2, 0)
        hi = jnp.minimum(pos - w // 2 + w, seq_len)
        cnt = (hi - lo).astype(F32)
        pm = ssum / cnt - p_ref[:, sl]
        pooled.append(((_dot(pm.astype(BF16), wp_ref[g]) + bp_ref[g:g + 1, :]) * sp_ref[:, sl]).astype(BF16))
    mixed = jnp.concatenate([att_ref[...].astype(BF16)] + pooled, axis=1)
    y = _dot(mixed, wout_ref[...])
    _residual_and_route(x_ref[...], y, mod_ref[0], g1_ref, wr_ref, br_ref,
                        xo_ref, hx_ref, meta_ref, counts_ref, cnt_ref)


def _odd_post(att, pool_in, w_pool, b_pool, s_pool, w_out, x, mod_i, g1, wr, br):
    pp_spec, pn_spec = _halo_specs(POOL_DIM, SMALL_HALO)
    ng = len(POOL_WINDOWS)
    return pl.pallas_call(
        _odd_post_kernel,
        grid=(N_TILES,),
        in_specs=[_tile_spec(ATT_W), _tile_spec(POOL_DIM), pp_spec, pn_spec,
                  _full_spec((ng, POOL_GROUP, POOL_GROUP)), _full_spec((ng, POOL_GROUP)),
                  _full_spec((1, POOL_DIM)), _full_spec((D_MODEL, D_MODEL)),
                  _tile_spec(D_MODEL), _mod_spec(), _full_spec((1, D_MODEL)),
                  _full_spec((D_MODEL, 2 * ROUTE_W)),
                  _full_spec((1, ROUTE_W))],
        out_specs=_post_out_specs(),
        out_shape=_POST_OUT_SHAPES,
        scratch_shapes=[pltpu.VMEM((ROW_TILE + 2 * SMALL_HALO, POOL_DIM), F32), _class_count_scratch()],
        compiler_params=_cparams(("arbitrary",)),
        name="odd_post",
    )(att, pool_in, pool_in, pool_in, w_pool, b_pool, s_pool, w_out, x, mod_i, g1, wr, br)


def _moe_plan(meta, counts):
    cls = meta[:, CLS_LANE - META_ROW0, :].reshape(N_ROWS).astype(jnp.int32)
    rank = meta[:, RANK_LANE - META_ROW0, :].reshape(N_ROWS).astype(jnp.int32)
    tiles = (counts[0, :N_CLASSES].astype(jnp.int32) + MOE_TM - 1) // MOE_TM
    tile_end = jnp.cumsum(tiles)
    first_row = (tile_end - tiles) * MOE_TM
    dest = rank
    for c in range(N_CLASSES):
        dest = dest + jnp.where(cls == c, first_row[c], 0)
    j = jnp.arange(MOE_TILES)
    used = tile_end[-1]
    j_eff = jnp.minimum(j, jnp.maximum(used - 1, 0))
    cls_of_tile = jnp.sum((j_eff[:, None] >= tile_end[None, :]).astype(jnp.int32), axis=1)
    cls_of_tile = jnp.minimum(cls_of_tile, N_CLASSES - 1)
    grp = cls_of_tile // PAIRS_PER_GROUP
    pair = cls_of_tile % PAIRS_PER_GROUP
    e_a = grp * EXPERTS_PER_GROUP + jnp.array(PAIR_A, jnp.int32)[pair]
    e_b = grp * EXPERTS_PER_GROUP + jnp.array(PAIR_B, jnp.int32)[pair]
    status = jnp.where(j < used, 1 + jnp.array(PAIR_A_IS_HIGHER, jnp.int32)[pair], 0)
    new_a = jnp.concatenate([jnp.ones((1,), jnp.int32), (e_a[1:] != e_a[:-1]).astype(jnp.int32)])
    new_b = jnp.concatenate([jnp.ones((1,), jnp.int32), (e_b[1:] != e_b[:-1]).astype(jnp.int32)])
    return dest, e_a, e_b, status, new_a + 2 * new_b


def _move_rows(n_rows, base, dest_ref, row_copy, block_copy):
    def issue(i, carry):
        for u in range(DMA_UNROLL):
            r = i * DMA_UNROLL + u
            row_copy(r, dest_ref[base + r]).start(priority=u % 2)
        return carry

    lax.fori_loop(0, n_rows // DMA_UNROLL, issue, 0)
    block_copy.wait()


def _dispatch_kernel(dest_ref, hx_ref, init_ref, xs_ref, sem):
    del init_ref
    _move_rows(DISPATCH_ROWS, pl.program_id(0) * DISPATCH_ROWS, dest_ref,
               lambda r, d: pltpu.make_async_copy(hx_ref.at[pl.ds(r, 1)], xs_ref.at[pl.ds(d, 1)], sem),
               pltpu.make_async_copy(hx_ref, xs_ref.at[pl.ds(0, DISPATCH_ROWS)], sem))


def _dispatch(dest, hx, init):
    shape = (MOE_ROWS, HX_W)
    return pl.pallas_call(
        _dispatch_kernel,
        grid_spec=pltpu.PrefetchScalarGridSpec(
            num_scalar_prefetch=1,
            grid=(N_ROWS // DISPATCH_ROWS,),
            in_specs=[pl.BlockSpec((DISPATCH_ROWS, HX_W), lambda t, dest: (t, 0)),
                      pl.BlockSpec(memory_space=pl.ANY)],
            out_specs=pl.BlockSpec(memory_space=pl.ANY),
            scratch_shapes=[pltpu.SemaphoreType.DMA(())]),
        out_shape=jax.ShapeDtypeStruct(shape, F32),
        input_output_aliases={2: 0},
        compiler_params=_cparams(("arbitrary",)),
        name="moe_dispatch",
    )(dest, hx, init)


def _moe_expert_kernel(ea_ref, eb_ref, status_ref, fresh_ref, xs_ref, w1a_ref, w3a_ref, w2a_ref,
                       w1b_ref, w3b_ref, w2b_ref, ys_ref, *bf_refs):
    del ea_ref, eb_ref
    status = status_ref[pl.program_id(0)]
    fresh = fresh_ref[pl.program_id(0)]
    slot_a, slot_b = bf_refs[0:3], bf_refs[3:6]

    @pl.when(fresh % 2 == 1)
    def _():
        for dst, src in zip(slot_a, (w1a_ref, w3a_ref, w2a_ref)):
            dst[...] = src[0, 0].astype(BF16)

    @pl.when(fresh >= 2)
    def _():
        for dst, src in zip(slot_b, (w1b_ref, w3b_ref, w2b_ref)):
            dst[...] = src[0, 0].astype(BF16)

    @pl.when(status == 0)
    def _():
        ys_ref[...] = jnp.zeros(ys_ref.shape, F32)

    @pl.when(status != 0)
    def _():
        h = xs_ref[:, 0:D_MODEL].astype(BF16)
        g_lo = xs_ref[:, D_MODEL + GLO_LANE:D_MODEL + GLO_LANE + 1]
        g_hi = xs_ref[:, D_MODEL + GHI_LANE:D_MODEL + GHI_LANE + 1]
        a_is_higher = status == 2
        y = jnp.zeros((MOE_TM, D_MODEL), F32)
        for g, (w1_ref, w3_ref, w2_ref) in ((jnp.where(a_is_higher, g_hi, g_lo), slot_a),
                                            (jnp.where(a_is_higher, g_lo, g_hi), slot_b)):
            hid = _silu(_dot(h, w1_ref[...])) * _dot(h, w3_ref[...])
            y = y + g * _dot(hid.astype(BF16), w2_ref[...])
        ys_ref[...] = y


def _moe_experts(e_a, e_b, status, fresh, xs_sorted, w1, w3, w2, layer):
    up_a = pl.BlockSpec((1, 1, D_MODEL, D_EXPERT), lambda j, ea, eb, status, fresh: (layer, ea[j], 0, 0))
    dn_a = pl.BlockSpec((1, 1, D_EXPERT, D_MODEL), lambda j, ea, eb, status, fresh: (layer, ea[j], 0, 0))
    up_b = pl.BlockSpec((1, 1, D_MODEL, D_EXPERT), lambda j, ea, eb, status, fresh: (layer, eb[j], 0, 0))
    dn_b = pl.BlockSpec((1, 1, D_EXPERT, D_MODEL), lambda j, ea, eb, status, fresh: (layer, eb[j], 0, 0))
    up_bf = pltpu.VMEM((D_MODEL, D_EXPERT), BF16)
    dn_bf = pltpu.VMEM((D_EXPERT, D_MODEL), BF16)
    return pl.pallas_call(
        _moe_expert_kernel,
        grid_spec=pltpu.PrefetchScalarGridSpec(
            num_scalar_prefetch=4,
            grid=(MOE_TILES,),
            in_specs=[pl.BlockSpec((MOE_TM, HX_W), lambda j, ea, eb, status, fresh: (j, 0)),
                      up_a, up_a, dn_a, up_b, up_b, dn_b],
            out_specs=pl.BlockSpec((MOE_TM, D_MODEL), lambda j, ea, eb, status, fresh: (j, 0)),
            scratch_shapes=[up_bf, up_bf, dn_bf, up_bf, up_bf, dn_bf]),
        out_shape=jax.ShapeDtypeStruct((MOE_ROWS, D_MODEL), F32),
        compiler_params=_cparams(("arbitrary",)),
        name="moe_experts",
    )(e_a, e_b, status, fresh, xs_sorted, w1, w3, w2, w1, w3, w2)


def _combine_kernel(dest_ref, x_ref, mod_ref, ys_ref, o_ref, buf_ref, sem):
    t = pl.program_id(0)
    _move_rows(COMBINE_ROWS, t * COMBINE_ROWS, dest_ref,
               lambda r, d: pltpu.make_async_copy(ys_ref.at[pl.ds(d, 1)], buf_ref.at[pl.ds(r, 1)], sem),
               pltpu.make_async_copy(ys_ref.at[pl.ds(0, COMBINE_ROWS)], buf_ref, sem))
    sub = COMBINE_ROWS // ROW_TILE
    for b in range(sub):
        rs = slice(b * ROW_TILE, (b + 1) * ROW_TILE)
        m5 = mod_ref[pl.ds(_mod_row(t * sub + b), 1)][0, 5:6]
        o_ref[rs, :] = x_ref[rs, :] + m5 * buf_ref[rs, :]


def _combine(dest, x, mod_i, ys):
    return pl.pallas_call(
        _combine_kernel,
        grid_spec=pltpu.PrefetchScalarGridSpec(
            num_scalar_prefetch=1,
            grid=(N_ROWS // COMBINE_ROWS,),
            in_specs=[pl.BlockSpec((COMBINE_ROWS, D_MODEL), lambda t, dest: (t, 0)),
                      pl.BlockSpec((MOD_ROWS, 6, D_MODEL), lambda t, dest: (0, 0, 0)),
                      pl.BlockSpec(memory_space=pl.ANY)],
            out_specs=pl.BlockSpec((COMBINE_ROWS, D_MODEL), lambda t, dest: (t, 0)),
            scratch_shapes=[pltpu.VMEM((COMBINE_ROWS, D_MODEL), F32), pltpu.SemaphoreType.DMA(())]),
        out_shape=jax.ShapeDtypeStruct((N_ROWS, D_MODEL), F32),
        compiler_params=_cparams(("arbitrary",)),
        name="moe_combine",
    )(dest, x, mod_i, ys)


def _combine_final_kernel(dest_ref, x_ref, mod_ref, g_ref, ys_ref, o_ref, buf_ref, sem):
    _move_rows(ROW_TILE, pl.program_id(0) * ROW_TILE, dest_ref,
               lambda r, d: pltpu.make_async_copy(ys_ref.at[pl.ds(d, 1)], buf_ref.at[pl.ds(r, 1)], sem),
               pltpu.make_async_copy(ys_ref.at[pl.ds(0, ROW_TILE)], buf_ref, sem))
    x = x_ref[...] + mod_ref[0][5:6] * buf_ref[...]
    o_ref[0] = x * lax.rsqrt(jnp.mean(x * x, axis=-1, keepdims=True) + EPS) * g_ref[...]


def _combine_final(dest, x, mod_i, ys, g):
    def out_map(t, dest):
        return (t // TILES_PER_BATCH, jnp.maximum(t % TILES_PER_BATCH - 1, 0), 0)

    return pl.pallas_call(
        _combine_final_kernel,
        grid_spec=pltpu.PrefetchScalarGridSpec(
            num_scalar_prefetch=1,
            grid=(N_TILES,),
            in_specs=[pl.BlockSpec((ROW_TILE, D_MODEL), lambda t, dest: (t, 0)),
                      pl.BlockSpec((1, 6, D_MODEL), lambda t, dest: (_mod_row(t), 0, 0)),
                      pl.BlockSpec((1, D_MODEL), lambda t, dest: (0, 0)),
                      pl.BlockSpec(memory_space=pl.ANY)],
            out_specs=pl.BlockSpec((1, ROW_TILE, D_MODEL), out_map),
            scratch_shapes=[pltpu.VMEM((ROW_TILE, D_MODEL), F32), pltpu.SemaphoreType.DMA(())]),
        out_shape=jax.ShapeDtypeStruct((BATCH, SEQ, D_MODEL), F32),
        compiler_params=_cparams(("arbitrary",)),
        name="moe_combine_final",
    )(dest, x, mod_i, g, ys)


_SWAP32 = tuple(list(range(8, 16)) + list(range(0, 8)) + list(range(24, 32)) + list(range(16, 24)))


def _rope_tables():
    f32 = np.float32
    rows = SEQ // GRID_W
    pos_row = np.repeat(np.arange(rows, dtype=f32), GRID_W)
    pos_col = np.tile(np.arange(GRID_W, dtype=f32), rows)
    inv = (f32(ROPE_BASE) ** (-np.arange(0, AXIS_DIM, 2, dtype=f32) / f32(AXIS_DIM))).astype(f32)
    ang = np.concatenate([pos_row[:, None] * inv, pos_col[:, None] * inv], axis=-1).astype(f32)
    cos, sin = np.cos(ang).astype(f32), np.sin(ang).astype(f32)
    p = AXIS_PAIRS
    c32 = np.concatenate([cos[:, :p], cos[:, :p], cos[:, p:], cos[:, p:]], axis=-1)
    s32 = np.concatenate([-sin[:, :p], sin[:, :p], -sin[:, p:], sin[:, p:]], axis=-1)
    pad = HEAD_PAD - MLA_NOPE - MLA_ROPE
    cos_lat = np.concatenate([np.ones((SEQ, MLA_NOPE), f32), c32, np.zeros((SEQ, pad), f32)], axis=-1)
    sin_lat = np.concatenate([np.zeros((SEQ, MLA_NOPE), f32), s32, np.zeros((SEQ, pad), f32)], axis=-1)
    cos_ctx = np.concatenate([np.ones((CTX_LEN, MLA_NOPE + MLA_ROPE), f32), np.zeros((CTX_LEN, pad), f32)], axis=-1)
    sin_ctx = np.zeros((CTX_LEN, HEAD_PAD), f32)
    return (jnp.asarray(np.concatenate([cos_ctx, cos_lat], axis=0)),
            jnp.asarray(np.concatenate([sin_ctx, sin_lat], axis=0)))


def _odd_weights(w_in, w_uq, w_ukv):
    swap = jnp.array(_SWAP32)
    o_kv, o_kr, o_pool = Q_LORA, Q_LORA + KV_LORA, Q_LORA + KV_LORA + MLA_ROPE
    w_kr = w_in[:, o_kr:o_pool]
    pad_l = jnp.zeros((D_MODEL, MLA_NOPE), F32)
    pad_r = jnp.zeros((D_MODEL, HEAD_PAD - MLA_NOPE - MLA_ROPE), F32)
    w_in2 = jnp.concatenate([w_in[:, :o_kr], pad_l, w_kr, pad_r, pad_l, w_kr[:, swap], pad_r,
                             w_in[:, o_pool:]], axis=-1).astype(BF16)
    wq = w_uq.reshape(Q_LORA, MLA_HEADS, MLA_NOPE + MLA_ROPE)
    zq = jnp.zeros((Q_LORA, MLA_HEADS, HEAD_PAD - MLA_NOPE - MLA_ROPE), F32)
    wq_main = jnp.concatenate([wq, zq], axis=-1).reshape(Q_LORA, QK_W).astype(BF16)
    wq_swap = jnp.concatenate([jnp.zeros((Q_LORA, MLA_HEADS, MLA_NOPE), F32),
                               wq[:, :, MLA_NOPE:][:, :, swap], zq], axis=-1).reshape(Q_LORA, QK_W).astype(BF16)
    wkv = w_ukv.reshape(KV_LORA, MLA_HEADS, MLA_NOPE + MLA_V)
    zk = jnp.zeros((KV_LORA, MLA_HEADS, HEAD_PAD - MLA_NOPE), F32)
    wk = jnp.concatenate([wkv[:, :, :MLA_NOPE], zk], axis=-1).reshape(KV_LORA, QK_W).astype(BF16)
    zv = jnp.zeros((KV_LORA, MLA_HEADS, V_AUG - MLA_V), F32)
    wv = jnp.concatenate([wkv[:, :, MLA_NOPE:], zv], axis=-1).reshape(KV_LORA, VT_W).astype(BF16)
    return w_in2, wq_main, wq_swap, wk, wv


def kernel(x, c, ctx, c_ctx, w_mod, b_mod, norm_g, ev_w_in, ev_conv_a_w, ev_conv_a_b, ev_ln_a_g, ev_ln_a_b,
           ev_conv_b_w, ev_w_out, od_w_in, od_q_norm_g, od_w_uq, od_kv_norm_g, od_w_ukv, od_w_pool, od_b_pool,
           od_s_pool, od_w_out, moe_wg, moe_bg, moe_we, moe_be, moe_w1, moe_w3, moe_w2, final_g):
    assert x.shape == (BATCH, SEQ, D_MODEL) and ctx.shape == (BATCH, CTX_LEN, D_MODEL)
    assert CTX_LEN == ROW_TILE
    stream = (x, ctx)
    cvec = jnp.concatenate([c, c_ctx[None, :], jnp.zeros((MOD_ROWS - BATCH - 1, D_MODEL), F32)], axis=0)
    mod = _modulation(cvec, w_mod, b_mod).reshape(DEPTH, MOD_ROWS, 6, D_MODEL)
    cos_t, sin_t = _rope_tables()
    sorted_rows = jnp.zeros((MOE_ROWS, HX_W), F32)

    for i in range(DEPTH):
        j = i // 2
        mod_i = mod[i]
        g0 = norm_g[i, 0][None, :]
        g1 = norm_g[i, 1][None, :]
        wr = jnp.concatenate([moe_wg[i], moe_we[i],
                              jnp.zeros((D_MODEL, ROUTE_W - N_GROUPS - N_EXPERTS), F32)], axis=-1)
        wr_hi = wr.astype(BF16)
        wr = jnp.concatenate([wr_hi, (wr - wr_hi.astype(F32)).astype(BF16)], axis=-1)
        br = jnp.concatenate([moe_bg[i], moe_be[i],
                              jnp.zeros((ROUTE_W - N_GROUPS - N_EXPERTS,), F32)])[None, :]
        if i % 2 == 0:
            a_pre, gb, gch = _even_pre(stream, mod_i, g0, ev_w_in[j].astype(BF16))
            xs, hx, meta, counts = _even_post(stream, a_pre, gb, gch, ev_conv_a_w[j], ev_conv_a_b[j][None, :],
                                              ev_ln_a_g[j][None, :], ev_ln_a_b[j][None, :], ev_conv_b_w[j],
                                              ev_w_out[j].astype(BF16), mod_i, g1, wr, br)
        else:
            w_in2, wq_main, wq_swap, wk, wv = _odd_weights(od_w_in[j], od_w_uq[j], od_w_ukv[j])
            q, k, v, pool_in = _odd_pre(xs, mod_i, g0, w_in2, od_q_norm_g[j][None, :], wq_main, wq_swap,
                                        od_kv_norm_g[j][None, :], wk, wv, cos_t, sin_t)
            att = _attention(q, k, v)
            xs, hx, meta, counts = _odd_post(att, pool_in, od_w_pool[j].astype(BF16), od_b_pool[j],
                                      od_s_pool[j][None, :], od_w_out[j].astype(BF16), xs, mod_i, g1,
                                      wr, br)
        dest, e_a, e_b, status, fresh = _moe_plan(meta, counts)
        sorted_rows = _dispatch(dest, hx, sorted_rows)
        ys = _moe_experts(e_a, e_b, status, fresh, sorted_rows, moe_w1, moe_w3, moe_w2, i)
        if i == DEPTH - 1:
            return _combine_final(dest, xs, mod_i, ys, final_g[None, :])
        xs = _combine(dest, xs, mod_i, ys)
        stream = (xs,)
```

```python
import functools

import numpy as np
import jax
import jax.numpy as jnp
from jax import lax
from jax.experimental import pallas as pl
from jax.experimental.pallas import tpu as pltpu

F32 = jnp.float32
BF16 = jnp.bfloat16

D_MODEL = 1024
BATCH = 2
SEQ = 8192
DEPTH = 4
GRID_W = 64
CTX_LEN = 256
EPS = 1e-6
CONV_A_DIM = 512
CONV_A_WIDTH = 31
CONV_B_DIM = 512
CONV_B_WIDTH = 3
MLA_HEADS = 8
MLA_NOPE = 64
MLA_ROPE = 32
MLA_V = 64
Q_LORA = 384
KV_LORA = 256
MLA_SCALE = (MLA_NOPE + MLA_ROPE) ** -0.5
AXIS_DIM = MLA_ROPE // 2
AXIS_PAIRS = AXIS_DIM // 2
ROPE_BASE = 10000.0
POOL_WINDOWS = (2, 4, 8, 16)
POOL_GROUP = 128
POOL_DIM = POOL_GROUP * len(POOL_WINDOWS)
N_GROUPS = 4
EXPERTS_PER_GROUP = 4
N_EXPERTS = 16
D_EXPERT = 512

LANES = 128
SUBLANES = 8
VMEM_LIMIT = 48 * 1024 * 1024

ROW_TILE = 256
ROWS_PER_BATCH = CTX_LEN + SEQ
N_ROWS = BATCH * ROWS_PER_BATCH
TILES_PER_BATCH = ROWS_PER_BATCH // ROW_TILE
N_TILES = N_ROWS // ROW_TILE
HEAD_PAD = LANES
QK_W = MLA_HEADS * HEAD_PAD
ATT_W = MLA_HEADS * MLA_V
V_AUG = MLA_V + 16
VT_W = MLA_HEADS * V_AUG
CONV_A_HALO = 16
SMALL_HALO = 8
ATT_CHUNK = 256
SOFTMAX_ROWS = 64
ATT_TRIP = 2
ATT_QTILES = 2
ATT_STEPS = 1 + (TILES_PER_BATCH - 1) // ATT_QTILES
ATT_VMEM_LIMIT = 56 * 1024 * 1024
LOG2E = 1.4426950408889634
ROUTE_W = LANES
GATE_LANE0 = N_GROUPS
CLS_LANE = GATE_LANE0 + N_EXPERTS
RANK_LANE = CLS_LANE + 1
GLO_LANE = RANK_LANE + 1
GHI_LANE = GLO_LANE + 1
META_ROW0 = (CLS_LANE // SUBLANES) * SUBLANES
HX_W = D_MODEL + ROUTE_W
PAIRS_PER_GROUP = 6
N_CLASSES = N_GROUPS * PAIRS_PER_GROUP
PAIR_A = (0, 0, 0, 1, 1, 3)
PAIR_B = (1, 2, 3, 3, 2, 2)
PAIR_A_IS_HIGHER = (0, 0, 0, 0, 0, 1)
MOE_TM = 256
MOE_TILES = -(-(N_ROWS + N_CLASSES * (MOE_TM - 1)) // MOE_TM)
MOE_ROWS = MOE_TILES * MOE_TM
DISPATCH_ROWS = 2816
COMBINE_ROWS = 1536
DMA_UNROLL = 8
MOD_ROWS = 8


def _mod_row(t):
    return jnp.where(t % TILES_PER_BATCH == 0, BATCH, t // TILES_PER_BATCH)


def _cparams(sem):
    return pltpu.CompilerParams(dimension_semantics=sem, vmem_limit_bytes=VMEM_LIMIT)


def _dot(a, b):
    return jnp.dot(a, b, preferred_element_type=F32)


def _sigmoid(x):
    return 1.0 / (1.0 + jnp.exp(-x))


def _silu(x):
    return x * _sigmoid(x)


def _norm_mod(x, g, shift, scale):
    y = x * lax.rsqrt(jnp.mean(x * x, axis=-1, keepdims=True) + EPS)
    return (y * g) * (1.0 + scale) + shift


def _mod_kernel(c_ref, w_ref, b_ref, o_ref):
    s = _silu(c_ref[...])
    o_ref[0] = jnp.dot(s, w_ref[0], preferred_element_type=F32,
                       precision=lax.Precision.HIGHEST) + b_ref[0]


def _modulation(cvec, w_mod, b_mod):
    nblk = 4
    bw = 6 * D_MODEL // nblk
    return pl.pallas_call(
        _mod_kernel,
        grid=(DEPTH, nblk),
        in_specs=[pl.BlockSpec((MOD_ROWS, D_MODEL), lambda i, j: (0, 0)),
                  pl.BlockSpec((1, D_MODEL, bw), lambda i, j: (i, 0, j)),
                  pl.BlockSpec((1, 1, bw), lambda i, j: (i, 0, j))],
        out_specs=pl.BlockSpec((1, MOD_ROWS, bw), lambda i, j: (i, 0, j)),
        out_shape=jax.ShapeDtypeStruct((DEPTH, MOD_ROWS, 6 * D_MODEL), F32),
        compiler_params=_cparams(("parallel", "parallel")),
        name="modulation",
    )(cvec, w_mod, b_mod.reshape(DEPTH, 1, 6 * D_MODEL))


def _route(h2, wr_ref, br_ref):
    hi = h2.astype(BF16)
    lo = (h2 - hi.astype(F32)).astype(BF16)
    both = _dot(hi, wr_ref[...])
    logits = (both[:, 0:ROUTE_W] + both[:, ROUTE_W:2 * ROUTE_W] + _dot(lo, wr_ref[:, 0:ROUTE_W])
              + br_ref[...])
    lane = lax.broadcasted_iota(jnp.int32, logits.shape, 1)
    lane_f = lane.astype(F32)
    neg = -jnp.inf
    big = float(ROUTE_W)
    gl = jnp.where(lane < N_GROUPS, logits, neg)
    gmax = jnp.max(gl, axis=-1, keepdims=True)
    gidx = jnp.min(jnp.where(gl == gmax, lane_f, big), axis=-1, keepdims=True)
    g_w = 1.0 / jnp.sum(jnp.exp(gl - gmax), axis=-1, keepdims=True)
    egrp = ((lane - GATE_LANE0) // EXPERTS_PER_GROUP).astype(F32)
    in_g = (lane >= GATE_LANE0) & (lane < GATE_LANE0 + N_EXPERTS) & (egrp == gidx)
    el = jnp.where(in_g, logits, neg)
    v1 = jnp.max(el, axis=-1, keepdims=True)
    i1 = jnp.min(jnp.where(el == v1, lane_f, big), axis=-1, keepdims=True)
    el2 = jnp.where(lane_f == i1, neg, el)
    v2 = jnp.max(el2, axis=-1, keepdims=True)
    i2 = jnp.min(jnp.where(el2 == v2, lane_f, big), axis=-1, keepdims=True)
    e21 = jnp.exp(v2 - v1)
    w1 = 1.0 / (1.0 + e21)
    w2 = e21 * w1
    g1 = w1 * g_w
    g2 = w2 * g_w
    gates = jnp.where(lane_f == i1, g1, jnp.where(lane_f == i2, g2, 0.0))
    base = GATE_LANE0 + gidx * EXPERTS_PER_GROUP
    lo = jnp.minimum(i1, i2) - base
    hi = jnp.maximum(i1, i2) - base
    pair = jnp.where(lo == 0.0, hi - 1.0, jnp.where(lo == 1.0, 6.0 - hi, 5.0))
    cls = gidx * PAIRS_PER_GROUP + pair
    g_lo = jnp.where(i1 < i2, g1, g2)
    g_hi = jnp.where(i1 < i2, g2, g1)
    return jnp.where(lane_f == float(GLO_LANE), g_lo, jnp.where(lane_f == float(GHI_LANE), g_hi, gates)), cls


def _residual_and_route(x, y, m, g1_ref, wr_ref, br_ref,
                        xo_ref, hx_ref, meta_ref, counts_ref, cnt_ref):
    @pl.when(pl.program_id(0) == 0)
    def _():
        cnt_ref[...] = jnp.zeros(cnt_ref.shape, F32)

    x_new = x + m[2:3] * y
    xo_ref[...] = x_new
    h2 = _norm_mod(x_new, g1_ref[...], m[3:4], m[4:5])
    hx_ref[:, 0:D_MODEL] = h2
    route, cls = _route(h2, wr_ref, br_ref)
    lane_f = lax.broadcasted_iota(jnp.int32, route.shape, 1).astype(F32)
    onehot = jnp.where(lane_f == cls, 1.0, 0.0)
    row = lax.broadcasted_iota(jnp.int32, (ROW_TILE, ROW_TILE), 0)
    col = lax.broadcasted_iota(jnp.int32, (ROW_TILE, ROW_TILE), 1)
    earlier = jnp.where(col < row, 1.0, 0.0).astype(BF16)
    before = _dot(earlier, onehot.astype(BF16)) + cnt_ref[...]
    rank = jnp.sum(onehot * before, axis=-1, keepdims=True)
    cnt_ref[...] = cnt_ref[...] + jnp.sum(onehot, axis=0, keepdims=True)
    counts_ref[...] = cnt_ref[...]
    route = jnp.where(lane_f == float(CLS_LANE), cls, jnp.where(lane_f == float(RANK_LANE), rank, route))
    hx_ref[:, D_MODEL:HX_W] = route
    meta_ref[0] = route.T[META_ROW0:META_ROW0 + SUBLANES, :]


def _tile_spec(width):
    return pl.BlockSpec((ROW_TILE, width), lambda t: (t, 0))


def _stream_specs(n_stream):
    if n_stream == 1:
        return [_tile_spec(D_MODEL)]
    return [pl.BlockSpec((1, ROW_TILE, D_MODEL),
                         lambda t: (t // TILES_PER_BATCH, jnp.maximum(t % TILES_PER_BATCH - 1, 0), 0)),
            pl.BlockSpec((1, CTX_LEN, D_MODEL), lambda t: (t // TILES_PER_BATCH, 0, 0))]


def _load_stream(refs):
    if len(refs) == 1:
        return refs[0][...]
    is_ctx = pl.program_id(0) % TILES_PER_BATCH == 0
    return jnp.where(is_ctx, refs[1][0], refs[0][0])


def _full_spec(shape):
    nd = len(shape)
    return pl.BlockSpec(shape, lambda t: (0,) * nd)


def _mod_spec():
    return pl.BlockSpec((1, 6, D_MODEL), lambda t: (_mod_row(t), 0, 0))


def _halo_specs(width, halo):
    per_tile = ROW_TILE // halo
    last = N_ROWS // halo - 1
    prev = pl.BlockSpec((halo, width), lambda t: (jnp.maximum(t * per_tile - 1, 0), 0))
    nxt = pl.BlockSpec((halo, width), lambda t: (jnp.minimum((t + 1) * per_tile, last), 0))
    return prev, nxt


def _seq_flags():
    w = pl.program_id(0) % TILES_PER_BATCH
    is_start = (w == 0) | (w == 1)
    is_end = (w == 0) | (w == TILES_PER_BATCH - 1)
    seq_tile = jnp.maximum(w - 1, 0)
    return is_start, is_end, seq_tile, w == 0


_POST_OUT_SHAPES = (jax.ShapeDtypeStruct((N_ROWS, D_MODEL), F32),
                    jax.ShapeDtypeStruct((N_ROWS, HX_W), F32),
                    jax.ShapeDtypeStruct((N_TILES, SUBLANES, ROW_TILE), F32),
                    jax.ShapeDtypeStruct((1, ROUTE_W), F32))


def _post_out_specs():
    return (_tile_spec(D_MODEL),
            _tile_spec(HX_W),
            pl.BlockSpec((1, SUBLANES, ROW_TILE), lambda t: (t, 0, 0)),
            pl.BlockSpec((1, ROUTE_W), lambda t: (0, 0)))


def _class_count_scratch():
    return pltpu.VMEM((1, ROUTE_W), F32)


def _even_pre_kernel(n_stream, *refs):
    mod_ref, g_ref, w_ref, a_ref, gb_ref, gch_ref = refs[n_stream:]
    m = mod_ref[0]
    h = _norm_mod(_load_stream(refs[:n_stream]), g_ref[...], m[0:1], m[1:2])
    u = _dot(h.astype(BF16), w_ref[...])
    ca, cb = CONV_A_DIM, CONV_B_DIM
    a_ref[...] = u[:, 0:ca] * _sigmoid(u[:, ca:2 * ca])
    gb_ref[...] = u[:, 2 * ca:2 * ca + cb]
    gch_ref[...] = u[:, 2 * ca + cb:2 * ca + 2 * cb] * u[:, 2 * ca + 2 * cb:2 * ca + 3 * cb]


def _even_pre(stream, mod_i, g0, w_in):
    n_in = w_in.shape[1]
    return pl.pallas_call(
        functools.partial(_even_pre_kernel, len(stream)),
        grid=(N_TILES,),
        in_specs=_stream_specs(len(stream)) + [_mod_spec(), _full_spec((1, D_MODEL)),
                                               _full_spec((D_MODEL, n_in))],
        out_specs=(_tile_spec(CONV_A_DIM), _tile_spec(CONV_B_DIM), _tile_spec(CONV_B_DIM)),
        out_shape=(jax.ShapeDtypeStruct((N_ROWS, CONV_A_DIM), F32),
                   jax.ShapeDtypeStruct((N_ROWS, CONV_B_DIM), F32),
                   jax.ShapeDtypeStruct((N_ROWS, CONV_B_DIM), F32)),
        compiler_params=_cparams(("parallel",)),
        name="even_pre",
    )(*stream, mod_i, g0, w_in)


def _even_post_kernel(n_stream, *refs):
    (a_ref, ap_ref, an_ref, gch_ref, gp_ref, gn_ref, gb_ref,
     cwa_ref, cba_ref, lng_ref, lnb_ref, cwb_ref, wout_ref,
     mod_ref, g1_ref, wr_ref, br_ref,
     xo_ref, hx_ref, meta_ref, counts_ref, exta_ref, extb_ref, cnt_ref) = refs[n_stream:]
    is_start, is_end, _, _ = _seq_flags()
    ha, hb = CONV_A_HALO, SMALL_HALO
    exta_ref[0:ha] = jnp.where(is_start, 0.0, ap_ref[...])
    exta_ref[ha:ha + ROW_TILE] = a_ref[...]
    exta_ref[ha + ROW_TILE:2 * ha + ROW_TILE] = jnp.where(is_end, 0.0, an_ref[...])
    extb_ref[0:hb] = jnp.where(is_start, 0.0, gp_ref[...])
    extb_ref[hb:hb + ROW_TILE] = gch_ref[...]
    extb_ref[hb + ROW_TILE:2 * hb + ROW_TILE] = jnp.where(is_end, 0.0, gn_ref[...])

    half_a = (CONV_A_WIDTH - 1) // 2
    lead = ha - half_a
    span = ROW_TILE + SUBLANES
    assert 0 <= lead and lead + SUBLANES - 1 <= SUBLANES
    parts = []
    for cblk in range(CONV_A_DIM // LANES):
        cs = slice(cblk * LANES, (cblk + 1) * LANES)
        acc = jnp.zeros((ROW_TILE, LANES), F32) + cba_ref[:, cs]
        for b in range(SUBLANES):
            z = jnp.zeros((span, LANES), F32)
            for k in range(b, CONV_A_WIDTH, SUBLANES):
                z = z + cwa_ref[k:k + 1, cs] * exta_ref[k - b:k - b + span, cs]
            acc = acc + z[lead + b:lead + b + ROW_TILE, :]
        parts.append(acc)
    acc = jnp.concatenate(parts, axis=1)
    mu = jnp.mean(acc, axis=-1, keepdims=True)
    xc = acc - mu
    a = xc * lax.rsqrt(jnp.mean(xc * xc, axis=-1, keepdims=True) + EPS)
    a = _silu(a * lng_ref[...] + lnb_ref[...])

    half_b = (CONV_B_WIDTH - 1) // 2
    cb = jnp.zeros((ROW_TILE, CONV_B_DIM), F32)
    for k in range(CONV_B_WIDTH):
        off = hb - half_b + k
        cb = cb + cwb_ref[k:k + 1, :] * extb_ref[off:off + ROW_TILE, :]
    b = gb_ref[...] * cb

    y = (_dot(a.astype(BF16), wout_ref[0:CONV_A_DIM, :])
         + _dot(b.astype(BF16), wout_ref[CONV_A_DIM:CONV_A_DIM + CONV_B_DIM, :]))
    _residual_and_route(_load_stream(refs[:n_stream]), y, mod_ref[0], g1_ref, wr_ref, br_ref,
                        xo_ref, hx_ref, meta_ref, counts_ref, cnt_ref)


def _even_post(stream, a_pre, gb, gch, cwa, cba, lng, lnb, cwb, w_out, mod_i, g1, wr, br):
    ap_spec, an_spec = _halo_specs(CONV_A_DIM, CONV_A_HALO)
    gp_spec, gn_spec = _halo_specs(CONV_B_DIM, SMALL_HALO)
    return pl.pallas_call(
        functools.partial(_even_post_kernel, len(stream)),
        grid=(N_TILES,),
        in_specs=_stream_specs(len(stream)) + [
                  _tile_spec(CONV_A_DIM), ap_spec, an_spec,
                  _tile_spec(CONV_B_DIM), gp_spec, gn_spec, _tile_spec(CONV_B_DIM),
                  _full_spec((CONV_A_WIDTH, CONV_A_DIM)), _full_spec((1, CONV_A_DIM)),
                  _full_spec((1, CONV_A_DIM)), _full_spec((1, CONV_A_DIM)),
                  _full_spec((CONV_B_WIDTH, CONV_B_DIM)), _full_spec((D_MODEL, D_MODEL)),
                  _mod_spec(), _full_spec((1, D_MODEL)),
                  _full_spec((D_MODEL, 2 * ROUTE_W)),
                  _full_spec((1, ROUTE_W))],
        out_specs=_post_out_specs(),
        out_shape=_POST_OUT_SHAPES,
        scratch_shapes=[pltpu.VMEM((ROW_TILE + 2 * CONV_A_HALO, CONV_A_DIM), F32),
                        pltpu.VMEM((ROW_TILE + 2 * SMALL_HALO, CONV_B_DIM), F32),
                        _class_count_scratch()],
        compiler_params=_cparams(("arbitrary",)),
        name="even_post",
    )(*stream, a_pre, a_pre, a_pre, gch, gch, gch, gb, cwa, cba, lng, lnb, cwb, w_out,
      mod_i, g1, wr, br)


def _odd_pre_kernel(x_ref, mod_ref, g_ref, w_ref, gq_ref, wqm_ref, wqs_ref, gkv_ref, wk_ref, wv_ref,
                    vaug_ref, cos_ref, sin_ref, qt_ref, k_ref, vt_ref, pool_ref):
    m = mod_ref[0]
    h = _norm_mod(x_ref[...], g_ref[...], m[0:1], m[1:2])
    u = _dot(h.astype(BF16), w_ref[...])
    o_kv = Q_LORA
    o_kr = o_kv + KV_LORA
    o_krs = o_kr + HEAD_PAD
    o_pool = o_krs + HEAD_PAD
    cos = cos_ref[...]
    sin = sin_ref[...]

    q_c = u[:, 0:o_kv]
    qn = (q_c * lax.rsqrt(jnp.mean(q_c * q_c, axis=-1, keepdims=True) + EPS) * gq_ref[...]).astype(BF16)
    q_main = _dot(qn, wqm_ref[...])
    q_swap = _dot(qn, wqs_ref[...])
    kv_c = u[:, o_kv:o_kr]
    kvn = (kv_c * lax.rsqrt(jnp.mean(kv_c * kv_c, axis=-1, keepdims=True) + EPS) * gkv_ref[...]).astype(BF16)
    k_nope = _dot(kvn, wk_ref[...])
    v = _dot(kvn, wv_ref[...]) + vaug_ref[...]
    for cc in range(ROW_TILE // ATT_CHUNK):
        vt_ref[cc] = v[cc * ATT_CHUNK:(cc + 1) * ATT_CHUNK, :].T.astype(BF16)
    k_rope = u[:, o_kr:o_krs] * cos + u[:, o_krs:o_pool] * sin
    for hd in range(MLA_HEADS):
        sl = slice(hd * HEAD_PAD, (hd + 1) * HEAD_PAD)
        qh = (q_main[:, sl] * cos + q_swap[:, sl] * sin) * (MLA_SCALE * LOG2E)
        qt_ref[0, sl, :] = qh.T.astype(BF16)
        k_ref[:, sl] = (k_nope[:, sl] + k_rope).astype(BF16)
    pool_ref[...] = u[:, o_pool:o_pool + POOL_DIM]


def _odd_pre(x, mod_i, g0, w_in, gq, wqm, wqs, gkv, wk, wv, cos_t, sin_t):
    n_in = w_in.shape[1]
    vaug = jnp.zeros((MLA_HEADS, V_AUG), F32).at[:, MLA_V].set(1.0).reshape(1, VT_W)
    return pl.pallas_call(
        _odd_pre_kernel,
        grid=(N_TILES,),
        in_specs=[_tile_spec(D_MODEL), _mod_spec(), _full_spec((1, D_MODEL)),
                  _full_spec((D_MODEL, n_in)), _full_spec((1, Q_LORA)),
                  _full_spec((Q_LORA, QK_W)), _full_spec((Q_LORA, QK_W)),
                  _full_spec((1, KV_LORA)), _full_spec((KV_LORA, QK_W)), _full_spec((KV_LORA, VT_W)),
                  _full_spec((1, VT_W)),
                  pl.BlockSpec((ROW_TILE, HEAD_PAD), lambda t: (t % TILES_PER_BATCH, 0)),
                  pl.BlockSpec((ROW_TILE, HEAD_PAD), lambda t: (t % TILES_PER_BATCH, 0))],
        out_specs=(pl.BlockSpec((1, QK_W, ROW_TILE), lambda t: (t, 0, 0)),
                   _tile_spec(QK_W),
                   pl.BlockSpec((ROW_TILE // ATT_CHUNK, VT_W, ATT_CHUNK), lambda t: (t, 0, 0)),
                   _tile_spec(POOL_DIM)),
        out_shape=(jax.ShapeDtypeStruct((N_TILES, QK_W, ROW_TILE), BF16),
                   jax.ShapeDtypeStruct((N_ROWS, QK_W), BF16),
                   jax.ShapeDtypeStruct((N_ROWS // ATT_CHUNK, VT_W, ATT_CHUNK), BF16),
                   jax.ShapeDtypeStruct((N_ROWS, POOL_DIM), F32)),
        compiler_params=_cparams(("parallel",)),
        name="odd_pre",
    )(x, mod_i, g0, w_in, gq, wqm, wqs, gkv, wk, wv, vaug, cos_t, sin_t)


def _attn_kernel(qta_ref, qtb_ref, k_ref, vt_ref, oa_ref, ob_ref, m_ref, acc_ref, s_ref, p_ref, a_ref):
    qt_refs = (qta_ref, qtb_ref)
    m_ref[...] = jnp.full(m_ref.shape, -jnp.inf, F32)
    acc_ref[...] = jnp.zeros(acc_ref.shape, F32)

    def scores_item(c, slot, qi, hd):
        r0 = pl.multiple_of(c * ATT_CHUNK, ATT_CHUNK)
        sl = slice(hd * HEAD_PAD, (hd + 1) * HEAD_PAD)
        s_ref[slot, qi * MLA_HEADS + hd] = _dot(k_ref[pl.ds(r0, ATT_CHUNK), sl], qt_refs[qi][0, sl, :])

    def softmax_item(slot, it):
        hs = slice(it, it + 1)
        m_old = m_ref[hs, :]
        m_new = jnp.maximum(m_old, jnp.max(s_ref[slot, it], axis=0, keepdims=True))
        m_ref[hs, :] = m_new
        a_ref[slot, hs, :] = jnp.exp2(m_old - m_new)
        for r0 in range(0, ATT_CHUNK, SOFTMAX_ROWS):
            rows = slice(r0, r0 + SOFTMAX_ROWS)
            p_ref[slot, it, rows, :] = jnp.exp2(s_ref[slot, it, rows, :] - m_new).astype(BF16)

    def pv_item(c, slot, qi, hd):
        it = qi * MLA_HEADS + hd
        rows = slice(it * V_AUG, (it + 1) * V_AUG)
        pv = _dot(vt_ref[c, hd * V_AUG:(hd + 1) * V_AUG, :], p_ref[slot, it])
        acc_ref[rows, :] = a_ref[slot, it:it + 1, :] * acc_ref[rows, :] + pv

    def stage_scores(c, slot, nq):
        for qi in range(nq):
            for hd in range(MLA_HEADS):
                scores_item(c, slot, qi, hd)

    def stage_softmax(slot, nq):
        for it in range(nq * MLA_HEADS):
            softmax_item(slot, it)

    def stage_pv(c, slot, nq):
        for qi in range(nq):
            for hd in range(MLA_HEADS):
                pv_item(c, slot, qi, hd)

    def steady(c, slot):
        stage_scores(c, slot, ATT_QTILES)
        stage_pv(c - 2, slot, ATT_QTILES)
        stage_softmax(1 - slot, ATT_QTILES)

    def result(qi):
        outs = []
        for hd in range(MLA_HEADS):
            it = qi * MLA_HEADS + hd
            blk = acc_ref[it * V_AUG:(it + 1) * V_AUG, :]
            outs.append(blk[0:MLA_V, :] / blk[MLA_V:MLA_V + 1, :])
        return jnp.concatenate(outs, axis=0).T

    is_ctx = pl.program_id(1) == 0

    @pl.when(is_ctx)
    def _():
        stage_scores(0, 0, 1)
        stage_softmax(0, 1)
        stage_pv(0, 0, 1)
        oa_ref[...] = result(0)
        ob_ref[...] = jnp.zeros(ob_ref.shape, F32)

    @pl.when(jnp.logical_not(is_ctx))
    def _():
        n = ROWS_PER_BATCH // ATT_CHUNK
        stage_scores(0, 0, ATT_QTILES)
        stage_scores(1, 1, ATT_QTILES)
        stage_softmax(0, ATT_QTILES)
        first = 2 + (n - 2) % ATT_TRIP
        for c in range(2, first):
            steady(c, c % 2)

        def trip(i, carry):
            c = first + ATT_TRIP * i
            for u in range(ATT_TRIP):
                steady(c + u, (first + u) % 2)
            return carry

        lax.fori_loop(0, (n - first) // ATT_TRIP, trip, 0)
        stage_pv(n - 2, (n - 2) % 2, ATT_QTILES)
        stage_softmax((n - 1) % 2, ATT_QTILES)
        stage_pv(n - 1, (n - 1) % 2, ATT_QTILES)
        oa_ref[...] = result(0)
        ob_ref[...] = result(1)


def _attention(qt, k, vt):
    cpb = ROWS_PER_BATCH // ATT_CHUNK
    assert cpb >= 3 and CTX_LEN == ATT_CHUNK and (TILES_PER_BATCH - 1) % ATT_QTILES == 0
    items = ATT_QTILES * MLA_HEADS

    def q_first(b, i):
        return (b * TILES_PER_BATCH + jnp.maximum(ATT_QTILES * i - 1, 0), 0, 0)

    def q_second(b, i):
        return (b * TILES_PER_BATCH + ATT_QTILES * i, 0, 0)

    out_spec = pl.BlockSpec((ROW_TILE, ATT_W), lambda b, i: (b * ATT_STEPS + i, 0))
    out_sds = jax.ShapeDtypeStruct((BATCH * ATT_STEPS * ROW_TILE, ATT_W), F32)
    return pl.pallas_call(
        _attn_kernel,
        grid=(BATCH, ATT_STEPS),
        in_specs=[pl.BlockSpec((1, QK_W, ROW_TILE), q_first),
                  pl.BlockSpec((1, QK_W, ROW_TILE), q_second),
                  pl.BlockSpec((ROWS_PER_BATCH, QK_W), lambda b, i: (b, 0), pipeline_mode=pl.Buffered(1)),
                  pl.BlockSpec((cpb, VT_W, ATT_CHUNK), lambda b, i: (b, 0, 0), pipeline_mode=pl.Buffered(1))],
        out_specs=(out_spec, out_spec),
        out_shape=(out_sds, out_sds),
        scratch_shapes=[pltpu.VMEM((items, ROW_TILE), F32),
                        pltpu.VMEM((items * V_AUG, ROW_TILE), F32),
                        pltpu.VMEM((2, items, ATT_CHUNK, ROW_TILE), F32),
                        pltpu.VMEM((2, items, ATT_CHUNK, ROW_TILE), BF16),
                        pltpu.VMEM((2, items, ROW_TILE), F32)],
        compiler_params=pltpu.CompilerParams(dimension_semantics=("parallel", "parallel"),
                                             vmem_limit_bytes=ATT_VMEM_LIMIT),
        name="attention",
    )(qt, qt, k, vt)


def _odd_post_kernel(atta_ref, attb_ref, p_ref, pp_ref, pn_ref, wp_ref, bp_ref, sp_ref, wout_ref,
                     x_ref, mod_ref, g1_ref, wr_ref, br_ref,
                     xo_ref, hx_ref, meta_ref, counts_ref, ext_ref, cnt_ref):
    is_start, is_end, seq_tile, is_ctx = _seq_flags()
    hp = SMALL_HALO
    ext_ref[0:hp] = jnp.where(is_start, 0.0, pp_ref[...])
    ext_ref[hp:hp + ROW_TILE] = p_ref[...]
    ext_ref[hp + ROW_TILE:2 * hp + ROW_TILE] = jnp.where(is_end, 0.0, pn_ref[...])
    seq_len = jnp.where(is_ctx, CTX_LEN, SEQ)
    pos = seq_tile * ROW_TILE + lax.broadcasted_iota(jnp.int32, (ROW_TILE, 1), 0)

    pooled = []
    for g, w in enumerate(POOL_WINDOWS):
        sl = slice(g * POOL_GROUP, (g + 1) * POOL_GROUP)
        ssum = jnp.zeros((ROW_TILE, POOL_GROUP), F32)
        for d in range(-(w // 2), w - w // 2):
            ssum = ssum + ext_ref[hp + d:hp + d + ROW_TILE, sl]
        lo = jnp.maximum(pos - w // 2, 0)
        hi = jnp.minimum(pos - w // 2 + w, seq_len)
        cnt = (hi - lo).astype(F32)
        pm = ssum / cnt - p_ref[:, sl]
        pooled.append(((_dot(pm.astype(BF16), wp_ref[g]) + bp_ref[g:g + 1, :]) * sp_ref[:, sl]).astype(BF16))
    w = pl.program_id(0) % TILES_PER_BATCH
    att = jnp.where((w > 0) & (w % ATT_QTILES == 0), attb_ref[...], atta_ref[...])
    mixed = jnp.concatenate([att.astype(BF16)] + pooled, axis=1)
    y = _dot(mixed, wout_ref[...])
    _residual_and_route(x_ref[...], y, mod_ref[0], g1_ref, wr_ref, br_ref,
                        xo_ref, hx_ref, meta_ref, counts_ref, cnt_ref)


def _odd_post(att_a, att_b, pool_in, w_pool, b_pool, s_pool, w_out, x, mod_i, g1, wr, br):
    pp_spec, pn_spec = _halo_specs(POOL_DIM, SMALL_HALO)
    ng = len(POOL_WINDOWS)
    att_spec = pl.BlockSpec((ROW_TILE, ATT_W), lambda t: ((t // TILES_PER_BATCH) * ATT_STEPS
                                                          + (t % TILES_PER_BATCH + 1) // ATT_QTILES, 0))
    return pl.pallas_call(
        _odd_post_kernel,
        grid=(N_TILES,),
        in_specs=[att_spec, att_spec, _tile_spec(POOL_DIM), pp_spec, pn_spec,
                  _full_spec((ng, POOL_GROUP, POOL_GROUP)), _full_spec((ng, POOL_GROUP)),
                  _full_spec((1, POOL_DIM)), _full_spec((D_MODEL, D_MODEL)),
                  _tile_spec(D_MODEL), _mod_spec(), _full_spec((1, D_MODEL)),
                  _full_spec((D_MODEL, 2 * ROUTE_W)),
                  _full_spec((1, ROUTE_W))],
        out_specs=_post_out_specs(),
        out_shape=_POST_OUT_SHAPES,
        scratch_shapes=[pltpu.VMEM((ROW_TILE + 2 * SMALL_HALO, POOL_DIM), F32), _class_count_scratch()],
        compiler_params=_cparams(("arbitrary",)),
        name="odd_post",
    )(att_a, att_b, pool_in, pool_in, pool_in, w_pool, b_pool, s_pool, w_out, x, mod_i, g1, wr, br)


def _moe_plan(meta, counts):
    cls = meta[:, CLS_LANE - META_ROW0, :].reshape(N_ROWS).astype(jnp.int32)
    rank = meta[:, RANK_LANE - META_ROW0, :].reshape(N_ROWS).astype(jnp.int32)
    tiles = (counts[0, :N_CLASSES].astype(jnp.int32) + MOE_TM - 1) // MOE_TM
    tile_end = jnp.cumsum(tiles)
    first_row = (tile_end - tiles) * MOE_TM
    dest = rank
    for c in range(N_CLASSES):
        dest = dest + jnp.where(cls == c, first_row[c], 0)
    j = jnp.arange(MOE_TILES)
    used = tile_end[-1]
    j_eff = jnp.minimum(j, jnp.maximum(used - 1, 0))
    cls_of_tile = jnp.sum((j_eff[:, None] >= tile_end[None, :]).astype(jnp.int32), axis=1)
    cls_of_tile = jnp.minimum(cls_of_tile, N_CLASSES - 1)
    grp = cls_of_tile // PAIRS_PER_GROUP
    pair = cls_of_tile % PAIRS_PER_GROUP
    e_a = grp * EXPERTS_PER_GROUP + jnp.array(PAIR_A, jnp.int32)[pair]
    e_b = grp * EXPERTS_PER_GROUP + jnp.array(PAIR_B, jnp.int32)[pair]
    status = jnp.where(j < used, 1 + jnp.array(PAIR_A_IS_HIGHER, jnp.int32)[pair], 0)
    new_a = jnp.concatenate([jnp.ones((1,), jnp.int32), (e_a[1:] != e_a[:-1]).astype(jnp.int32)])
    new_b = jnp.concatenate([jnp.ones((1,), jnp.int32), (e_b[1:] != e_b[:-1]).astype(jnp.int32)])
    return dest, e_a, e_b, status, new_a + 2 * new_b


def _move_rows(n_rows, base, dest_ref, row_copy, block_copy):
    def issue(i, carry):
        for u in range(DMA_UNROLL):
            r = i * DMA_UNROLL + u
            row_copy(r, dest_ref[base + r]).start(priority=u % 2)
        return carry

    lax.fori_loop(0, n_rows // DMA_UNROLL, issue, 0)
    block_copy.wait()


def _dispatch_kernel(dest_ref, hx_ref, init_ref, xs_ref, sem):
    del init_ref
    _move_rows(DISPATCH_ROWS, pl.program_id(0) * DISPATCH_ROWS, dest_ref,
               lambda r, d: pltpu.make_async_copy(hx_ref.at[pl.ds(r, 1)], xs_ref.at[pl.ds(d, 1)], sem),
               pltpu.make_async_copy(hx_ref, xs_ref.at[pl.ds(0, DISPATCH_ROWS)], sem))


def _dispatch(dest, hx, init):
    shape = (MOE_ROWS, HX_W)
    return pl.pallas_call(
        _dispatch_kernel,
        grid_spec=pltpu.PrefetchScalarGridSpec(
            num_scalar_prefetch=1,
            grid=(N_ROWS // DISPATCH_ROWS,),
            in_specs=[pl.BlockSpec((DISPATCH_ROWS, HX_W), lambda t, dest: (t, 0)),
                      pl.BlockSpec(memory_space=pl.ANY)],
            out_specs=pl.BlockSpec(memory_space=pl.ANY),
            scratch_shapes=[pltpu.SemaphoreType.DMA(())]),
        out_shape=jax.ShapeDtypeStruct(shape, F32),
        input_output_aliases={2: 0},
        compiler_params=_cparams(("arbitrary",)),
        name="moe_dispatch",
    )(dest, hx, init)


def _moe_expert_kernel(ea_ref, eb_ref, status_ref, fresh_ref, xs_ref, w1a_ref, w3a_ref, w2a_ref,
                       w1b_ref, w3b_ref, w2b_ref, ys_ref, *bf_refs):
    del ea_ref, eb_ref
    status = status_ref[pl.program_id(0)]
    fresh = fresh_ref[pl.program_id(0)]
    slot_a, slot_b = bf_refs[0:3], bf_refs[3:6]

    @pl.when(fresh % 2 == 1)
    def _():
        for dst, src in zip(slot_a, (w1a_ref, w3a_ref, w2a_ref)):
            dst[...] = src[0, 0].astype(BF16)

    @pl.when(fresh >= 2)
    def _():
        for dst, src in zip(slot_b, (w1b_ref, w3b_ref, w2b_ref)):
            dst[...] = src[0, 0].astype(BF16)

    @pl.when(status == 0)
    def _():
        ys_ref[...] = jnp.zeros(ys_ref.shape, F32)

    @pl.when(status != 0)
    def _():
        h = xs_ref[:, 0:D_MODEL].astype(BF16)
        g_lo = xs_ref[:, D_MODEL + GLO_LANE:D_MODEL + GLO_LANE + 1]
        g_hi = xs_ref[:, D_MODEL + GHI_LANE:D_MODEL + GHI_LANE + 1]
        a_is_higher = status == 2
        y = jnp.zeros((MOE_TM, D_MODEL), F32)
        for g, (w1_ref, w3_ref, w2_ref) in ((jnp.where(a_is_higher, g_hi, g_lo), slot_a),
                                            (jnp.where(a_is_higher, g_lo, g_hi), slot_b)):
            hid = _silu(_dot(h, w1_ref[...])) * _dot(h, w3_ref[...])
            y = y + g * _dot(hid.astype(BF16), w2_ref[...])
        ys_ref[...] = y


def _moe_experts(e_a, e_b, status, fresh, xs_sorted, w1, w3, w2, layer):
    up_a = pl.BlockSpec((1, 1, D_MODEL, D_EXPERT), lambda j, ea, eb, status, fresh: (layer, ea[j], 0, 0))
    dn_a = pl.BlockSpec((1, 1, D_EXPERT, D_MODEL), lambda j, ea, eb, status, fresh: (layer, ea[j], 0, 0))
    up_b = pl.BlockSpec((1, 1, D_MODEL, D_EXPERT), lambda j, ea, eb, status, fresh: (layer, eb[j], 0, 0))
    dn_b = pl.BlockSpec((1, 1, D_EXPERT, D_MODEL), lambda j, ea, eb, status, fresh: (layer, eb[j], 0, 0))
    up_bf = pltpu.VMEM((D_MODEL, D_EXPERT), BF16)
    dn_bf = pltpu.VMEM((D_EXPERT, D_MODEL), BF16)
    return pl.pallas_call(
        _moe_expert_kernel,
        grid_spec=pltpu.PrefetchScalarGridSpec(
            num_scalar_prefetch=4,
            grid=(MOE_TILES,),
            in_specs=[pl.BlockSpec((MOE_TM, HX_W), lambda j, ea, eb, status, fresh: (j, 0)),
                      up_a, up_a, dn_a, up_b, up_b, dn_b],
            out_specs=pl.BlockSpec((MOE_TM, D_MODEL), lambda j, ea, eb, status, fresh: (j, 0)),
            scratch_shapes=[up_bf, up_bf, dn_bf, up_bf, up_bf, dn_bf]),
        out_shape=jax.ShapeDtypeStruct((MOE_ROWS, D_MODEL), F32),
        compiler_params=_cparams(("arbitrary",)),
        name="moe_experts",
    )(e_a, e_b, status, fresh, xs_sorted, w1, w3, w2, w1, w3, w2)


def _combine_kernel(dest_ref, x_ref, mod_ref, ys_ref, o_ref, buf_ref, sem):
    t = pl.program_id(0)
    _move_rows(COMBINE_ROWS, t * COMBINE_ROWS, dest_ref,
               lambda r, d: pltpu.make_async_copy(ys_ref.at[pl.ds(d, 1)], buf_ref.at[pl.ds(r, 1)], sem),
               pltpu.make_async_copy(ys_ref.at[pl.ds(0, COMBINE_ROWS)], buf_ref, sem))
    sub = COMBINE_ROWS // ROW_TILE
    for b in range(sub):
        rs = slice(b * ROW_TILE, (b + 1) * ROW_TILE)
        m5 = mod_ref[pl.ds(_mod_row(t * sub + b), 1)][0, 5:6]
        o_ref[rs, :] = x_ref[rs, :] + m5 * buf_ref[rs, :]


def _combine(dest, x, mod_i, ys):
    return pl.pallas_call(
        _combine_kernel,
        grid_spec=pltpu.PrefetchScalarGridSpec(
            num_scalar_prefetch=1,
            grid=(N_ROWS // COMBINE_ROWS,),
            in_specs=[pl.BlockSpec((COMBINE_ROWS, D_MODEL), lambda t, dest: (t, 0)),
                      pl.BlockSpec((MOD_ROWS, 6, D_MODEL), lambda t, dest: (0, 0, 0)),
                      pl.BlockSpec(memory_space=pl.ANY)],
            out_specs=pl.BlockSpec((COMBINE_ROWS, D_MODEL), lambda t, dest: (t, 0)),
            scratch_shapes=[pltpu.VMEM((COMBINE_ROWS, D_MODEL), F32), pltpu.SemaphoreType.DMA(())]),
        out_shape=jax.ShapeDtypeStruct((N_ROWS, D_MODEL), F32),
        compiler_params=_cparams(("arbitrary",)),
        name="moe_combine",
    )(dest, x, mod_i, ys)


def _combine_final_kernel(dest_ref, x_ref, mod_ref, g_ref, ys_ref, o_ref, buf_ref, sem):
    _move_rows(ROW_TILE, pl.program_id(0) * ROW_TILE, dest_ref,
               lambda r, d: pltpu.make_async_copy(ys_ref.at[pl.ds(d, 1)], buf_ref.at[pl.ds(r, 1)], sem),
               pltpu.make_async_copy(ys_ref.at[pl.ds(0, ROW_TILE)], buf_ref, sem))
    x = x_ref[...] + mod_ref[0][5:6] * buf_ref[...]
    o_ref[0] = x * lax.rsqrt(jnp.mean(x * x, axis=-1, keepdims=True) + EPS) * g_ref[...]


def _combine_final(dest, x, mod_i, ys, g):
    def out_map(t, dest):
        return (t // TILES_PER_BATCH, jnp.maximum(t % TILES_PER_BATCH - 1, 0), 0)

    return pl.pallas_call(
        _combine_final_kernel,
        grid_spec=pltpu.PrefetchScalarGridSpec(
            num_scalar_prefetch=1,
            grid=(N_TILES,),
            in_specs=[pl.BlockSpec((ROW_TILE, D_MODEL), lambda t, dest: (t, 0)),
                      pl.BlockSpec((1, 6, D_MODEL), lambda t, dest: (_mod_row(t), 0, 0)),
                      pl.BlockSpec((1, D_MODEL), lambda t, dest: (0, 0)),
                      pl.BlockSpec(memory_space=pl.ANY)],
            out_specs=pl.BlockSpec((1, ROW_TILE, D_MODEL), out_map),
            scratch_shapes=[pltpu.VMEM((ROW_TILE, D_MODEL), F32), pltpu.SemaphoreType.DMA(())]),
        out_shape=jax.ShapeDtypeStruct((BATCH, SEQ, D_MODEL), F32),
        compiler_params=_cparams(("arbitrary",)),
        name="moe_combine_final",
    )(dest, x, mod_i, g, ys)


_SWAP32 = tuple(list(range(8, 16)) + list(range(0, 8)) + list(range(24, 32)) + list(range(16, 24)))


def _rope_tables():
    f32 = np.float32
    rows = SEQ // GRID_W
    pos_row = np.repeat(np.arange(rows, dtype=f32), GRID_W)
    pos_col = np.tile(np.arange(GRID_W, dtype=f32), rows)
    inv = (f32(ROPE_BASE) ** (-np.arange(0, AXIS_DIM, 2, dtype=f32) / f32(AXIS_DIM))).astype(f32)
    ang = np.concatenate([pos_row[:, None] * inv, pos_col[:, None] * inv], axis=-1).astype(f32)
    cos, sin = np.cos(ang).astype(f32), np.sin(ang).astype(f32)
    p = AXIS_PAIRS
    c32 = np.concatenate([cos[:, :p], cos[:, :p], cos[:, p:], cos[:, p:]], axis=-1)
    s32 = np.concatenate([-sin[:, :p], sin[:, :p], -sin[:, p:], sin[:, p:]], axis=-1)
    pad = HEAD_PAD - MLA_NOPE - MLA_ROPE
    cos_lat = np.concatenate([np.ones((SEQ, MLA_NOPE), f32), c32, np.zeros((SEQ, pad), f32)], axis=-1)
    sin_lat = np.concatenate([np.zeros((SEQ, MLA_NOPE), f32), s32, np.zeros((SEQ, pad), f32)], axis=-1)
    cos_ctx = np.concatenate([np.ones((CTX_LEN, MLA_NOPE + MLA_ROPE), f32), np.zeros((CTX_LEN, pad), f32)], axis=-1)
    sin_ctx = np.zeros((CTX_LEN, HEAD_PAD), f32)
    return (jnp.asarray(np.concatenate([cos_ctx, cos_lat], axis=0)),
            jnp.asarray(np.concatenate([sin_ctx, sin_lat], axis=0)))


def _odd_weights(w_in, w_uq, w_ukv):
    swap = jnp.array(_SWAP32)
    o_kv, o_kr, o_pool = Q_LORA, Q_LORA + KV_LORA, Q_LORA + KV_LORA + MLA_ROPE
    w_kr = w_in[:, o_kr:o_pool]
    pad_l = jnp.zeros((D_MODEL, MLA_NOPE), F32)
    pad_r = jnp.zeros((D_MODEL, HEAD_PAD - MLA_NOPE - MLA_ROPE), F32)
    w_in2 = jnp.concatenate([w_in[:, :o_kr], pad_l, w_kr, pad_r, pad_l, w_kr[:, swap], pad_r,
                             w_in[:, o_pool:]], axis=-1).astype(BF16)
    wq = w_uq.reshape(Q_LORA, MLA_HEADS, MLA_NOPE + MLA_ROPE)
    zq = jnp.zeros((Q_LORA, MLA_HEADS, HEAD_PAD - MLA_NOPE - MLA_ROPE), F32)
    wq_main = jnp.concatenate([wq, zq], axis=-1).reshape(Q_LORA, QK_W).astype(BF16)
    wq_swap = jnp.concatenate([jnp.zeros((Q_LORA, MLA_HEADS, MLA_NOPE), F32),
                               wq[:, :, MLA_NOPE:][:, :, swap], zq], axis=-1).reshape(Q_LORA, QK_W).astype(BF16)
    wkv = w_ukv.reshape(KV_LORA, MLA_HEADS, MLA_NOPE + MLA_V)
    zk = jnp.zeros((KV_LORA, MLA_HEADS, HEAD_PAD - MLA_NOPE), F32)
    wk = jnp.concatenate([wkv[:, :, :MLA_NOPE], zk], axis=-1).reshape(KV_LORA, QK_W).astype(BF16)
    zv = jnp.zeros((KV_LORA, MLA_HEADS, V_AUG - MLA_V), F32)
    wv = jnp.concatenate([wkv[:, :, MLA_NOPE:], zv], axis=-1).reshape(KV_LORA, VT_W).astype(BF16)
    return w_in2, wq_main, wq_swap, wk, wv


def kernel(x, c, ctx, c_ctx, w_mod, b_mod, norm_g, ev_w_in, ev_conv_a_w, ev_conv_a_b, ev_ln_a_g, ev_ln_a_b,
           ev_conv_b_w, ev_w_out, od_w_in, od_q_norm_g, od_w_uq, od_kv_norm_g, od_w_ukv, od_w_pool, od_b_pool,
           od_s_pool, od_w_out, moe_wg, moe_bg, moe_we, moe_be, moe_w1, moe_w3, moe_w2, final_g):
    assert x.shape == (BATCH, SEQ, D_MODEL) and ctx.shape == (BATCH, CTX_LEN, D_MODEL)
    assert CTX_LEN == ROW_TILE
    stream = (x, ctx)
    cvec = jnp.concatenate([c, c_ctx[None, :], jnp.zeros((MOD_ROWS - BATCH - 1, D_MODEL), F32)], axis=0)
    mod = _modulation(cvec, w_mod, b_mod).reshape(DEPTH, MOD_ROWS, 6, D_MODEL)
    cos_t, sin_t = _rope_tables()
    sorted_rows = jnp.zeros((MOE_ROWS, HX_W), F32)

    for i in range(DEPTH):
        j = i // 2
        mod_i = mod[i]
        g0 = norm_g[i, 0][None, :]
        g1 = norm_g[i, 1][None, :]
        wr = jnp.concatenate([moe_wg[i], moe_we[i],
                              jnp.zeros((D_MODEL, ROUTE_W - N_GROUPS - N_EXPERTS), F32)], axis=-1)
        wr_hi = wr.astype(BF16)
        wr = jnp.concatenate([wr_hi, (wr - wr_hi.astype(F32)).astype(BF16)], axis=-1)
        br = jnp.concatenate([moe_bg[i], moe_be[i],
                              jnp.zeros((ROUTE_W - N_GROUPS - N_EXPERTS,), F32)])[None, :]
        if i % 2 == 0:
            a_pre, gb, gch = _even_pre(stream, mod_i, g0, ev_w_in[j].astype(BF16))
            xs, hx, meta, counts = _even_post(stream, a_pre, gb, gch, ev_conv_a_w[j], ev_conv_a_b[j][None, :],
                                              ev_ln_a_g[j][None, :], ev_ln_a_b[j][None, :], ev_conv_b_w[j],
                                              ev_w_out[j].astype(BF16), mod_i, g1, wr, br)
        else:
            w_in2, wq_main, wq_swap, wk, wv = _odd_weights(od_w_in[j], od_w_uq[j], od_w_ukv[j])
            q, k, v, pool_in = _odd_pre(xs, mod_i, g0, w_in2, od_q_norm_g[j][None, :], wq_main, wq_swap,
                                        od_kv_norm_g[j][None, :], wk, wv, cos_t, sin_t)
            att_a, att_b = _attention(q, k, v)
            xs, hx, meta, counts = _odd_post(att_a, att_b, pool_in, od_w_pool[j].astype(BF16), od_b_pool[j],
                                      od_s_pool[j][None, :], od_w_out[j].astype(BF16), xs, mod_i, g1,
                                      wr, br)
        dest, e_a, e_b, status, fresh = _moe_plan(meta, counts)
        sorted_rows = _dispatch(dest, hx, sorted_rows)
        ys = _moe_experts(e_a, e_b, status, fresh, sorted_rows, moe_w1, moe_w3, moe_w2, i)
        if i == DEPTH - 1:
            return _combine_final(dest, xs, mod_i, ys, final_g[None, :])
        xs = _combine(dest, xs, mod_i, ys)
        stream = (xs,)
```

```python
import functools

import numpy as np
import jax
import jax.numpy as jnp
from jax import lax
from jax.experimental import pallas as pl
from jax.experimental.pallas import tpu as pltpu

F32 = jnp.float32
BF16 = jnp.bfloat16

D_MODEL = 1024
BATCH = 2
SEQ = 8192
DEPTH = 4
GRID_W = 64
CTX_LEN = 256
EPS = 1e-6
CONV_A_DIM = 512
CONV_A_WIDTH = 31
CONV_B_DIM = 512
CONV_B_WIDTH = 3
MLA_HEADS = 8
MLA_NOPE = 64
MLA_ROPE = 32
MLA_V = 64
Q_LORA = 384
KV_LORA = 256
MLA_SCALE = (MLA_NOPE + MLA_ROPE) ** -0.5
AXIS_DIM = MLA_ROPE // 2
AXIS_PAIRS = AXIS_DIM // 2
ROPE_BASE = 10000.0
POOL_WINDOWS = (2, 4, 8, 16)
POOL_GROUP = 128
POOL_DIM = POOL_GROUP * len(POOL_WINDOWS)
N_GROUPS = 4
EXPERTS_PER_GROUP = 4
N_EXPERTS = 16
D_EXPERT = 512

LANES = 128
SUBLANES = 8
VMEM_LIMIT = 48 * 1024 * 1024

ROW_TILE = 256
ROWS_PER_BATCH = CTX_LEN + SEQ
N_ROWS = BATCH * ROWS_PER_BATCH
TILES_PER_BATCH = ROWS_PER_BATCH // ROW_TILE
N_TILES = N_ROWS // ROW_TILE
HEAD_PAD = LANES
QK_W = MLA_HEADS * HEAD_PAD
ATT_W = MLA_HEADS * MLA_V
V_AUG = MLA_V + 16
VT_W = MLA_HEADS * V_AUG
CONV_A_HALO = 16
SMALL_HALO = 8
ATT_CHUNK = 256
SOFTMAX_ROWS = 64
ATT_TRIP = 2
ATT_QTILES = 2
ATT_STEPS = 1 + (TILES_PER_BATCH - 1) // ATT_QTILES
ATT_VMEM_LIMIT = 56 * 1024 * 1024
LOG2E = 1.4426950408889634
ROUTE_W = LANES
GATE_LANE0 = N_GROUPS
CLS_LANE = GATE_LANE0 + N_EXPERTS
RANK_LANE = CLS_LANE + 1
GLO_LANE = RANK_LANE + 1
GHI_LANE = GLO_LANE + 1
META_ROW0 = (CLS_LANE // SUBLANES) * SUBLANES
HX_W = D_MODEL + ROUTE_W
PAIRS_PER_GROUP = 6
N_CLASSES = N_GROUPS * PAIRS_PER_GROUP
PAIR_A = (0, 0, 0, 1, 1, 3)
PAIR_B = (1, 2, 3, 3, 2, 2)
PAIR_A_IS_HIGHER = (0, 0, 0, 0, 0, 1)
MOE_TM = 256
MOE_TILES = -(-(N_ROWS + N_CLASSES * (MOE_TM - 1)) // MOE_TM)
MOE_ROWS = MOE_TILES * MOE_TM
DISPATCH_ROWS = 2816
COMBINE_ROWS = 1536
DMA_UNROLL = 32
MOD_ROWS = 8


def _mod_row(t):
    return jnp.where(t % TILES_PER_BATCH == 0, BATCH, t // TILES_PER_BATCH)


def _cparams(sem):
    return pltpu.CompilerParams(dimension_semantics=sem, vmem_limit_bytes=VMEM_LIMIT)


def _dot(a, b):
    return jnp.dot(a, b, preferred_element_type=F32)


def _sigmoid(x):
    return 1.0 / (1.0 + jnp.exp(-x))


def _silu(x):
    return x * _sigmoid(x)


def _norm_mod(x, g, shift, scale):
    y = x * lax.rsqrt(jnp.mean(x * x, axis=-1, keepdims=True) + EPS)
    return (y * g) * (1.0 + scale) + shift


def _mod_kernel(c_ref, w_ref, b_ref, o_ref):
    s = _silu(c_ref[...])
    o_ref[0] = jnp.dot(s, w_ref[0], preferred_element_type=F32,
                       precision=lax.Precision.HIGHEST) + b_ref[0]


def _modulation(cvec, w_mod, b_mod):
    nblk = 4
    bw = 6 * D_MODEL // nblk
    return pl.pallas_call(
        _mod_kernel,
        grid=(DEPTH, nblk),
        in_specs=[pl.BlockSpec((MOD_ROWS, D_MODEL), lambda i, j: (0, 0)),
                  pl.BlockSpec((1, D_MODEL, bw), lambda i, j: (i, 0, j)),
                  pl.BlockSpec((1, 1, bw), lambda i, j: (i, 0, j))],
        out_specs=pl.BlockSpec((1, MOD_ROWS, bw), lambda i, j: (i, 0, j)),
        out_shape=jax.ShapeDtypeStruct((DEPTH, MOD_ROWS, 6 * D_MODEL), F32),
        compiler_params=_cparams(("parallel", "parallel")),
        name="modulation",
    )(cvec, w_mod, b_mod.reshape(DEPTH, 1, 6 * D_MODEL))


def _route(h2, wr_ref, br_ref):
    hi = h2.astype(BF16)
    lo = (h2 - hi.astype(F32)).astype(BF16)
    both = _dot(hi, wr_ref[...])
    logits = (both[:, 0:ROUTE_W] + both[:, ROUTE_W:2 * ROUTE_W] + _dot(lo, wr_ref[:, 0:ROUTE_W])
              + br_ref[...])
    lane = lax.broadcasted_iota(jnp.int32, logits.shape, 1)
    lane_f = lane.astype(F32)
    neg = -jnp.inf
    big = float(ROUTE_W)
    gl = jnp.where(lane < N_GROUPS, logits, neg)
    gmax = jnp.max(gl, axis=-1, keepdims=True)
    gidx = jnp.min(jnp.where(gl == gmax, lane_f, big), axis=-1, keepdims=True)
    g_w = 1.0 / jnp.sum(jnp.exp(gl - gmax), axis=-1, keepdims=True)
    egrp = ((lane - GATE_LANE0) // EXPERTS_PER_GROUP).astype(F32)
    in_g = (lane >= GATE_LANE0) & (lane < GATE_LANE0 + N_EXPERTS) & (egrp == gidx)
    el = jnp.where(in_g, logits, neg)
    v1 = jnp.max(el, axis=-1, keepdims=True)
    i1 = jnp.min(jnp.where(el == v1, lane_f, big), axis=-1, keepdims=True)
    el2 = jnp.where(lane_f == i1, neg, el)
    v2 = jnp.max(el2, axis=-1, keepdims=True)
    i2 = jnp.min(jnp.where(el2 == v2, lane_f, big), axis=-1, keepdims=True)
    e21 = jnp.exp(v2 - v1)
    w1 = 1.0 / (1.0 + e21)
    w2 = e21 * w1
    g1 = w1 * g_w
    g2 = w2 * g_w
    gates = jnp.where(lane_f == i1, g1, jnp.where(lane_f == i2, g2, 0.0))
    base = GATE_LANE0 + gidx * EXPERTS_PER_GROUP
    lo = jnp.minimum(i1, i2) - base
    hi = jnp.maximum(i1, i2) - base
    pair = jnp.where(lo == 0.0, hi - 1.0, jnp.where(lo == 1.0, 6.0 - hi, 5.0))
    cls = gidx * PAIRS_PER_GROUP + pair
    g_lo = jnp.where(i1 < i2, g1, g2)
    g_hi = jnp.where(i1 < i2, g2, g1)
    return jnp.where(lane_f == float(GLO_LANE), g_lo, jnp.where(lane_f == float(GHI_LANE), g_hi, gates)), cls


def _residual_and_route(x, y, m, g1_ref, wr_ref, br_ref,
                        xo_ref, hx_ref, meta_ref, counts_ref, cnt_ref):
    @pl.when(pl.program_id(0) == 0)
    def _():
        cnt_ref[...] = jnp.zeros(cnt_ref.shape, F32)

    x_new = x + m[2:3] * y
    xo_ref[...] = x_new
    h2 = _norm_mod(x_new, g1_ref[...], m[3:4], m[4:5])
    hx_ref[:, 0:D_MODEL] = h2
    route, cls = _route(h2, wr_ref, br_ref)
    lane_f = lax.broadcasted_iota(jnp.int32, route.shape, 1).astype(F32)
    onehot = jnp.where(lane_f == cls, 1.0, 0.0)
    row = lax.broadcasted_iota(jnp.int32, (ROW_TILE, ROW_TILE), 0)
    col = lax.broadcasted_iota(jnp.int32, (ROW_TILE, ROW_TILE), 1)
    earlier = jnp.where(col < row, 1.0, 0.0).astype(BF16)
    before = _dot(earlier, onehot.astype(BF16)) + cnt_ref[...]
    rank = jnp.sum(onehot * before, axis=-1, keepdims=True)
    cnt_ref[...] = cnt_ref[...] + jnp.sum(onehot, axis=0, keepdims=True)
    counts_ref[...] = cnt_ref[...]
    route = jnp.where(lane_f == float(CLS_LANE), cls, jnp.where(lane_f == float(RANK_LANE), rank, route))
    hx_ref[:, D_MODEL:HX_W] = route
    meta_ref[0] = route.T[META_ROW0:META_ROW0 + SUBLANES, :]


def _tile_spec(width):
    return pl.BlockSpec((ROW_TILE, width), lambda t: (t, 0))


def _stream_specs(n_stream):
    if n_stream == 1:
        return [_tile_spec(D_MODEL)]
    return [pl.BlockSpec((1, ROW_TILE, D_MODEL),
                         lambda t: (t // TILES_PER_BATCH, jnp.maximum(t % TILES_PER_BATCH - 1, 0), 0)),
            pl.BlockSpec((1, CTX_LEN, D_MODEL), lambda t: (t // TILES_PER_BATCH, 0, 0))]


def _load_stream(refs):
    if len(refs) == 1:
        return refs[0][...]
    is_ctx = pl.program_id(0) % TILES_PER_BATCH == 0
    return jnp.where(is_ctx, refs[1][0], refs[0][0])


def _full_spec(shape):
    nd = len(shape)
    return pl.BlockSpec(shape, lambda t: (0,) * nd)


def _mod_spec():
    return pl.BlockSpec((1, 6, D_MODEL), lambda t: (_mod_row(t), 0, 0))


def _halo_specs(width, halo):
    per_tile = ROW_TILE // halo
    last = N_ROWS // halo - 1
    prev = pl.BlockSpec((halo, width), lambda t: (jnp.maximum(t * per_tile - 1, 0), 0))
    nxt = pl.BlockSpec((halo, width), lambda t: (jnp.minimum((t + 1) * per_tile, last), 0))
    return prev, nxt


def _seq_flags():
    w = pl.program_id(0) % TILES_PER_BATCH
    is_start = (w == 0) | (w == 1)
    is_end = (w == 0) | (w == TILES_PER_BATCH - 1)
    seq_tile = jnp.maximum(w - 1, 0)
    return is_start, is_end, seq_tile, w == 0


_POST_OUT_SHAPES = (jax.ShapeDtypeStruct((N_ROWS, D_MODEL), F32),
                    jax.ShapeDtypeStruct((N_ROWS, HX_W), F32),
                    jax.ShapeDtypeStruct((N_TILES, SUBLANES, ROW_TILE), F32),
                    jax.ShapeDtypeStruct((1, ROUTE_W), F32))


def _post_out_specs():
    return (_tile_spec(D_MODEL),
            _tile_spec(HX_W),
            pl.BlockSpec((1, SUBLANES, ROW_TILE), lambda t: (t, 0, 0)),
            pl.BlockSpec((1, ROUTE_W), lambda t: (0, 0)))


def _class_count_scratch():
    return pltpu.VMEM((1, ROUTE_W), F32)


def _even_pre_kernel(n_stream, *refs):
    mod_ref, g_ref, w_ref, a_ref, gb_ref, gch_ref = refs[n_stream:]
    m = mod_ref[0]
    h = _norm_mod(_load_stream(refs[:n_stream]), g_ref[...], m[0:1], m[1:2])
    u = _dot(h.astype(BF16), w_ref[...])
    ca, cb = CONV_A_DIM, CONV_B_DIM
    a_ref[...] = u[:, 0:ca] * _sigmoid(u[:, ca:2 * ca])
    gb_ref[...] = u[:, 2 * ca:2 * ca + cb]
    gch_ref[...] = u[:, 2 * ca + cb:2 * ca + 2 * cb] * u[:, 2 * ca + 2 * cb:2 * ca + 3 * cb]


def _even_pre(stream, mod_i, g0, w_in):
    n_in = w_in.shape[1]
    return pl.pallas_call(
        functools.partial(_even_pre_kernel, len(stream)),
        grid=(N_TILES,),
        in_specs=_stream_specs(len(stream)) + [_mod_spec(), _full_spec((1, D_MODEL)),
                                               _full_spec((D_MODEL, n_in))],
        out_specs=(_tile_spec(CONV_A_DIM), _tile_spec(CONV_B_DIM), _tile_spec(CONV_B_DIM)),
        out_shape=(jax.ShapeDtypeStruct((N_ROWS, CONV_A_DIM), F32),
                   jax.ShapeDtypeStruct((N_ROWS, CONV_B_DIM), F32),
                   jax.ShapeDtypeStruct((N_ROWS, CONV_B_DIM), F32)),
        compiler_params=_cparams(("parallel",)),
        name="even_pre",
    )(*stream, mod_i, g0, w_in)


def _even_post_kernel(n_stream, *refs):
    (a_ref, ap_ref, an_ref, gch_ref, gp_ref, gn_ref, gb_ref,
     cwa_ref, cba_ref, lng_ref, lnb_ref, cwb_ref, wout_ref,
     mod_ref, g1_ref, wr_ref, br_ref,
     xo_ref, hx_ref, meta_ref, counts_ref, exta_ref, extb_ref, cnt_ref) = refs[n_stream:]
    is_start, is_end, _, _ = _seq_flags()
    ha, hb = CONV_A_HALO, SMALL_HALO
    exta_ref[0:ha] = jnp.where(is_start, 0.0, ap_ref[...])
    exta_ref[ha:ha + ROW_TILE] = a_ref[...]
    exta_ref[ha + ROW_TILE:2 * ha + ROW_TILE] = jnp.where(is_end, 0.0, an_ref[...])
    extb_ref[0:hb] = jnp.where(is_start, 0.0, gp_ref[...])
    extb_ref[hb:hb + ROW_TILE] = gch_ref[...]
    extb_ref[hb + ROW_TILE:2 * hb + ROW_TILE] = jnp.where(is_end, 0.0, gn_ref[...])

    half_a = (CONV_A_WIDTH - 1) // 2
    lead = ha - half_a
    span = ROW_TILE + SUBLANES
    assert 0 <= lead and lead + SUBLANES - 1 <= SUBLANES
    parts = []
    for cblk in range(CONV_A_DIM // LANES):
        cs = slice(cblk * LANES, (cblk + 1) * LANES)
        acc = jnp.zeros((ROW_TILE, LANES), F32) + cba_ref[:, cs]
        for b in range(SUBLANES):
            z = jnp.zeros((span, LANES), F32)
            for k in range(b, CONV_A_WIDTH, SUBLANES):
                z = z + cwa_ref[k:k + 1, cs] * exta_ref[k - b:k - b + span, cs]
            acc = acc + z[lead + b:lead + b + ROW_TILE, :]
        parts.append(acc)
    acc = jnp.concatenate(parts, axis=1)
    mu = jnp.mean(acc, axis=-1, keepdims=True)
    xc = acc - mu
    a = xc * lax.rsqrt(jnp.mean(xc * xc, axis=-1, keepdims=True) + EPS)
    a = _silu(a * lng_ref[...] + lnb_ref[...])

    half_b = (CONV_B_WIDTH - 1) // 2
    cb = jnp.zeros((ROW_TILE, CONV_B_DIM), F32)
    for k in range(CONV_B_WIDTH):
        off = hb - half_b + k
        cb = cb + cwb_ref[k:k + 1, :] * extb_ref[off:off + ROW_TILE, :]
    b = gb_ref[...] * cb

    y = (_dot(a.astype(BF16), wout_ref[0:CONV_A_DIM, :])
         + _dot(b.astype(BF16), wout_ref[CONV_A_DIM:CONV_A_DIM + CONV_B_DIM, :]))
    _residual_and_route(_load_stream(refs[:n_stream]), y, mod_ref[0], g1_ref, wr_ref, br_ref,
                        xo_ref, hx_ref, meta_ref, counts_ref, cnt_ref)


def _even_post(stream, a_pre, gb, gch, cwa, cba, lng, lnb, cwb, w_out, mod_i, g1, wr, br):
    ap_spec, an_spec = _halo_specs(CONV_A_DIM, CONV_A_HALO)
    gp_spec, gn_spec = _halo_specs(CONV_B_DIM, SMALL_HALO)
    return pl.pallas_call(
        functools.partial(_even_post_kernel, len(stream)),
        grid=(N_TILES,),
        in_specs=_stream_specs(len(stream)) + [
                  _tile_spec(CONV_A_DIM), ap_spec, an_spec,
                  _tile_spec(CONV_B_DIM), gp_spec, gn_spec, _tile_spec(CONV_B_DIM),
                  _full_spec((CONV_A_WIDTH, CONV_A_DIM)), _full_spec((1, CONV_A_DIM)),
                  _full_spec((1, CONV_A_DIM)), _full_spec((1, CONV_A_DIM)),
                  _full_spec((CONV_B_WIDTH, CONV_B_DIM)), _full_spec((D_MODEL, D_MODEL)),
                  _mod_spec(), _full_spec((1, D_MODEL)),
                  _full_spec((D_MODEL, 2 * ROUTE_W)),
                  _full_spec((1, ROUTE_W))],
        out_specs=_post_out_specs(),
        out_shape=_POST_OUT_SHAPES,
        scratch_shapes=[pltpu.VMEM((ROW_TILE + 2 * CONV_A_HALO, CONV_A_DIM), F32),
                        pltpu.VMEM((ROW_TILE + 2 * SMALL_HALO, CONV_B_DIM), F32),
                        _class_count_scratch()],
        compiler_params=_cparams(("arbitrary",)),
        name="even_post",
    )(*stream, a_pre, a_pre, a_pre, gch, gch, gch, gb, cwa, cba, lng, lnb, cwb, w_out,
      mod_i, g1, wr, br)


def _odd_pre_kernel(x_ref, mod_ref, g_ref, w_ref, gq_ref, wqm_ref, wqs_ref, gkv_ref, wk_ref, wv_ref,
                    vaug_ref, cos_ref, sin_ref, qt_ref, k_ref, vt_ref, pool_ref):
    m = mod_ref[0]
    h = _norm_mod(x_ref[...], g_ref[...], m[0:1], m[1:2])
    u = _dot(h.astype(BF16), w_ref[...])
    o_kv = Q_LORA
    o_kr = o_kv + KV_LORA
    o_krs = o_kr + HEAD_PAD
    o_pool = o_krs + HEAD_PAD
    cos = cos_ref[...]
    sin = sin_ref[...]

    q_c = u[:, 0:o_kv]
    qn = (q_c * lax.rsqrt(jnp.mean(q_c * q_c, axis=-1, keepdims=True) + EPS) * gq_ref[...]).astype(BF16)
    q_main = _dot(qn, wqm_ref[...])
    q_swap = _dot(qn, wqs_ref[...])
    kv_c = u[:, o_kv:o_kr]
    kvn = (kv_c * lax.rsqrt(jnp.mean(kv_c * kv_c, axis=-1, keepdims=True) + EPS) * gkv_ref[...]).astype(BF16)
    k_nope = _dot(kvn, wk_ref[...])
    v = _dot(kvn, wv_ref[...]) + vaug_ref[...]
    for cc in range(ROW_TILE // ATT_CHUNK):
        vt_ref[cc] = v[cc * ATT_CHUNK:(cc + 1) * ATT_CHUNK, :].T.astype(BF16)
    k_rope = u[:, o_kr:o_krs] * cos + u[:, o_krs:o_pool] * sin
    for hd in range(MLA_HEADS):
        sl = slice(hd * HEAD_PAD, (hd + 1) * HEAD_PAD)
        qh = (q_main[:, sl] * cos + q_swap[:, sl] * sin) * (MLA_SCALE * LOG2E)
        qt_ref[0, sl, :] = qh.T.astype(BF16)
        k_ref[:, sl] = (k_nope[:, sl] + k_rope).astype(BF16)
    pool_ref[...] = u[:, o_pool:o_pool + POOL_DIM]


def _odd_pre(x, mod_i, g0, w_in, gq, wqm, wqs, gkv, wk, wv, cos_t, sin_t):
    n_in = w_in.shape[1]
    vaug = jnp.zeros((MLA_HEADS, V_AUG), F32).at[:, MLA_V].set(1.0).reshape(1, VT_W)
    return pl.pallas_call(
        _odd_pre_kernel,
        grid=(N_TILES,),
        in_specs=[_tile_spec(D_MODEL), _mod_spec(), _full_spec((1, D_MODEL)),
                  _full_spec((D_MODEL, n_in)), _full_spec((1, Q_LORA)),
                  _full_spec((Q_LORA, QK_W)), _full_spec((Q_LORA, QK_W)),
                  _full_spec((1, KV_LORA)), _full_spec((KV_LORA, QK_W)), _full_spec((KV_LORA, VT_W)),
                  _full_spec((1, VT_W)),
                  pl.BlockSpec((ROW_TILE, HEAD_PAD), lambda t: (t % TILES_PER_BATCH, 0)),
                  pl.BlockSpec((ROW_TILE, HEAD_PAD), lambda t: (t % TILES_PER_BATCH, 0))],
        out_specs=(pl.BlockSpec((1, QK_W, ROW_TILE), lambda t: (t, 0, 0)),
                   _tile_spec(QK_W),
                   pl.BlockSpec((ROW_TILE // ATT_CHUNK, VT_W, ATT_CHUNK), lambda t: (t, 0, 0)),
                   _tile_spec(POOL_DIM)),
        out_shape=(jax.ShapeDtypeStruct((N_TILES, QK_W, ROW_TILE), BF16),
                   jax.ShapeDtypeStruct((N_ROWS, QK_W), BF16),
                   jax.ShapeDtypeStruct((N_ROWS // ATT_CHUNK, VT_W, ATT_CHUNK), BF16),
                   jax.ShapeDtypeStruct((N_ROWS, POOL_DIM), F32)),
        compiler_params=_cparams(("parallel",)),
        name="odd_pre",
    )(x, mod_i, g0, w_in, gq, wqm, wqs, gkv, wk, wv, vaug, cos_t, sin_t)


def _attn_kernel(qta_ref, qtb_ref, k_ref, vt_ref, oa_ref, ob_ref, m_ref, acc_ref, s_ref, p_ref, a_ref):
    qt_refs = (qta_ref, qtb_ref)
    m_ref[...] = jnp.full(m_ref.shape, -jnp.inf, F32)
    acc_ref[...] = jnp.zeros(acc_ref.shape, F32)

    def scores_item(c, slot, qi, hd):
        r0 = pl.multiple_of(c * ATT_CHUNK, ATT_CHUNK)
        sl = slice(hd * HEAD_PAD, (hd + 1) * HEAD_PAD)
        s_ref[slot, qi * MLA_HEADS + hd] = _dot(k_ref[pl.ds(r0, ATT_CHUNK), sl], qt_refs[qi][0, sl, :])

    def softmax_item(slot, it):
        hs = slice(it, it + 1)
        m_old = m_ref[hs, :]
        m_new = jnp.maximum(m_old, jnp.max(s_ref[slot, it], axis=0, keepdims=True))
        m_ref[hs, :] = m_new
        a_ref[slot, hs, :] = jnp.exp2(m_old - m_new)
        for r0 in range(0, ATT_CHUNK, SOFTMAX_ROWS):
            rows = slice(r0, r0 + SOFTMAX_ROWS)
            p_ref[slot, it, rows, :] = jnp.exp2(s_ref[slot, it, rows, :] - m_new).astype(BF16)

    def pv_item(c, slot, qi, hd):
        it = qi * MLA_HEADS + hd
        rows = slice(it * V_AUG, (it + 1) * V_AUG)
        pv = _dot(vt_ref[c, hd * V_AUG:(hd + 1) * V_AUG, :], p_ref[slot, it])
        acc_ref[rows, :] = a_ref[slot, it:it + 1, :] * acc_ref[rows, :] + pv

    def stage_scores(c, slot, nq):
        for qi in range(nq):
            for hd in range(MLA_HEADS):
                scores_item(c, slot, qi, hd)

    def stage_softmax(slot, nq):
        for it in range(nq * MLA_HEADS):
            softmax_item(slot, it)

    def stage_pv(c, slot, nq):
        for qi in range(nq):
            for hd in range(MLA_HEADS):
                pv_item(c, slot, qi, hd)

    def steady(c, slot):
        stage_scores(c, slot, ATT_QTILES)
        stage_pv(c - 2, slot, ATT_QTILES)
        stage_softmax(1 - slot, ATT_QTILES)

    def result(qi):
        outs = []
        for hd in range(MLA_HEADS):
            it = qi * MLA_HEADS + hd
            blk = acc_ref[it * V_AUG:(it + 1) * V_AUG, :]
            outs.append(blk[0:MLA_V, :] / blk[MLA_V:MLA_V + 1, :])
        return jnp.concatenate(outs, axis=0).T

    is_ctx = pl.program_id(1) == 0

    @pl.when(is_ctx)
    def _():
        stage_scores(0, 0, 1)
        stage_softmax(0, 1)
        stage_pv(0, 0, 1)
        oa_ref[...] = result(0)
        ob_ref[...] = jnp.zeros(ob_ref.shape, F32)

    @pl.when(jnp.logical_not(is_ctx))
    def _():
        n = ROWS_PER_BATCH // ATT_CHUNK
        stage_scores(0, 0, ATT_QTILES)
        stage_scores(1, 1, ATT_QTILES)
        stage_softmax(0, ATT_QTILES)
        first = 2 + (n - 2) % ATT_TRIP
        for c in range(2, first):
            steady(c, c % 2)

        def trip(i, carry):
            c = first + ATT_TRIP * i
            for u in range(ATT_TRIP):
                steady(c + u, (first + u) % 2)
            return carry

        lax.fori_loop(0, (n - first) // ATT_TRIP, trip, 0)
        stage_pv(n - 2, (n - 2) % 2, ATT_QTILES)
        stage_softmax((n - 1) % 2, ATT_QTILES)
        stage_pv(n - 1, (n - 1) % 2, ATT_QTILES)
        oa_ref[...] = result(0)
        ob_ref[...] = result(1)


def _attention(qt, k, vt):
    cpb = ROWS_PER_BATCH // ATT_CHUNK
    assert cpb >= 3 and CTX_LEN == ATT_CHUNK and (TILES_PER_BATCH - 1) % ATT_QTILES == 0
    items = ATT_QTILES * MLA_HEADS

    def q_first(b, i):
        return (b * TILES_PER_BATCH + jnp.maximum(ATT_QTILES * i - 1, 0), 0, 0)

    def q_second(b, i):
        return (b * TILES_PER_BATCH + ATT_QTILES * i, 0, 0)

    out_spec = pl.BlockSpec((ROW_TILE, ATT_W), lambda b, i: (b * ATT_STEPS + i, 0))
    out_sds = jax.ShapeDtypeStruct((BATCH * ATT_STEPS * ROW_TILE, ATT_W), F32)
    return pl.pallas_call(
        _attn_kernel,
        grid=(BATCH, ATT_STEPS),
        in_specs=[pl.BlockSpec((1, QK_W, ROW_TILE), q_first),
                  pl.BlockSpec((1, QK_W, ROW_TILE), q_second),
                  pl.BlockSpec((ROWS_PER_BATCH, QK_W), lambda b, i: (b, 0), pipeline_mode=pl.Buffered(1)),
                  pl.BlockSpec((cpb, VT_W, ATT_CHUNK), lambda b, i: (b, 0, 0), pipeline_mode=pl.Buffered(1))],
        out_specs=(out_spec, out_spec),
        out_shape=(out_sds, out_sds),
        scratch_shapes=[pltpu.VMEM((items, ROW_TILE), F32),
                        pltpu.VMEM((items * V_AUG, ROW_TILE), F32),
                        pltpu.VMEM((2, items, ATT_CHUNK, ROW_TILE), F32),
                        pltpu.VMEM((2, items, ATT_CHUNK, ROW_TILE), BF16),
                        pltpu.VMEM((2, items, ROW_TILE), F32)],
        compiler_params=pltpu.CompilerParams(dimension_semantics=("parallel", "parallel"),
                                             vmem_limit_bytes=ATT_VMEM_LIMIT),
        name="attention",
    )(qt, qt, k, vt)


def _odd_post_kernel(atta_ref, attb_ref, p_ref, pp_ref, pn_ref, wp_ref, bp_ref, sp_ref, wout_ref,
                     x_ref, mod_ref, g1_ref, wr_ref, br_ref,
                     xo_ref, hx_ref, meta_ref, counts_ref, ext_ref, cnt_ref):
    is_start, is_end, seq_tile, is_ctx = _seq_flags()
    hp = SMALL_HALO
    ext_ref[0:hp] = jnp.where(is_start, 0.0, pp_ref[...])
    ext_ref[hp:hp + ROW_TILE] = p_ref[...]
    ext_ref[hp + ROW_TILE:2 * hp + ROW_TILE] = jnp.where(is_end, 0.0, pn_ref[...])
    seq_len = jnp.where(is_ctx, CTX_LEN, SEQ)
    pos = seq_tile * ROW_TILE + lax.broadcasted_iota(jnp.int32, (ROW_TILE, 1), 0)

    pooled = []
    for g, w in enumerate(POOL_WINDOWS):
        sl = slice(g * POOL_GROUP, (g + 1) * POOL_GROUP)
        ssum = jnp.zeros((ROW_TILE, POOL_GROUP), F32)
        for d in range(-(w // 2), w - w // 2):
            ssum = ssum + ext_ref[hp + d:hp + d + ROW_TILE, sl]
        lo = jnp.maximum(pos - w // 2, 0)
        hi = jnp.minimum(pos - w // 2 + w, seq_len)
        cnt = (hi - lo).astype(F32)
        pm = ssum / cnt - p_ref[:, sl]
        pooled.append(((_dot(pm.astype(BF16), wp_ref[g]) + bp_ref[g:g + 1, :]) * sp_ref[:, sl]).astype(BF16))
    w = pl.program_id(0) % TILES_PER_BATCH
    att = jnp.where((w > 0) & (w % ATT_QTILES == 0), attb_ref[...], atta_ref[...])
    mixed = jnp.concatenate([att.astype(BF16)] + pooled, axis=1)
    y = _dot(mixed, wout_ref[...])
    _residual_and_route(x_ref[...], y, mod_ref[0], g1_ref, wr_ref, br_ref,
                        xo_ref, hx_ref, meta_ref, counts_ref, cnt_ref)


def _odd_post(att_a, att_b, pool_in, w_pool, b_pool, s_pool, w_out, x, mod_i, g1, wr, br):
    pp_spec, pn_spec = _halo_specs(POOL_DIM, SMALL_HALO)
    ng = len(POOL_WINDOWS)
    att_spec = pl.BlockSpec((ROW_TILE, ATT_W), lambda t: ((t // TILES_PER_BATCH) * ATT_STEPS
                                                          + (t % TILES_PER_BATCH + 1) // ATT_QTILES, 0))
    return pl.pallas_call(
        _odd_post_kernel,
        grid=(N_TILES,),
        in_specs=[att_spec, att_spec, _tile_spec(POOL_DIM), pp_spec, pn_spec,
                  _full_spec((ng, POOL_GROUP, POOL_GROUP)), _full_spec((ng, POOL_GROUP)),
                  _full_spec((1, POOL_DIM)), _full_spec((D_MODEL, D_MODEL)),
                  _tile_spec(D_MODEL), _mod_spec(), _full_spec((1, D_MODEL)),
                  _full_spec((D_MODEL, 2 * ROUTE_W)),
                  _full_spec((1, ROUTE_W))],
        out_specs=_post_out_specs(),
        out_shape=_POST_OUT_SHAPES,
        scratch_shapes=[pltpu.VMEM((ROW_TILE + 2 * SMALL_HALO, POOL_DIM), F32), _class_count_scratch()],
        compiler_params=_cparams(("arbitrary",)),
        name="odd_post",
    )(att_a, att_b, pool_in, pool_in, pool_in, w_pool, b_pool, s_pool, w_out, x, mod_i, g1, wr, br)


def _moe_plan(meta, counts):
    cls = meta[:, CLS_LANE - META_ROW0, :].reshape(N_ROWS).astype(jnp.int32)
    rank = meta[:, RANK_LANE - META_ROW0, :].reshape(N_ROWS).astype(jnp.int32)
    tiles = (counts[0, :N_CLASSES].astype(jnp.int32) + MOE_TM - 1) // MOE_TM
    tile_end = jnp.cumsum(tiles)
    first_row = (tile_end - tiles) * MOE_TM
    dest = rank
    for c in range(N_CLASSES):
        dest = dest + jnp.where(cls == c, first_row[c], 0)
    j = jnp.arange(MOE_TILES)
    used = tile_end[-1]
    j_eff = jnp.minimum(j, jnp.maximum(used - 1, 0))
    cls_of_tile = jnp.sum((j_eff[:, None] >= tile_end[None, :]).astype(jnp.int32), axis=1)
    cls_of_tile = jnp.minimum(cls_of_tile, N_CLASSES - 1)
    grp = cls_of_tile // PAIRS_PER_GROUP
    pair = cls_of_tile % PAIRS_PER_GROUP
    e_a = grp * EXPERTS_PER_GROUP + jnp.array(PAIR_A, jnp.int32)[pair]
    e_b = grp * EXPERTS_PER_GROUP + jnp.array(PAIR_B, jnp.int32)[pair]
    status = jnp.where(j < used, 1 + jnp.array(PAIR_A_IS_HIGHER, jnp.int32)[pair], 0)
    new_a = jnp.concatenate([jnp.ones((1,), jnp.int32), (e_a[1:] != e_a[:-1]).astype(jnp.int32)])
    new_b = jnp.concatenate([jnp.ones((1,), jnp.int32), (e_b[1:] != e_b[:-1]).astype(jnp.int32)])
    return dest, e_a, e_b, status, new_a + 2 * new_b


def _move_rows(n_rows, base, dest_ref, row_copy, block_copy):
    def issue(i, carry):
        for u in range(DMA_UNROLL):
            r = i * DMA_UNROLL + u
            row_copy(r, dest_ref[base + r]).start(priority=u % 2)
        return carry

    lax.fori_loop(0, n_rows // DMA_UNROLL, issue, 0)
    block_copy.wait()


def _dispatch_kernel(dest_ref, hx_ref, init_ref, xs_ref, sem):
    del init_ref
    _move_rows(DISPATCH_ROWS, pl.program_id(0) * DISPATCH_ROWS, dest_ref,
               lambda r, d: pltpu.make_async_copy(hx_ref.at[pl.ds(r, 1)], xs_ref.at[pl.ds(d, 1)], sem),
               pltpu.make_async_copy(hx_ref, xs_ref.at[pl.ds(0, DISPATCH_ROWS)], sem))


def _dispatch(dest, hx, init):
    shape = (MOE_ROWS, HX_W)
    return pl.pallas_call(
        _dispatch_kernel,
        grid_spec=pltpu.PrefetchScalarGridSpec(
            num_scalar_prefetch=1,
            grid=(N_ROWS // DISPATCH_ROWS,),
            in_specs=[pl.BlockSpec((DISPATCH_ROWS, HX_W), lambda t, dest: (t, 0)),
                      pl.BlockSpec(memory_space=pl.ANY)],
            out_specs=pl.BlockSpec(memory_space=pl.ANY),
            scratch_shapes=[pltpu.SemaphoreType.DMA(())]),
        out_shape=jax.ShapeDtypeStruct(shape, F32),
        input_output_aliases={2: 0},
        compiler_params=_cparams(("arbitrary",)),
        name="moe_dispatch",
    )(dest, hx, init)


def _moe_expert_kernel(ea_ref, eb_ref, status_ref, fresh_ref, xs_ref, w1a_ref, w3a_ref, w2a_ref,
                       w1b_ref, w3b_ref, w2b_ref, ys_ref, *bf_refs):
    del ea_ref, eb_ref
    status = status_ref[pl.program_id(0)]
    fresh = fresh_ref[pl.program_id(0)]
    slot_a, slot_b = bf_refs[0:3], bf_refs[3:6]

    @pl.when(fresh % 2 == 1)
    def _():
        for dst, src in zip(slot_a, (w1a_ref, w3a_ref, w2a_ref)):
            dst[...] = src[0, 0].astype(BF16)

    @pl.when(fresh >= 2)
    def _():
        for dst, src in zip(slot_b, (w1b_ref, w3b_ref, w2b_ref)):
            dst[...] = src[0, 0].astype(BF16)

    @pl.when(status == 0)
    def _():
        ys_ref[...] = jnp.zeros(ys_ref.shape, F32)

    @pl.when(status != 0)
    def _():
        h = xs_ref[:, 0:D_MODEL].astype(BF16)
        g_lo = xs_ref[:, D_MODEL + GLO_LANE:D_MODEL + GLO_LANE + 1]
        g_hi = xs_ref[:, D_MODEL + GHI_LANE:D_MODEL + GHI_LANE + 1]
        a_is_higher = status == 2
        y = jnp.zeros((MOE_TM, D_MODEL), F32)
        for g, (w1_ref, w3_ref, w2_ref) in ((jnp.where(a_is_higher, g_hi, g_lo), slot_a),
                                            (jnp.where(a_is_higher, g_lo, g_hi), slot_b)):
            hid = _silu(_dot(h, w1_ref[...])) * _dot(h, w3_ref[...])
            y = y + g * _dot(hid.astype(BF16), w2_ref[...])
        ys_ref[...] = y


def _moe_experts(e_a, e_b, status, fresh, xs_sorted, w1, w3, w2, layer):
    up_a = pl.BlockSpec((1, 1, D_MODEL, D_EXPERT), lambda j, ea, eb, status, fresh: (layer, ea[j], 0, 0))
    dn_a = pl.BlockSpec((1, 1, D_EXPERT, D_MODEL), lambda j, ea, eb, status, fresh: (layer, ea[j], 0, 0))
    up_b = pl.BlockSpec((1, 1, D_MODEL, D_EXPERT), lambda j, ea, eb, status, fresh: (layer, eb[j], 0, 0))
    dn_b = pl.BlockSpec((1, 1, D_EXPERT, D_MODEL), lambda j, ea, eb, status, fresh: (layer, eb[j], 0, 0))
    up_bf = pltpu.VMEM((D_MODEL, D_EXPERT), BF16)
    dn_bf = pltpu.VMEM((D_EXPERT, D_MODEL), BF16)
    return pl.pallas_call(
        _moe_expert_kernel,
        grid_spec=pltpu.PrefetchScalarGridSpec(
            num_scalar_prefetch=4,
            grid=(MOE_TILES,),
            in_specs=[pl.BlockSpec((MOE_TM, HX_W), lambda j, ea, eb, status, fresh: (j, 0)),
                      up_a, up_a, dn_a, up_b, up_b, dn_b],
            out_specs=pl.BlockSpec((MOE_TM, D_MODEL), lambda j, ea, eb, status, fresh: (j, 0)),
            scratch_shapes=[up_bf, up_bf, dn_bf, up_bf, up_bf, dn_bf]),
        out_shape=jax.ShapeDtypeStruct((MOE_ROWS, D_MODEL), F32),
        compiler_params=_cparams(("arbitrary",)),
        name="moe_experts",
    )(e_a, e_b, status, fresh, xs_sorted, w1, w3, w2, w1, w3, w2)


def _combine_kernel(dest_ref, x_ref, mod_ref, ys_ref, o_ref, buf_ref, sem):
    t = pl.program_id(0)
    _move_rows(COMBINE_ROWS, t * COMBINE_ROWS, dest_ref,
               lambda r, d: pltpu.make_async_copy(ys_ref.at[pl.ds(d, 1)], buf_ref.at[pl.ds(r, 1)], sem),
               pltpu.make_async_copy(ys_ref.at[pl.ds(0, COMBINE_ROWS)], buf_ref, sem))
    sub = COMBINE_ROWS // ROW_TILE
    for b in range(sub):
        rs = slice(b * ROW_TILE, (b + 1) * ROW_TILE)
        m5 = mod_ref[pl.ds(_mod_row(t * sub + b), 1)][0, 5:6]
        o_ref[rs, :] = x_ref[rs, :] + m5 * buf_ref[rs, :]


def _combine(dest, x, mod_i, ys):
    return pl.pallas_call(
        _combine_kernel,
        grid_spec=pltpu.PrefetchScalarGridSpec(
            num_scalar_prefetch=1,
            grid=(N_ROWS // COMBINE_ROWS,),
            in_specs=[pl.BlockSpec((COMBINE_ROWS, D_MODEL), lambda t, dest: (t, 0)),
                      pl.BlockSpec((MOD_ROWS, 6, D_MODEL), lambda t, dest: (0, 0, 0)),
                      pl.BlockSpec(memory_space=pl.ANY)],
            out_specs=pl.BlockSpec((COMBINE_ROWS, D_MODEL), lambda t, dest: (t, 0)),
            scratch_shapes=[pltpu.VMEM((COMBINE_ROWS, D_MODEL), F32), pltpu.SemaphoreType.DMA(())]),
        out_shape=jax.ShapeDtypeStruct((N_ROWS, D_MODEL), F32),
        compiler_params=_cparams(("arbitrary",)),
        name="moe_combine",
    )(dest, x, mod_i, ys)


def _combine_final_kernel(dest_ref, x_ref, mod_ref, g_ref, ys_ref, o_ref, buf_ref, sem):
    _move_rows(ROW_TILE, pl.program_id(0) * ROW_TILE, dest_ref,
               lambda r, d: pltpu.make_async_copy(ys_ref.at[pl.ds(d, 1)], buf_ref.at[pl.ds(r, 1)], sem),
               pltpu.make_async_copy(ys_ref.at[pl.ds(0, ROW_TILE)], buf_ref, sem))
    x = x_ref[...] + mod_ref[0][5:6] * buf_ref[...]
    o_ref[0] = x * lax.rsqrt(jnp.mean(x * x, axis=-1, keepdims=True) + EPS) * g_ref[...]


def _combine_final(dest, x, mod_i, ys, g):
    def out_map(t, dest):
        return (t // TILES_PER_BATCH, jnp.maximum(t % TILES_PER_BATCH - 1, 0), 0)

    return pl.pallas_call(
        _combine_final_kernel,
        grid_spec=pltpu.PrefetchScalarGridSpec(
            num_scalar_prefetch=1,
            grid=(N_TILES,),
            in_specs=[pl.BlockSpec((ROW_TILE, D_MODEL), lambda t, dest: (t, 0)),
                      pl.BlockSpec((1, 6, D_MODEL), lambda t, dest: (_mod_row(t), 0, 0)),
                      pl.BlockSpec((1, D_MODEL), lambda t, dest: (0, 0)),
                      pl.BlockSpec(memory_space=pl.ANY)],
            out_specs=pl.BlockSpec((1, ROW_TILE, D_MODEL), out_map),
            scratch_shapes=[pltpu.VMEM((ROW_TILE, D_MODEL), F32), pltpu.SemaphoreType.DMA(())]),
        out_shape=jax.ShapeDtypeStruct((BATCH, SEQ, D_MODEL), F32),
        compiler_params=_cparams(("arbitrary",)),
        name="moe_combine_final",
    )(dest, x, mod_i, g, ys)


_SWAP32 = tuple(list(range(8, 16)) + list(range(0, 8)) + list(range(24, 32)) + list(range(16, 24)))


def _rope_tables():
    f32 = np.float32
    rows = SEQ // GRID_W
    pos_row = np.repeat(np.arange(rows, dtype=f32), GRID_W)
    pos_col = np.tile(np.arange(GRID_W, dtype=f32), rows)
    inv = (f32(ROPE_BASE) ** (-np.arange(0, AXIS_DIM, 2, dtype=f32) / f32(AXIS_DIM))).astype(f32)
    ang = np.concatenate([pos_row[:, None] * inv, pos_col[:, None] * inv], axis=-1).astype(f32)
    cos, sin = np.cos(ang).astype(f32), np.sin(ang).astype(f32)
    p = AXIS_PAIRS
    c32 = np.concatenate([cos[:, :p], cos[:, :p], cos[:, p:], cos[:, p:]], axis=-1)
    s32 = np.concatenate([-sin[:, :p], sin[:, :p], -sin[:, p:], sin[:, p:]], axis=-1)
    pad = HEAD_PAD - MLA_NOPE - MLA_ROPE
    cos_lat = np.concatenate([np.ones((SEQ, MLA_NOPE), f32), c32, np.zeros((SEQ, pad), f32)], axis=-1)
    sin_lat = np.concatenate([np.zeros((SEQ, MLA_NOPE), f32), s32, np.zeros((SEQ, pad), f32)], axis=-1)
    cos_ctx = np.concatenate([np.ones((CTX_LEN, MLA_NOPE + MLA_ROPE), f32), np.zeros((CTX_LEN, pad), f32)], axis=-1)
    sin_ctx = np.zeros((CTX_LEN, HEAD_PAD), f32)
    return (jnp.asarray(np.concatenate([cos_ctx, cos_lat], axis=0)),
            jnp.asarray(np.concatenate([sin_ctx, sin_lat], axis=0)))


def _odd_weights(w_in, w_uq, w_ukv):
    swap = jnp.array(_SWAP32)
    o_kv, o_kr, o_pool = Q_LORA, Q_LORA + KV_LORA, Q_LORA + KV_LORA + MLA_ROPE
    w_kr = w_in[:, o_kr:o_pool]
    pad_l = jnp.zeros((D_MODEL, MLA_NOPE), F32)
    pad_r = jnp.zeros((D_MODEL, HEAD_PAD - MLA_NOPE - MLA_ROPE), F32)
    w_in2 = jnp.concatenate([w_in[:, :o_kr], pad_l, w_kr, pad_r, pad_l, w_kr[:, swap], pad_r,
                             w_in[:, o_pool:]], axis=-1).astype(BF16)
    wq = w_uq.reshape(Q_LORA, MLA_HEADS, MLA_NOPE + MLA_ROPE)
    zq = jnp.zeros((Q_LORA, MLA_HEADS, HEAD_PAD - MLA_NOPE - MLA_ROPE), F32)
    wq_main = jnp.concatenate([wq, zq], axis=-1).reshape(Q_LORA, QK_W).astype(BF16)
    wq_swap = jnp.concatenate([jnp.zeros((Q_LORA, MLA_HEADS, MLA_NOPE), F32),
                               wq[:, :, MLA_NOPE:][:, :, swap], zq], axis=-1).reshape(Q_LORA, QK_W).astype(BF16)
    wkv = w_ukv.reshape(KV_LORA, MLA_HEADS, MLA_NOPE + MLA_V)
    zk = jnp.zeros((KV_LORA, MLA_HEADS, HEAD_PAD - MLA_NOPE), F32)
    wk = jnp.concatenate([wkv[:, :, :MLA_NOPE], zk], axis=-1).reshape(KV_LORA, QK_W).astype(BF16)
    zv = jnp.zeros((KV_LORA, MLA_HEADS, V_AUG - MLA_V), F32)
    wv = jnp.concatenate([wkv[:, :, MLA_NOPE:], zv], axis=-1).reshape(KV_LORA, VT_W).astype(BF16)
    return w_in2, wq_main, wq_swap, wk, wv


def kernel(x, c, ctx, c_ctx, w_mod, b_mod, norm_g, ev_w_in, ev_conv_a_w, ev_conv_a_b, ev_ln_a_g, ev_ln_a_b,
           ev_conv_b_w, ev_w_out, od_w_in, od_q_norm_g, od_w_uq, od_kv_norm_g, od_w_ukv, od_w_pool, od_b_pool,
           od_s_pool, od_w_out, moe_wg, moe_bg, moe_we, moe_be, moe_w1, moe_w3, moe_w2, final_g):
    assert x.shape == (BATCH, SEQ, D_MODEL) and ctx.shape == (BATCH, CTX_LEN, D_MODEL)
    assert CTX_LEN == ROW_TILE
    stream = (x, ctx)
    cvec = jnp.concatenate([c, c_ctx[None, :], jnp.zeros((MOD_ROWS - BATCH - 1, D_MODEL), F32)], axis=0)
    mod = _modulation(cvec, w_mod, b_mod).reshape(DEPTH, MOD_ROWS, 6, D_MODEL)
    cos_t, sin_t = _rope_tables()
    sorted_rows = jnp.zeros((MOE_ROWS, HX_W), F32)

    for i in range(DEPTH):
        j = i // 2
        mod_i = mod[i]
        g0 = norm_g[i, 0][None, :]
        g1 = norm_g[i, 1][None, :]
        wr = jnp.concatenate([moe_wg[i], moe_we[i],
                              jnp.zeros((D_MODEL, ROUTE_W - N_GROUPS - N_EXPERTS), F32)], axis=-1)
        wr_hi = wr.astype(BF16)
        wr = jnp.concatenate([wr_hi, (wr - wr_hi.astype(F32)).astype(BF16)], axis=-1)
        br = jnp.concatenate([moe_bg[i], moe_be[i],
                              jnp.zeros((ROUTE_W - N_GROUPS - N_EXPERTS,), F32)])[None, :]
        if i % 2 == 0:
            a_pre, gb, gch = _even_pre(stream, mod_i, g0, ev_w_in[j].astype(BF16))
            xs, hx, meta, counts = _even_post(stream, a_pre, gb, gch, ev_conv_a_w[j], ev_conv_a_b[j][None, :],
                                              ev_ln_a_g[j][None, :], ev_ln_a_b[j][None, :], ev_conv_b_w[j],
                                              ev_w_out[j].astype(BF16), mod_i, g1, wr, br)
        else:
            w_in2, wq_main, wq_swap, wk, wv = _odd_weights(od_w_in[j], od_w_uq[j], od_w_ukv[j])
            q, k, v, pool_in = _odd_pre(xs, mod_i, g0, w_in2, od_q_norm_g[j][None, :], wq_main, wq_swap,
                                        od_kv_norm_g[j][None, :], wk, wv, cos_t, sin_t)
            att_a, att_b = _attention(q, k, v)
            xs, hx, meta, counts = _odd_post(att_a, att_b, pool_in, od_w_pool[j].astype(BF16), od_b_pool[j],
                                      od_s_pool[j][None, :], od_w_out[j].astype(BF16), xs, mod_i, g1,
                                      wr, br)
        dest, e_a, e_b, status, fresh = _moe_plan(meta, counts)
        sorted_rows = _dispatch(dest, hx, sorted_rows)
        ys = _moe_experts(e_a, e_b, status, fresh, sorted_rows, moe_w1, moe_w3, moe_w2, i)
        if i == DEPTH - 1:
            return _combine_final(dest, xs, mod_i, ys, final_g[None, :])
        xs = _combine(dest, xs, mod_i, ys)
        stream = (xs,)
```

```python
import functools

import numpy as np
import jax
import jax.numpy as jnp
from jax import lax
from jax.experimental import pallas as pl
from jax.experimental.pallas import tpu as pltpu

F32 = jnp.float32
BF16 = jnp.bfloat16

D_MODEL = 1024
BATCH = 2
SEQ = 8192
DEPTH = 4
GRID_W = 64
CTX_LEN = 256
EPS = 1e-6
CONV_A_DIM = 512
CONV_A_WIDTH = 31
CONV_B_DIM = 512
CONV_B_WIDTH = 3
MLA_HEADS = 8
MLA_NOPE = 64
MLA_ROPE = 32
MLA_V = 64
Q_LORA = 384
KV_LORA = 256
MLA_SCALE = (MLA_NOPE + MLA_ROPE) ** -0.5
AXIS_DIM = MLA_ROPE // 2
AXIS_PAIRS = AXIS_DIM // 2
ROPE_BASE = 10000.0
POOL_WINDOWS = (2, 4, 8, 16)
POOL_GROUP = 128
POOL_DIM = POOL_GROUP * len(POOL_WINDOWS)
N_GROUPS = 4
EXPERTS_PER_GROUP = 4
N_EXPERTS = 16
D_EXPERT = 512

LANES = 128
SUBLANES = 8
VMEM_LIMIT = 48 * 1024 * 1024

ROW_TILE = 256
ROWS_PER_BATCH = CTX_LEN + SEQ
N_ROWS = BATCH * ROWS_PER_BATCH
TILES_PER_BATCH = ROWS_PER_BATCH // ROW_TILE
N_TILES = N_ROWS // ROW_TILE
HEAD_PAD = LANES
QK_W = MLA_HEADS * HEAD_PAD
ATT_W = MLA_HEADS * MLA_V
V_AUG = MLA_V + 16
VT_W = MLA_HEADS * V_AUG
CONV_A_HALO = 16
SMALL_HALO = 8
ATT_CHUNK = 256
SOFTMAX_ROWS = 64
ATT_TRIP = 2
ATT_QTILES = 2
ATT_STEPS = 1 + (TILES_PER_BATCH - 1) // ATT_QTILES
ATT_VMEM_LIMIT = 56 * 1024 * 1024
LOG2E = 1.4426950408889634
ROUTE_W = LANES
GATE_LANE0 = N_GROUPS
CLS_LANE = GATE_LANE0 + N_EXPERTS
RANK_LANE = CLS_LANE + 1
GLO_LANE = RANK_LANE + 1
GHI_LANE = GLO_LANE + 1
META_ROW0 = (CLS_LANE // SUBLANES) * SUBLANES
HX_W = D_MODEL + ROUTE_W
PAIRS_PER_GROUP = 6
N_CLASSES = N_GROUPS * PAIRS_PER_GROUP
PAIR_A = (0, 0, 0, 1, 1, 3)
PAIR_B = (1, 2, 3, 3, 2, 2)
PAIR_A_IS_HIGHER = (0, 0, 0, 0, 0, 1)
MOE_TM = 256
MOE_TILES = -(-(N_ROWS + N_CLASSES * (MOE_TM - 1)) // MOE_TM)
MOE_ROWS = MOE_TILES * MOE_TM
DISPATCH_ROWS = 2816
COMBINE_ROWS = 1536
DMA_UNROLL = 32
MOD_ROWS = 8


def _mod_row(t):
    return jnp.where(t % TILES_PER_BATCH == 0, BATCH, t // TILES_PER_BATCH)


def _cparams(sem):
    return pltpu.CompilerParams(dimension_semantics=sem, vmem_limit_bytes=VMEM_LIMIT)


def _dot(a, b):
    return jnp.dot(a, b, preferred_element_type=F32)


def _sigmoid(x):
    return 1.0 / (1.0 + jnp.exp(-x))


def _silu(x):
    return x * _sigmoid(x)


def _norm_mod(x, g, shift, scale):
    y = x * lax.rsqrt(jnp.mean(x * x, axis=-1, keepdims=True) + EPS)
    return (y * g) * (1.0 + scale) + shift


def _mod_kernel(c_ref, w_ref, b_ref, o_ref):
    s = _silu(c_ref[...])
    o_ref[0] = jnp.dot(s, w_ref[0], preferred_element_type=F32,
                       precision=lax.Precision.HIGHEST) + b_ref[0]


def _modulation(cvec, w_mod, b_mod):
    nblk = 4
    bw = 6 * D_MODEL // nblk
    return pl.pallas_call(
        _mod_kernel,
        grid=(DEPTH, nblk),
        in_specs=[pl.BlockSpec((MOD_ROWS, D_MODEL), lambda i, j: (0, 0)),
                  pl.BlockSpec((1, D_MODEL, bw), lambda i, j: (i, 0, j)),
                  pl.BlockSpec((1, 1, bw), lambda i, j: (i, 0, j))],
        out_specs=pl.BlockSpec((1, MOD_ROWS, bw), lambda i, j: (i, 0, j)),
        out_shape=jax.ShapeDtypeStruct((DEPTH, MOD_ROWS, 6 * D_MODEL), F32),
        compiler_params=_cparams(("parallel", "parallel")),
        name="modulation",
    )(cvec, w_mod, b_mod.reshape(DEPTH, 1, 6 * D_MODEL))


def _route(h2, wr_ref, br_ref):
    hi = h2.astype(BF16)
    lo = (h2 - hi.astype(F32)).astype(BF16)
    both = _dot(hi, wr_ref[...])
    logits = (both[:, 0:ROUTE_W] + both[:, ROUTE_W:2 * ROUTE_W] + _dot(lo, wr_ref[:, 0:ROUTE_W])
              + br_ref[...])
    lane = lax.broadcasted_iota(jnp.int32, logits.shape, 1)
    lane_f = lane.astype(F32)
    neg = -jnp.inf
    big = float(ROUTE_W)
    gl = jnp.where(lane < N_GROUPS, logits, neg)
    gmax = jnp.max(gl, axis=-1, keepdims=True)
    gidx = jnp.min(jnp.where(gl == gmax, lane_f, big), axis=-1, keepdims=True)
    g_w = 1.0 / jnp.sum(jnp.exp(gl - gmax), axis=-1, keepdims=True)
    egrp = ((lane - GATE_LANE0) // EXPERTS_PER_GROUP).astype(F32)
    in_g = (lane >= GATE_LANE0) & (lane < GATE_LANE0 + N_EXPERTS) & (egrp == gidx)
    el = jnp.where(in_g, logits, neg)
    v1 = jnp.max(el, axis=-1, keepdims=True)
    i1 = jnp.min(jnp.where(el == v1, lane_f, big), axis=-1, keepdims=True)
    el2 = jnp.where(lane_f == i1, neg, el)
    v2 = jnp.max(el2, axis=-1, keepdims=True)
    i2 = jnp.min(jnp.where(el2 == v2, lane_f, big), axis=-1, keepdims=True)
    e21 = jnp.exp(v2 - v1)
    w1 = 1.0 / (1.0 + e21)
    w2 = e21 * w1
    g1 = w1 * g_w
    g2 = w2 * g_w
    gates = jnp.where(lane_f == i1, g1, jnp.where(lane_f == i2, g2, 0.0))
    base = GATE_LANE0 + gidx * EXPERTS_PER_GROUP
    lo = jnp.minimum(i1, i2) - base
    hi = jnp.maximum(i1, i2) - base
    pair = jnp.where(lo == 0.0, hi - 1.0, jnp.where(lo == 1.0, 6.0 - hi, 5.0))
    cls = gidx * PAIRS_PER_GROUP + pair
    g_lo = jnp.where(i1 < i2, g1, g2)
    g_hi = jnp.where(i1 < i2, g2, g1)
    return jnp.where(lane_f == float(GLO_LANE), g_lo, jnp.where(lane_f == float(GHI_LANE), g_hi, gates)), cls


def _residual_and_route(x, y, m, g1_ref, wr_ref, br_ref,
                        xo_ref, hx_ref, meta_ref, counts_ref, cnt_ref):
    @pl.when(pl.program_id(0) == 0)
    def _():
        cnt_ref[...] = jnp.zeros(cnt_ref.shape, F32)

    x_new = x + m[2:3] * y
    xo_ref[...] = x_new
    h2 = _norm_mod(x_new, g1_ref[...], m[3:4], m[4:5])
    hx_ref[:, 0:D_MODEL] = h2
    route, cls = _route(h2, wr_ref, br_ref)
    lane_f = lax.broadcasted_iota(jnp.int32, route.shape, 1).astype(F32)
    onehot = jnp.where(lane_f == cls, 1.0, 0.0)
    row = lax.broadcasted_iota(jnp.int32, (ROW_TILE, ROW_TILE), 0)
    col = lax.broadcasted_iota(jnp.int32, (ROW_TILE, ROW_TILE), 1)
    earlier = jnp.where(col < row, 1.0, 0.0).astype(BF16)
    before = _dot(earlier, onehot.astype(BF16)) + cnt_ref[...]
    rank = jnp.sum(onehot * before, axis=-1, keepdims=True)
    cnt_ref[...] = cnt_ref[...] + jnp.sum(onehot, axis=0, keepdims=True)
    counts_ref[...] = cnt_ref[...]
    route = jnp.where(lane_f == float(CLS_LANE), cls, jnp.where(lane_f == float(RANK_LANE), rank, route))
    hx_ref[:, D_MODEL:HX_W] = route
    meta_ref[0] = route.T[META_ROW0:META_ROW0 + SUBLANES, :]


def _tile_spec(width):
    return pl.BlockSpec((ROW_TILE, width), lambda t: (t, 0))


def _stream_specs(n_stream):
    if n_stream == 1:
        return [_tile_spec(D_MODEL)]
    return [pl.BlockSpec((1, ROW_TILE, D_MODEL),
                         lambda t: (t // TILES_PER_BATCH, jnp.maximum(t % TILES_PER_BATCH - 1, 0), 0)),
            pl.BlockSpec((1, CTX_LEN, D_MODEL), lambda t: (t // TILES_PER_BATCH, 0, 0))]


def _load_stream(refs):
    if len(refs) == 1:
        return refs[0][...]
    is_ctx = pl.program_id(0) % TILES_PER_BATCH == 0
    return jnp.where(is_ctx, refs[1][0], refs[0][0])


def _full_spec(shape):
    nd = len(shape)
    return pl.BlockSpec(shape, lambda t: (0,) * nd)


def _mod_spec():
    return pl.BlockSpec((1, 6, D_MODEL), lambda t: (_mod_row(t), 0, 0))


def _halo_specs(width, halo):
    per_tile = ROW_TILE // halo
    last = N_ROWS // halo - 1
    prev = pl.BlockSpec((halo, width), lambda t: (jnp.maximum(t * per_tile - 1, 0), 0))
    nxt = pl.BlockSpec((halo, width), lambda t: (jnp.minimum((t + 1) * per_tile, last), 0))
    return prev, nxt


def _seq_flags():
    w = pl.program_id(0) % TILES_PER_BATCH
    is_start = (w == 0) | (w == 1)
    is_end = (w == 0) | (w == TILES_PER_BATCH - 1)
    seq_tile = jnp.maximum(w - 1, 0)
    return is_start, is_end, seq_tile, w == 0


_POST_OUT_SHAPES = (jax.ShapeDtypeStruct((N_ROWS, D_MODEL), F32),
                    jax.ShapeDtypeStruct((N_ROWS, HX_W), F32),
                    jax.ShapeDtypeStruct((N_TILES, SUBLANES, ROW_TILE), F32),
                    jax.ShapeDtypeStruct((1, ROUTE_W), F32))


def _post_out_specs():
    return (_tile_spec(D_MODEL),
            _tile_spec(HX_W),
            pl.BlockSpec((1, SUBLANES, ROW_TILE), lambda t: (t, 0, 0)),
            pl.BlockSpec((1, ROUTE_W), lambda t: (0, 0)))


def _class_count_scratch():
    return pltpu.VMEM((1, ROUTE_W), F32)


def _even_pre_kernel(n_stream, *refs):
    mod_ref, g_ref, w_ref, a_ref, gb_ref, gch_ref = refs[n_stream:]
    m = mod_ref[0]
    h = _norm_mod(_load_stream(refs[:n_stream]), g_ref[...], m[0:1], m[1:2])
    u = _dot(h.astype(BF16), w_ref[...])
    ca, cb = CONV_A_DIM, CONV_B_DIM
    a_ref[...] = u[:, 0:ca] * _sigmoid(u[:, ca:2 * ca])
    gb_ref[...] = u[:, 2 * ca:2 * ca + cb]
    gch_ref[...] = u[:, 2 * ca + cb:2 * ca + 2 * cb] * u[:, 2 * ca + 2 * cb:2 * ca + 3 * cb]


def _even_pre(stream, mod_i, g0, w_in):
    n_in = w_in.shape[1]
    return pl.pallas_call(
        functools.partial(_even_pre_kernel, len(stream)),
        grid=(N_TILES,),
        in_specs=_stream_specs(len(stream)) + [_mod_spec(), _full_spec((1, D_MODEL)),
                                               _full_spec((D_MODEL, n_in))],
        out_specs=(_tile_spec(CONV_A_DIM), _tile_spec(CONV_B_DIM), _tile_spec(CONV_B_DIM)),
        out_shape=(jax.ShapeDtypeStruct((N_ROWS, CONV_A_DIM), F32),
                   jax.ShapeDtypeStruct((N_ROWS, CONV_B_DIM), F32),
                   jax.ShapeDtypeStruct((N_ROWS, CONV_B_DIM), F32)),
        compiler_params=_cparams(("parallel",)),
        name="even_pre",
    )(*stream, mod_i, g0, w_in)


def _even_post_kernel(n_stream, *refs):
    (a_ref, ap_ref, an_ref, gch_ref, gp_ref, gn_ref, gb_ref,
     cwa_ref, cba_ref, lng_ref, lnb_ref, cwb_ref, wout_ref,
     mod_ref, g1_ref, wr_ref, br_ref,
     xo_ref, hx_ref, meta_ref, counts_ref, exta_ref, extb_ref, cnt_ref) = refs[n_stream:]
    is_start, is_end, _, _ = _seq_flags()
    ha, hb = CONV_A_HALO, SMALL_HALO
    exta_ref[0:ha] = jnp.where(is_start, 0.0, ap_ref[...])
    exta_ref[ha:ha + ROW_TILE] = a_ref[...]
    exta_ref[ha + ROW_TILE:2 * ha + ROW_TILE] = jnp.where(is_end, 0.0, an_ref[...])
    extb_ref[0:hb] = jnp.where(is_start, 0.0, gp_ref[...])
    extb_ref[hb:hb + ROW_TILE] = gch_ref[...]
    extb_ref[hb + ROW_TILE:2 * hb + ROW_TILE] = jnp.where(is_end, 0.0, gn_ref[...])

    half_a = (CONV_A_WIDTH - 1) // 2
    lead = ha - half_a
    span = ROW_TILE + SUBLANES
    assert 0 <= lead and lead + SUBLANES - 1 <= SUBLANES
    parts = []
    for cblk in range(CONV_A_DIM // LANES):
        cs = slice(cblk * LANES, (cblk + 1) * LANES)
        acc = jnp.zeros((ROW_TILE, LANES), F32) + cba_ref[:, cs]
        for b in range(SUBLANES):
            z = jnp.zeros((span, LANES), F32)
            for k in range(b, CONV_A_WIDTH, SUBLANES):
                z = z + cwa_ref[k:k + 1, cs] * exta_ref[k - b:k - b + span, cs]
            acc = acc + z[lead + b:lead + b + ROW_TILE, :]
        parts.append(acc)
    acc = jnp.concatenate(parts, axis=1)
    mu = jnp.mean(acc, axis=-1, keepdims=True)
    xc = acc - mu
    a = xc * lax.rsqrt(jnp.mean(xc * xc, axis=-1, keepdims=True) + EPS)
    a = _silu(a * lng_ref[...] + lnb_ref[...])

    half_b = (CONV_B_WIDTH - 1) // 2
    cb = jnp.zeros((ROW_TILE, CONV_B_DIM), F32)
    for k in range(CONV_B_WIDTH):
        off = hb - half_b + k
        cb = cb + cwb_ref[k:k + 1, :] * extb_ref[off:off + ROW_TILE, :]
    b = gb_ref[...] * cb

    y = (_dot(a.astype(BF16), wout_ref[0:CONV_A_DIM, :])
         + _dot(b.astype(BF16), wout_ref[CONV_A_DIM:CONV_A_DIM + CONV_B_DIM, :]))
    _residual_and_route(_load_stream(refs[:n_stream]), y, mod_ref[0], g1_ref, wr_ref, br_ref,
                        xo_ref, hx_ref, meta_ref, counts_ref, cnt_ref)


def _even_post(stream, a_pre, gb, gch, cwa, cba, lng, lnb, cwb, w_out, mod_i, g1, wr, br):
    ap_spec, an_spec = _halo_specs(CONV_A_DIM, CONV_A_HALO)
    gp_spec, gn_spec = _halo_specs(CONV_B_DIM, SMALL_HALO)
    return pl.pallas_call(
        functools.partial(_even_post_kernel, len(stream)),
        grid=(N_TILES,),
        in_specs=_stream_specs(len(stream)) + [
                  _tile_spec(CONV_A_DIM), ap_spec, an_spec,
                  _tile_spec(CONV_B_DIM), gp_spec, gn_spec, _tile_spec(CONV_B_DIM),
                  _full_spec((CONV_A_WIDTH, CONV_A_DIM)), _full_spec((1, CONV_A_DIM)),
                  _full_spec((1, CONV_A_DIM)), _full_spec((1, CONV_A_DIM)),
                  _full_spec((CONV_B_WIDTH, CONV_B_DIM)), _full_spec((D_MODEL, D_MODEL)),
                  _mod_spec(), _full_spec((1, D_MODEL)),
                  _full_spec((D_MODEL, 2 * ROUTE_W)),
                  _full_spec((1, ROUTE_W))],
        out_specs=_post_out_specs(),
        out_shape=_POST_OUT_SHAPES,
        scratch_shapes=[pltpu.VMEM((ROW_TILE + 2 * CONV_A_HALO, CONV_A_DIM), F32),
                        pltpu.VMEM((ROW_TILE + 2 * SMALL_HALO, CONV_B_DIM), F32),
                        _class_count_scratch()],
        compiler_params=_cparams(("arbitrary",)),
        name="even_post",
    )(*stream, a_pre, a_pre, a_pre, gch, gch, gch, gb, cwa, cba, lng, lnb, cwb, w_out,
      mod_i, g1, wr, br)


def _odd_pre_kernel(x_ref, mod_ref, g_ref, w_ref, gq_ref, wqm_ref, wqs_ref, gkv_ref, wk_ref, wv_ref,
                    vaug_ref, cos_ref, sin_ref, qt_ref, k_ref, vt_ref, pool_ref):
    m = mod_ref[0]
    h = _norm_mod(x_ref[...], g_ref[...], m[0:1], m[1:2])
    u = _dot(h.astype(BF16), w_ref[...])
    o_kv = Q_LORA
    o_kr = o_kv + KV_LORA
    o_krs = o_kr + HEAD_PAD
    o_pool = o_krs + HEAD_PAD
    cos = cos_ref[...]
    sin = sin_ref[...]

    q_c = u[:, 0:o_kv]
    qn = (q_c * lax.rsqrt(jnp.mean(q_c * q_c, axis=-1, keepdims=True) + EPS) * gq_ref[...]).astype(BF16)
    q_main = _dot(qn, wqm_ref[...])
    q_swap = _dot(qn, wqs_ref[...])
    kv_c = u[:, o_kv:o_kr]
    kvn = (kv_c * lax.rsqrt(jnp.mean(kv_c * kv_c, axis=-1, keepdims=True) + EPS) * gkv_ref[...]).astype(BF16)
    k_nope = _dot(kvn, wk_ref[...])
    v = _dot(kvn, wv_ref[...]) + vaug_ref[...]
    for cc in range(ROW_TILE // ATT_CHUNK):
        vt_ref[cc] = v[cc * ATT_CHUNK:(cc + 1) * ATT_CHUNK, :].T.astype(BF16)
    k_rope = u[:, o_kr:o_krs] * cos + u[:, o_krs:o_pool] * sin
    for hd in range(MLA_HEADS):
        sl = slice(hd * HEAD_PAD, (hd + 1) * HEAD_PAD)
        qh = (q_main[:, sl] * cos + q_swap[:, sl] * sin) * (MLA_SCALE * LOG2E)
        qt_ref[0, sl, :] = qh.T.astype(BF16)
        k_ref[:, sl] = (k_nope[:, sl] + k_rope).astype(BF16)
    pool_ref[...] = u[:, o_pool:o_pool + POOL_DIM]


def _odd_pre(x, mod_i, g0, w_in, gq, wqm, wqs, gkv, wk, wv, cos_t, sin_t):
    n_in = w_in.shape[1]
    vaug = jnp.zeros((MLA_HEADS, V_AUG), F32).at[:, MLA_V].set(1.0).reshape(1, VT_W)
    return pl.pallas_call(
        _odd_pre_kernel,
        grid=(N_TILES,),
        in_specs=[_tile_spec(D_MODEL), _mod_spec(), _full_spec((1, D_MODEL)),
                  _full_spec((D_MODEL, n_in)), _full_spec((1, Q_LORA)),
                  _full_spec((Q_LORA, QK_W)), _full_spec((Q_LORA, QK_W)),
                  _full_spec((1, KV_LORA)), _full_spec((KV_LORA, QK_W)), _full_spec((KV_LORA, VT_W)),
                  _full_spec((1, VT_W)),
                  pl.BlockSpec((ROW_TILE, HEAD_PAD), lambda t: (t % TILES_PER_BATCH, 0)),
                  pl.BlockSpec((ROW_TILE, HEAD_PAD), lambda t: (t % TILES_PER_BATCH, 0))],
        out_specs=(pl.BlockSpec((1, QK_W, ROW_TILE), lambda t: (t, 0, 0)),
                   _tile_spec(QK_W),
                   pl.BlockSpec((ROW_TILE // ATT_CHUNK, VT_W, ATT_CHUNK), lambda t: (t, 0, 0)),
                   _tile_spec(POOL_DIM)),
        out_shape=(jax.ShapeDtypeStruct((N_TILES, QK_W, ROW_TILE), BF16),
                   jax.ShapeDtypeStruct((N_ROWS, QK_W), BF16),
                   jax.ShapeDtypeStruct((N_ROWS // ATT_CHUNK, VT_W, ATT_CHUNK), BF16),
                   jax.ShapeDtypeStruct((N_ROWS, POOL_DIM), F32)),
        compiler_params=_cparams(("parallel",)),
        name="odd_pre",
    )(x, mod_i, g0, w_in, gq, wqm, wqs, gkv, wk, wv, vaug, cos_t, sin_t)


def _attn_kernel(qta_ref, qtb_ref, k_ref, vt_ref, oa_ref, ob_ref, m_ref, acc_ref, s_ref, p_ref, a_ref):
    qt_refs = (qta_ref, qtb_ref)
    m_ref[...] = jnp.full(m_ref.shape, -jnp.inf, F32)
    acc_ref[...] = jnp.zeros(acc_ref.shape, F32)

    def scores_item(c, slot, qi, hd):
        r0 = pl.multiple_of(c * ATT_CHUNK, ATT_CHUNK)
        sl = slice(hd * HEAD_PAD, (hd + 1) * HEAD_PAD)
        s_ref[slot, qi * MLA_HEADS + hd] = _dot(k_ref[pl.ds(r0, ATT_CHUNK), sl], qt_refs[qi][0, sl, :])

    def softmax_item(slot, it):
        hs = slice(it, it + 1)
        m_old = m_ref[hs, :]
        m_new = jnp.maximum(m_old, jnp.max(s_ref[slot, it], axis=0, keepdims=True))
        m_ref[hs, :] = m_new
        a_ref[slot, hs, :] = jnp.exp2(m_old - m_new)
        for r0 in range(0, ATT_CHUNK, SOFTMAX_ROWS):
            rows = slice(r0, r0 + SOFTMAX_ROWS)
            p_ref[slot, it, rows, :] = jnp.exp2(s_ref[slot, it, rows, :] - m_new).astype(BF16)

    def pv_item(c, slot, qi, hd):
        it = qi * MLA_HEADS + hd
        rows = slice(it * V_AUG, (it + 1) * V_AUG)
        pv = _dot(vt_ref[c, hd * V_AUG:(hd + 1) * V_AUG, :], p_ref[slot, it])
        acc_ref[rows, :] = a_ref[slot, it:it + 1, :] * acc_ref[rows, :] + pv

    def stage_scores(c, slot, nq):
        for qi in range(nq):
            for hd in range(MLA_HEADS):
                scores_item(c, slot, qi, hd)

    def stage_softmax(slot, nq):
        for it in range(nq * MLA_HEADS):
            softmax_item(slot, it)

    def stage_pv(c, slot, nq):
        for qi in range(nq):
            for hd in range(MLA_HEADS):
                pv_item(c, slot, qi, hd)

    def steady(c, slot):
        stage_scores(c, slot, ATT_QTILES)
        stage_pv(c - 2, slot, ATT_QTILES)
        stage_softmax(1 - slot, ATT_QTILES)

    def result(qi):
        outs = []
        for hd in range(MLA_HEADS):
            it = qi * MLA_HEADS + hd
            blk = acc_ref[it * V_AUG:(it + 1) * V_AUG, :]
            outs.append(blk[0:MLA_V, :] / blk[MLA_V:MLA_V + 1, :])
        return jnp.concatenate(outs, axis=0).T

    is_ctx = pl.program_id(1) == 0

    @pl.when(is_ctx)
    def _():
        stage_scores(0, 0, 1)
        stage_softmax(0, 1)
        stage_pv(0, 0, 1)
        oa_ref[...] = result(0)
        ob_ref[...] = jnp.zeros(ob_ref.shape, F32)

    @pl.when(jnp.logical_not(is_ctx))
    def _():
        n = ROWS_PER_BATCH // ATT_CHUNK
        stage_scores(0, 0, ATT_QTILES)
        stage_scores(1, 1, ATT_QTILES)
        stage_softmax(0, ATT_QTILES)
        first = 2 + (n - 2) % ATT_TRIP
        for c in range(2, first):
            steady(c, c % 2)

        def trip(i, carry):
            c = first + ATT_TRIP * i
            for u in range(ATT_TRIP):
                steady(c + u, (first + u) % 2)
            return carry

        lax.fori_loop(0, (n - first) // ATT_TRIP, trip, 0)
        stage_pv(n - 2, (n - 2) % 2, ATT_QTILES)
        stage_softmax((n - 1) % 2, ATT_QTILES)
        stage_pv(n - 1, (n - 1) % 2, ATT_QTILES)
        oa_ref[...] = result(0)
        ob_ref[...] = result(1)


def _attention(qt, k, vt):
    cpb = ROWS_PER_BATCH // ATT_CHUNK
    assert cpb >= 3 and CTX_LEN == ATT_CHUNK and (TILES_PER_BATCH - 1) % ATT_QTILES == 0
    items = ATT_QTILES * MLA_HEADS

    def q_first(b, i):
        return (b * TILES_PER_BATCH + jnp.maximum(ATT_QTILES * i - 1, 0), 0, 0)

    def q_second(b, i):
        return (b * TILES_PER_BATCH + ATT_QTILES * i, 0, 0)

    out_spec = pl.BlockSpec((ROW_TILE, ATT_W), lambda b, i: (b * ATT_STEPS + i, 0))
    out_sds = jax.ShapeDtypeStruct((BATCH * ATT_STEPS * ROW_TILE, ATT_W), F32)
    return pl.pallas_call(
        _attn_kernel,
        grid=(BATCH, ATT_STEPS),
        in_specs=[pl.BlockSpec((1, QK_W, ROW_TILE), q_first),
                  pl.BlockSpec((1, QK_W, ROW_TILE), q_second),
                  pl.BlockSpec((ROWS_PER_BATCH, QK_W), lambda b, i: (b, 0), pipeline_mode=pl.Buffered(1)),
                  pl.BlockSpec((cpb, VT_W, ATT_CHUNK), lambda b, i: (b, 0, 0), pipeline_mode=pl.Buffered(1))],
        out_specs=(out_spec, out_spec),
        out_shape=(out_sds, out_sds),
        scratch_shapes=[pltpu.VMEM((items, ROW_TILE), F32),
                        pltpu.VMEM((items * V_AUG, ROW_TILE), F32),
                        pltpu.VMEM((2, items, ATT_CHUNK, ROW_TILE), F32),
                        pltpu.VMEM((2, items, ATT_CHUNK, ROW_TILE), BF16),
                        pltpu.VMEM((2, items, ROW_TILE), F32)],
        compiler_params=pltpu.CompilerParams(dimension_semantics=("parallel", "parallel"),
                                             vmem_limit_bytes=ATT_VMEM_LIMIT),
        name="attention",
    )(qt, qt, k, vt)


def _odd_post_kernel(atta_ref, attb_ref, p_ref, pp_ref, pn_ref, wp_ref, bp_ref, sp_ref, wout_ref,
                     x_ref, mod_ref, g1_ref, wr_ref, br_ref,
                     xo_ref, hx_ref, meta_ref, counts_ref, ext_ref, cnt_ref):
    is_start, is_end, seq_tile, is_ctx = _seq_flags()
    hp = SMALL_HALO
    ext_ref[0:hp] = jnp.where(is_start, 0.0, pp_ref[...])
    ext_ref[hp:hp + ROW_TILE] = p_ref[...]
    ext_ref[hp + ROW_TILE:2 * hp + ROW_TILE] = jnp.where(is_end, 0.0, pn_ref[...])
    seq_len = jnp.where(is_ctx, CTX_LEN, SEQ)
    pos = seq_tile * ROW_TILE + lax.broadcasted_iota(jnp.int32, (ROW_TILE, 1), 0)

    pooled = []
    for g, w in enumerate(POOL_WINDOWS):
        sl = slice(g * POOL_GROUP, (g + 1) * POOL_GROUP)
        ssum = jnp.zeros((ROW_TILE, POOL_GROUP), F32)
        for d in range(-(w // 2), w - w // 2):
            ssum = ssum + ext_ref[hp + d:hp + d + ROW_TILE, sl]
        lo = jnp.maximum(pos - w // 2, 0)
        hi = jnp.minimum(pos - w // 2 + w, seq_len)
        cnt = (hi - lo).astype(F32)
        pm = ssum / cnt - p_ref[:, sl]
        pooled.append(((_dot(pm.astype(BF16), wp_ref[g]) + bp_ref[g:g + 1, :]) * sp_ref[:, sl]).astype(BF16))
    w = pl.program_id(0) % TILES_PER_BATCH
    att = jnp.where((w > 0) & (w % ATT_QTILES == 0), attb_ref[...], atta_ref[...])
    mixed = jnp.concatenate([att.astype(BF16)] + pooled, axis=1)
    y = _dot(mixed, wout_ref[...])
    _residual_and_route(x_ref[...], y, mod_ref[0], g1_ref, wr_ref, br_ref,
                        xo_ref, hx_ref, meta_ref, counts_ref, cnt_ref)


def _odd_post(att_a, att_b, pool_in, w_pool, b_pool, s_pool, w_out, x, mod_i, g1, wr, br):
    pp_spec, pn_spec = _halo_specs(POOL_DIM, SMALL_HALO)
    ng = len(POOL_WINDOWS)
    att_spec = pl.BlockSpec((ROW_TILE, ATT_W), lambda t: ((t // TILES_PER_BATCH) * ATT_STEPS
                                                          + (t % TILES_PER_BATCH + 1) // ATT_QTILES, 0))
    return pl.pallas_call(
        _odd_post_kernel,
        grid=(N_TILES,),
        in_specs=[att_spec, att_spec, _tile_spec(POOL_DIM), pp_spec, pn_spec,
                  _full_spec((ng, POOL_GROUP, POOL_GROUP)), _full_spec((ng, POOL_GROUP)),
                  _full_spec((1, POOL_DIM)), _full_spec((D_MODEL, D_MODEL)),
                  _tile_spec(D_MODEL), _mod_spec(), _full_spec((1, D_MODEL)),
                  _full_spec((D_MODEL, 2 * ROUTE_W)),
                  _full_spec((1, ROUTE_W))],
        out_specs=_post_out_specs(),
        out_shape=_POST_OUT_SHAPES,
        scratch_shapes=[pltpu.VMEM((ROW_TILE + 2 * SMALL_HALO, POOL_DIM), F32), _class_count_scratch()],
        compiler_params=_cparams(("arbitrary",)),
        name="odd_post",
    )(att_a, att_b, pool_in, pool_in, pool_in, w_pool, b_pool, s_pool, w_out, x, mod_i, g1, wr, br)


def _moe_plan(meta, counts):
    cls = meta[:, CLS_LANE - META_ROW0, :].reshape(N_ROWS).astype(jnp.int32)
    rank = meta[:, RANK_LANE - META_ROW0, :].reshape(N_ROWS).astype(jnp.int32)
    tiles = (counts[0, :N_CLASSES].astype(jnp.int32) + MOE_TM - 1) // MOE_TM
    tile_end = jnp.cumsum(tiles)
    first_row = (tile_end - tiles) * MOE_TM
    dest = rank
    for c in range(N_CLASSES):
        dest = dest + jnp.where(cls == c, first_row[c], 0)
    j = jnp.arange(MOE_TILES)
    used = tile_end[-1]
    j_eff = jnp.minimum(j, jnp.maximum(used - 1, 0))
    cls_of_tile = jnp.sum((j_eff[:, None] >= tile_end[None, :]).astype(jnp.int32), axis=1)
    cls_of_tile = jnp.minimum(cls_of_tile, N_CLASSES - 1)
    grp = cls_of_tile // PAIRS_PER_GROUP
    pair = cls_of_tile % PAIRS_PER_GROUP
    e_a = grp * EXPERTS_PER_GROUP + jnp.array(PAIR_A, jnp.int32)[pair]
    e_b = grp * EXPERTS_PER_GROUP + jnp.array(PAIR_B, jnp.int32)[pair]
    status = jnp.where(j < used, 1 + jnp.array(PAIR_A_IS_HIGHER, jnp.int32)[pair], 0)
    new_a = jnp.concatenate([jnp.ones((1,), jnp.int32), (e_a[1:] != e_a[:-1]).astype(jnp.int32)])
    new_b = jnp.concatenate([jnp.ones((1,), jnp.int32), (e_b[1:] != e_b[:-1]).astype(jnp.int32)])
    return dest, e_a, e_b, status, new_a + 2 * new_b


def _move_rows(n_rows, base, dest_ref, row_copy, block_copy):
    def issue(i, carry):
        for u in range(DMA_UNROLL):
            r = i * DMA_UNROLL + u
            row_copy(r, dest_ref[base + r]).start(priority=u % 2)
        return carry

    lax.fori_loop(0, n_rows // DMA_UNROLL, issue, 0)
    block_copy.wait()


def _dispatch_kernel(dest_ref, hx_ref, init_ref, xs_ref, sem):
    del init_ref
    _move_rows(DISPATCH_ROWS, pl.program_id(0) * DISPATCH_ROWS, dest_ref,
               lambda r, d: pltpu.make_async_copy(hx_ref.at[pl.ds(r, 1)], xs_ref.at[pl.ds(d, 1)], sem),
               pltpu.make_async_copy(hx_ref, xs_ref.at[pl.ds(0, DISPATCH_ROWS)], sem))


def _dispatch(dest, hx, init):
    shape = (MOE_ROWS, HX_W)
    return pl.pallas_call(
        _dispatch_kernel,
        grid_spec=pltpu.PrefetchScalarGridSpec(
            num_scalar_prefetch=1,
            grid=(N_ROWS // DISPATCH_ROWS,),
            in_specs=[pl.BlockSpec((DISPATCH_ROWS, HX_W), lambda t, dest: (t, 0)),
                      pl.BlockSpec(memory_space=pl.ANY)],
            out_specs=pl.BlockSpec(memory_space=pl.ANY),
            scratch_shapes=[pltpu.SemaphoreType.DMA(())]),
        out_shape=jax.ShapeDtypeStruct(shape, F32),
        input_output_aliases={2: 0},
        compiler_params=_cparams(("arbitrary",)),
        name="moe_dispatch",
    )(dest, hx, init)


def _moe_expert_kernel(ea_ref, eb_ref, status_ref, fresh_ref, xs_ref, w1a_ref, w3a_ref, w2a_ref,
                       w1b_ref, w3b_ref, w2b_ref, ys_ref, *bf_refs):
    del ea_ref, eb_ref
    status = status_ref[pl.program_id(0)]
    fresh = fresh_ref[pl.program_id(0)]
    slot_a, slot_b = bf_refs[0:3], bf_refs[3:6]

    @pl.when(fresh % 2 == 1)
    def _():
        for dst, src in zip(slot_a, (w1a_ref, w3a_ref, w2a_ref)):
            dst[...] = src[0, 0].astype(BF16)

    @pl.when(fresh >= 2)
    def _():
        for dst, src in zip(slot_b, (w1b_ref, w3b_ref, w2b_ref)):
            dst[...] = src[0, 0].astype(BF16)

    @pl.when(status == 0)
    def _():
        ys_ref[...] = jnp.zeros(ys_ref.shape, F32)

    @pl.when(status != 0)
    def _():
        h = xs_ref[:, 0:D_MODEL].astype(BF16)
        g_lo = xs_ref[:, D_MODEL + GLO_LANE:D_MODEL + GLO_LANE + 1]
        g_hi = xs_ref[:, D_MODEL + GHI_LANE:D_MODEL + GHI_LANE + 1]
        a_is_higher = status == 2
        y = jnp.zeros((MOE_TM, D_MODEL), F32)
        for g, (w1_ref, w3_ref, w2_ref) in ((jnp.where(a_is_higher, g_hi, g_lo), slot_a),
                                            (jnp.where(a_is_higher, g_lo, g_hi), slot_b)):
            hid = _silu(_dot(h, w1_ref[...])) * _dot(h, w3_ref[...])
            y = y + _dot((g * hid).astype(BF16), w2_ref[...])
        ys_ref[...] = y


def _moe_experts(e_a, e_b, status, fresh, xs_sorted, w1, w3, w2, layer):
    up_a = pl.BlockSpec((1, 1, D_MODEL, D_EXPERT), lambda j, ea, eb, status, fresh: (layer, ea[j], 0, 0))
    dn_a = pl.BlockSpec((1, 1, D_EXPERT, D_MODEL), lambda j, ea, eb, status, fresh: (layer, ea[j], 0, 0))
    up_b = pl.BlockSpec((1, 1, D_MODEL, D_EXPERT), lambda j, ea, eb, status, fresh: (layer, eb[j], 0, 0))
    dn_b = pl.BlockSpec((1, 1, D_EXPERT, D_MODEL), lambda j, ea, eb, status, fresh: (layer, eb[j], 0, 0))
    up_bf = pltpu.VMEM((D_MODEL, D_EXPERT), BF16)
    dn_bf = pltpu.VMEM((D_EXPERT, D_MODEL), BF16)
    return pl.pallas_call(
        _moe_expert_kernel,
        grid_spec=pltpu.PrefetchScalarGridSpec(
            num_scalar_prefetch=4,
            grid=(MOE_TILES,),
            in_specs=[pl.BlockSpec((MOE_TM, HX_W), lambda j, ea, eb, status, fresh: (j, 0)),
                      up_a, up_a, dn_a, up_b, up_b, dn_b],
            out_specs=pl.BlockSpec((MOE_TM, D_MODEL), lambda j, ea, eb, status, fresh: (j, 0)),
            scratch_shapes=[up_bf, up_bf, dn_bf, up_bf, up_bf, dn_bf]),
        out_shape=jax.ShapeDtypeStruct((MOE_ROWS, D_MODEL), F32),
        compiler_params=_cparams(("arbitrary",)),
        name="moe_experts",
    )(e_a, e_b, status, fresh, xs_sorted, w1, w3, w2, w1, w3, w2)


def _combine_kernel(dest_ref, x_ref, mod_ref, ys_ref, o_ref, buf_ref, sem):
    t = pl.program_id(0)
    _move_rows(COMBINE_ROWS, t * COMBINE_ROWS, dest_ref,
               lambda r, d: pltpu.make_async_copy(ys_ref.at[pl.ds(d, 1)], buf_ref.at[pl.ds(r, 1)], sem),
               pltpu.make_async_copy(ys_ref.at[pl.ds(0, COMBINE_ROWS)], buf_ref, sem))
    sub = COMBINE_ROWS // ROW_TILE
    for b in range(sub):
        rs = slice(b * ROW_TILE, (b + 1) * ROW_TILE)
        m5 = mod_ref[pl.ds(_mod_row(t * sub + b), 1)][0, 5:6]
        o_ref[rs, :] = x_ref[rs, :] + m5 * buf_ref[rs, :]


def _combine(dest, x, mod_i, ys):
    return pl.pallas_call(
        _combine_kernel,
        grid_spec=pltpu.PrefetchScalarGridSpec(
            num_scalar_prefetch=1,
            grid=(N_ROWS // COMBINE_ROWS,),
            in_specs=[pl.BlockSpec((COMBINE_ROWS, D_MODEL), lambda t, dest: (t, 0)),
                      pl.BlockSpec((MOD_ROWS, 6, D_MODEL), lambda t, dest: (0, 0, 0)),
                      pl.BlockSpec(memory_space=pl.ANY)],
            out_specs=pl.BlockSpec((COMBINE_ROWS, D_MODEL), lambda t, dest: (t, 0)),
            scratch_shapes=[pltpu.VMEM((COMBINE_ROWS, D_MODEL), F32), pltpu.SemaphoreType.DMA(())]),
        out_shape=jax.ShapeDtypeStruct((N_ROWS, D_MODEL), F32),
        compiler_params=_cparams(("arbitrary",)),
        name="moe_combine",
    )(dest, x, mod_i, ys)


def _combine_final_kernel(dest_ref, x_ref, mod_ref, g_ref, ys_ref, o_ref, buf_ref, sem):
    _move_rows(ROW_TILE, pl.program_id(0) * ROW_TILE, dest_ref,
               lambda r, d: pltpu.make_async_copy(ys_ref.at[pl.ds(d, 1)], buf_ref.at[pl.ds(r, 1)], sem),
               pltpu.make_async_copy(ys_ref.at[pl.ds(0, ROW_TILE)], buf_ref, sem))
    x = x_ref[...] + mod_ref[0][5:6] * buf_ref[...]
    o_ref[0] = x * lax.rsqrt(jnp.mean(x * x, axis=-1, keepdims=True) + EPS) * g_ref[...]


def _combine_final(dest, x, mod_i, ys, g):
    def out_map(t, dest):
        return (t // TILES_PER_BATCH, jnp.maximum(t % TILES_PER_BATCH - 1, 0), 0)

    return pl.pallas_call(
        _combine_final_kernel,
        grid_spec=pltpu.PrefetchScalarGridSpec(
            num_scalar_prefetch=1,
            grid=(N_TILES,),
            in_specs=[pl.BlockSpec((ROW_TILE, D_MODEL), lambda t, dest: (t, 0)),
                      pl.BlockSpec((1, 6, D_MODEL), lambda t, dest: (_mod_row(t), 0, 0)),
                      pl.BlockSpec((1, D_MODEL), lambda t, dest: (0, 0)),
                      pl.BlockSpec(memory_space=pl.ANY)],
            out_specs=pl.BlockSpec((1, ROW_TILE, D_MODEL), out_map),
            scratch_shapes=[pltpu.VMEM((ROW_TILE, D_MODEL), F32), pltpu.SemaphoreType.DMA(())]),
        out_shape=jax.ShapeDtypeStruct((BATCH, SEQ, D_MODEL), F32),
        compiler_params=_cparams(("arbitrary",)),
        name="moe_combine_final",
    )(dest, x, mod_i, g, ys)


_SWAP32 = tuple(list(range(8, 16)) + list(range(0, 8)) + list(range(24, 32)) + list(range(16, 24)))


def _rope_tables():
    f32 = np.float32
    rows = SEQ // GRID_W
    pos_row = np.repeat(np.arange(rows, dtype=f32), GRID_W)
    pos_col = np.tile(np.arange(GRID_W, dtype=f32), rows)
    inv = (f32(ROPE_BASE) ** (-np.arange(0, AXIS_DIM, 2, dtype=f32) / f32(AXIS_DIM))).astype(f32)
    ang = np.concatenate([pos_row[:, None] * inv, pos_col[:, None] * inv], axis=-1).astype(f32)
    cos, sin = np.cos(ang).astype(f32), np.sin(ang).astype(f32)
    p = AXIS_PAIRS
    c32 = np.concatenate([cos[:, :p], cos[:, :p], cos[:, p:], cos[:, p:]], axis=-1)
    s32 = np.concatenate([-sin[:, :p], sin[:, :p], -sin[:, p:], sin[:, p:]], axis=-1)
    pad = HEAD_PAD - MLA_NOPE - MLA_ROPE
    cos_lat = np.concatenate([np.ones((SEQ, MLA_NOPE), f32), c32, np.zeros((SEQ, pad), f32)], axis=-1)
    sin_lat = np.concatenate([np.zeros((SEQ, MLA_NOPE), f32), s32, np.zeros((SEQ, pad), f32)], axis=-1)
    cos_ctx = np.concatenate([np.ones((CTX_LEN, MLA_NOPE + MLA_ROPE), f32), np.zeros((CTX_LEN, pad), f32)], axis=-1)
    sin_ctx = np.zeros((CTX_LEN, HEAD_PAD), f32)
    return (jnp.asarray(np.concatenate([cos_ctx, cos_lat], axis=0)),
            jnp.asarray(np.concatenate([sin_ctx, sin_lat], axis=0)))


def _odd_weights(w_in, w_uq, w_ukv):
    swap = jnp.array(_SWAP32)
    o_kv, o_kr, o_pool = Q_LORA, Q_LORA + KV_LORA, Q_LORA + KV_LORA + MLA_ROPE
    w_kr = w_in[:, o_kr:o_pool]
    pad_l = jnp.zeros((D_MODEL, MLA_NOPE), F32)
    pad_r = jnp.zeros((D_MODEL, HEAD_PAD - MLA_NOPE - MLA_ROPE), F32)
    w_in2 = jnp.concatenate([w_in[:, :o_kr], pad_l, w_kr, pad_r, pad_l, w_kr[:, swap], pad_r,
                             w_in[:, o_pool:]], axis=-1).astype(BF16)
    wq = w_uq.reshape(Q_LORA, MLA_HEADS, MLA_NOPE + MLA_ROPE)
    zq = jnp.zeros((Q_LORA, MLA_HEADS, HEAD_PAD - MLA_NOPE - MLA_ROPE), F32)
    wq_main = jnp.concatenate([wq, zq], axis=-1).reshape(Q_LORA, QK_W).astype(BF16)
    wq_swap = jnp.concatenate([jnp.zeros((Q_LORA, MLA_HEADS, MLA_NOPE), F32),
                               wq[:, :, MLA_NOPE:][:, :, swap], zq], axis=-1).reshape(Q_LORA, QK_W).astype(BF16)
    wkv = w_ukv.reshape(KV_LORA, MLA_HEADS, MLA_NOPE + MLA_V)
    zk = jnp.zeros((KV_LORA, MLA_HEADS, HEAD_PAD - MLA_NOPE), F32)
    wk = jnp.concatenate([wkv[:, :, :MLA_NOPE], zk], axis=-1).reshape(KV_LORA, QK_W).astype(BF16)
    zv = jnp.zeros((KV_LORA, MLA_HEADS, V_AUG - MLA_V), F32)
    wv = jnp.concatenate([wkv[:, :, MLA_NOPE:], zv], axis=-1).reshape(KV_LORA, VT_W).astype(BF16)
    return w_in2, wq_main, wq_swap, wk, wv


def kernel(x, c, ctx, c_ctx, w_mod, b_mod, norm_g, ev_w_in, ev_conv_a_w, ev_conv_a_b, ev_ln_a_g, ev_ln_a_b,
           ev_conv_b_w, ev_w_out, od_w_in, od_q_norm_g, od_w_uq, od_kv_norm_g, od_w_ukv, od_w_pool, od_b_pool,
           od_s_pool, od_w_out, moe_wg, moe_bg, moe_we, moe_be, moe_w1, moe_w3, moe_w2, final_g):
    assert x.shape == (BATCH, SEQ, D_MODEL) and ctx.shape == (BATCH, CTX_LEN, D_MODEL)
    assert CTX_LEN == ROW_TILE
    stream = (x, ctx)
    cvec = jnp.concatenate([c, c_ctx[None, :], jnp.zeros((MOD_ROWS - BATCH - 1, D_MODEL), F32)], axis=0)
    mod = _modulation(cvec, w_mod, b_mod).reshape(DEPTH, MOD_ROWS, 6, D_MODEL)
    cos_t, sin_t = _rope_tables()
    sorted_rows = jnp.zeros((MOE_ROWS, HX_W), F32)

    for i in range(DEPTH):
        j = i // 2
        mod_i = mod[i]
        g0 = norm_g[i, 0][None, :]
        g1 = norm_g[i, 1][None, :]
        wr = jnp.concatenate([moe_wg[i], moe_we[i],
                              jnp.zeros((D_MODEL, ROUTE_W - N_GROUPS - N_EXPERTS), F32)], axis=-1)
        wr_hi = wr.astype(BF16)
        wr = jnp.concatenate([wr_hi, (wr - wr_hi.astype(F32)).astype(BF16)], axis=-1)
        br = jnp.concatenate([moe_bg[i], moe_be[i],
                              jnp.zeros((ROUTE_W - N_GROUPS - N_EXPERTS,), F32)])[None, :]
        if i % 2 == 0:
            a_pre, gb, gch = _even_pre(stream, mod_i, g0, ev_w_in[j].astype(BF16))
            xs, hx, meta, counts = _even_post(stream, a_pre, gb, gch, ev_conv_a_w[j], ev_conv_a_b[j][None, :],
                                              ev_ln_a_g[j][None, :], ev_ln_a_b[j][None, :], ev_conv_b_w[j],
                                              ev_w_out[j].astype(BF16), mod_i, g1, wr, br)
        else:
            w_in2, wq_main, wq_swap, wk, wv = _odd_weights(od_w_in[j], od_w_uq[j], od_w_ukv[j])
            q, k, v, pool_in = _odd_pre(xs, mod_i, g0, w_in2, od_q_norm_g[j][None, :], wq_main, wq_swap,
                                        od_kv_norm_g[j][None, :], wk, wv, cos_t, sin_t)
            att_a, att_b = _attention(q, k, v)
            xs, hx, meta, counts = _odd_post(att_a, att_b, pool_in, od_w_pool[j].astype(BF16), od_b_pool[j],
                                      od_s_pool[j][None, :], od_w_out[j].astype(BF16), xs, mod_i, g1,
                                      wr, br)
        dest, e_a, e_b, status, fresh = _moe_plan(meta, counts)
        sorted_rows = _dispatch(dest, hx, sorted_rows)
        ys = _moe_experts(e_a, e_b, status, fresh, sorted_rows, moe_w1, moe_w3, moe_w2, i)
        if i == DEPTH - 1:
            return _combine_final(dest, xs, mod_i, ys, final_g[None, :])
        xs = _combine(dest, xs, mod_i, ys)
        stream = (xs,)
```
